```python
import math
import jax
import jax.numpy as jnp
from jax import lax
import numpy as np

D_MODEL = 1024
BATCH = 4
SEQ = 4096
DEPTH = 2

MLSTM_HEADS = 4
MLSTM_WIDTH = 768
MLSTM_HD = MLSTM_WIDTH // MLSTM_HEADS
MLSTM_CONV = 4
GLA_HEADS = 4
GLA_WIDTH = 768
GLA_KEY_WIDTH = GLA_WIDTH // 2
GLA_DK = GLA_KEY_WIDTH // GLA_HEADS
GLA_DV = GLA_WIDTH // GLA_HEADS
GLA_GATE_RANK = 16
GLA_GATE_TAU = 16.0
S5_WIDTH = 512
S5_GROUP = 16
S5_GROUPS = S5_WIDTH // S5_GROUP
S5_STATE = 64
DT_MIN = 1e-3
DT_MAX = 1e-1
CHUNK = 64
EPS = 1e-6
IN_SIZES = (MLSTM_WIDTH, MLSTM_WIDTH, MLSTM_WIDTH, MLSTM_WIDTH, MLSTM_HEADS, MLSTM_HEADS, MLSTM_WIDTH,
            GLA_KEY_WIDTH, GLA_KEY_WIDTH, GLA_WIDTH, GLA_GATE_RANK, GLA_WIDTH,
            S5_WIDTH, S5_WIDTH,
            3 * D_MODEL)
IN_WIDTH = sum(IN_SIZES)

kernel_name = 'hybrid_mlstm_gla_s5_gated_block'


def rmsnorm(x, w):
    xf = x.astype(jnp.float32)
    y = xf * lax.rsqrt(jnp.mean(xf * xf, axis=-1, keepdims=True) + EPS)
    return (y * w.astype(jnp.float32)).astype(x.dtype)


def head_norm(x, w, n_heads, center):
    b, l, width = x.shape
    xh = x.astype(jnp.float32).reshape(b, l, n_heads, width // n_heads)
    if center:
        xh = xh - jnp.mean(xh, axis=-1, keepdims=True)
    xh = xh * lax.rsqrt(jnp.mean(xh * xh, axis=-1, keepdims=True) + EPS)
    return xh.reshape(b, l, width) * w.astype(jnp.float32)


def causal_dwconv(x, w):
    taps = w.shape[0]
    l = x.shape[1]
    xp = jnp.pad(x, ((0, 0), (taps - 1, 0), (0, 0)))
    return sum(xp[:, j:j + l] * w[j] for j in range(taps))


def to_chunks(x, n_heads):
    b, l, width = x.shape
    x = x.reshape(b, l // CHUNK, CHUNK, n_heads, width // n_heads)
    return x.transpose(1, 0, 3, 2, 4)


def from_chunks(y):
    n, b, h, t, d = y.shape
    return y.transpose(1, 0, 3, 2, 4).reshape(b, n * t, h * d)


def gate_chunks(g):
    b, l, h = g.shape
    return g.reshape(b, l // CHUNK, CHUNK, h).transpose(1, 0, 3, 2)


def mlstm_chunkwise(q, k, v, i_pre, logf):
    b, l, _ = q.shape
    h, d = MLSTM_HEADS, MLSTM_HD
    qc = to_chunks(q, h)
    kc = to_chunks(k * d ** -0.5, h)
    vc = to_chunks(v, h)
    ic = gate_chunks(i_pre)
    fc = gate_chunks(logf)
    causal = jnp.tril(jnp.ones((CHUNK, CHUNK), dtype=bool))

    def step(carry, inp):
        c_mat, n_vec, m = carry
        qt, kt, vt, it, ft = inp
        bcum = jnp.cumsum(ft, axis=-1)
        dmat = bcum[..., :, None] - bcum[..., None, :] + it[..., None, :]
        dmat = jnp.where(causal, dmat, -jnp.inf)
        inter = bcum + m[..., None]
        m_t = jnp.maximum(inter, jnp.max(dmat, axis=-1))
        w_intra = jnp.exp(dmat - m_t[..., None])
        w_inter = jnp.exp(inter - m_t)
        s = jnp.einsum('bhtd,bhsd->bhts', qt, kt) * w_intra
        num = (jnp.einsum('bhts,bhse->bhte', s, vt)
               + w_inter[..., None] * jnp.einsum('bhtd,bhde->bhte', qt, c_mat))
        den = jnp.sum(s, axis=-1) + w_inter * jnp.einsum('bhtd,bhd->bht', qt, n_vec)
        out = num / jnp.maximum(jnp.abs(den), jnp.exp(-m_t))[..., None]
        b_end = bcum[..., -1]
        to_end = b_end[..., None] - bcum + it
        m_new = jnp.maximum(b_end + m, jnp.max(to_end, axis=-1))
        wk = jnp.exp(to_end - m_new[..., None])
        w_prev = jnp.exp(b_end + m - m_new)
        c_new = w_prev[..., None, None] * c_mat + jnp.einsum('bhs,bhsd,bhse->bhde', wk, kt, vt)
        n_new = w_prev[..., None] * n_vec + jnp.einsum('bhs,bhsd->bhd', wk, kt)
        return (c_new, n_new, m_new), out

    init = (jnp.zeros((b, h, d, d), jnp.float32), jnp.zeros((b, h, d), jnp.float32),
            jnp.zeros((b, h), jnp.float32))
    _, ys = lax.scan(step, init, (qc, kc, vc, ic, fc))
    return from_chunks(ys)


def gla_chunkwise(q, k, v, log_a):
    b, l, _ = q.shape
    h = GLA_HEADS
    qc = to_chunks(q * GLA_DK ** -0.5, h)
    kc = to_chunks(k, h)
    vc = to_chunks(v, h)
    ac = to_chunks(log_a, h)
    causal = jnp.tril(jnp.ones((CHUNK, CHUNK), dtype=bool))[:, :, None]

    def step(state, inp):
        qt, kt, vt, at = inp
        bcum = jnp.cumsum(at, axis=2)
        diff = bcum[:, :, :, None, :] - bcum[:, :, None, :, :]
        decay = jnp.exp(jnp.where(causal, diff, -jnp.inf))
        attn = jnp.einsum('bhtd,bhsd,bhtsd->bhts', qt, kt, decay)
        out = (jnp.einsum('bhts,bhse->bhte', attn, vt)
               + jnp.einsum('bhtd,bhde->bhte', qt * jnp.exp(bcum), state))
        last = bcum[:, :, -1:, :]
        new_state = (jnp.exp(last[:, :, 0])[..., None] * state
                     + jnp.einsum('bhsd,bhse->bhde', kt * jnp.exp(last - bcum), vt))
        return new_state, out

    init = jnp.zeros((b, h, GLA_DK, GLA_DV), jnp.float32)
    _, ys = lax.scan(step, init, (qc, kc, vc, ac))
    return from_chunks(ys)


def s5_ssm(u, lam_re, lam_im, log_dt, b_re, b_im, c_re, c_im, d_skip):
    f32 = jnp.float32
    bsz, l, _ = u.shape
    ug = u.reshape(bsz, l, S5_GROUPS, S5_GROUP)
    lr = jnp.minimum(lam_re.astype(f32), -1e-4)
    li = lam_im.astype(f32)
    dt = jnp.exp(log_dt.astype(f32))[:, None]
    mag = jnp.exp(lr * dt)
    ab_re = mag * jnp.cos(li * dt)
    ab_im = mag * jnp.sin(li * dt)
    nr = ab_re - 1.0
    den = lr * lr + li * li
    coef_re = (nr * lr + ab_im * li) / den
    coef_im = (ab_im * lr - nr * li) / den
    br, bi = b_re.astype(f32), b_im.astype(f32)
    bb_re = coef_re[..., None] * br - coef_im[..., None] * bi
    bb_im = coef_re[..., None] * bi + coef_im[..., None] * br
    bu_re = jnp.einsum('blgh,gph->blgp', ug, bb_re)
    bu_im = jnp.einsum('blgh,gph->blgp', ug, bb_im)
    a_re = jnp.broadcast_to(ab_re, bu_re.shape)
    a_im = jnp.broadcast_to(ab_im, bu_re.shape)

    def combine(e1, e2):
        a1r, a1i, b1r, b1i = e1
        a2r, a2i, b2r, b2i = e2
        return (a2r * a1r - a2i * a1i, a2r * a1i + a2i * a1r,
                a2r * b1r - a2i * b1i + b2r, a2r * b1i + a2i * b1r + b2i)

    _, _, x_re, x_im = lax.associative_scan(combine, (a_re, a_im, bu_re, bu_im), axis=1)
    y = (jnp.einsum('blgp,ghp->blgh', x_re, c_re.astype(f32))
         - jnp.einsum('blgp,ghp->blgh', x_im, c_im.astype(f32)))
    return y.reshape(bsz, l, S5_WIDTH) + d_skip.astype(f32) * u


def hybrid_layer(x, norm_w, w_in, mlstm_conv, mlstm_gate_b, mlstm_norm, gla_w_alpha, gla_b_alpha,
                 gla_norm, s5_lam_re, s5_lam_im, s5_log_dt, s5_B_re, s5_B_im, s5_C_re, s5_C_im,
                 s5_D, s5_w_glu, w_branch_mlstm, w_branch_gla, w_branch_s5, w_out):
    f32 = jnp.float32
    dt = x.dtype
    hn = rmsnorm(x, norm_w)
    proj = hn @ w_in
    split_at = np.cumsum(IN_SIZES)[:-1].tolist()
    (aq, ak, av, ao, ai, af, az, bq, bk, bv, ba, bz, cu, cz, g) = jnp.split(proj, split_at, axis=-1)

    qk = jax.nn.silu(causal_dwconv(jnp.concatenate([aq, ak], axis=-1), mlstm_conv))
    aq, ak = jnp.split(qk, 2, axis=-1)
    i_pre = (ai + mlstm_gate_b[0]).astype(f32)
    logf = jax.nn.log_sigmoid((af + mlstm_gate_b[1]).astype(f32))
    ha = mlstm_chunkwise(aq.astype(f32), ak.astype(f32), av.astype(f32), i_pre, logf)
    ha = head_norm(ha, mlstm_norm, MLSTM_HEADS, True).astype(dt)
    ya = jax.nn.sigmoid(ao) * ha * jax.nn.silu(az)

    log_a = jax.nn.log_sigmoid((ba @ gla_w_alpha + gla_b_alpha).astype(f32)) / GLA_GATE_TAU
    hb = gla_chunkwise(bq.astype(f32), bk.astype(f32), bv.astype(f32), log_a)
    yb = head_norm(hb, gla_norm, GLA_HEADS, False).astype(dt) * jax.nn.silu(bz)

    yc = jax.nn.gelu(s5_ssm(cu.astype(f32), s5_lam_re, s5_lam_im, s5_log_dt, s5_B_re, s5_B_im,
                            s5_C_re, s5_C_im, s5_D)).astype(dt)
    yc = yc * jax.nn.sigmoid(yc @ s5_w_glu) * jax.nn.silu(cz)

    ga, gb, gc = jnp.split(jax.nn.sigmoid(g), 3, axis=-1)
    merged = ga * (ya @ w_branch_mlstm) + gb * (yb @ w_branch_gla) + gc * (yc @ w_branch_s5)
    return x + merged @ w_out


def setup_inputs(seed: int = 0) -> dict:
    key = jax.random.key(seed)
    ks = jax.random.split(key, 24)

    def nrm(k, shape, scale):
        return scale * jax.random.normal(k, shape, jnp.float32)

    x = nrm(ks[0], (BATCH, SEQ, D_MODEL), 1.0)
    norm_w = 1.0 + nrm(ks[1], (DEPTH, D_MODEL), 0.02)
    w_in = nrm(ks[2], (DEPTH, D_MODEL, IN_WIDTH), D_MODEL ** -0.5)
    mlstm_conv = nrm(ks[3], (DEPTH, MLSTM_CONV, 2 * MLSTM_WIDTH), MLSTM_CONV ** -0.5)
    f_bias = jnp.linspace(3.0, 6.0, MLSTM_HEADS, dtype=jnp.float32)
    mlstm_gate_b = jnp.stack([nrm(ks[4], (DEPTH, MLSTM_HEADS), 0.1),
                              f_bias + nrm(ks[5], (DEPTH, MLSTM_HEADS), 0.1)], axis=1)
    mlstm_norm = 1.0 + nrm(ks[6], (DEPTH, MLSTM_WIDTH), 0.02)
    gla_w_alpha = nrm(ks[7], (DEPTH, GLA_GATE_RANK, GLA_KEY_WIDTH), GLA_GATE_RANK ** -0.5)
    gla_b_alpha = nrm(ks[8], (DEPTH, GLA_KEY_WIDTH), 0.1)
    gla_norm = 1.0 + nrm(ks[9], (DEPTH, GLA_WIDTH), 0.02)
    s5_lam_re = -0.5 + nrm(ks[10], (DEPTH, S5_GROUPS, S5_STATE), 0.01)
    s5_lam_im = (math.pi * jnp.arange(S5_STATE, dtype=jnp.float32)
                 + nrm(ks[11], (DEPTH, S5_GROUPS, S5_STATE), 0.01))
    s5_log_dt = jax.random.uniform(ks[12], (DEPTH, S5_GROUPS), jnp.float32,
                                   math.log(DT_MIN), math.log(DT_MAX))
    s5_B_re = nrm(ks[13], (DEPTH, S5_GROUPS, S5_STATE, S5_GROUP), (2 * S5_GROUP) ** -0.5)
    s5_B_im = nrm(ks[14], (DEPTH, S5_GROUPS, S5_STATE, S5_GROUP), (2 * S5_GROUP) ** -0.5)
    s5_C_re = nrm(ks[15], (DEPTH, S5_GROUPS, S5_GROUP, S5_STATE), S5_STATE ** -0.5)
    s5_C_im = nrm(ks[16], (DEPTH, S5_GROUPS, S5_GROUP, S5_STATE), S5_STATE ** -0.5)
    s5_D = nrm(ks[17], (DEPTH, S5_WIDTH), 0.5)
    s5_w_glu = nrm(ks[18], (DEPTH, S5_WIDTH, S5_WIDTH), S5_WIDTH ** -0.5)
    w_branch_mlstm = nrm(ks[19], (DEPTH, MLSTM_WIDTH, D_MODEL), MLSTM_WIDTH ** -0.5)
    w_branch_gla = nrm(ks[20], (DEPTH, GLA_WIDTH, D_MODEL), GLA_WIDTH ** -0.5)
    w_branch_s5 = nrm(ks[21], (DEPTH, S5_WIDTH, D_MODEL), S5_WIDTH ** -0.5)
    w_out = nrm(ks[22], (DEPTH, D_MODEL, D_MODEL), 0.5 * D_MODEL ** -0.5)
    final_norm = 1.0 + nrm(ks[23], (D_MODEL,), 0.02)
    return {'x': x, 'norm_w': norm_w, 'w_in': w_in, 'mlstm_conv': mlstm_conv,
            'mlstm_gate_b': mlstm_gate_b, 'mlstm_norm': mlstm_norm, 'gla_w_alpha': gla_w_alpha,
            'gla_b_alpha': gla_b_alpha, 'gla_norm': gla_norm, 's5_lam_re': s5_lam_re,
            's5_lam_im': s5_lam_im, 's5_log_dt': s5_log_dt, 's5_B_re': s5_B_re, 's5_B_im': s5_B_im,
            's5_C_re': s5_C_re, 's5_C_im': s5_C_im, 's5_D': s5_D, 's5_w_glu': s5_w_glu,
            'w_branch_mlstm': w_branch_mlstm, 'w_branch_gla': w_branch_gla,
            'w_branch_s5': w_branch_s5, 'w_out': w_out, 'final_norm': final_norm}


def reference(x, norm_w, w_in, mlstm_conv, mlstm_gate_b, mlstm_norm, gla_w_alpha, gla_b_alpha,
              gla_norm, s5_lam_re, s5_lam_im, s5_log_dt, s5_B_re, s5_B_im, s5_C_re, s5_C_im,
              s5_D, s5_w_glu, w_branch_mlstm, w_branch_gla, w_branch_s5, w_out, final_norm):
    for layer in range(DEPTH):
        x = hybrid_layer(x, norm_w[layer], w_in[layer], mlstm_conv[layer], mlstm_gate_b[layer],
                         mlstm_norm[layer], gla_w_alpha[layer], gla_b_alpha[layer], gla_norm[layer],
                         s5_lam_re[layer], s5_lam_im[layer], s5_log_dt[layer], s5_B_re[layer],
                         s5_B_im[layer], s5_C_re[layer], s5_C_im[layer], s5_D[layer],
                         s5_w_glu[layer], w_branch_mlstm[layer], w_branch_gla[layer],
                         w_branch_s5[layer], w_out[layer])
    return rmsnorm(x, final_norm)
```

```python
import functools
import math

import jax
import jax.numpy as jnp
from jax import lax
from jax.experimental import pallas as pl
from jax.experimental.pallas import tpu as pltpu

F32 = jnp.float32
BF16 = jnp.bfloat16
HIGHEST = lax.Precision.HIGHEST

EPS = 1e-6
D_MODEL = 1024
LANES = 128
M_HEADS = 4
M_HD = 192
M_HDP = 256
M_W = M_HEADS * M_HD
M_WP = M_HEADS * M_HDP
M_CONV = 4
G_HEADS = 4
G_DK = 96
G_DKP = 128
G_DV = 192
G_DVP = 256
G_KW = G_HEADS * G_DK
G_KWP = G_HEADS * G_DKP
G_VW = G_HEADS * G_DV
G_VWP = G_HEADS * G_DVP
G_RANK = 16
G_TAU = 16.0
S_W = 512
S_GROUP = 16
S_GROUPS = 32
S_STATE = 64
S_BLOCKS = S_W // LANES
S_GPB = LANES // S_GROUP
S_SW = S_GPB * S_STATE
S_Q = 8

IN_SIZES = (M_W, M_W, M_W, M_W, M_HEADS, M_HEADS, M_W,
            G_KW, G_KW, G_VW, G_RANK, G_VW,
            S_W, S_W, 3 * D_MODEL)

P_G = 0
P_MQ = 3 * D_MODEL
P_GQ = P_MQ + 5 * M_WP
P_GV = P_GQ + 2 * G_KWP
P_CZ = P_GV + 2 * G_VWP
P_WIDTH = P_CZ + S_W
SMALL_W = S_W + LANES

NEG = -1e30
VMEM_LIMIT = 56 * 1024 * 1024


def _silu(x):
    return x * jax.nn.sigmoid(x)


def _log_sigmoid(x):
    return jnp.minimum(x, 0.0) - jnp.log1p(jnp.exp(-jnp.abs(x)))


def _dot(a, b):
    return jnp.dot(a, b, preferred_element_type=F32)


def _dot_nt(a, b):
    return lax.dot_general(a, b, (((1,), (1,)), ((), ())), preferred_element_type=F32)


def _dot_tn(a, b):
    return lax.dot_general(a, b, (((0,), (0,)), ((), ())), preferred_element_type=F32)


def _params(*sem):
    return pltpu.CompilerParams(dimension_semantics=sem, vmem_limit_bytes=VMEM_LIMIT)


def _rms_hn(x, nw):
    ms = jnp.mean(x * x, axis=-1, keepdims=True)
    return (x * lax.rsqrt(ms + EPS) * nw).astype(BF16)


def _proj_kernel(x_ref, nw_ref, w_ref, o_ref, hn_ref):
    @pl.when(pl.program_id(1) == 0)
    def _():
        hn_ref[...] = _rms_hn(x_ref[...], nw_ref[...])

    o_ref[...] = _dot(hn_ref[...], w_ref[...]).astype(o_ref.dtype)


def _proj(x2, nw, w, tm, tn):
    n = x2.shape[0]
    width = w.shape[1]
    return pl.pallas_call(
        _proj_kernel,
        grid=(n // tm, width // tn),
        in_specs=[pl.BlockSpec((tm, D_MODEL), lambda i, j: (i, 0)),
                  pl.BlockSpec((1, D_MODEL), lambda i, j: (0, 0)),
                  pl.BlockSpec((D_MODEL, tn), lambda i, j: (0, j))],
        out_specs=pl.BlockSpec((tm, tn), lambda i, j: (i, j)),
        out_shape=jax.ShapeDtypeStruct((n, width), BF16),
        scratch_shapes=[pltpu.VMEM((tm, D_MODEL), BF16)],
        compiler_params=_params("arbitrary", "arbitrary"),
        name="proj",
    )(x2, nw, w)


def _proj_small_kernel(x_ref, nw_ref, w_ref, cu_ref, g_ref):
    r = _dot(_rms_hn(x_ref[...], nw_ref[...]), w_ref[...])
    for j in range(S_BLOCKS):
        cu_ref[j] = r[:, j * LANES:(j + 1) * LANES]
    g_ref[...] = r[:, S_W:]


def _proj_small(x2, nw, w, tm):
    n = x2.shape[0]
    return pl.pallas_call(
        _proj_small_kernel,
        grid=(n // tm,),
        in_specs=[pl.BlockSpec((tm, D_MODEL), lambda i: (i, 0)),
                  pl.BlockSpec((1, D_MODEL), lambda i: (0, 0)),
                  pl.BlockSpec((D_MODEL, SMALL_W), lambda i: (0, 0))],
        out_specs=[pl.BlockSpec((S_BLOCKS, tm, LANES), lambda i: (0, i, 0)),
                   pl.BlockSpec((tm, LANES), lambda i: (i, 0))],
        out_shape=[jax.ShapeDtypeStruct((S_BLOCKS, n, LANES), F32),
                   jax.ShapeDtypeStruct((n, LANES), F32)],
        compiler_params=_params("arbitrary"),
        name="proj_small",
    )(x2, nw, w)


def _mlstm_kernel(q_ref, k_ref, v_ref, o_ref, z_ref, g_ref, gb_ref, cq_ref, ck_ref, nw_ref,
                  out_ref, qbuf, kbuf, c_scr, m_scr, *, tc):
    @pl.when(pl.program_id(1) == 0)
    def _():
        qbuf[0:8, :] = jnp.zeros((8, M_WP), F32)
        kbuf[0:8, :] = jnp.zeros((8, M_WP), F32)
        c_scr[...] = jnp.zeros_like(c_scr)
        m_scr[...] = jnp.zeros_like(m_scr)

    qbuf[8:8 + tc, :] = q_ref[...].astype(F32)
    kbuf[8:8 + tc, :] = k_ref[...].astype(F32)

    gates = g_ref[...] + gb_ref[...]
    row = lax.broadcasted_iota(jnp.int32, (tc, tc), 0)
    col = lax.broadcasted_iota(jnp.int32, (tc, tc), 1)
    eye = row == col
    causal = col <= row
    lane = lax.broadcasted_iota(jnp.int32, (tc, M_HDP), 1)
    real = lane < M_HD

    def conv(buf, w_ref, sl):
        acc = w_ref[0:1, sl] * buf[5:5 + tc, sl]
        for j in range(1, M_CONV):
            acc = acc + w_ref[j:j + 1, sl] * buf[5 + j:5 + j + tc, sl]
        return acc

    def to_row(c):
        return jnp.sum(jnp.where(eye, c, 0.0), axis=0, keepdims=True)

    for h in range(M_HEADS):
        sl = slice(h * M_HDP, (h + 1) * M_HDP)
        q = _silu(conv(qbuf, cq_ref, sl))
        k = _silu(conv(kbuf, ck_ref, sl)) * (M_HD ** -0.5)
        qb = q.astype(BF16)
        vext = jnp.where(lane == M_HD, 1.0, v_ref[:, sl].astype(F32)).astype(BF16)

        i_col = gates[:, h:h + 1]
        logf_col = _log_sigmoid(gates[:, M_HEADS + h:M_HEADS + h + 1])
        i_row = to_row(i_col)
        logf_row = to_row(logf_col)
        bcum_row = jnp.sum(jnp.where(row <= col, logf_col, 0.0), axis=0, keepdims=True)
        bcum_col = jnp.sum(jnp.where(causal, logf_row, 0.0), axis=1, keepdims=True)
        m_prev = m_scr[h, 0:1, 0:1]

        dmat = jnp.where(causal, bcum_col - bcum_row + i_row, NEG)
        inter = bcum_col + m_prev
        m_t = jnp.maximum(inter, jnp.max(dmat, axis=1, keepdims=True))
        w_intra = jnp.exp(dmat - m_t)
        w_inter = jnp.exp(inter - m_t)
        s = _dot_nt(qb, k.astype(BF16)) * w_intra
        c_mat = c_scr[h]
        num = _dot(s.astype(BF16), vext) + w_inter * _dot(qb, c_mat.astype(BF16))
        den = num[:, M_HD:M_HD + 1]
        out = num / jnp.maximum(jnp.abs(den), jnp.exp(-m_t))

        b_end = bcum_col[tc - 1:tc, :]
        to_end = b_end - bcum_col + i_col
        m_new = jnp.maximum(b_end + m_prev, jnp.max(to_end, axis=0, keepdims=True))
        wk = jnp.exp(to_end - m_new)
        w_prev = jnp.exp(b_end + m_prev - m_new)
        c_scr[h] = w_prev * c_mat + _dot_tn((wk * k).astype(BF16), vext)
        m_scr[h] = jnp.broadcast_to(m_new, (8, LANES))

        outm = jnp.where(real, out, 0.0)
        mean = jnp.sum(outm, axis=1, keepdims=True) * (1.0 / M_HD)
        xc = jnp.where(real, out - mean, 0.0)
        var = jnp.sum(xc * xc, axis=1, keepdims=True) * (1.0 / M_HD)
        ha = xc * lax.rsqrt(var + EPS) * nw_ref[:, sl]
        og = jax.nn.sigmoid(o_ref[:, sl].astype(F32))
        out_ref[:, sl] = (og * ha * _silu(z_ref[:, sl].astype(F32))).astype(BF16)

    qbuf[0:8, :] = qbuf[tc:tc + 8, :]
    kbuf[0:8, :] = kbuf[tc:tc + 8, :]


def _mlstm(p16, gates, gb, cq, ck, nw, batch, seq, tc):
    n = batch * seq
    nt = seq // tc
    blk = P_MQ // M_WP

    def col(c):
        return pl.BlockSpec((tc, M_WP), lambda b, t, c=c: (b * nt + t, blk + c))

    def full(shape):
        return pl.BlockSpec(shape, lambda b, t: (0, 0))

    return pl.pallas_call(
        functools.partial(_mlstm_kernel, tc=tc),
        grid=(batch, nt),
        in_specs=[col(0), col(1), col(2), col(3), col(4),
                  pl.BlockSpec((tc, LANES), lambda b, t: (b * nt + t, 0)),
                  full((1, LANES)), full((M_CONV, M_WP)), full((M_CONV, M_WP)), full((1, M_WP))],
        out_specs=pl.BlockSpec((tc, M_WP), lambda b, t: (b * nt + t, 0)),
        out_shape=jax.ShapeDtypeStruct((n, M_WP), BF16),
        scratch_shapes=[pltpu.VMEM((tc + 8, M_WP), F32), pltpu.VMEM((tc + 8, M_WP), F32),
                        pltpu.VMEM((M_HEADS, M_HDP, M_HDP), F32),
                        pltpu.VMEM((M_HEADS, 8, LANES), F32)],
        compiler_params=_params("arbitrary", "arbitrary"),
        name="mlstm",
    )(p16, p16, p16, p16, p16, gates, gb, cq, ck, nw)


def _gla_kernel(q_ref, k_ref, v_ref, z_ref, g_ref, wal_ref, bal_ref, nw_ref,
                out_ref, kbuf, bbuf, st_scr, *, t, cs):
    @pl.when(pl.program_id(1) == 0)
    def _():
        st_scr[...] = jnp.zeros_like(st_scr)

    kbuf[0:cs, :] = jnp.zeros((cs, G_DKP), F32)
    bbuf[0:cs, :] = jnp.zeros((cs, G_DKP), F32)

    za = _dot(g_ref[...].astype(BF16), wal_ref[...]) + bal_ref[...]
    log_a = _log_sigmoid(za) * (1.0 / G_TAU)
    row = lax.broadcasted_iota(jnp.int32, (t, t), 0)
    col = lax.broadcasted_iota(jnp.int32, (t, t), 1)
    tril = jnp.where(col <= row, 1.0, 0.0).astype(F32)
    bcum = jnp.dot(tril, log_a, precision=HIGHEST, preferred_element_type=F32)
    rowv = lax.broadcasted_iota(jnp.int32, (t, G_DKP), 0)
    sub = rowv % cs
    lag = row - col

    for h in range(G_HEADS):
        sl = slice(h * G_DKP, (h + 1) * G_DKP)
        slv = slice(h * G_DVP, (h + 1) * G_DVP)
        qs = q_ref[:, sl].astype(F32) * (G_DK ** -0.5)
        kk = k_ref[:, sl].astype(F32)
        v = v_ref[:, slv]
        b = bcum[:, sl]
        blast = b[t - 1:t, :]
        st = st_scr[h]

        o = _dot_nt((qs * jnp.exp(b)).astype(BF16), st.astype(BF16))

        qparts, kparts = [], []
        for i in range(1, t // cs):
            r = b[i * cs - 1:i * cs, :]
            inblk = (rowv >= i * cs) & (rowv < (i + 1) * cs)
            qparts.append((qs * jnp.exp(jnp.where(inblk, b - r, NEG))).astype(BF16))
            kparts.append((kk * jnp.exp(jnp.where(rowv < i * cs, r - b, NEG))).astype(BF16))
        attn = _dot_nt(jnp.concatenate(qparts, axis=1), jnp.concatenate(kparts, axis=1))

        kbuf[cs:cs + t, :] = kk
        bbuf[cs:cs + t, :] = b
        for j in range(cs):
            ksh = kbuf[cs - j:cs - j + t, :]
            bsh = bbuf[cs - j:cs - j + t, :]
            e = jnp.exp(jnp.where(sub >= j, b - bsh, NEG))
            dj = jnp.sum(qs * ksh * e, axis=1, keepdims=True)
            attn = attn + jnp.where(lag == j, dj, 0.0)

        o = o + _dot(attn.astype(BF16), v)
        kdec = (kk * jnp.exp(blast - b)).astype(BF16)
        st_scr[h] = jnp.exp(blast) * st + _dot_tn(v, kdec)

        var = jnp.sum(o * o, axis=1, keepdims=True) * (1.0 / G_DV)
        hb = o * lax.rsqrt(var + EPS) * nw_ref[:, slv]
        out_ref[:, slv] = (hb * _silu(z_ref[:, slv].astype(F32))).astype(BF16)


def _gla(p16, gates, wal, bal, nw, batch, seq, t, cs):
    n = batch * seq
    nt = seq // t

    def col(width, offset):
        return pl.BlockSpec((t, width), lambda b, c: (b * nt + c, offset // width))

    def full(shape):
        return pl.BlockSpec(shape, lambda b, c: (0, 0))

    return pl.pallas_call(
        functools.partial(_gla_kernel, t=t, cs=cs),
        grid=(batch, nt),
        in_specs=[col(G_KWP, P_GQ), col(G_KWP, P_GQ + G_KWP), col(G_VWP, P_GV), col(G_VWP, P_GV + G_VWP),
                  pl.BlockSpec((t, LANES), lambda b, c: (b * nt + c, 0)),
                  full((LANES, G_KWP)), full((1, G_KWP)), full((1, G_VWP))],
        out_specs=pl.BlockSpec((t, G_VWP), lambda b, c: (b * nt + c, 0)),
        out_shape=jax.ShapeDtypeStruct((n, G_VWP), BF16),
        scratch_shapes=[pltpu.VMEM((t + cs, G_DKP), F32), pltpu.VMEM((t + cs, G_DKP), F32),
                        pltpu.VMEM((G_HEADS, G_DVP, G_DKP), F32)],
        compiler_params=_params("arbitrary", "arbitrary"),
        name="gla",
    )(p16, p16, p16, p16, gates, wal, bal, nw)


def _gelu_tanh(x):
    return 0.5 * x * (1.0 + jnp.tanh(math.sqrt(2.0 / math.pi) * (x + 0.044715 * (x * x * x))))


def _s5_kernel(u_ref, kms_ref, mo_ref, p_ref, d_ref, y_ref, carry, *, nch):
    @pl.when(pl.program_id(2) == 0)
    def _():
        carry[...] = jnp.zeros_like(carry)

    qw = S_Q * LANES
    ucat = jnp.concatenate([u_ref[0, pl.ds(s, nch, stride=S_Q), :] for s in range(S_Q)],
                           axis=1).astype(BF16)
    r = _dot(ucat, kms_ref[0])
    y = r[:, :qw]
    xr = r[:, qw:qw + S_SW]
    xi = r[:, qw + S_SW:]

    rowi = lax.broadcasted_iota(jnp.int32, (nch, S_SW), 0)
    d = 1
    while d < nch:
        pr = p_ref[0, d - 1:d, :S_SW]
        pi = p_ref[0, d - 1:d, S_SW:]
        keep = rowi >= d
        sr = jnp.where(keep, pltpu.roll(xr, d, 0), 0.0)
        si = jnp.where(keep, pltpu.roll(xi, d, 0), 0.0)
        xr, xi = xr + (pr * sr - pi * si), xi + (pr * si + pi * sr)
        d *= 2
    cr = carry[0:1, :S_SW]
    ci = carry[0:1, S_SW:]
    pr = p_ref[0, :, :S_SW]
    pi = p_ref[0, :, S_SW:]
    xr, xi = xr + (pr * cr - pi * ci), xi + (pr * ci + pi * cr)
    first = rowi == 0
    xpr = jnp.where(first, cr, pltpu.roll(xr, 1, 0))
    xpi = jnp.where(first, ci, pltpu.roll(xi, 1, 0))
    carry[0:1, :S_SW] = xr[nch - 1:nch, :]
    carry[0:1, S_SW:] = xi[nch - 1:nch, :]

    y = y + _dot(jnp.concatenate([xpr, xpi], axis=1).astype(BF16), mo_ref[0])
    for t in range(S_Q):
        y_ref[0, pl.ds(t, nch, stride=S_Q), :] = y[:, t * LANES:(t + 1) * LANES]
    y_ref[0] = _gelu_tanh(y_ref[0] + d_ref[0] * u_ref[0])


def _s5(cu, kms, mo, ptab, dskip, batch, seq, tm):
    n = batch * seq
    nt = seq // tm
    nch = tm // S_Q
    qw = S_Q * LANES
    return pl.pallas_call(
        functools.partial(_s5_kernel, nch=nch),
        grid=(S_BLOCKS, batch, nt),
        in_specs=[pl.BlockSpec((1, tm, LANES), lambda j, b, t: (j, b * nt + t, 0)),
                  pl.BlockSpec((1, qw, qw + 2 * S_SW), lambda j, b, t: (j, 0, 0)),
                  pl.BlockSpec((1, 2 * S_SW, qw), lambda j, b, t: (j, 0, 0)),
                  pl.BlockSpec((1, nch, 2 * S_SW), lambda j, b, t: (j, 0, 0)),
                  pl.BlockSpec((1, 1, LANES), lambda j, b, t: (j, 0, 0))],
        out_specs=pl.BlockSpec((1, tm, LANES), lambda j, b, t: (j, b * nt + t, 0)),
        out_shape=jax.ShapeDtypeStruct((S_BLOCKS, n, LANES), F32),
        scratch_shapes=[pltpu.VMEM((8, 2 * S_SW), F32)],
        compiler_params=_params("arbitrary", "arbitrary", "arbitrary"),
        name="s5",
    )(cu, kms, mo, ptab, dskip)


def _s5_tables(lam_re, lam_im, log_dt, b_re, b_im, c_re, c_im, nch):
    q = S_Q
    lr = jnp.minimum(lam_re.astype(F32), -1e-4)
    li = lam_im.astype(F32)
    dt = jnp.exp(log_dt.astype(F32))[:, None]
    mag = jnp.exp(lr * dt)
    ab_re = mag * jnp.cos(li * dt)
    ab_im = mag * jnp.sin(li * dt)
    nr = ab_re - 1.0
    den = lr * lr + li * li
    coef_re = (nr * lr + ab_im * li) / den
    coef_im = (ab_im * lr - nr * li) / den
    br, bi = b_re.astype(F32), b_im.astype(F32)
    bb_re = coef_re[..., None] * br - coef_im[..., None] * bi
    bb_im = coef_re[..., None] * bi + coef_im[..., None] * br

    def apow(nvals):
        e = jnp.asarray(nvals, F32)[:, None, None]
        m = jnp.exp(e * (lr * dt))
        ang = e * (li * dt)
        return m * jnp.cos(ang), m * jnp.sin(ang)

    eye_g = jnp.eye(S_GPB, dtype=F32)
    ar, ai = apow(list(range(q + 1)))
    cr, ci = c_re.astype(F32), c_im.astype(F32)
    ca_re = cr[None] * ar[:, :, None, :] - ci[None] * ai[:, :, None, :]
    ca_im = cr[None] * ai[:, :, None, :] + ci[None] * ar[:, :, None, :]
    kt = (jnp.einsum('tghp,gpk->tgkh', ca_re[:q], bb_re, precision=HIGHEST)
          - jnp.einsum('tghp,gpk->tgkh', ca_im[:q], bb_im, precision=HIGHEST))
    kt = kt.reshape(q, S_BLOCKS, S_GPB, S_GROUP, S_GROUP)
    kd = jnp.einsum('tjgkh,ge->tjgkeh', kt, eye_g).reshape(q, S_BLOCKS, LANES, LANES)
    lagi = jnp.arange(q)[None, :] - jnp.arange(q)[:, None]
    kbig = jnp.where((lagi >= 0)[:, :, None, None, None],
                     kd[jnp.clip(lagi, 0, q - 1)], 0.0)
    kbig = kbig.transpose(2, 0, 3, 1, 4).reshape(S_BLOCKS, q * LANES, q * LANES)
    pw_r, pw_i = ar[:q][::-1], ai[:q][::-1]
    ms_re = pw_r[..., None] * bb_re[None] - pw_i[..., None] * bb_im[None]
    ms_im = pw_r[..., None] * bb_im[None] + pw_i[..., None] * bb_re[None]

    def blk_in(m):
        m = m.reshape(q, S_BLOCKS, S_GPB, S_STATE, S_GROUP)
        m = jnp.einsum('sjgpk,ge->jsgkep', m, eye_g)
        return m.reshape(S_BLOCKS, q * LANES, S_SW)

    kms = jnp.concatenate([kbig, blk_in(ms_re), blk_in(ms_im)], axis=2).astype(BF16)

    def blk_out(m):
        m = m.reshape(q, S_BLOCKS, S_GPB, S_GROUP, S_STATE)
        m = jnp.einsum('tjghp,ge->jgpteh', m, eye_g)
        return m.reshape(S_BLOCKS, S_SW, q * LANES)

    mo = jnp.concatenate([blk_out(ca_re[1:]), blk_out(-ca_im[1:])], axis=1).astype(BF16)
    pr, pi = apow([q * (c + 1) for c in range(nch)])
    pr = pr.reshape(nch, S_BLOCKS, S_SW).transpose(1, 0, 2)
    pi = pi.reshape(nch, S_BLOCKS, S_SW).transpose(1, 0, 2)
    ptab = jnp.concatenate([pr, pi], axis=2)
    return kms, mo, ptab


def _merge_kernel(ya_ref, yb_ref, yc_ref, cz_ref, g_ref, x_ref, wa_ref, wb_ref, wglu_ref, wc_ref,
                  wo_ref, fn_ref, o_ref, *, final):
    yc0 = jnp.concatenate([yc_ref[j] for j in range(S_BLOCKS)], axis=1)
    glu = _dot(yc0.astype(BF16), wglu_ref[...])
    yc = yc0 * jax.nn.sigmoid(glu) * _silu(cz_ref[...].astype(F32))
    merged = (jax.nn.sigmoid(g_ref[:, 0:D_MODEL].astype(F32)) * _dot(ya_ref[...], wa_ref[...])
              + jax.nn.sigmoid(g_ref[:, D_MODEL:2 * D_MODEL].astype(F32)) * _dot(yb_ref[...], wb_ref[...])
              + jax.nn.sigmoid(g_ref[:, 2 * D_MODEL:].astype(F32)) * _dot(yc.astype(BF16), wc_ref[...]))
    out = x_ref[...] + _dot(merged.astype(BF16), wo_ref[...])
    if final:
        ms = jnp.mean(out * out, axis=-1, keepdims=True)
        out = out * lax.rsqrt(ms + EPS) * fn_ref[...]
    o_ref[...] = out


def _merge(ya, yb, yc, p16, x2, wa, wb, wglu, wc, wo, fn, tm, final):
    n = x2.shape[0]

    def full(shape):
        return pl.BlockSpec(shape, lambda i: (0, 0))

    return pl.pallas_call(
        functools.partial(_merge_kernel, final=final),
        grid=(n // tm,),
        in_specs=[pl.BlockSpec((tm, M_WP), lambda i: (i, 0)),
                  pl.BlockSpec((tm, G_VWP), lambda i: (i, 0)),
                  pl.BlockSpec((S_BLOCKS, tm, LANES), lambda i: (0, i, 0)),
                  pl.BlockSpec((tm, S_W), lambda i: (i, P_CZ // S_W)),
                  pl.BlockSpec((tm, 3 * D_MODEL), lambda i: (i, 0)),
                  pl.BlockSpec((tm, D_MODEL), lambda i: (i, 0)),
                  full((M_WP, D_MODEL)), full((G_VWP, D_MODEL)), full((S_W, S_W)),
                  full((S_W, D_MODEL)), full((D_MODEL, D_MODEL)), full((1, D_MODEL))],
        out_specs=pl.BlockSpec((tm, D_MODEL), lambda i: (i, 0)),
        out_shape=jax.ShapeDtypeStruct((n, D_MODEL), F32),
        compiler_params=_params("arbitrary"),
        name="merge",
    )(ya, yb, yc, p16, p16, x2, wa, wb, wglu, wc, wo, fn)


def _pad_heads(w, heads, d, dp):
    lead = w.shape[:-1]
    w = w.reshape(lead + (heads, d))
    w = jnp.pad(w, [(0, 0)] * len(lead) + [(0, 0), (0, dp - d)])
    return w.reshape(lead + (heads * dp,))


def _pad_head_rows(w, heads, d, dp):
    return _pad_heads(w.T, heads, d, dp).T


def _layer(x2, batch, seq, norm_w, w_in, mlstm_conv, mlstm_gate_b, mlstm_norm, gla_w_alpha, gla_b_alpha,
           gla_norm, s5_lam_re, s5_lam_im, s5_log_dt, s5_B_re, s5_B_im, s5_C_re, s5_C_im,
           s5_D, s5_w_glu, w_branch_mlstm, w_branch_gla, w_branch_s5, w_out, final_norm, final):
    n = batch * seq
    offs = [0]
    for s in IN_SIZES:
        offs.append(offs[-1] + s)
    (aq, ak, av, ao, ai, af, az, bq, bk, bv, ba, bz, cu, cz, g) = [
        w_in[:, offs[i]:offs[i + 1]] for i in range(len(IN_SIZES))]
    mp = functools.partial(_pad_heads, heads=M_HEADS, d=M_HD, dp=M_HDP)
    gkp = functools.partial(_pad_heads, heads=G_HEADS, d=G_DK, dp=G_DKP)
    gvp = functools.partial(_pad_heads, heads=G_HEADS, d=G_DV, dp=G_DVP)
    w_big = jnp.concatenate([g, mp(aq), mp(ak), mp(av), mp(ao), mp(az),
                             gkp(bq), gkp(bk), gvp(bv), gvp(bz), cz], axis=1).astype(BF16)
    w_small = jnp.concatenate([cu, ai, af, ba,
                               jnp.zeros((D_MODEL, LANES - 2 * M_HEADS - G_RANK), F32)], axis=1).astype(BF16)
    nw = norm_w.reshape(1, D_MODEL)

    tm = min(1024, n)
    p16 = _proj(x2, nw, w_big, tm, P_WIDTH // 4)
    cu_s, gates = _proj_small(x2, nw, w_small, tm)

    gb = jnp.zeros((1, LANES), F32).at[0, 0:M_HEADS].set(mlstm_gate_b[0]).at[0, M_HEADS:2 * M_HEADS].set(
        mlstm_gate_b[1])
    cq = mp(mlstm_conv[:, :M_W])
    ck = mp(mlstm_conv[:, M_W:])
    ya = _mlstm(p16, gates, gb, cq, ck, mp(mlstm_norm).reshape(1, M_WP), batch, seq, min(256, seq))

    wal = jnp.zeros((LANES, G_KWP), F32).at[2 * M_HEADS:2 * M_HEADS + G_RANK].set(gkp(gla_w_alpha)).astype(BF16)
    bal = gkp(gla_b_alpha).reshape(1, G_KWP)
    yb = _gla(p16, gates, wal, bal, gvp(gla_norm).reshape(1, G_VWP), batch, seq, 64, 16)

    tms = min(1024, seq)
    kms, mo, ptab = _s5_tables(s5_lam_re, s5_lam_im, s5_log_dt, s5_B_re, s5_B_im, s5_C_re, s5_C_im,
                               tms // S_Q)
    yc = _s5(cu_s, kms, mo, ptab, s5_D.reshape(S_BLOCKS, 1, LANES), batch, seq, tms)

    wa = _pad_head_rows(w_branch_mlstm, M_HEADS, M_HD, M_HDP).astype(BF16)
    wb = _pad_head_rows(w_branch_gla, G_HEADS, G_DV, G_DVP).astype(BF16)
    return _merge(ya, yb, yc, p16, x2, wa, wb, s5_w_glu.astype(BF16), w_branch_s5.astype(BF16),
                  w_out.astype(BF16), final_norm.reshape(1, D_MODEL), min(512, n), final)


def kernel(x, norm_w, w_in, mlstm_conv, mlstm_gate_b, mlstm_norm, gla_w_alpha, gla_b_alpha, gla_norm,
           s5_lam_re, s5_lam_im, s5_log_dt, s5_B_re, s5_B_im, s5_C_re, s5_C_im, s5_D, s5_w_glu,
           w_branch_mlstm, w_branch_gla, w_branch_s5, w_out, final_norm):
    batch, seq, _ = x.shape
    depth = norm_w.shape[0]
    x2 = x.reshape(batch * seq, D_MODEL)
    for l in range(depth):
        x2 = _layer(x2, batch, seq, norm_w[l], w_in[l], mlstm_conv[l], mlstm_gate_b[l], mlstm_norm[l],
                    gla_w_alpha[l], gla_b_alpha[l], gla_norm[l], s5_lam_re[l], s5_lam_im[l],
                    s5_log_dt[l], s5_B_re[l], s5_B_im[l], s5_C_re[l], s5_C_im[l], s5_D[l],
                    s5_w_glu[l], w_branch_mlstm[l], w_branch_gla[l], w_branch_s5[l], w_out[l],
                    final_norm, l == depth - 1)
    return x2.reshape(batch, seq, D_MODEL)
```

```python
import functools
import math

import jax
import jax.numpy as jnp
from jax import lax
from jax.experimental import pallas as pl
from jax.experimental.pallas import tpu as pltpu

F32 = jnp.float32
BF16 = jnp.bfloat16
HIGHEST = lax.Precision.HIGHEST

EPS = 1e-6
D_MODEL = 1024
LANES = 128
M_HEADS = 4
M_HD = 192
M_HDP = 256
M_W = M_HEADS * M_HD
M_WP = M_HEADS * M_HDP
M_CONV = 4
G_HEADS = 4
G_DK = 96
G_DKP = 128
G_DV = 192
G_DVP = 256
G_KW = G_HEADS * G_DK
G_KWP = G_HEADS * G_DKP
G_VW = G_HEADS * G_DV
G_VWP = G_HEADS * G_DVP
G_RANK = 16
G_TAU = 16.0
S_W = 512
S_GROUP = 16
S_GROUPS = 32
S_STATE = 64
S_BLOCKS = S_W // LANES
S_GPB = LANES // S_GROUP
S_SW = S_GPB * S_STATE
S_Q = 8

IN_SIZES = (M_W, M_W, M_W, M_W, M_HEADS, M_HEADS, M_W,
            G_KW, G_KW, G_VW, G_RANK, G_VW,
            S_W, S_W, 3 * D_MODEL)
SMALL_W = S_W + LANES
CW = 512

NEG = -1e30
G_SAFE_LOG_DECAY = -60.0
VMEM_LIMIT = 56 * 1024 * 1024


def _sigmoid(x):
    return 0.5 * jnp.tanh(0.5 * x) + 0.5


def _silu(x):
    h = 0.5 * x
    return h * jnp.tanh(h) + h


def _log_sigmoid(x):
    return jnp.minimum(x, 0.0) - jnp.log1p(jnp.exp(-jnp.abs(x)))


def _dot(a, b):
    return jnp.dot(a, b, preferred_element_type=F32)


def _dot_nt(a, b):
    return lax.dot_general(a, b, (((1,), (1,)), ((), ())), preferred_element_type=F32)


def _dot_tn(a, b):
    return lax.dot_general(a, b, (((0,), (0,)), ((), ())), preferred_element_type=F32)


def _params(*sem):
    return pltpu.CompilerParams(dimension_semantics=sem, vmem_limit_bytes=VMEM_LIMIT)


def _rows(tm, width):
    return pl.BlockSpec((tm, width), lambda i: (i, 0))


def _whole(shape):
    return pl.BlockSpec(shape, lambda i: (0,) * len(shape))


def _rms_kernel(x_ref, nw_ref, o_ref):
    x = x_ref[...]
    ms = jnp.mean(x * x, axis=-1, keepdims=True)
    o_ref[...] = (x * lax.rsqrt(ms + EPS) * nw_ref[...]).astype(BF16)


def _rms(x2, nw, tm):
    n = x2.shape[0]
    return pl.pallas_call(
        _rms_kernel, grid=(n // tm,),
        in_specs=[_rows(tm, D_MODEL), _whole((1, D_MODEL))],
        out_specs=_rows(tm, D_MODEL),
        out_shape=jax.ShapeDtypeStruct((n, D_MODEL), BF16),
        compiler_params=_params("arbitrary"), name="rms",
    )(x2, nw)


def _proj_mlstm_kernel(hn_ref, w_ref, cq_ref, ck_ref, q_ref, k_ref, oz_ref, halo_x, halo_b, *, tm,
                       tiles_per_seq):
    @pl.when(pl.program_id(0) % tiles_per_seq == 0)
    def _():
        halo_x[...] = jnp.zeros_like(halo_x)
        halo_b[...] = jnp.zeros_like(halo_b)

    hn = hn_ref[...]
    row8 = lax.broadcasted_iota(jnp.int32, (8, CW), 0)

    def shift(x, prev8, d):
        rolled = pltpu.roll(x, d, 0)
        head = jnp.where(row8 < d, pltpu.roll(prev8, d, 0), rolled[0:8])
        return jnp.concatenate([head, rolled[8:]], axis=0)

    nchunk = M_WP // CW
    for c in range(2 * nchunk):
        cw_ref, dst, scale = (cq_ref, q_ref, 1.0) if c < nchunk else (ck_ref, k_ref, M_HD ** -0.5)
        lc = slice((c % nchunk) * CW, (c % nchunk + 1) * CW)
        x = _dot(hn, w_ref[:, c * CW:(c + 1) * CW])
        x1 = shift(x, halo_x[c], 1)
        pair = cw_ref[1:2, lc] * x + cw_ref[0:1, lc] * x1
        y = _silu(cw_ref[3:4, lc] * x + cw_ref[2:3, lc] * x1 + shift(pair, halo_b[c], 2))
        halo_x[c] = x[tm - 8:tm]
        halo_b[c] = pair[tm - 8:tm]
        dst[:, lc] = (y if scale == 1.0 else y * scale).astype(BF16)
    for c in range(nchunk):
        lc = slice(c * CW, (c + 1) * CW)
        o = _dot(hn, w_ref[:, 2 * M_WP + c * CW:2 * M_WP + (c + 1) * CW])
        z = _dot(hn, w_ref[:, 3 * M_WP + c * CW:3 * M_WP + (c + 1) * CW])
        oz_ref[:, lc] = (_sigmoid(o) * _silu(z)).astype(BF16)


def _proj_mlstm(hn, w, cq, ck, seq, tm):
    n = hn.shape[0]
    out = jax.ShapeDtypeStruct((n, M_WP), BF16)
    return pl.pallas_call(
        functools.partial(_proj_mlstm_kernel, tm=tm, tiles_per_seq=seq // tm),
        grid=(n // tm,),
        in_specs=[_rows(tm, D_MODEL), _whole((D_MODEL, 4 * M_WP)), _whole((M_CONV, M_WP)),
                  _whole((M_CONV, M_WP))],
        out_specs=[_rows(tm, M_WP)] * 3,
        out_shape=[out] * 3,
        scratch_shapes=[pltpu.VMEM((2 * M_WP // CW, 8, CW), F32)] * 2,
        compiler_params=_params("arbitrary"), name="proj_mlstm",
    )(hn, w, cq, ck)


def _proj_plain_kernel(hn_ref, w_ref, mv_ref, gv_ref, gq_ref, gk_ref):
    hn = hn_ref[...]
    lane = lax.broadcasted_iota(jnp.int32, (hn.shape[0], CW), 1)
    ones_lane = lane % M_HDP == M_HD
    off = 0
    for dst, width, kind in ((mv_ref, M_WP, "ones"), (gv_ref, G_VWP, None), (gq_ref, G_KWP, "scale"),
                             (gk_ref, G_KWP, None)):
        for c in range(width // CW):
            r = _dot(hn, w_ref[:, off + c * CW:off + (c + 1) * CW])
            if kind == "ones":
                r = jnp.where(ones_lane, 1.0, r)
            elif kind == "scale":
                r = r * (G_DK ** -0.5)
            dst[:, c * CW:(c + 1) * CW] = r.astype(BF16)
        off += width


def _proj_plain(hn, w, tm):
    n = hn.shape[0]
    widths = (M_WP, G_VWP, G_KWP, G_KWP)
    return pl.pallas_call(
        _proj_plain_kernel, grid=(n // tm,),
        in_specs=[_rows(tm, D_MODEL), _whole((D_MODEL, sum(widths)))],
        out_specs=[_rows(tm, wd) for wd in widths],
        out_shape=[jax.ShapeDtypeStruct((n, wd), BF16) for wd in widths],
        compiler_params=_params("arbitrary"), name="proj_plain",
    )(hn, w)


def _proj_gate_kernel(hn_ref, w_ref, gz_ref, cz_ref, g_ref):
    hn = hn_ref[...]
    off = 0
    for dst, width, act in ((gz_ref, G_VWP, _silu), (cz_ref, S_W, _silu), (g_ref, 3 * D_MODEL, _sigmoid)):
        for c in range(width // CW):
            r = _dot(hn, w_ref[:, off + c * CW:off + (c + 1) * CW])
            dst[:, c * CW:(c + 1) * CW] = act(r).astype(BF16)
        off += width


def _proj_gate(hn, w, tm):
    n = hn.shape[0]
    widths = (G_VWP, S_W, 3 * D_MODEL)
    return pl.pallas_call(
        _proj_gate_kernel, grid=(n // tm,),
        in_specs=[_rows(tm, D_MODEL), _whole((D_MODEL, sum(widths)))],
        out_specs=[_rows(tm, wd) for wd in widths],
        out_shape=[jax.ShapeDtypeStruct((n, wd), BF16) for wd in widths],
        compiler_params=_params("arbitrary"), name="proj_gate",
    )(hn, w)


def _proj_small_kernel(hn_ref, w_ref, cu_ref, g_ref):
    r = _dot(hn_ref[...], w_ref[...])
    for j in range(S_BLOCKS):
        cu_ref[j] = r[:, j * LANES:(j + 1) * LANES]
    g_ref[...] = r[:, S_W:]


def _proj_small(hn, w, tm):
    n = hn.shape[0]
    return pl.pallas_call(
        _proj_small_kernel, grid=(n // tm,),
        in_specs=[_rows(tm, D_MODEL), _whole((D_MODEL, SMALL_W))],
        out_specs=[pl.BlockSpec((S_BLOCKS, tm, LANES), lambda i: (0, i, 0)), _rows(tm, LANES)],
        out_shape=[jax.ShapeDtypeStruct((S_BLOCKS, n, LANES), F32),
                   jax.ShapeDtypeStruct((n, LANES), F32)],
        compiler_params=_params("arbitrary"), name="proj_small",
    )(hn, w)


def _mlstm_kernel(q_ref, k_ref, v_ref, oz_ref, g_ref, gb_ref, nw_ref, out_ref, c_scr, m_scr, *, tc):
    @pl.when(pl.program_id(1) == 0)
    def _():
        c_scr[...] = jnp.zeros_like(c_scr)
        m_scr[...] = jnp.zeros_like(m_scr)

    gates = g_ref[...] + gb_ref[...]
    row = lax.broadcasted_iota(jnp.int32, (tc, tc), 0)
    col = lax.broadcasted_iota(jnp.int32, (tc, tc), 1)
    causal = col <= row
    tril = jnp.where(causal, 1.0, 0.0).astype(F32)
    bcum = jnp.dot(tril, _log_sigmoid(gates), precision=HIGHEST, preferred_element_type=F32)
    gates_t = gates.T
    bcum_t = bcum.T
    real = lax.broadcasted_iota(jnp.int32, (tc, M_HDP), 1) < M_HD

    for h in range(M_HEADS):
        sl = slice(h * M_HDP, (h + 1) * M_HDP)
        qb = q_ref[:, sl]
        kb = k_ref[:, sl]
        vext = v_ref[:, sl]
        i_col = gates[:, h:h + 1]
        i_row = gates_t[h:h + 1, :]
        bcum_col = bcum[:, M_HEADS + h:M_HEADS + h + 1]
        bcum_row = bcum_t[M_HEADS + h:M_HEADS + h + 1, :]
        m_prev = m_scr[h, 0:1, 0:1]

        dmat = jnp.where(causal, bcum_col - bcum_row + i_row, NEG)
        inter = bcum_col + m_prev
        m_t = jnp.maximum(inter, jnp.max(dmat, axis=1, keepdims=True))
        w_intra = jnp.exp(dmat - m_t)
        w_inter = jnp.exp(inter - m_t)
        s = _dot_nt(qb, kb) * w_intra
        c_mat = c_scr[h]
        num = _dot(s.astype(BF16), vext) + w_inter * _dot(qb, c_mat.astype(BF16))
        den = num[:, M_HD:M_HD + 1]
        out = num / jnp.maximum(jnp.abs(den), jnp.exp(-m_t))

        b_end = bcum_col[tc - 1:tc, :]
        to_end = b_end - bcum_col + i_col
        m_new = jnp.maximum(b_end + m_prev, jnp.max(to_end, axis=0, keepdims=True))
        wk = jnp.exp(to_end - m_new)
        w_prev = jnp.exp(b_end + m_prev - m_new)
        c_scr[h] = w_prev * c_mat + _dot_tn((wk * kb.astype(F32)).astype(BF16), vext)
        m_scr[h] = jnp.broadcast_to(m_new, (8, LANES))

        outm = jnp.where(real, out, 0.0)
        mean = jnp.sum(outm, axis=1, keepdims=True) * (1.0 / M_HD)
        xc = jnp.where(real, out - mean, 0.0)
        var = jnp.sum(xc * xc, axis=1, keepdims=True) * (1.0 / M_HD)
        ha = xc * lax.rsqrt(var + EPS) * nw_ref[:, sl]
        out_ref[:, sl] = (ha * oz_ref[:, sl].astype(F32)).astype(BF16)


def _mlstm(q, k, v, oz, gates, gb, nw, batch, seq, tc):
    n = batch * seq
    nt = seq // tc
    blk = pl.BlockSpec((tc, M_WP), lambda b, t: (b * nt + t, 0))

    def full(shape):
        return pl.BlockSpec(shape, lambda b, t: (0, 0))

    return pl.pallas_call(
        functools.partial(_mlstm_kernel, tc=tc),
        grid=(batch, nt),
        in_specs=[blk, blk, blk, blk, pl.BlockSpec((tc, LANES), lambda b, t: (b * nt + t, 0)),
                  full((1, LANES)), full((1, M_WP))],
        out_specs=blk,
        out_shape=jax.ShapeDtypeStruct((n, M_WP), BF16),
        scratch_shapes=[pltpu.VMEM((M_HEADS, M_HDP, M_HDP), F32),
                        pltpu.VMEM((M_HEADS, 8, LANES), F32)],
        compiler_params=_params("arbitrary", "arbitrary"),
        name="mlstm",
    )(q, k, v, oz, gates, gb, nw)


def _gla_kernel(q_ref, k_ref, v_ref, z_ref, g_ref, wal_ref, bal_ref, nw_ref,
                out_ref, kbuf, bbuf, attn_scr, st_scr, *, t, cs):
    @pl.when(pl.program_id(1) == 0)
    def _():
        st_scr[...] = jnp.zeros_like(st_scr)

    za = _dot(g_ref[...].astype(BF16), wal_ref[...]) + bal_ref[...]
    log_a = _log_sigmoid(za) * (1.0 / G_TAU)
    row = lax.broadcasted_iota(jnp.int32, (t, t), 0)
    col = lax.broadcasted_iota(jnp.int32, (t, t), 1)
    causal = col <= row
    tril = jnp.where(causal, 1.0, 0.0).astype(F32)
    bcum = jnp.dot(tril, log_a, precision=HIGHEST, preferred_element_type=F32)

    def operands(h):
        sl = slice(h * G_DKP, (h + 1) * G_DKP)
        return q_ref[:, sl].astype(F32), k_ref[:, sl].astype(F32), bcum[:, sl]

    safe = jnp.min(bcum[t - 1:t, :]) >= G_SAFE_LOG_DECAY

    @pl.when(safe)
    def _():
        for h in range(G_HEADS):
            qs, kk, b = operands(h)
            a = _dot_nt((qs * jnp.exp(b)).astype(BF16), (kk * jnp.exp(-b)).astype(BF16))
            attn_scr[h] = jnp.where(causal, a, 0.0)

    @pl.when(jnp.logical_not(safe))
    def _():
        rowv = lax.broadcasted_iota(jnp.int32, (t, G_DKP), 0)
        sub = rowv % cs
        lag = row - col
        kbuf[0:cs, :] = jnp.zeros((cs, G_DKP), F32)
        bbuf[0:cs, :] = jnp.zeros((cs, G_DKP), F32)
        for h in range(G_HEADS):
            qs, kk, b = operands(h)
            qparts, kparts = [], []
            for i in range(1, t // cs):
                r = b[i * cs - 1:i * cs, :]
                inblk = (rowv >= i * cs) & (rowv < (i + 1) * cs)
                qparts.append((qs * jnp.exp(jnp.where(inblk, b - r, NEG))).astype(BF16))
                kparts.append((kk * jnp.exp(jnp.where(rowv < i * cs, r - b, NEG))).astype(BF16))
            attn = _dot_nt(jnp.concatenate(qparts, axis=1), jnp.concatenate(kparts, axis=1))
            kbuf[cs:cs + t, :] = kk
            bbuf[cs:cs + t, :] = b
            for j in range(cs):
                ksh = kbuf[cs - j:cs - j + t, :]
                bsh = bbuf[cs - j:cs - j + t, :]
                e = jnp.exp(jnp.where(sub >= j, b - bsh, NEG))
                dj = jnp.sum(qs * ksh * e, axis=1, keepdims=True)
                attn = attn + jnp.where(lag == j, dj, 0.0)
            attn_scr[h] = attn

    for h in range(G_HEADS):
        slv = slice(h * G_DVP, (h + 1) * G_DVP)
        qs, kk, b = operands(h)
        v = v_ref[:, slv]
        blast = b[t - 1:t, :]
        st = st_scr[h]
        o = (_dot_nt((qs * jnp.exp(b)).astype(BF16), st.astype(BF16))
             + _dot(attn_scr[h].astype(BF16), v))
        kdec = (kk * jnp.exp(blast - b)).astype(BF16)
        st_scr[h] = jnp.exp(blast) * st + _dot_tn(v, kdec)

        var = jnp.sum(o * o, axis=1, keepdims=True) * (1.0 / G_DV)
        hb = o * lax.rsqrt(var + EPS) * nw_ref[:, slv]
        out_ref[:, slv] = (hb * z_ref[:, slv].astype(F32)).astype(BF16)


def _gla(q, k, v, z, gates, wal, bal, nw, batch, seq, t, cs):
    n = batch * seq
    nt = seq // t

    def blk(width):
        return pl.BlockSpec((t, width), lambda b, c: (b * nt + c, 0))

    def full(shape):
        return pl.BlockSpec(shape, lambda b, c: (0, 0))

    return pl.pallas_call(
        functools.partial(_gla_kernel, t=t, cs=cs),
        grid=(batch, nt),
        in_specs=[blk(G_KWP), blk(G_KWP), blk(G_VWP), blk(G_VWP), blk(LANES),
                  full((LANES, G_KWP)), full((1, G_KWP)), full((1, G_VWP))],
        out_specs=blk(G_VWP),
        out_shape=jax.ShapeDtypeStruct((n, G_VWP), BF16),
        scratch_shapes=[pltpu.VMEM((t + cs, G_DKP), F32), pltpu.VMEM((t + cs, G_DKP), F32),
                        pltpu.VMEM((G_HEADS, t, t), F32),
                        pltpu.VMEM((G_HEADS, G_DVP, G_DKP), F32)],
        compiler_params=_params("arbitrary", "arbitrary"),
        name="gla",
    )(q, k, v, z, gates, wal, bal, nw)


def _gelu_tanh(x):
    return 0.5 * x * (1.0 + jnp.tanh(math.sqrt(2.0 / math.pi) * (x + 0.044715 * (x * x * x))))


def _s5_kernel(u_ref, kms_ref, mo_ref, p_ref, d_ref, y_ref, carry, *, nch):
    @pl.when(pl.program_id(2) == 0)
    def _():
        carry[...] = jnp.zeros_like(carry)

    qw = S_Q * LANES
    ucat = jnp.concatenate([u_ref[0, pl.ds(s, nch, stride=S_Q), :] for s in range(S_Q)],
                           axis=1).astype(BF16)
    r = _dot(ucat, kms_ref[0])
    y = r[:, :qw]
    xr = r[:, qw:qw + S_SW]
    xi = r[:, qw + S_SW:]

    rowi = lax.broadcasted_iota(jnp.int32, (nch, S_SW), 0)
    d = 1
    while d < nch:
        pr = p_ref[0, d - 1:d, :S_SW]
        pi = p_ref[0, d - 1:d, S_SW:]
        keep = rowi >= d
        sr = jnp.where(keep, pltpu.roll(xr, d, 0), 0.0)
        si = jnp.where(keep, pltpu.roll(xi, d, 0), 0.0)
        xr, xi = xr + (pr * sr - pi * si), xi + (pr * si + pi * sr)
        d *= 2
    cr = carry[0:1, :S_SW]
    ci = carry[0:1, S_SW:]
    pr = p_ref[0, :, :S_SW]
    pi = p_ref[0, :, S_SW:]
    xr, xi = xr + (pr * cr - pi * ci), xi + (pr * ci + pi * cr)
    first = rowi == 0
    xpr = jnp.where(first, cr, pltpu.roll(xr, 1, 0))
    xpi = jnp.where(first, ci, pltpu.roll(xi, 1, 0))
    carry[0:1, :S_SW] = xr[nch - 1:nch, :]
    carry[0:1, S_SW:] = xi[nch - 1:nch, :]

    y = y + _dot(jnp.concatenate([xpr, xpi], axis=1).astype(BF16), mo_ref[0])
    for t in range(S_Q):
        y_ref[0, pl.ds(t, nch, stride=S_Q), :] = y[:, t * LANES:(t + 1) * LANES]
    y_ref[0] = _gelu_tanh(y_ref[0] + d_ref[0] * u_ref[0])


def _s5(cu, kms, mo, ptab, dskip, batch, seq, tm):
    n = batch * seq
    nt = seq // tm
    nch = tm // S_Q
    qw = S_Q * LANES
    return pl.pallas_call(
        functools.partial(_s5_kernel, nch=nch),
        grid=(S_BLOCKS, batch, nt),
        in_specs=[pl.BlockSpec((1, tm, LANES), lambda j, b, t: (j, b * nt + t, 0)),
                  pl.BlockSpec((1, qw, qw + 2 * S_SW), lambda j, b, t: (j, 0, 0)),
                  pl.BlockSpec((1, 2 * S_SW, qw), lambda j, b, t: (j, 0, 0)),
                  pl.BlockSpec((1, nch, 2 * S_SW), lambda j, b, t: (j, 0, 0)),
                  pl.BlockSpec((1, 1, LANES), lambda j, b, t: (j, 0, 0))],
        out_specs=pl.BlockSpec((1, tm, LANES), lambda j, b, t: (j, b * nt + t, 0)),
        out_shape=jax.ShapeDtypeStruct((S_BLOCKS, n, LANES), F32),
        scratch_shapes=[pltpu.VMEM((8, 2 * S_SW), F32)],
        compiler_params=_params("arbitrary", "arbitrary", "arbitrary"),
        name="s5",
    )(cu, kms, mo, ptab, dskip)


def _s5_tables(lam_re, lam_im, log_dt, b_re, b_im, c_re, c_im, nch):
    q = S_Q
    lr = jnp.minimum(lam_re.astype(F32), -1e-4)
    li = lam_im.astype(F32)
    dt = jnp.exp(log_dt.astype(F32))[:, None]
    mag = jnp.exp(lr * dt)
    ab_re = mag * jnp.cos(li * dt)
    ab_im = mag * jnp.sin(li * dt)
    nr = ab_re - 1.0
    den = lr * lr + li * li
    coef_re = (nr * lr + ab_im * li) / den
    coef_im = (ab_im * lr - nr * li) / den
    br, bi = b_re.astype(F32), b_im.astype(F32)
    bb_re = coef_re[..., None] * br - coef_im[..., None] * bi
    bb_im = coef_re[..., None] * bi + coef_im[..., None] * br

    def apow(nvals):
        e = jnp.asarray(nvals, F32)[:, None, None]
        m = jnp.exp(e * (lr * dt))
        ang = e * (li * dt)
        return m * jnp.cos(ang), m * jnp.sin(ang)

    eye_g = jnp.eye(S_GPB, dtype=F32)
    ar, ai = apow(list(range(q + 1)))
    cr, ci = c_re.astype(F32), c_im.astype(F32)
    ca_re = cr[None] * ar[:, :, None, :] - ci[None] * ai[:, :, None, :]
    ca_im = cr[None] * ai[:, :, None, :] + ci[None] * ar[:, :, None, :]
    kt = (jnp.einsum('tghp,gpk->tgkh', ca_re[:q], bb_re, precision=HIGHEST)
          - jnp.einsum('tghp,gpk->tgkh', ca_im[:q], bb_im, precision=HIGHEST))
    kt = kt.reshape(q, S_BLOCKS, S_GPB, S_GROUP, S_GROUP)
    kd = jnp.einsum('tjgkh,ge->tjgkeh', kt, eye_g).reshape(q, S_BLOCKS, LANES, LANES)
    lagi = jnp.arange(q)[None, :] - jnp.arange(q)[:, None]
    kbig = jnp.where((lagi >= 0)[:, :, None, None, None],
                     kd[jnp.clip(lagi, 0, q - 1)], 0.0)
    kbig = kbig.transpose(2, 0, 3, 1, 4).reshape(S_BLOCKS, q * LANES, q * LANES)
    pw_r, pw_i = ar[:q][::-1], ai[:q][::-1]
    ms_re = pw_r[..., None] * bb_re[None] - pw_i[..., None] * bb_im[None]
    ms_im = pw_r[..., None] * bb_im[None] + pw_i[..., None] * bb_re[None]

    def blk_in(m):
        m = m.reshape(q, S_BLOCKS, S_GPB, S_STATE, S_GROUP)
        m = jnp.einsum('sjgpk,ge->jsgkep', m, eye_g)
        return m.reshape(S_BLOCKS, q * LANES, S_SW)

    kms = jnp.concatenate([kbig, blk_in(ms_re), blk_in(ms_im)], axis=2).astype(BF16)

    def blk_out(m):
        m = m.reshape(q, S_BLOCKS, S_GPB, S_GROUP, S_STATE)
        m = jnp.einsum('tjghp,ge->jgpteh', m, eye_g)
        return m.reshape(S_BLOCKS, S_SW, q * LANES)

    mo = jnp.concatenate([blk_out(ca_re[1:]), blk_out(-ca_im[1:])], axis=1).astype(BF16)
    pr, pi = apow([q * (c + 1) for c in range(nch)])
    pr = pr.reshape(nch, S_BLOCKS, S_SW).transpose(1, 0, 2)
    pi = pi.reshape(nch, S_BLOCKS, S_SW).transpose(1, 0, 2)
    ptab = jnp.concatenate([pr, pi], axis=2)
    return kms, mo, ptab


def _merge_kernel(ya_ref, yb_ref, yc_ref, cz_ref, g_ref, x_ref, wa_ref, wb_ref, wglu_ref, wc_ref,
                  wo_ref, nw_ref, *out_refs, final):
    yc0 = jnp.concatenate([yc_ref[j] for j in range(S_BLOCKS)], axis=1)
    glu = _dot(yc0.astype(BF16), wglu_ref[...])
    yc = yc0 * _sigmoid(glu) * cz_ref[...].astype(F32)
    merged = (g_ref[:, 0:D_MODEL].astype(F32) * _dot(ya_ref[...], wa_ref[...])
              + g_ref[:, D_MODEL:2 * D_MODEL].astype(F32) * _dot(yb_ref[...], wb_ref[...])
              + g_ref[:, 2 * D_MODEL:].astype(F32) * _dot(yc.astype(BF16), wc_ref[...]))
    out = x_ref[...] + _dot(merged.astype(BF16), wo_ref[...])
    ms = jnp.mean(out * out, axis=-1, keepdims=True)
    normed = out * lax.rsqrt(ms + EPS) * nw_ref[...]
    if final:
        out_refs[0][...] = normed
    else:
        out_refs[0][...] = out
        out_refs[1][...] = normed.astype(BF16)


def _merge(ya, yb, yc, cz, g, x2, wa, wb, wglu, wc, wo, nw, tm, final):
    n = x2.shape[0]
    x_out = jax.ShapeDtypeStruct((n, D_MODEL), F32)
    hn_out = jax.ShapeDtypeStruct((n, D_MODEL), BF16)
    return pl.pallas_call(
        functools.partial(_merge_kernel, final=final),
        grid=(n // tm,),
        in_specs=[_rows(tm, M_WP), _rows(tm, G_VWP),
                  pl.BlockSpec((S_BLOCKS, tm, LANES), lambda i: (0, i, 0)),
                  _rows(tm, S_W), _rows(tm, 3 * D_MODEL), _rows(tm, D_MODEL),
                  _whole((M_WP, D_MODEL)), _whole((G_VWP, D_MODEL)), _whole((S_W, S_W)),
                  _whole((S_W, D_MODEL)), _whole((D_MODEL, D_MODEL)), _whole((1, D_MODEL))],
        out_specs=[_rows(tm, D_MODEL)] if final else [_rows(tm, D_MODEL), _rows(tm, D_MODEL)],
        out_shape=[x_out] if final else [x_out, hn_out],
        compiler_params=_params("arbitrary"),
        name="merge",
    )(ya, yb, yc, cz, g, x2, wa, wb, wglu, wc, wo, nw)


def _pad_heads(w, heads, d, dp):
    zeros = jnp.zeros(w.shape[:-1] + (dp - d,), w.dtype)
    parts = []
    for h in range(heads):
        parts += [w[..., h * d:(h + 1) * d], zeros]
    return jnp.concatenate(parts, axis=-1)


def _pad_head_rows(w, heads, d, dp):
    zeros = jnp.zeros((dp - d,) + w.shape[1:], w.dtype)
    parts = []
    for h in range(heads):
        parts += [w[h * d:(h + 1) * d], zeros]
    return jnp.concatenate(parts, axis=0)


def _layer(x2, hn, batch, seq, w_in, mlstm_conv, mlstm_gate_b, mlstm_norm, gla_w_alpha, gla_b_alpha,
           gla_norm, s5_lam_re, s5_lam_im, s5_log_dt, s5_B_re, s5_B_im, s5_C_re, s5_C_im,
           s5_D, s5_w_glu, w_branch_mlstm, w_branch_gla, w_branch_s5, w_out, next_norm, final):
    n = batch * seq
    w_in = w_in.astype(BF16)
    offs = [0]
    for s in IN_SIZES:
        offs.append(offs[-1] + s)
    (aq, ak, av, ao, ai, af, az, bq, bk, bv, ba, bz, cu, cz, g) = [
        w_in[:, offs[i]:offs[i + 1]] for i in range(len(IN_SIZES))]
    mp = functools.partial(_pad_heads, heads=M_HEADS, d=M_HD, dp=M_HDP)
    gkp = functools.partial(_pad_heads, heads=G_HEADS, d=G_DK, dp=G_DKP)
    gvp = functools.partial(_pad_heads, heads=G_HEADS, d=G_DV, dp=G_DVP)
    w_mlstm = jnp.concatenate([mp(aq), mp(ak), mp(ao), mp(az)], axis=1)
    w_plain = jnp.concatenate([mp(av), gvp(bv), gkp(bq), gkp(bk)], axis=1)
    w_gate = jnp.concatenate([gvp(bz), cz, g], axis=1)
    w_small = jnp.concatenate([cu, ai, af, ba,
                               jnp.zeros((D_MODEL, LANES - 2 * M_HEADS - G_RANK), BF16)], axis=1)

    tm = min(1024, seq)
    q, k, oz = _proj_mlstm(hn, w_mlstm, mp(mlstm_conv[:, :M_W]), mp(mlstm_conv[:, M_W:]), seq, tm)
    mv, gv, gq, gk = _proj_plain(hn, w_plain, tm)
    gz, czs, gs = _proj_gate(hn, w_gate, tm)
    cu_s, gates = _proj_small(hn, w_small, tm)

    gb = jnp.concatenate([mlstm_gate_b[0], mlstm_gate_b[1],
                          jnp.zeros((LANES - 2 * M_HEADS,), F32)]).reshape(1, LANES)
    ya = _mlstm(q, k, mv, oz, gates, gb, mp(mlstm_norm).reshape(1, M_WP), batch, seq, min(256, seq))

    wal = jnp.concatenate([jnp.zeros((2 * M_HEADS, G_KWP), F32), gkp(gla_w_alpha),
                           jnp.zeros((LANES - 2 * M_HEADS - G_RANK, G_KWP), F32)], axis=0).astype(BF16)
    bal = gkp(gla_b_alpha).reshape(1, G_KWP)
    yb = _gla(gq, gk, gv, gz, gates, wal, bal, gvp(gla_norm).reshape(1, G_VWP), batch, seq, 64, 16)

    tms = min(1024, seq)
    kms, mo, ptab = _s5_tables(s5_lam_re, s5_lam_im, s5_log_dt, s5_B_re, s5_B_im, s5_C_re, s5_C_im,
                               tms // S_Q)
    yc = _s5(cu_s, kms, mo, ptab, s5_D.reshape(S_BLOCKS, 1, LANES), batch, seq, tms)

    wa = _pad_head_rows(w_branch_mlstm.astype(BF16), M_HEADS, M_HD, M_HDP)
    wb = _pad_head_rows(w_branch_gla.astype(BF16), G_HEADS, G_DV, G_DVP)
    return _merge(ya, yb, yc, czs, gs, x2, wa, wb, s5_w_glu.astype(BF16), w_branch_s5.astype(BF16),
                  w_out.astype(BF16), next_norm.reshape(1, D_MODEL), min(512, n), final)


def kernel(x, norm_w, w_in, mlstm_conv, mlstm_gate_b, mlstm_norm, gla_w_alpha, gla_b_alpha, gla_norm,
           s5_lam_re, s5_lam_im, s5_log_dt, s5_B_re, s5_B_im, s5_C_re, s5_C_im, s5_D, s5_w_glu,
           w_branch_mlstm, w_branch_gla, w_branch_s5, w_out, final_norm):
    batch, seq, _ = x.shape
    depth = norm_w.shape[0]
    x2 = x.reshape(batch * seq, D_MODEL)
    hn = _rms(x2, norm_w[0].reshape(1, D_MODEL), min(1024, batch * seq))
    for l in range(depth):
        final = l == depth - 1
        res = _layer(x2, hn, batch, seq, w_in[l], mlstm_conv[l], mlstm_gate_b[l], mlstm_norm[l],
                     gla_w_alpha[l], gla_b_alpha[l], gla_norm[l], s5_lam_re[l], s5_lam_im[l],
                     s5_log_dt[l], s5_B_re[l], s5_B_im[l], s5_C_re[l], s5_C_im[l], s5_D[l],
                     s5_w_glu[l], w_branch_mlstm[l], w_branch_gla[l], w_branch_s5[l], w_out[l],
                     final_norm if final else norm_w[l + 1], final)
        if final:
            x2 = res[0]
        else:
            x2, hn = res
    return x2.reshape(batch, seq, D_MODEL)
```

```python
import functools
import math

import jax
import jax.numpy as jnp
from jax import lax
from jax.experimental import pallas as pl
from jax.experimental.pallas import tpu as pltpu

F32 = jnp.float32
BF16 = jnp.bfloat16
HIGHEST = lax.Precision.HIGHEST

EPS = 1e-6
D_MODEL = 1024
LANES = 128
M_HEADS = 4
M_HD = 192
M_HDP = 256
M_W = M_HEADS * M_HD
M_WP = M_HEADS * M_HDP
M_CONV = 4
G_HEADS = 4
G_DK = 96
G_DKP = 128
G_DV = 192
G_DVP = 256
G_KW = G_HEADS * G_DK
G_KWP = G_HEADS * G_DKP
G_VW = G_HEADS * G_DV
G_VWP = G_HEADS * G_DVP
G_RANK = 16
G_TAU = 16.0
S_W = 512
S_GROUP = 16
S_GROUPS = 32
S_STATE = 64
S_BLOCKS = S_W // LANES
S_GPB = LANES // S_GROUP
S_SW = S_GPB * S_STATE
S_Q = 8

IN_SIZES = (M_W, M_W, M_W, M_W, M_HEADS, M_HEADS, M_W,
            G_KW, G_KW, G_VW, G_RANK, G_VW,
            S_W, S_W, 3 * D_MODEL)
SMALL_W = S_W + LANES
CW = 512

NEG = -1e30
G_SAFE_LOG_DECAY = -60.0
VMEM_LIMIT = 56 * 1024 * 1024


def _sigmoid(x):
    return 0.5 * jnp.tanh(0.5 * x) + 0.5


def _silu(x):
    h = 0.5 * x
    return h * jnp.tanh(h) + h


def _log_sigmoid(x):
    return jnp.minimum(x, 0.0) - jnp.log1p(jnp.exp(-jnp.abs(x)))


def _dot(a, b):
    return jnp.dot(a, b, preferred_element_type=F32)


def _dot_nt(a, b):
    return lax.dot_general(a, b, (((1,), (1,)), ((), ())), preferred_element_type=F32)


def _dot_tn(a, b):
    return lax.dot_general(a, b, (((0,), (0,)), ((), ())), preferred_element_type=F32)


def _params(*sem):
    return pltpu.CompilerParams(dimension_semantics=sem, vmem_limit_bytes=VMEM_LIMIT)


def _rows(tm, width):
    return pl.BlockSpec((tm, width), lambda i: (i, 0))


def _whole(shape):
    return pl.BlockSpec(shape, lambda i: (0,) * len(shape))


def _rms_kernel(x_ref, nw_ref, o_ref):
    x = x_ref[...]
    ms = jnp.mean(x * x, axis=-1, keepdims=True)
    o_ref[...] = (x * lax.rsqrt(ms + EPS) * nw_ref[...]).astype(BF16)


def _rms(x2, nw, tm):
    n = x2.shape[0]
    return pl.pallas_call(
        _rms_kernel, grid=(n // tm,),
        in_specs=[_rows(tm, D_MODEL), _whole((1, D_MODEL))],
        out_specs=_rows(tm, D_MODEL),
        out_shape=jax.ShapeDtypeStruct((n, D_MODEL), BF16),
        compiler_params=_params("arbitrary"), name="rms",
    )(x2, nw)


def _proj_mlstm_kernel(hn_ref, w_ref, cq_ref, ck_ref, q_ref, k_ref, oz_ref, halo_x, halo_b, *, tm,
                       tiles_per_seq):
    @pl.when(pl.program_id(0) % tiles_per_seq == 0)
    def _():
        halo_x[...] = jnp.zeros_like(halo_x)
        halo_b[...] = jnp.zeros_like(halo_b)

    hn = hn_ref[...]
    row8 = lax.broadcasted_iota(jnp.int32, (8, CW), 0)

    def shift(x, prev8, d):
        rolled = pltpu.roll(x, d, 0)
        head = jnp.where(row8 < d, pltpu.roll(prev8, d, 0), rolled[0:8])
        return jnp.concatenate([head, rolled[8:]], axis=0)

    nchunk = M_WP // CW
    for c in range(2 * nchunk):
        cw_ref, dst, scale = (cq_ref, q_ref, 1.0) if c < nchunk else (ck_ref, k_ref, M_HD ** -0.5)
        lc = slice((c % nchunk) * CW, (c % nchunk + 1) * CW)
        x = _dot(hn, w_ref[:, c * CW:(c + 1) * CW])
        x1 = shift(x, halo_x[c], 1)
        pair = cw_ref[1:2, lc] * x + cw_ref[0:1, lc] * x1
        y = _silu(cw_ref[3:4, lc] * x + cw_ref[2:3, lc] * x1 + shift(pair, halo_b[c], 2))
        halo_x[c] = x[tm - 8:tm]
        halo_b[c] = pair[tm - 8:tm]
        dst[:, lc] = (y if scale == 1.0 else y * scale).astype(BF16)
    for c in range(nchunk):
        lc = slice(c * CW, (c + 1) * CW)
        o = _dot(hn, w_ref[:, 2 * M_WP + c * CW:2 * M_WP + (c + 1) * CW])
        z = _dot(hn, w_ref[:, 3 * M_WP + c * CW:3 * M_WP + (c + 1) * CW])
        oz_ref[:, lc] = (_sigmoid(o) * _silu(z)).astype(BF16)


def _proj_mlstm(hn, w, cq, ck, seq, tm):
    n = hn.shape[0]
    out = jax.ShapeDtypeStruct((n, M_WP), BF16)
    return pl.pallas_call(
        functools.partial(_proj_mlstm_kernel, tm=tm, tiles_per_seq=seq // tm),
        grid=(n // tm,),
        in_specs=[_rows(tm, D_MODEL), _whole((D_MODEL, 4 * M_WP)), _whole((M_CONV, M_WP)),
                  _whole((M_CONV, M_WP))],
        out_specs=[_rows(tm, M_WP)] * 3,
        out_shape=[out] * 3,
        scratch_shapes=[pltpu.VMEM((2 * M_WP // CW, 8, CW), F32)] * 2,
        compiler_params=_params("arbitrary"), name="proj_mlstm",
    )(hn, w, cq, ck)


def _proj_plain_kernel(hn_ref, w_ref, mv_ref, gv_ref, gq_ref, gk_ref):
    hn = hn_ref[...]
    lane = lax.broadcasted_iota(jnp.int32, (hn.shape[0], CW), 1)
    ones_lane = lane % M_HDP == M_HD
    off = 0
    for dst, width, kind in ((mv_ref, M_WP, "ones"), (gv_ref, G_VWP, None), (gq_ref, G_KWP, "scale"),
                             (gk_ref, G_KWP, None)):
        for c in range(width // CW):
            r = _dot(hn, w_ref[:, off + c * CW:off + (c + 1) * CW])
            if kind == "ones":
                r = jnp.where(ones_lane, 1.0, r)
            elif kind == "scale":
                r = r * (G_DK ** -0.5)
            dst[:, c * CW:(c + 1) * CW] = r.astype(BF16)
        off += width


def _proj_plain(hn, w, tm):
    n = hn.shape[0]
    widths = (M_WP, G_VWP, G_KWP, G_KWP)
    return pl.pallas_call(
        _proj_plain_kernel, grid=(n // tm,),
        in_specs=[_rows(tm, D_MODEL), _whole((D_MODEL, sum(widths)))],
        out_specs=[_rows(tm, wd) for wd in widths],
        out_shape=[jax.ShapeDtypeStruct((n, wd), BF16) for wd in widths],
        compiler_params=_params("arbitrary"), name="proj_plain",
    )(hn, w)


def _proj_gate_kernel(hn_ref, w_ref, gz_ref, cz_ref, g_ref):
    hn = hn_ref[...]
    off = 0
    for dst, width, act in ((gz_ref, G_VWP, _silu), (cz_ref, S_W, _silu), (g_ref, 3 * D_MODEL, _sigmoid)):
        for c in range(width // CW):
            r = _dot(hn, w_ref[:, off + c * CW:off + (c + 1) * CW])
            dst[:, c * CW:(c + 1) * CW] = act(r).astype(BF16)
        off += width


def _proj_gate(hn, w, tm):
    n = hn.shape[0]
    widths = (G_VWP, S_W, 3 * D_MODEL)
    return pl.pallas_call(
        _proj_gate_kernel, grid=(n // tm,),
        in_specs=[_rows(tm, D_MODEL), _whole((D_MODEL, sum(widths)))],
        out_specs=[_rows(tm, wd) for wd in widths],
        out_shape=[jax.ShapeDtypeStruct((n, wd), BF16) for wd in widths],
        compiler_params=_params("arbitrary"), name="proj_gate",
    )(hn, w)


def _segment_cumsum(x, seg):
    pos = lax.broadcasted_iota(jnp.int32, x.shape, 0) % seg
    d = 1
    while d < seg:
        x = x + jnp.where(pos >= d, pltpu.roll(x, d, 0), 0.0)
        d *= 2
    return x


def _proj_aux_kernel(hn_ref, w_ref, gb_ref, wal_ref, bal_ref, cu_ref, gm_ref, gmt_ref, gcum_ref, *,
                     m_chunk, g_chunk):
    r = _dot(hn_ref[...], w_ref[...])
    for j in range(S_BLOCKS):
        cu_ref[j] = r[:, j * LANES:(j + 1) * LANES]
    raw = r[:, S_W:]
    gates = raw + gb_ref[...]
    lane = lax.broadcasted_iota(jnp.int32, gates.shape, 1)
    fcum = _segment_cumsum(_log_sigmoid(gates), m_chunk)
    gm = jnp.where((lane >= M_HEADS) & (lane < 2 * M_HEADS), fcum, gates)
    gm_ref[...] = gm
    gmt_ref[...] = gm.T
    za = _dot(raw.astype(BF16), wal_ref[...]) + bal_ref[...]
    gcum_ref[...] = _segment_cumsum(_log_sigmoid(za) * (1.0 / G_TAU), g_chunk)


def _proj_aux(hn, w, gb, wal, bal, tm, m_chunk, g_chunk):
    n = hn.shape[0]
    return pl.pallas_call(
        functools.partial(_proj_aux_kernel, m_chunk=m_chunk, g_chunk=g_chunk), grid=(n // tm,),
        in_specs=[_rows(tm, D_MODEL), _whole((D_MODEL, SMALL_W)), _whole((1, LANES)),
                  _whole((LANES, G_KWP)), _whole((1, G_KWP))],
        out_specs=[pl.BlockSpec((S_BLOCKS, tm, LANES), lambda i: (0, i, 0)), _rows(tm, LANES),
                   pl.BlockSpec((LANES, tm), lambda i: (0, i)), _rows(tm, G_KWP)],
        out_shape=[jax.ShapeDtypeStruct((S_BLOCKS, n, LANES), F32),
                   jax.ShapeDtypeStruct((n, LANES), F32),
                   jax.ShapeDtypeStruct((LANES, n), F32),
                   jax.ShapeDtypeStruct((n, G_KWP), F32)],
        compiler_params=_params("arbitrary"), name="proj_aux",
    )(hn, w, gb, wal, bal)


def _mlstm_kernel(q_ref, k_ref, v_ref, oz_ref, gm_ref, gmt_ref, nw_ref, out_ref, c_scr, m_scr, *, tc):
    @pl.when(pl.program_id(1) == 0)
    def _():
        c_scr[...] = jnp.zeros_like(c_scr)
        m_scr[...] = jnp.zeros_like(m_scr)

    row = lax.broadcasted_iota(jnp.int32, (tc, tc), 0)
    col = lax.broadcasted_iota(jnp.int32, (tc, tc), 1)
    causal = col <= row
    real = lax.broadcasted_iota(jnp.int32, (tc, M_HDP), 1) < M_HD

    for h in range(M_HEADS):
        sl = slice(h * M_HDP, (h + 1) * M_HDP)
        qb = q_ref[:, sl]
        kb = k_ref[:, sl]
        vext = v_ref[:, sl]
        i_col = gm_ref[:, h:h + 1]
        bcum_col = gm_ref[:, M_HEADS + h:M_HEADS + h + 1]
        g_row = gmt_ref[h:h + 1, :] - gmt_ref[M_HEADS + h:M_HEADS + h + 1, :]
        m_prev = m_scr[h, 0:1, 0:1]

        inter = bcum_col + m_prev
        m_t = jnp.maximum(inter, bcum_col + jnp.max(jnp.where(causal, g_row, NEG), axis=1, keepdims=True))
        w_intra = jnp.exp(jnp.where(causal, g_row + (bcum_col - m_t), NEG))
        w_inter = jnp.exp(inter - m_t)
        s = _dot_nt(qb, kb) * w_intra
        c_mat = c_scr[h]
        num = _dot(s.astype(BF16), vext) + w_inter * _dot(qb, c_mat.astype(BF16))
        den = num[:, M_HD:M_HD + 1]
        rden = 1.0 / jnp.maximum(jnp.abs(den), jnp.exp(-m_t))

        b_end = bcum_col[tc - 1:tc, :]
        to_end = b_end - bcum_col + i_col
        m_new = jnp.maximum(b_end + m_prev, jnp.max(to_end, axis=0, keepdims=True))
        wk = jnp.exp(to_end - m_new)
        w_prev = jnp.exp(b_end + m_prev - m_new)
        c_scr[h] = w_prev * c_mat + _dot_tn((wk * kb.astype(F32)).astype(BF16), vext)
        m_scr[h] = jnp.broadcast_to(m_new, (8, LANES))

        mean = jnp.sum(jnp.where(real, num, 0.0), axis=1, keepdims=True) * (1.0 / M_HD)
        xc = jnp.where(real, num - mean, 0.0)
        var = jnp.sum(xc * xc, axis=1, keepdims=True) * (1.0 / M_HD)
        scale = rden * lax.rsqrt(rden * rden * var + EPS)
        out_ref[:, sl] = (xc * scale * nw_ref[:, sl] * oz_ref[:, sl].astype(F32)).astype(BF16)


def _mlstm(q, k, v, oz, gm, gmt, nw, batch, seq, tc):
    n = batch * seq
    nt = seq // tc
    blk = pl.BlockSpec((tc, M_WP), lambda b, t: (b * nt + t, 0))
    return pl.pallas_call(
        functools.partial(_mlstm_kernel, tc=tc),
        grid=(batch, nt),
        in_specs=[blk, blk, blk, blk, pl.BlockSpec((tc, LANES), lambda b, t: (b * nt + t, 0)),
                  pl.BlockSpec((LANES, tc), lambda b, t: (0, b * nt + t)),
                  pl.BlockSpec((1, M_WP), lambda b, t: (0, 0))],
        out_specs=blk,
        out_shape=jax.ShapeDtypeStruct((n, M_WP), BF16),
        scratch_shapes=[pltpu.VMEM((M_HEADS, M_HDP, M_HDP), F32),
                        pltpu.VMEM((M_HEADS, 8, LANES), F32)],
        compiler_params=_params("arbitrary", "arbitrary"),
        name="mlstm",
    )(q, k, v, oz, gm, gmt, nw)


def _gla_kernel(q_ref, k_ref, v_ref, z_ref, b_ref, nw_ref,
                out_ref, kbuf, bbuf, attn_scr, st_scr, *, t, cs):
    @pl.when(pl.program_id(1) == 0)
    def _():
        st_scr[...] = jnp.zeros_like(st_scr)

    row = lax.broadcasted_iota(jnp.int32, (t, t), 0)
    col = lax.broadcasted_iota(jnp.int32, (t, t), 1)
    causal = col <= row

    def operands(h):
        sl = slice(h * G_DKP, (h + 1) * G_DKP)
        return q_ref[:, sl].astype(F32), k_ref[:, sl].astype(F32), b_ref[:, sl]

    safe = jnp.min(b_ref[t - 1:t, :]) >= G_SAFE_LOG_DECAY

    @pl.when(safe)
    def _():
        for h in range(G_HEADS):
            qs, kk, b = operands(h)
            a = _dot_nt((qs * jnp.exp(b)).astype(BF16), (kk * jnp.exp(-b)).astype(BF16))
            attn_scr[h] = jnp.where(causal, a, 0.0)

    @pl.when(jnp.logical_not(safe))
    def _():
        rowv = lax.broadcasted_iota(jnp.int32, (t, G_DKP), 0)
        sub = rowv % cs
        lag = row - col
        kbuf[0:cs, :] = jnp.zeros((cs, G_DKP), F32)
        bbuf[0:cs, :] = jnp.zeros((cs, G_DKP), F32)
        for h in range(G_HEADS):
            qs, kk, b = operands(h)
            qparts, kparts = [], []
            for i in range(1, t // cs):
                r = b[i * cs - 1:i * cs, :]
                inblk = (rowv >= i * cs) & (rowv < (i + 1) * cs)
                qparts.append((qs * jnp.exp(jnp.where(inblk, b - r, NEG))).astype(BF16))
                kparts.append((kk * jnp.exp(jnp.where(rowv < i * cs, r - b, NEG))).astype(BF16))
            attn = _dot_nt(jnp.concatenate(qparts, axis=1), jnp.concatenate(kparts, axis=1))
            kbuf[cs:cs + t, :] = kk
            bbuf[cs:cs + t, :] = b
            for j in range(cs):
                ksh = kbuf[cs - j:cs - j + t, :]
                bsh = bbuf[cs - j:cs - j + t, :]
                e = jnp.exp(jnp.where(sub >= j, b - bsh, NEG))
                dj = jnp.sum(qs * ksh * e, axis=1, keepdims=True)
                attn = attn + jnp.where(lag == j, dj, 0.0)
            attn_scr[h] = attn

    for h in range(G_HEADS):
        slv = slice(h * G_DVP, (h + 1) * G_DVP)
        qs, kk, b = operands(h)
        v = v_ref[:, slv]
        blast = b[t - 1:t, :]
        st = st_scr[h]
        o = (_dot_nt((qs * jnp.exp(b)).astype(BF16), st.astype(BF16))
             + _dot(attn_scr[h].astype(BF16), v))
        kdec = (kk * jnp.exp(blast - b)).astype(BF16)
        st_scr[h] = jnp.exp(blast) * st + _dot_tn(v, kdec)

        var = jnp.sum(o * o, axis=1, keepdims=True) * (1.0 / G_DV)
        hb = o * lax.rsqrt(var + EPS) * nw_ref[:, slv]
        out_ref[:, slv] = (hb * z_ref[:, slv].astype(F32)).astype(BF16)


def _gla(q, k, v, z, bcum, nw, batch, seq, t, cs):
    n = batch * seq
    nt = seq // t

    def blk(width):
        return pl.BlockSpec((t, width), lambda b, c: (b * nt + c, 0))

    return pl.pallas_call(
        functools.partial(_gla_kernel, t=t, cs=cs),
        grid=(batch, nt),
        in_specs=[blk(G_KWP), blk(G_KWP), blk(G_VWP), blk(G_VWP), blk(G_KWP),
                  pl.BlockSpec((1, G_VWP), lambda b, c: (0, 0))],
        out_specs=blk(G_VWP),
        out_shape=jax.ShapeDtypeStruct((n, G_VWP), BF16),
        scratch_shapes=[pltpu.VMEM((t + cs, G_DKP), F32), pltpu.VMEM((t + cs, G_DKP), F32),
                        pltpu.VMEM((G_HEADS, t, t), F32),
                        pltpu.VMEM((G_HEADS, G_DVP, G_DKP), F32)],
        compiler_params=_params("arbitrary", "arbitrary"),
        name="gla",
    )(q, k, v, z, bcum, nw)


def _gelu_tanh(x):
    return 0.5 * x * (1.0 + jnp.tanh(math.sqrt(2.0 / math.pi) * (x + 0.044715 * (x * x * x))))


def _s5_kernel(u_ref, kms_ref, mo_ref, p_ref, d_ref, y_ref, carry, *, nch):
    @pl.when(pl.program_id(2) == 0)
    def _():
        carry[...] = jnp.zeros_like(carry)

    qw = S_Q * LANES
    ucat = jnp.concatenate([u_ref[0, pl.ds(s, nch, stride=S_Q), :] for s in range(S_Q)],
                           axis=1).astype(BF16)
    r = _dot(ucat, kms_ref[0])
    y = r[:, :qw]
    xr = r[:, qw:qw + S_SW]
    xi = r[:, qw + S_SW:]

    rowi = lax.broadcasted_iota(jnp.int32, (nch, S_SW), 0)
    d = 1
    while d < nch:
        pr = p_ref[0, d - 1:d, :S_SW]
        pi = p_ref[0, d - 1:d, S_SW:]
        keep = rowi >= d
        sr = jnp.where(keep, pltpu.roll(xr, d, 0), 0.0)
        si = jnp.where(keep, pltpu.roll(xi, d, 0), 0.0)
        xr, xi = xr + (pr * sr - pi * si), xi + (pr * si + pi * sr)
        d *= 2
    cr = carry[0:1, :S_SW]
    ci = carry[0:1, S_SW:]
    pr = p_ref[0, :, :S_SW]
    pi = p_ref[0, :, S_SW:]
    xr, xi = xr + (pr * cr - pi * ci), xi + (pr * ci + pi * cr)
    first = rowi == 0
    xpr = jnp.where(first, cr, pltpu.roll(xr, 1, 0))
    xpi = jnp.where(first, ci, pltpu.roll(xi, 1, 0))
    carry[0:1, :S_SW] = xr[nch - 1:nch, :]
    carry[0:1, S_SW:] = xi[nch - 1:nch, :]

    y = y + _dot(jnp.concatenate([xpr, xpi], axis=1).astype(BF16), mo_ref[0])
    for t in range(S_Q):
        y_ref[0, pl.ds(t, nch, stride=S_Q), :] = y[:, t * LANES:(t + 1) * LANES]
    y_ref[0] = _gelu_tanh(y_ref[0] + d_ref[0] * u_ref[0])


def _s5(cu, kms, mo, ptab, dskip, batch, seq, tm):
    n = batch * seq
    nt = seq // tm
    nch = tm // S_Q
    qw = S_Q * LANES
    return pl.pallas_call(
        functools.partial(_s5_kernel, nch=nch),
        grid=(S_BLOCKS, batch, nt),
        in_specs=[pl.BlockSpec((1, tm, LANES), lambda j, b, t: (j, b * nt + t, 0)),
                  pl.BlockSpec((1, qw, qw + 2 * S_SW), lambda j, b, t: (j, 0, 0)),
                  pl.BlockSpec((1, 2 * S_SW, qw), lambda j, b, t: (j, 0, 0)),
                  pl.BlockSpec((1, nch, 2 * S_SW), lambda j, b, t: (j, 0, 0)),
                  pl.BlockSpec((1, 1, LANES), lambda j, b, t: (j, 0, 0))],
        out_specs=pl.BlockSpec((1, tm, LANES), lambda j, b, t: (j, b * nt + t, 0)),
        out_shape=jax.ShapeDtypeStruct((S_BLOCKS, n, LANES), F32),
        scratch_shapes=[pltpu.VMEM((8, 2 * S_SW), F32)],
        compiler_params=_params("arbitrary", "arbitrary", "arbitrary"),
        name="s5",
    )(cu, kms, mo, ptab, dskip)


def _s5_tables(lam_re, lam_im, log_dt, b_re, b_im, c_re, c_im, nch):
    q = S_Q
    dep = lam_re.shape[0]
    lr = jnp.minimum(lam_re.astype(F32), -1e-4)
    li = lam_im.astype(F32)
    dt = jnp.exp(log_dt.astype(F32))[..., None]
    mag = jnp.exp(lr * dt)
    ab_re = mag * jnp.cos(li * dt)
    ab_im = mag * jnp.sin(li * dt)
    nr = ab_re - 1.0
    den = lr * lr + li * li
    coef_re = ((nr * lr + ab_im * li) / den)[:, :, None, :]
    coef_im = ((ab_im * lr - nr * li) / den)[:, :, None, :]
    brt = jnp.swapaxes(b_re.astype(F32), -1, -2)
    bit = jnp.swapaxes(b_im.astype(F32), -1, -2)
    bb_re = coef_re * brt - coef_im * bit
    bb_im = coef_re * bit + coef_im * brt

    def apow(nvals):
        e = jnp.asarray(nvals, F32)[:, None, None, None]
        m = jnp.exp(e * (lr * dt))
        ang = e * (li * dt)
        return m * jnp.cos(ang), m * jnp.sin(ang)

    def expand(m, rows, width):
        lead = m.shape[:-3]
        tile = (lax.broadcasted_iota(jnp.int32, (width, S_GPB * width), 1) % width
                == lax.broadcasted_iota(jnp.int32, (width, S_GPB * width), 0)).astype(F32)
        x = jnp.dot(m.reshape(-1, width), tile, precision=HIGHEST)
        x = x.reshape(lead + (S_BLOCKS, S_GPB * rows, S_GPB * width))
        diag = (lax.broadcasted_iota(jnp.int32, (S_GPB * rows, S_GPB * width), 0) // rows
                == lax.broadcasted_iota(jnp.int32, (S_GPB * rows, S_GPB * width), 1) // width)
        return jnp.where(diag, x, 0.0)

    ar, ai = apow(list(range(q + 1)))
    cr, ci = c_re.astype(F32), c_im.astype(F32)
    ca_re = cr[None] * ar[:, :, :, None, :] - ci[None] * ai[:, :, :, None, :]
    ca_im = cr[None] * ai[:, :, :, None, :] + ci[None] * ar[:, :, :, None, :]
    kt = (jnp.einsum('tdghp,dgkp->tdgkh', ca_re[:q], bb_re, precision=HIGHEST)
          - jnp.einsum('tdghp,dgkp->tdgkh', ca_im[:q], bb_im, precision=HIGHEST))
    kd = expand(kt, S_GROUP, S_GROUP)
    lagi = jnp.arange(q)[None, :] - jnp.arange(q)[:, None]
    kbig = jnp.where((lagi >= 0)[:, :, None, None, None, None],
                     kd[jnp.clip(lagi, 0, q - 1)], 0.0)
    kbig = kbig.transpose(2, 3, 0, 4, 1, 5).reshape(dep, S_BLOCKS, q * LANES, q * LANES)
    pw_r = ar[:q][::-1][:, :, :, None, :]
    pw_i = ai[:q][::-1][:, :, :, None, :]

    def blk_in(m):
        m = expand(m, S_GROUP, S_STATE)
        return m.transpose(1, 2, 0, 3, 4).reshape(dep, S_BLOCKS, q * LANES, S_SW)

    kms = jnp.concatenate([kbig, blk_in(pw_r * bb_re - pw_i * bb_im),
                           blk_in(pw_r * bb_im + pw_i * bb_re)], axis=3).astype(BF16)

    def blk_out(m):
        m = expand(jnp.swapaxes(m, -1, -2), S_STATE, S_GROUP)
        return m.transpose(1, 2, 3, 0, 4).reshape(dep, S_BLOCKS, S_SW, q * LANES)

    mo = jnp.concatenate([blk_out(ca_re[1:]), blk_out(-ca_im[1:])], axis=2).astype(BF16)
    pr, pi = apow([q * (c + 1) for c in range(nch)])
    pr = pr.reshape(nch, dep, S_BLOCKS, S_SW).transpose(1, 2, 0, 3)
    pi = pi.reshape(nch, dep, S_BLOCKS, S_SW).transpose(1, 2, 0, 3)
    ptab = jnp.concatenate([pr, pi], axis=3)
    return kms, mo, ptab


def _merge_kernel(ya_ref, yb_ref, yc_ref, cz_ref, g_ref, x_ref, wa_ref, wb_ref, wglu_ref, wc_ref,
                  wo_ref, nw_ref, *out_refs, final):
    yc0 = jnp.concatenate([yc_ref[j] for j in range(S_BLOCKS)], axis=1)
    glu = _dot(yc0.astype(BF16), wglu_ref[...])
    yc = yc0 * _sigmoid(glu) * cz_ref[...].astype(F32)
    merged = (g_ref[:, 0:D_MODEL].astype(F32) * _dot(ya_ref[...], wa_ref[...])
              + g_ref[:, D_MODEL:2 * D_MODEL].astype(F32) * _dot(yb_ref[...], wb_ref[...])
              + g_ref[:, 2 * D_MODEL:].astype(F32) * _dot(yc.astype(BF16), wc_ref[...]))
    out = x_ref[...] + _dot(merged.astype(BF16), wo_ref[...])
    ms = jnp.mean(out * out, axis=-1, keepdims=True)
    normed = out * lax.rsqrt(ms + EPS) * nw_ref[...]
    if final:
        out_refs[0][...] = normed
    else:
        out_refs[0][...] = out
        out_refs[1][...] = normed.astype(BF16)


def _merge(ya, yb, yc, cz, g, x2, wa, wb, wglu, wc, wo, nw, tm, final):
    n = x2.shape[0]
    x_out = jax.ShapeDtypeStruct((n, D_MODEL), F32)
    hn_out = jax.ShapeDtypeStruct((n, D_MODEL), BF16)
    return pl.pallas_call(
        functools.partial(_merge_kernel, final=final),
        grid=(n // tm,),
        in_specs=[_rows(tm, M_WP), _rows(tm, G_VWP),
                  pl.BlockSpec((S_BLOCKS, tm, LANES), lambda i: (0, i, 0)),
                  _rows(tm, S_W), _rows(tm, 3 * D_MODEL), _rows(tm, D_MODEL),
                  _whole((M_WP, D_MODEL)), _whole((G_VWP, D_MODEL)), _whole((S_W, S_W)),
                  _whole((S_W, D_MODEL)), _whole((D_MODEL, D_MODEL)), _whole((1, D_MODEL))],
        out_specs=[_rows(tm, D_MODEL)] if final else [_rows(tm, D_MODEL), _rows(tm, D_MODEL)],
        out_shape=[x_out] if final else [x_out, hn_out],
        compiler_params=_params("arbitrary"),
        name="merge",
    )(ya, yb, yc, cz, g, x2, wa, wb, wglu, wc, wo, nw)


def _pad_heads(w, heads, d, dp):
    zeros = jnp.zeros(w.shape[:-1] + (dp - d,), w.dtype)
    parts = []
    for h in range(heads):
        parts += [w[..., h * d:(h + 1) * d], zeros]
    return jnp.concatenate(parts, axis=-1)


def _pad_head_rows(w, heads, d, dp):
    zeros = jnp.zeros((w.shape[0], dp - d, w.shape[2]), w.dtype)
    parts = []
    for h in range(heads):
        parts += [w[:, h * d:(h + 1) * d], zeros]
    return jnp.concatenate(parts, axis=1)


def _prepare(p, nch):
    depth = p["w_in"].shape[0]
    w_in = p["w_in"].astype(BF16)
    offs = [0]
    for s in IN_SIZES:
        offs.append(offs[-1] + s)
    (aq, ak, av, ao, ai, af, az, bq, bk, bv, ba, bz, cu, cz, g) = [
        w_in[:, :, offs[i]:offs[i + 1]] for i in range(len(IN_SIZES))]
    mp = functools.partial(_pad_heads, heads=M_HEADS, d=M_HD, dp=M_HDP)
    gkp = functools.partial(_pad_heads, heads=G_HEADS, d=G_DK, dp=G_DKP)
    gvp = functools.partial(_pad_heads, heads=G_HEADS, d=G_DV, dp=G_DVP)
    kms, mo, ptab = _s5_tables(p["s5_lam_re"], p["s5_lam_im"], p["s5_log_dt"], p["s5_B_re"], p["s5_B_im"],
                               p["s5_C_re"], p["s5_C_im"], nch)
    gate_b = p["mlstm_gate_b"]
    return dict(
        w_mlstm=jnp.concatenate([mp(aq), mp(ak), mp(ao), mp(az)], axis=2),
        w_plain=jnp.concatenate([mp(av), gvp(bv), gkp(bq), gkp(bk)], axis=2),
        w_gate=jnp.concatenate([gvp(bz), cz, g], axis=2),
        w_small=jnp.concatenate([cu, ai, af, ba,
                                 jnp.zeros((depth, D_MODEL, LANES - 2 * M_HEADS - G_RANK), BF16)], axis=2),
        cq=mp(p["mlstm_conv"][:, :, :M_W]), ck=mp(p["mlstm_conv"][:, :, M_W:]),
        gb=jnp.concatenate([gate_b[:, 0], gate_b[:, 1],
                            jnp.zeros((depth, LANES - 2 * M_HEADS), F32)], axis=1)[:, None, :],
        m_nw=mp(p["mlstm_norm"])[:, None, :],
        wal=jnp.concatenate([jnp.zeros((depth, 2 * M_HEADS, G_KWP), F32), gkp(p["gla_w_alpha"]),
                             jnp.zeros((depth, LANES - 2 * M_HEADS - G_RANK, G_KWP), F32)],
                            axis=1).astype(BF16),
        bal=gkp(p["gla_b_alpha"])[:, None, :],
        g_nw=gvp(p["gla_norm"])[:, None, :],
        kms=kms, mo=mo, ptab=ptab,
        dskip=p["s5_D"].reshape(depth, S_BLOCKS, 1, LANES),
        wa=_pad_head_rows(p["w_branch_mlstm"].astype(BF16), M_HEADS, M_HD, M_HDP),
        wb=_pad_head_rows(p["w_branch_gla"].astype(BF16), G_HEADS, G_DV, G_DVP),
        wglu=p["s5_w_glu"].astype(BF16), wc=p["w_branch_s5"].astype(BF16), wo=p["w_out"].astype(BF16),
    )


def _tiles(batch, seq):
    return dict(proj=min(1024, seq), m_chunk=min(256, seq), g_chunk=min(128, seq), g_sub=16,
                s5=min(1024, seq), merge=min(512, batch * seq))


def _layer(x2, hn, batch, seq, w, next_norm, final, tl):
    tm = tl["proj"]
    q, k, oz = _proj_mlstm(hn, w["w_mlstm"], w["cq"], w["ck"], seq, tm)
    mv, gv, gq, gk = _proj_plain(hn, w["w_plain"], tm)
    gz, czs, gs = _proj_gate(hn, w["w_gate"], tm)
    cu_s, gm, gmt, gcum = _proj_aux(hn, w["w_small"], w["gb"], w["wal"], w["bal"], tm,
                                    tl["m_chunk"], tl["g_chunk"])
    ya = _mlstm(q, k, mv, oz, gm, gmt, w["m_nw"], batch, seq, tl["m_chunk"])
    yb = _gla(gq, gk, gv, gz, gcum, w["g_nw"], batch, seq, tl["g_chunk"], tl["g_sub"])
    yc = _s5(cu_s, w["kms"], w["mo"], w["ptab"], w["dskip"], batch, seq, tl["s5"])
    return _merge(ya, yb, yc, czs, gs, x2, w["wa"], w["wb"], w["wglu"], w["wc"], w["wo"],
                  next_norm.reshape(1, D_MODEL), tl["merge"], final)


def kernel(x, norm_w, w_in, mlstm_conv, mlstm_gate_b, mlstm_norm, gla_w_alpha, gla_b_alpha, gla_norm,
           s5_lam_re, s5_lam_im, s5_log_dt, s5_B_re, s5_B_im, s5_C_re, s5_C_im, s5_D, s5_w_glu,
           w_branch_mlstm, w_branch_gla, w_branch_s5, w_out, final_norm):
    batch, seq, _ = x.shape
    depth = norm_w.shape[0]
    tl = _tiles(batch, seq)
    prep = _prepare(dict(w_in=w_in, mlstm_conv=mlstm_conv, mlstm_gate_b=mlstm_gate_b, mlstm_norm=mlstm_norm,
                         gla_w_alpha=gla_w_alpha, gla_b_alpha=gla_b_alpha, gla_norm=gla_norm,
                         s5_lam_re=s5_lam_re, s5_lam_im=s5_lam_im, s5_log_dt=s5_log_dt, s5_B_re=s5_B_re,
                         s5_B_im=s5_B_im, s5_C_re=s5_C_re, s5_C_im=s5_C_im, s5_D=s5_D, s5_w_glu=s5_w_glu,
                         w_branch_mlstm=w_branch_mlstm, w_branch_gla=w_branch_gla,
                         w_branch_s5=w_branch_s5, w_out=w_out), tl["s5"] // S_Q)
    x2 = x.reshape(batch * seq, D_MODEL)
    hn = _rms(x2, norm_w[0].reshape(1, D_MODEL), min(1024, batch * seq))
    for l in range(depth):
        final = l == depth - 1
        res = _layer(x2, hn, batch, seq, {name: v[l] for name, v in prep.items()},
                     final_norm if final else norm_w[l + 1], final, tl)
        if final:
            x2 = res[0]
        else:
            x2, hn = res
    return x2.reshape(batch, seq, D_MODEL)
```

```python
import functools
import math

import jax
import jax.numpy as jnp
from jax import lax
from jax.experimental import pallas as pl
from jax.experimental.pallas import tpu as pltpu

F32 = jnp.float32
BF16 = jnp.bfloat16
HIGHEST = lax.Precision.HIGHEST

EPS = 1e-6
D_MODEL = 1024
LANES = 128
M_HEADS = 4
M_HD = 192
M_HDP = 256
M_W = M_HEADS * M_HD
M_WP = M_HEADS * M_HDP
M_CONV = 4
G_HEADS = 4
G_DK = 96
G_DKP = 128
G_DV = 192
G_DVP = 256
G_KW = G_HEADS * G_DK
G_KWP = G_HEADS * G_DKP
G_VW = G_HEADS * G_DV
G_VWP = G_HEADS * G_DVP
G_RANK = 16
G_TAU = 16.0
S_W = 512
S_GROUP = 16
S_GROUPS = 32
S_STATE = 64
S_BLOCKS = S_W // LANES
S_GPB = LANES // S_GROUP
S_SW = S_GPB * S_STATE
S_Q = 8

IN_SIZES = (M_W, M_W, M_W, M_W, M_HEADS, M_HEADS, M_W,
            G_KW, G_KW, G_VW, G_RANK, G_VW,
            S_W, S_W, 3 * D_MODEL)
SMALL_W = S_W + LANES
CW = 512

NEG = -1e30
G_SAFE_LOG_DECAY = -60.0
VMEM_LIMIT = 56 * 1024 * 1024


def _sigmoid(x):
    return 0.5 * jnp.tanh(0.5 * x) + 0.5


def _silu(x):
    h = 0.5 * x
    return h * jnp.tanh(h) + h


def _log_sigmoid(x):
    return jnp.minimum(x, 0.0) - jnp.log1p(jnp.exp(-jnp.abs(x)))


def _dot(a, b):
    return jnp.dot(a, b, preferred_element_type=F32)


def _dot_nt(a, b):
    return lax.dot_general(a, b, (((1,), (1,)), ((), ())), preferred_element_type=F32)


def _dot_tn(a, b):
    return lax.dot_general(a, b, (((0,), (0,)), ((), ())), preferred_element_type=F32)


def _params(*sem):
    return pltpu.CompilerParams(dimension_semantics=sem, vmem_limit_bytes=VMEM_LIMIT)


def _rows(tm, width):
    return pl.BlockSpec((tm, width), lambda i: (i, 0))


def _whole(shape):
    return pl.BlockSpec(shape, lambda i: (0,) * len(shape))


def _rms_kernel(x_ref, nw_ref, o_ref):
    x = x_ref[...]
    ms = jnp.mean(x * x, axis=-1, keepdims=True)
    o_ref[...] = (x * lax.rsqrt(ms + EPS) * nw_ref[...]).astype(BF16)


def _rms(x2, nw, tm):
    n = x2.shape[0]
    return pl.pallas_call(
        _rms_kernel, grid=(n // tm,),
        in_specs=[_rows(tm, D_MODEL), _whole((1, D_MODEL))],
        out_specs=_rows(tm, D_MODEL),
        out_shape=jax.ShapeDtypeStruct((n, D_MODEL), BF16),
        compiler_params=_params("arbitrary"), name="rms",
    )(x2, nw)


W_OFF = {}
_o = 0
for _name, _size in zip(("aq", "ak", "av", "ao", "ai", "af", "az", "bq", "bk", "bv", "ba", "bz", "cu", "cz", "g"),
                        IN_SIZES):
    W_OFF[_name] = _o
    _o += _size
IN_WIDTH = _o
W_MLSTM, W_PLAIN, W_GATE = 4 * M_WP, M_WP + G_VWP + 2 * G_KWP, G_VWP + S_W + 3 * D_MODEL


def _wprep_kernel(w_ref, wm_ref, wp_ref, wg_ref, ws_ref):
    rb = w_ref.shape[1]

    def heads(dst, dst_off, name, n_heads, d, dp):
        for h in range(n_heads):
            src = W_OFF[name] + h * d
            dst[0, :, dst_off + h * dp:dst_off + h * dp + d] = w_ref[0, :, src:src + d].astype(BF16)
            dst[0, :, dst_off + h * dp + d:dst_off + (h + 1) * dp] = jnp.zeros((rb, dp - d), BF16)

    def plain(dst, dst_off, name, width):
        dst[0, :, dst_off:dst_off + width] = w_ref[0, :, W_OFF[name]:W_OFF[name] + width].astype(BF16)

    for i, name in enumerate(("aq", "ak", "ao", "az")):
        heads(wm_ref, i * M_WP, name, M_HEADS, M_HD, M_HDP)
    heads(wp_ref, 0, "av", M_HEADS, M_HD, M_HDP)
    heads(wp_ref, M_WP, "bv", G_HEADS, G_DV, G_DVP)
    heads(wp_ref, M_WP + G_VWP, "bq", G_HEADS, G_DK, G_DKP)
    heads(wp_ref, M_WP + G_VWP + G_KWP, "bk", G_HEADS, G_DK, G_DKP)
    heads(wg_ref, 0, "bz", G_HEADS, G_DV, G_DVP)
    plain(wg_ref, G_VWP, "cz", S_W)
    plain(wg_ref, G_VWP + S_W, "g", 3 * D_MODEL)
    plain(ws_ref, 0, "cu", S_W)
    plain(ws_ref, S_W, "ai", 2 * M_HEADS)
    plain(ws_ref, S_W + 2 * M_HEADS, "ba", G_RANK)
    used = S_W + 2 * M_HEADS + G_RANK
    ws_ref[0, :, used:] = jnp.zeros((rb, SMALL_W - used), BF16)


def _wprep(w_in, rb):
    depth = w_in.shape[0]
    widths = (W_MLSTM, W_PLAIN, W_GATE, SMALL_W)
    return pl.pallas_call(
        _wprep_kernel, grid=(depth, D_MODEL // rb),
        in_specs=[pl.BlockSpec((1, rb, IN_WIDTH), lambda l, i: (l, i, 0))],
        out_specs=[pl.BlockSpec((1, rb, wd), lambda l, i: (l, i, 0)) for wd in widths],
        out_shape=[jax.ShapeDtypeStruct((depth, D_MODEL, wd), BF16) for wd in widths],
        compiler_params=_params("arbitrary", "arbitrary"), name="wprep",
    )(w_in)


def _proj_mlstm_kernel(hn_ref, w_ref, cq_ref, ck_ref, q_ref, k_ref, oz_ref, halo_x, halo_b, *, tm,
                       tiles_per_seq):
    @pl.when(pl.program_id(0) % tiles_per_seq == 0)
    def _():
        halo_x[...] = jnp.zeros_like(halo_x)
        halo_b[...] = jnp.zeros_like(halo_b)

    hn = hn_ref[...]
    row8 = lax.broadcasted_iota(jnp.int32, (8, CW), 0)

    def shift(x, prev8, d):
        rolled = pltpu.roll(x, d, 0)
        head = jnp.where(row8 < d, pltpu.roll(prev8, d, 0), rolled[0:8])
        return jnp.concatenate([head, rolled[8:]], axis=0)

    nchunk = M_WP // CW
    for c in range(2 * nchunk):
        cw_ref, dst, scale = (cq_ref, q_ref, 1.0) if c < nchunk else (ck_ref, k_ref, M_HD ** -0.5)
        lc = slice((c % nchunk) * CW, (c % nchunk + 1) * CW)
        x = _dot(hn, w_ref[:, c * CW:(c + 1) * CW])
        x1 = shift(x, halo_x[c], 1)
        pair = cw_ref[1:2, lc] * x + cw_ref[0:1, lc] * x1
        y = _silu(cw_ref[3:4, lc] * x + cw_ref[2:3, lc] * x1 + shift(pair, halo_b[c], 2))
        halo_x[c] = x[tm - 8:tm]
        halo_b[c] = pair[tm - 8:tm]
        dst[:, lc] = (y if scale == 1.0 else y * scale).astype(BF16)
    for c in range(nchunk):
        lc = slice(c * CW, (c + 1) * CW)
        o = _dot(hn, w_ref[:, 2 * M_WP + c * CW:2 * M_WP + (c + 1) * CW])
        z = _dot(hn, w_ref[:, 3 * M_WP + c * CW:3 * M_WP + (c + 1) * CW])
        oz_ref[:, lc] = (_sigmoid(o) * _silu(z)).astype(BF16)


def _proj_mlstm(hn, w, cq, ck, seq, tm):
    n = hn.shape[0]
    out = jax.ShapeDtypeStruct((n, M_WP), BF16)
    return pl.pallas_call(
        functools.partial(_proj_mlstm_kernel, tm=tm, tiles_per_seq=seq // tm),
        grid=(n // tm,),
        in_specs=[_rows(tm, D_MODEL), _whole((D_MODEL, 4 * M_WP)), _whole((M_CONV, M_WP)),
                  _whole((M_CONV, M_WP))],
        out_specs=[_rows(tm, M_WP)] * 3,
        out_shape=[out] * 3,
        scratch_shapes=[pltpu.VMEM((2 * M_WP // CW, 8, CW), F32)] * 2,
        compiler_params=_params("arbitrary"), name="proj_mlstm",
    )(hn, w, cq, ck)


def _proj_plain_kernel(hn_ref, w_ref, mv_ref, gv_ref, gq_ref, gk_ref):
    hn = hn_ref[...]
    lane = lax.broadcasted_iota(jnp.int32, (hn.shape[0], CW), 1)
    ones_lane = lane % M_HDP == M_HD
    off = 0
    for dst, width, kind in ((mv_ref, M_WP, "ones"), (gv_ref, G_VWP, None), (gq_ref, G_KWP, "scale"),
                             (gk_ref, G_KWP, None)):
        for c in range(width // CW):
            r = _dot(hn, w_ref[:, off + c * CW:off + (c + 1) * CW])
            if kind == "ones":
                r = jnp.where(ones_lane, 1.0, r)
            elif kind == "scale":
                r = r * (G_DK ** -0.5)
            dst[:, c * CW:(c + 1) * CW] = r.astype(BF16)
        off += width


def _proj_plain(hn, w, tm):
    n = hn.shape[0]
    widths = (M_WP, G_VWP, G_KWP, G_KWP)
    return pl.pallas_call(
        _proj_plain_kernel, grid=(n // tm,),
        in_specs=[_rows(tm, D_MODEL), _whole((D_MODEL, sum(widths)))],
        out_specs=[_rows(tm, wd) for wd in widths],
        out_shape=[jax.ShapeDtypeStruct((n, wd), BF16) for wd in widths],
        compiler_params=_params("arbitrary"), name="proj_plain",
    )(hn, w)


def _proj_gate_kernel(hn_ref, w_ref, gz_ref, cz_ref, g_ref):
    hn = hn_ref[...]
    off = 0
    for dst, width, act in ((gz_ref, G_VWP, _silu), (cz_ref, S_W, _silu), (g_ref, 3 * D_MODEL, _sigmoid)):
        for c in range(width // CW):
            r = _dot(hn, w_ref[:, off + c * CW:off + (c + 1) * CW])
            dst[:, c * CW:(c + 1) * CW] = act(r).astype(BF16)
        off += width


def _proj_gate(hn, w, tm):
    n = hn.shape[0]
    widths = (G_VWP, S_W, 3 * D_MODEL)
    return pl.pallas_call(
        _proj_gate_kernel, grid=(n // tm,),
        in_specs=[_rows(tm, D_MODEL), _whole((D_MODEL, sum(widths)))],
        out_specs=[_rows(tm, wd) for wd in widths],
        out_shape=[jax.ShapeDtypeStruct((n, wd), BF16) for wd in widths],
        compiler_params=_params("arbitrary"), name="proj_gate",
    )(hn, w)


def _segment_cumsum(x, seg):
    pos = lax.broadcasted_iota(jnp.int32, x.shape, 0) % seg
    d = 1
    while d < seg:
        x = x + jnp.where(pos >= d, pltpu.roll(x, d, 0), 0.0)
        d *= 2
    return x


def _proj_aux_kernel(hn_ref, w_ref, gb_ref, wal_ref, bal_ref, cu_ref, gm_ref, gmt_ref, gcum_ref, *,
                     m_chunk, g_chunk):
    r = _dot(hn_ref[...], w_ref[...])
    for j in range(S_BLOCKS):
        cu_ref[j] = r[:, j * LANES:(j + 1) * LANES]
    raw = r[:, S_W:]
    gates = raw + gb_ref[...]
    lane = lax.broadcasted_iota(jnp.int32, gates.shape, 1)
    fcum = _segment_cumsum(_log_sigmoid(gates), m_chunk)
    gm = jnp.where((lane >= M_HEADS) & (lane < 2 * M_HEADS), fcum, gates)
    gm_ref[...] = gm
    gmt_ref[...] = gm.T
    za = _dot(raw.astype(BF16), wal_ref[...]) + bal_ref[...]
    gcum_ref[...] = _segment_cumsum(_log_sigmoid(za) * (1.0 / G_TAU), g_chunk)


def _proj_aux(hn, w, gb, wal, bal, tm, m_chunk, g_chunk):
    n = hn.shape[0]
    return pl.pallas_call(
        functools.partial(_proj_aux_kernel, m_chunk=m_chunk, g_chunk=g_chunk), grid=(n // tm,),
        in_specs=[_rows(tm, D_MODEL), _whole((D_MODEL, SMALL_W)), _whole((1, LANES)),
                  _whole((LANES, G_KWP)), _whole((1, G_KWP))],
        out_specs=[pl.BlockSpec((S_BLOCKS, tm, LANES), lambda i: (0, i, 0)), _rows(tm, LANES),
                   pl.BlockSpec((LANES, tm), lambda i: (0, i)), _rows(tm, G_KWP)],
        out_shape=[jax.ShapeDtypeStruct((S_BLOCKS, n, LANES), F32),
                   jax.ShapeDtypeStruct((n, LANES), F32),
                   jax.ShapeDtypeStruct((LANES, n), F32),
                   jax.ShapeDtypeStruct((n, G_KWP), F32)],
        compiler_params=_params("arbitrary"), name="proj_aux",
    )(hn, w, gb, wal, bal)


def _mlstm_kernel(q_ref, k_ref, v_ref, oz_ref, gm_ref, gmt_ref, nw_ref, out_ref, c_scr, m_scr, *, tc):
    @pl.when(pl.program_id(1) == 0)
    def _():
        c_scr[...] = jnp.zeros_like(c_scr)
        m_scr[...] = jnp.zeros_like(m_scr)

    row = lax.broadcasted_iota(jnp.int32, (tc, tc), 0)
    col = lax.broadcasted_iota(jnp.int32, (tc, tc), 1)
    causal = col <= row
    real = lax.broadcasted_iota(jnp.int32, (tc, M_HDP), 1) < M_HD

    for h in range(M_HEADS):
        sl = slice(h * M_HDP, (h + 1) * M_HDP)
        qb = q_ref[:, sl]
        kb = k_ref[:, sl]
        vext = v_ref[:, sl]
        i_col = gm_ref[:, h:h + 1]
        bcum_col = gm_ref[:, M_HEADS + h:M_HEADS + h + 1]
        g_row = gmt_ref[h:h + 1, :] - gmt_ref[M_HEADS + h:M_HEADS + h + 1, :]
        m_prev = m_scr[h, 0:1, 0:1]

        inter = bcum_col + m_prev
        m_t = jnp.maximum(inter, bcum_col + jnp.max(jnp.where(causal, g_row, NEG), axis=1, keepdims=True))
        w_intra = jnp.exp(jnp.where(causal, g_row + (bcum_col - m_t), NEG))
        w_inter = jnp.exp(inter - m_t)
        s = _dot_nt(qb, kb) * w_intra
        c_mat = c_scr[h]
        num = _dot(s.astype(BF16), vext) + w_inter * _dot(qb, c_mat.astype(BF16))
        den = num[:, M_HD:M_HD + 1]
        rden = 1.0 / jnp.maximum(jnp.abs(den), jnp.exp(-m_t))

        b_end = bcum_col[tc - 1:tc, :]
        to_end = b_end - bcum_col + i_col
        m_new = jnp.maximum(b_end + m_prev, jnp.max(to_end, axis=0, keepdims=True))
        wk = jnp.exp(to_end - m_new)
        w_prev = jnp.exp(b_end + m_prev - m_new)
        c_scr[h] = w_prev * c_mat + _dot_tn((wk * kb.astype(F32)).astype(BF16), vext)
        m_scr[h] = jnp.broadcast_to(m_new, (8, LANES))

        mean = jnp.sum(jnp.where(real, num, 0.0), axis=1, keepdims=True) * (1.0 / M_HD)
        xc = jnp.where(real, num - mean, 0.0)
        var = jnp.sum(xc * xc, axis=1, keepdims=True) * (1.0 / M_HD)
        scale = rden * lax.rsqrt(rden * rden * var + EPS)
        out_ref[:, sl] = (xc * scale * nw_ref[:, sl] * oz_ref[:, sl].astype(F32)).astype(BF16)


def _mlstm(q, k, v, oz, gm, gmt, nw, batch, seq, tc):
    n = batch * seq
    nt = seq // tc
    blk = pl.BlockSpec((tc, M_WP), lambda b, t: (b * nt + t, 0))
    return pl.pallas_call(
        functools.partial(_mlstm_kernel, tc=tc),
        grid=(batch, nt),
        in_specs=[blk, blk, blk, blk, pl.BlockSpec((tc, LANES), lambda b, t: (b * nt + t, 0)),
                  pl.BlockSpec((LANES, tc), lambda b, t: (0, b * nt + t)),
                  pl.BlockSpec((1, M_WP), lambda b, t: (0, 0))],
        out_specs=blk,
        out_shape=jax.ShapeDtypeStruct((n, M_WP), BF16),
        scratch_shapes=[pltpu.VMEM((M_HEADS, M_HDP, M_HDP), F32),
                        pltpu.VMEM((M_HEADS, 8, LANES), F32)],
        compiler_params=_params("arbitrary", "arbitrary"),
        name="mlstm",
    )(q, k, v, oz, gm, gmt, nw)


def _gla_kernel(q_ref, k_ref, v_ref, z_ref, b_ref, nw_ref,
                out_ref, kbuf, bbuf, attn_scr, st_scr, *, t, cs):
    @pl.when(pl.program_id(1) == 0)
    def _():
        st_scr[...] = jnp.zeros_like(st_scr)

    row = lax.broadcasted_iota(jnp.int32, (t, t), 0)
    col = lax.broadcasted_iota(jnp.int32, (t, t), 1)
    causal = col <= row

    def operands(h):
        sl = slice(h * G_DKP, (h + 1) * G_DKP)
        return q_ref[:, sl].astype(F32), k_ref[:, sl].astype(F32), b_ref[:, sl]

    safe = jnp.min(b_ref[t - 1:t, :]) >= G_SAFE_LOG_DECAY

    @pl.when(safe)
    def _():
        for h in range(G_HEADS):
            qs, kk, b = operands(h)
            a = _dot_nt((qs * jnp.exp(b)).astype(BF16), (kk * jnp.exp(-b)).astype(BF16))
            attn_scr[h] = jnp.where(causal, a, 0.0)

    @pl.when(jnp.logical_not(safe))
    def _():
        rowv = lax.broadcasted_iota(jnp.int32, (t, G_DKP), 0)
        sub = rowv % cs
        lag = row - col
        kbuf[0:cs, :] = jnp.zeros((cs, G_DKP), F32)
        bbuf[0:cs, :] = jnp.zeros((cs, G_DKP), F32)
        for h in range(G_HEADS):
            qs, kk, b = operands(h)
            qparts, kparts = [], []
            for i in range(1, t // cs):
                r = b[i * cs - 1:i * cs, :]
                inblk = (rowv >= i * cs) & (rowv < (i + 1) * cs)
                qparts.append((qs * jnp.exp(jnp.where(inblk, b - r, NEG))).astype(BF16))
                kparts.append((kk * jnp.exp(jnp.where(rowv < i * cs, r - b, NEG))).astype(BF16))
            attn = _dot_nt(jnp.concatenate(qparts, axis=1), jnp.concatenate(kparts, axis=1))
            kbuf[cs:cs + t, :] = kk
            bbuf[cs:cs + t, :] = b
            for j in range(cs):
                ksh = kbuf[cs - j:cs - j + t, :]
                bsh = bbuf[cs - j:cs - j + t, :]
                e = jnp.exp(jnp.where(sub >= j, b - bsh, NEG))
                dj = jnp.sum(qs * ksh * e, axis=1, keepdims=True)
                attn = attn + jnp.where(lag == j, dj, 0.0)
            attn_scr[h] = attn

    for h in range(G_HEADS):
        slv = slice(h * G_DVP, (h + 1) * G_DVP)
        qs, kk, b = operands(h)
        v = v_ref[:, slv]
        blast = b[t - 1:t, :]
        st = st_scr[h]
        o = (_dot_nt((qs * jnp.exp(b)).astype(BF16), st.astype(BF16))
             + _dot(attn_scr[h].astype(BF16), v))
        kdec = (kk * jnp.exp(blast - b)).astype(BF16)
        st_scr[h] = jnp.exp(blast) * st + _dot_tn(v, kdec)

        var = jnp.sum(o * o, axis=1, keepdims=True) * (1.0 / G_DV)
        hb = o * lax.rsqrt(var + EPS) * nw_ref[:, slv]
        out_ref[:, slv] = (hb * z_ref[:, slv].astype(F32)).astype(BF16)


def _gla(q, k, v, z, bcum, nw, batch, seq, t, cs):
    n = batch * seq
    nt = seq // t

    def blk(width):
        return pl.BlockSpec((t, width), lambda b, c: (b * nt + c, 0))

    return pl.pallas_call(
        functools.partial(_gla_kernel, t=t, cs=cs),
        grid=(batch, nt),
        in_specs=[blk(G_KWP), blk(G_KWP), blk(G_VWP), blk(G_VWP), blk(G_KWP),
                  pl.BlockSpec((1, G_VWP), lambda b, c: (0, 0))],
        out_specs=blk(G_VWP),
        out_shape=jax.ShapeDtypeStruct((n, G_VWP), BF16),
        scratch_shapes=[pltpu.VMEM((t + cs, G_DKP), F32), pltpu.VMEM((t + cs, G_DKP), F32),
                        pltpu.VMEM((G_HEADS, t, t), F32),
                        pltpu.VMEM((G_HEADS, G_DVP, G_DKP), F32)],
        compiler_params=_params("arbitrary", "arbitrary"),
        name="gla",
    )(q, k, v, z, bcum, nw)


def _gelu_tanh(x):
    return 0.5 * x * (1.0 + jnp.tanh(math.sqrt(2.0 / math.pi) * (x + 0.044715 * (x * x * x))))


def _s5_kernel(u_ref, kms_ref, mo_ref, p_ref, d_ref, y_ref, carry, *, nch):
    @pl.when(pl.program_id(2) == 0)
    def _():
        carry[...] = jnp.zeros_like(carry)

    qw = S_Q * LANES
    ucat = jnp.concatenate([u_ref[0, pl.ds(s, nch, stride=S_Q), :] for s in range(S_Q)],
                           axis=1).astype(BF16)
    r = _dot(ucat, kms_ref[0])
    y = r[:, :qw]
    xr = r[:, qw:qw + S_SW]
    xi = r[:, qw + S_SW:]

    rowi = lax.broadcasted_iota(jnp.int32, (nch, S_SW), 0)
    d = 1
    while d < nch:
        pr = p_ref[0, d - 1:d, :S_SW]
        pi = p_ref[0, d - 1:d, S_SW:]
        keep = rowi >= d
        sr = jnp.where(keep, pltpu.roll(xr, d, 0), 0.0)
        si = jnp.where(keep, pltpu.roll(xi, d, 0), 0.0)
        xr, xi = xr + (pr * sr - pi * si), xi + (pr * si + pi * sr)
        d *= 2
    cr = carry[0:1, :S_SW]
    ci = carry[0:1, S_SW:]
    pr = p_ref[0, :, :S_SW]
    pi = p_ref[0, :, S_SW:]
    xr, xi = xr + (pr * cr - pi * ci), xi + (pr * ci + pi * cr)
    first = rowi == 0
    xpr = jnp.where(first, cr, pltpu.roll(xr, 1, 0))
    xpi = jnp.where(first, ci, pltpu.roll(xi, 1, 0))
    carry[0:1, :S_SW] = xr[nch - 1:nch, :]
    carry[0:1, S_SW:] = xi[nch - 1:nch, :]

    y = y + _dot(jnp.concatenate([xpr, xpi], axis=1).astype(BF16), mo_ref[0])
    for t in range(S_Q):
        y_ref[0, pl.ds(t, nch, stride=S_Q), :] = y[:, t * LANES:(t + 1) * LANES]
    y_ref[0] = _gelu_tanh(y_ref[0] + d_ref[0] * u_ref[0])


def _s5(cu, kms, mo, ptab, dskip, batch, seq, tm):
    n = batch * seq
    nt = seq // tm
    nch = tm // S_Q
    qw = S_Q * LANES
    return pl.pallas_call(
        functools.partial(_s5_kernel, nch=nch),
        grid=(S_BLOCKS, batch, nt),
        in_specs=[pl.BlockSpec((1, tm, LANES), lambda j, b, t: (j, b * nt + t, 0)),
                  pl.BlockSpec((1, qw, qw + 2 * S_SW), lambda j, b, t: (j, 0, 0)),
                  pl.BlockSpec((1, 2 * S_SW, qw), lambda j, b, t: (j, 0, 0)),
                  pl.BlockSpec((1, nch, 2 * S_SW), lambda j, b, t: (j, 0, 0)),
                  pl.BlockSpec((1, 1, LANES), lambda j, b, t: (j, 0, 0))],
        out_specs=pl.BlockSpec((1, tm, LANES), lambda j, b, t: (j, b * nt + t, 0)),
        out_shape=jax.ShapeDtypeStruct((S_BLOCKS, n, LANES), F32),
        scratch_shapes=[pltpu.VMEM((8, 2 * S_SW), F32)],
        compiler_params=_params("arbitrary", "arbitrary", "arbitrary"),
        name="s5",
    )(cu, kms, mo, ptab, dskip)


def _s5_expand_kernel(kt_ref, msr_ref, msi_ref, mor_ref, moi_ref, kms_ref, mo_ref):
    q = S_Q
    qw = q * LANES

    def embed(m, rows, width):
        shape = (width, S_GPB * width)
        tile = (lax.broadcasted_iota(jnp.int32, shape, 1) % width
                == lax.broadcasted_iota(jnp.int32, shape, 0)).astype(BF16)
        oshape = (S_GPB * rows, S_GPB * width)
        diag = (lax.broadcasted_iota(jnp.int32, oshape, 0) // rows
                == lax.broadcasted_iota(jnp.int32, oshape, 1) // width)
        return jnp.where(diag, _dot(m, tile), 0.0).astype(BF16)

    kd = [embed(kt_ref[0, 0, tau], S_GROUP, S_GROUP) for tau in range(q)]
    zero = jnp.zeros((LANES, LANES), BF16)
    for s in range(q):
        rows = slice(s * LANES, (s + 1) * LANES)
        for t in range(q):
            kms_ref[0, 0, rows, t * LANES:(t + 1) * LANES] = kd[t - s] if t >= s else zero
        kms_ref[0, 0, rows, qw:qw + S_SW] = embed(msr_ref[0, 0, s], S_GROUP, S_STATE)
        kms_ref[0, 0, rows, qw + S_SW:] = embed(msi_ref[0, 0, s], S_GROUP, S_STATE)
    for t in range(q):
        cols = slice(t * LANES, (t + 1) * LANES)
        mo_ref[0, 0, 0:S_SW, cols] = embed(mor_ref[0, 0, t], S_STATE, S_GROUP)
        mo_ref[0, 0, S_SW:, cols] = embed(moi_ref[0, 0, t], S_STATE, S_GROUP)


def _s5_expand(kt, msr, msi, mor, moi):
    depth = kt.shape[0]
    qw = S_Q * LANES

    def blk(a):
        return pl.BlockSpec((1, 1) + a.shape[2:], lambda d, j: (d, j, 0, 0, 0))

    return pl.pallas_call(
        _s5_expand_kernel, grid=(depth, S_BLOCKS),
        in_specs=[blk(a) for a in (kt, msr, msi, mor, moi)],
        out_specs=[pl.BlockSpec((1, 1, qw, qw + 2 * S_SW), lambda d, j: (d, j, 0, 0)),
                   pl.BlockSpec((1, 1, 2 * S_SW, qw), lambda d, j: (d, j, 0, 0))],
        out_shape=[jax.ShapeDtypeStruct((depth, S_BLOCKS, qw, qw + 2 * S_SW), BF16),
                   jax.ShapeDtypeStruct((depth, S_BLOCKS, 2 * S_SW, qw), BF16)],
        compiler_params=_params("arbitrary", "arbitrary"), name="s5_expand",
    )(kt, msr, msi, mor, moi)


def _s5_tables(lam_re, lam_im, log_dt, b_re, b_im, c_re, c_im, nch):
    q = S_Q
    dep = lam_re.shape[0]
    lr = jnp.minimum(lam_re.astype(F32), -1e-4)
    li = lam_im.astype(F32)
    dt = jnp.exp(log_dt.astype(F32))[..., None]
    mag = jnp.exp(lr * dt)
    ab_re = mag * jnp.cos(li * dt)
    ab_im = mag * jnp.sin(li * dt)
    nr = ab_re - 1.0
    den = lr * lr + li * li
    coef_re = ((nr * lr + ab_im * li) / den)[:, :, None, :]
    coef_im = ((ab_im * lr - nr * li) / den)[:, :, None, :]
    brt = jnp.swapaxes(b_re.astype(F32), -1, -2)
    bit = jnp.swapaxes(b_im.astype(F32), -1, -2)
    bb_re = coef_re * brt - coef_im * bit
    bb_im = coef_re * bit + coef_im * brt

    def apow(nvals):
        e = jnp.asarray(nvals, F32)[:, None, None, None]
        m = jnp.exp(e * (lr * dt))
        ang = e * (li * dt)
        return m * jnp.cos(ang), m * jnp.sin(ang)

    def per_block(m, rows):
        m = m.reshape(q, dep, S_BLOCKS, S_GPB * rows, m.shape[-1])
        return m.transpose(1, 2, 0, 3, 4).astype(BF16)

    ar, ai = apow(list(range(q + 1)))
    cr, ci = c_re.astype(F32), c_im.astype(F32)
    ca_re = cr[None] * ar[:, :, :, None, :] - ci[None] * ai[:, :, :, None, :]
    ca_im = cr[None] * ai[:, :, :, None, :] + ci[None] * ar[:, :, :, None, :]
    kt = (jnp.einsum('tdghp,dgkp->tdgkh', ca_re[:q], bb_re, precision=HIGHEST)
          - jnp.einsum('tdghp,dgkp->tdgkh', ca_im[:q], bb_im, precision=HIGHEST))
    pw_r = ar[:q][::-1][:, :, :, None, :]
    pw_i = ai[:q][::-1][:, :, :, None, :]
    kms, mo = _s5_expand(per_block(kt, S_GROUP),
                         per_block(pw_r * bb_re - pw_i * bb_im, S_GROUP),
                         per_block(pw_r * bb_im + pw_i * bb_re, S_GROUP),
                         per_block(jnp.swapaxes(ca_re[1:], -1, -2), S_STATE),
                         per_block(jnp.swapaxes(-ca_im[1:], -1, -2), S_STATE))
    pr, pi = apow([q * (c + 1) for c in range(nch)])
    pr = pr.reshape(nch, dep, S_BLOCKS, S_SW).transpose(1, 2, 0, 3)
    pi = pi.reshape(nch, dep, S_BLOCKS, S_SW).transpose(1, 2, 0, 3)
    ptab = jnp.concatenate([pr, pi], axis=3)
    return kms, mo, ptab


def _merge_kernel(ya_ref, yb_ref, yc_ref, cz_ref, g_ref, x_ref, wa_ref, wb_ref, wglu_ref, wc_ref,
                  wo_ref, nw_ref, *out_refs, final):
    yc0 = jnp.concatenate([yc_ref[j] for j in range(S_BLOCKS)], axis=1)
    glu = _dot(yc0.astype(BF16), wglu_ref[...])
    yc = yc0 * _sigmoid(glu) * cz_ref[...].astype(F32)
    merged = (g_ref[:, 0:D_MODEL].astype(F32) * _dot(ya_ref[...], wa_ref[...])
              + g_ref[:, D_MODEL:2 * D_MODEL].astype(F32) * _dot(yb_ref[...], wb_ref[...])
              + g_ref[:, 2 * D_MODEL:].astype(F32) * _dot(yc.astype(BF16), wc_ref[...]))
    out = x_ref[...] + _dot(merged.astype(BF16), wo_ref[...])
    ms = jnp.mean(out * out, axis=-1, keepdims=True)
    normed = out * lax.rsqrt(ms + EPS) * nw_ref[...]
    if final:
        out_refs[0][...] = normed
    else:
        out_refs[0][...] = out
        out_refs[1][...] = normed.astype(BF16)


def _merge(ya, yb, yc, cz, g, x2, wa, wb, wglu, wc, wo, nw, tm, final):
    n = x2.shape[0]
    x_out = jax.ShapeDtypeStruct((n, D_MODEL), F32)
    hn_out = jax.ShapeDtypeStruct((n, D_MODEL), BF16)
    return pl.pallas_call(
        functools.partial(_merge_kernel, final=final),
        grid=(n // tm,),
        in_specs=[_rows(tm, M_WP), _rows(tm, G_VWP),
                  pl.BlockSpec((S_BLOCKS, tm, LANES), lambda i: (0, i, 0)),
                  _rows(tm, S_W), _rows(tm, 3 * D_MODEL), _rows(tm, D_MODEL),
                  _whole((M_WP, D_MODEL)), _whole((G_VWP, D_MODEL)), _whole((S_W, S_W)),
                  _whole((S_W, D_MODEL)), _whole((D_MODEL, D_MODEL)), _whole((1, D_MODEL))],
        out_specs=[_rows(tm, D_MODEL)] if final else [_rows(tm, D_MODEL), _rows(tm, D_MODEL)],
        out_shape=[x_out] if final else [x_out, hn_out],
        compiler_params=_params("arbitrary"),
        name="merge",
    )(ya, yb, yc, cz, g, x2, wa, wb, wglu, wc, wo, nw)


def _pad_heads(w, heads, d, dp):
    zeros = jnp.zeros(w.shape[:-1] + (dp - d,), w.dtype)
    parts = []
    for h in range(heads):
        parts += [w[..., h * d:(h + 1) * d], zeros]
    return jnp.concatenate(parts, axis=-1)


def _pad_head_rows(w, heads, d, dp):
    zeros = jnp.zeros((w.shape[0], dp - d, w.shape[2]), w.dtype)
    parts = []
    for h in range(heads):
        parts += [w[:, h * d:(h + 1) * d], zeros]
    return jnp.concatenate(parts, axis=1)


def _prepare(p, nch):
    depth = p["w_in"].shape[0]
    w_mlstm, w_plain, w_gate, w_small = _wprep(p["w_in"], 128)
    mp = functools.partial(_pad_heads, heads=M_HEADS, d=M_HD, dp=M_HDP)
    gkp = functools.partial(_pad_heads, heads=G_HEADS, d=G_DK, dp=G_DKP)
    gvp = functools.partial(_pad_heads, heads=G_HEADS, d=G_DV, dp=G_DVP)
    kms, mo, ptab = _s5_tables(p["s5_lam_re"], p["s5_lam_im"], p["s5_log_dt"], p["s5_B_re"], p["s5_B_im"],
                               p["s5_C_re"], p["s5_C_im"], nch)
    gate_b = p["mlstm_gate_b"]
    return dict(
        w_mlstm=w_mlstm, w_plain=w_plain, w_gate=w_gate, w_small=w_small,
        cq=mp(p["mlstm_conv"][:, :, :M_W]), ck=mp(p["mlstm_conv"][:, :, M_W:]),
        gb=jnp.concatenate([gate_b[:, 0], gate_b[:, 1],
                            jnp.zeros((depth, LANES - 2 * M_HEADS), F32)], axis=1)[:, None, :],
        m_nw=mp(p["mlstm_norm"])[:, None, :],
        wal=jnp.concatenate([jnp.zeros((depth, 2 * M_HEADS, G_KWP), F32), gkp(p["gla_w_alpha"]),
                             jnp.zeros((depth, LANES - 2 * M_HEADS - G_RANK, G_KWP), F32)],
                            axis=1).astype(BF16),
        bal=gkp(p["gla_b_alpha"])[:, None, :],
        g_nw=gvp(p["gla_norm"])[:, None, :],
        kms=kms, mo=mo, ptab=ptab,
        dskip=p["s5_D"].reshape(depth, S_BLOCKS, 1, LANES),
        wa=_pad_head_rows(p["w_branch_mlstm"].astype(BF16), M_HEADS, M_HD, M_HDP),
        wb=_pad_head_rows(p["w_branch_gla"].astype(BF16), G_HEADS, G_DV, G_DVP),
        wglu=p["s5_w_glu"].astype(BF16), wc=p["w_branch_s5"].astype(BF16), wo=p["w_out"].astype(BF16),
    )


def _tiles(batch, seq):
    return dict(proj=min(1024, seq), m_chunk=min(256, seq), g_chunk=min(128, seq), g_sub=16,
                s5=min(1024, seq), merge=min(512, batch * seq))


def _layer(x2, hn, batch, seq, w, next_norm, final, tl):
    tm = tl["proj"]
    q, k, oz = _proj_mlstm(hn, w["w_mlstm"], w["cq"], w["ck"], seq, tm)
    mv, gv, gq, gk = _proj_plain(hn, w["w_plain"], tm)
    gz, czs, gs = _proj_gate(hn, w["w_gate"], tm)
    cu_s, gm, gmt, gcum = _proj_aux(hn, w["w_small"], w["gb"], w["wal"], w["bal"], tm,
                                    tl["m_chunk"], tl["g_chunk"])
    ya = _mlstm(q, k, mv, oz, gm, gmt, w["m_nw"], batch, seq, tl["m_chunk"])
    yb = _gla(gq, gk, gv, gz, gcum, w["g_nw"], batch, seq, tl["g_chunk"], tl["g_sub"])
    yc = _s5(cu_s, w["kms"], w["mo"], w["ptab"], w["dskip"], batch, seq, tl["s5"])
    return _merge(ya, yb, yc, czs, gs, x2, w["wa"], w["wb"], w["wglu"], w["wc"], w["wo"],
                  next_norm.reshape(1, D_MODEL), tl["merge"], final)


def kernel(x, norm_w, w_in, mlstm_conv, mlstm_gate_b, mlstm_norm, gla_w_alpha, gla_b_alpha, gla_norm,
           s5_lam_re, s5_lam_im, s5_log_dt, s5_B_re, s5_B_im, s5_C_re, s5_C_im, s5_D, s5_w_glu,
           w_branch_mlstm, w_branch_gla, w_branch_s5, w_out, final_norm):
    batch, seq, _ = x.shape
    depth = norm_w.shape[0]
    tl = _tiles(batch, seq)
    prep = _prepare(dict(w_in=w_in, mlstm_conv=mlstm_conv, mlstm_gate_b=mlstm_gate_b, mlstm_norm=mlstm_norm,
                         gla_w_alpha=gla_w_alpha, gla_b_alpha=gla_b_alpha, gla_norm=gla_norm,
                         s5_lam_re=s5_lam_re, s5_lam_im=s5_lam_im, s5_log_dt=s5_log_dt, s5_B_re=s5_B_re,
                         s5_B_im=s5_B_im, s5_C_re=s5_C_re, s5_C_im=s5_C_im, s5_D=s5_D, s5_w_glu=s5_w_glu,
                         w_branch_mlstm=w_branch_mlstm, w_branch_gla=w_branch_gla,
                         w_branch_s5=w_branch_s5, w_out=w_out), tl["s5"] // S_Q)
    x2 = x.reshape(batch * seq, D_MODEL)
    hn = _rms(x2, norm_w[0].reshape(1, D_MODEL), min(1024, batch * seq))
    for l in range(depth):
        final = l == depth - 1
        res = _layer(x2, hn, batch, seq, {name: v[l] for name, v in prep.items()},
                     final_norm if final else norm_w[l + 1], final, tl)
        if final:
            x2 = res[0]
        else:
            x2, hn = res
    return x2.reshape(batch, seq, D_MODEL)
```

```python
import functools
import math

import jax
import jax.numpy as jnp
from jax import lax
from jax.experimental import pallas as pl
from jax.experimental.pallas import tpu as pltpu

F32 = jnp.float32
BF16 = jnp.bfloat16
HIGHEST = lax.Precision.HIGHEST

EPS = 1e-6
D_MODEL = 1024
LANES = 128
M_HEADS = 4
M_HD = 192
M_HDP = 256
M_W = M_HEADS * M_HD
M_WP = M_HEADS * M_HDP
M_CONV = 4
G_HEADS = 4
G_DK = 96
G_DKP = 128
G_DV = 192
G_DVP = 256
G_KW = G_HEADS * G_DK
G_KWP = G_HEADS * G_DKP
G_VW = G_HEADS * G_DV
G_VWP = G_HEADS * G_DVP
G_RANK = 16
G_TAU = 16.0
S_W = 512
S_GROUP = 16
S_GROUPS = 32
S_STATE = 64
S_BLOCKS = S_W // LANES
S_GPB = LANES // S_GROUP
S_SW = S_GPB * S_STATE
S_Q = 8

IN_SIZES = (M_W, M_W, M_W, M_W, M_HEADS, M_HEADS, M_W,
            G_KW, G_KW, G_VW, G_RANK, G_VW,
            S_W, S_W, 3 * D_MODEL)
SMALL_W = S_W + LANES
CW = 512

NEG = -1e30
G_SAFE_LOG_DECAY = -60.0
VMEM_LIMIT = 56 * 1024 * 1024


def _sigmoid(x):
    return 0.5 * jnp.tanh(0.5 * x) + 0.5


def _silu(x):
    h = 0.5 * x
    return h * jnp.tanh(h) + h


def _log_sigmoid(x):
    return jnp.minimum(x, 0.0) - jnp.log1p(jnp.exp(-jnp.abs(x)))


def _dot(a, b):
    return jnp.dot(a, b, preferred_element_type=F32)


def _dot_nt(a, b):
    return lax.dot_general(a, b, (((1,), (1,)), ((), ())), preferred_element_type=F32)


def _dot_tn(a, b):
    return lax.dot_general(a, b, (((0,), (0,)), ((), ())), preferred_element_type=F32)


def _params(*sem):
    return pltpu.CompilerParams(dimension_semantics=sem, vmem_limit_bytes=VMEM_LIMIT)


def _rows(tm, width):
    return pl.BlockSpec((tm, width), lambda i: (i, 0))


def _whole(shape):
    return pl.BlockSpec(shape, lambda i: (0,) * len(shape))


def _of_layer(l, shape):
    return pl.BlockSpec((None,) + shape, lambda *_: (l,) + (0,) * len(shape))


def _rms_kernel(x_ref, nw_ref, o_ref):
    x = x_ref[...]
    ms = jnp.mean(x * x, axis=-1, keepdims=True)
    o_ref[...] = (x * lax.rsqrt(ms + EPS) * nw_ref[...]).astype(BF16)


def _rms(x2, nw, tm):
    n = x2.shape[0]
    return pl.pallas_call(
        _rms_kernel, grid=(n // tm,),
        in_specs=[_rows(tm, D_MODEL), _whole((1, D_MODEL))],
        out_specs=_rows(tm, D_MODEL),
        out_shape=jax.ShapeDtypeStruct((n, D_MODEL), BF16),
        compiler_params=_params("arbitrary"), name="rms",
    )(x2, nw)


W_OFF = {}
_o = 0
for _name, _size in zip(("aq", "ak", "av", "ao", "ai", "af", "az", "bq", "bk", "bv", "ba", "bz", "cu", "cz", "g"),
                        IN_SIZES):
    W_OFF[_name] = _o
    _o += _size
IN_WIDTH = _o
W_MLSTM, W_PLAIN, W_GATE = 4 * M_WP, M_WP + G_VWP + 2 * G_KWP, G_VWP + S_W + 3 * D_MODEL


def _wprep_kernel(w_ref, wm_ref, wp_ref, wg_ref, ws_ref):
    kb = w_ref.shape[2]

    def put(dst, dst_off, parts, width):
        xs = [w_ref[0, off:off + rows, :] for off, rows in parts]
        used = sum(rows for _, rows in parts)
        if used < width:
            xs.append(jnp.zeros((width - used, kb), F32))
        x = xs[0] if len(xs) == 1 else jnp.concatenate(xs, axis=0)
        dst[0, :, dst_off:dst_off + width] = x.T.astype(BF16)

    def heads(dst, dst_off, name, n_heads, d, dp):
        for h in range(n_heads):
            put(dst, dst_off + h * dp, [(W_OFF[name] + h * d, d)], dp)

    def plain(dst, dst_off, name, width):
        for c in range(0, width, CW):
            put(dst, dst_off + c, [(W_OFF[name] + c, CW)], CW)

    for i, name in enumerate(("aq", "ak", "ao", "az")):
        heads(wm_ref, i * M_WP, name, M_HEADS, M_HD, M_HDP)
    heads(wp_ref, 0, "av", M_HEADS, M_HD, M_HDP)
    heads(wp_ref, M_WP, "bv", G_HEADS, G_DV, G_DVP)
    heads(wp_ref, M_WP + G_VWP, "bq", G_HEADS, G_DK, G_DKP)
    heads(wp_ref, M_WP + G_VWP + G_KWP, "bk", G_HEADS, G_DK, G_DKP)
    heads(wg_ref, 0, "bz", G_HEADS, G_DV, G_DVP)
    plain(wg_ref, G_VWP, "cz", S_W)
    plain(wg_ref, G_VWP + S_W, "g", 3 * D_MODEL)
    plain(ws_ref, 0, "cu", S_W)
    put(ws_ref, S_W, [(W_OFF["ai"], 2 * M_HEADS), (W_OFF["ba"], G_RANK)], LANES)


def _wprep(w_t, kb):
    depth = w_t.shape[0]
    widths = (W_MLSTM, W_PLAIN, W_GATE, SMALL_W)
    return pl.pallas_call(
        _wprep_kernel, grid=(depth, D_MODEL // kb),
        in_specs=[pl.BlockSpec((1, IN_WIDTH, kb), lambda l, i: (l, 0, i))],
        out_specs=[pl.BlockSpec((1, kb, wd), lambda l, i: (l, i, 0)) for wd in widths],
        out_shape=[jax.ShapeDtypeStruct((depth, D_MODEL, wd), BF16) for wd in widths],
        compiler_params=_params("arbitrary", "arbitrary"), name="wprep",
    )(w_t)


def _proj_mlstm_kernel(hn_ref, w_ref, cq_ref, ck_ref, q_ref, k_ref, oz_ref, halo_x, halo_b, *, tm,
                       tiles_per_seq):
    @pl.when(pl.program_id(0) % tiles_per_seq == 0)
    def _():
        halo_x[...] = jnp.zeros_like(halo_x)
        halo_b[...] = jnp.zeros_like(halo_b)

    hn = hn_ref[...]
    row8 = lax.broadcasted_iota(jnp.int32, (8, CW), 0)

    def shift(x, prev8, d):
        rolled = pltpu.roll(x, d, 0)
        head = jnp.where(row8 < d, pltpu.roll(prev8, d, 0), rolled[0:8])
        return jnp.concatenate([head, rolled[8:]], axis=0)

    nchunk = M_WP // CW
    for c in range(2 * nchunk):
        cw_ref, dst, scale = (cq_ref, q_ref, 1.0) if c < nchunk else (ck_ref, k_ref, M_HD ** -0.5)
        lc = slice((c % nchunk) * CW, (c % nchunk + 1) * CW)
        x = _dot(hn, w_ref[:, c * CW:(c + 1) * CW])
        x1 = shift(x, halo_x[c], 1)
        pair = cw_ref[1:2, lc] * x + cw_ref[0:1, lc] * x1
        y = _silu(cw_ref[3:4, lc] * x + cw_ref[2:3, lc] * x1 + shift(pair, halo_b[c], 2))
        halo_x[c] = x[tm - 8:tm]
        halo_b[c] = pair[tm - 8:tm]
        dst[:, lc] = (y if scale == 1.0 else y * scale).astype(BF16)
    for c in range(nchunk):
        lc = slice(c * CW, (c + 1) * CW)
        o = _dot(hn, w_ref[:, 2 * M_WP + c * CW:2 * M_WP + (c + 1) * CW])
        z = _dot(hn, w_ref[:, 3 * M_WP + c * CW:3 * M_WP + (c + 1) * CW])
        oz_ref[:, lc] = (_sigmoid(o) * _silu(z)).astype(BF16)


def _proj_mlstm(hn, l, w, cq, ck, seq, tm):
    n = hn.shape[0]
    out = jax.ShapeDtypeStruct((n, M_WP), BF16)
    return pl.pallas_call(
        functools.partial(_proj_mlstm_kernel, tm=tm, tiles_per_seq=seq // tm),
        grid=(n // tm,),
        in_specs=[_rows(tm, D_MODEL), _of_layer(l, (D_MODEL, 4 * M_WP)), _of_layer(l, (M_CONV, M_WP)),
                  _of_layer(l, (M_CONV, M_WP))],
        out_specs=[_rows(tm, M_WP)] * 3,
        out_shape=[out] * 3,
        scratch_shapes=[pltpu.VMEM((2 * M_WP // CW, 8, CW), F32)] * 2,
        compiler_params=_params("arbitrary"), name="proj_mlstm",
    )(hn, w, cq, ck)


def _proj_plain_kernel(hn_ref, w_ref, mv_ref, gv_ref, gq_ref, gk_ref):
    hn = hn_ref[...]
    lane = lax.broadcasted_iota(jnp.int32, (hn.shape[0], CW), 1)
    ones_lane = lane % M_HDP == M_HD
    off = 0
    for dst, width, kind in ((mv_ref, M_WP, "ones"), (gv_ref, G_VWP, None), (gq_ref, G_KWP, "scale"),
                             (gk_ref, G_KWP, None)):
        for c in range(width // CW):
            r = _dot(hn, w_ref[:, off + c * CW:off + (c + 1) * CW])
            if kind == "ones":
                r = jnp.where(ones_lane, 1.0, r)
            elif kind == "scale":
                r = r * (G_DK ** -0.5)
            dst[:, c * CW:(c + 1) * CW] = r.astype(BF16)
        off += width


def _proj_plain(hn, l, w, tm):
    n = hn.shape[0]
    widths = (M_WP, G_VWP, G_KWP, G_KWP)
    return pl.pallas_call(
        _proj_plain_kernel, grid=(n // tm,),
        in_specs=[_rows(tm, D_MODEL), _of_layer(l, (D_MODEL, sum(widths)))],
        out_specs=[_rows(tm, wd) for wd in widths],
        out_shape=[jax.ShapeDtypeStruct((n, wd), BF16) for wd in widths],
        compiler_params=_params("arbitrary"), name="proj_plain",
    )(hn, w)


def _proj_gate_kernel(hn_ref, w_ref, gz_ref, cz_ref, g_ref):
    hn = hn_ref[...]
    off = 0
    for dst, width, act in ((gz_ref, G_VWP, _silu), (cz_ref, S_W, _silu), (g_ref, 3 * D_MODEL, _sigmoid)):
        for c in range(width // CW):
            r = _dot(hn, w_ref[:, off + c * CW:off + (c + 1) * CW])
            dst[:, c * CW:(c + 1) * CW] = act(r).astype(BF16)
        off += width


def _proj_gate(hn, l, w, tm):
    n = hn.shape[0]
    widths = (G_VWP, S_W, 3 * D_MODEL)
    return pl.pallas_call(
        _proj_gate_kernel, grid=(n // tm,),
        in_specs=[_rows(tm, D_MODEL), _of_layer(l, (D_MODEL, sum(widths)))],
        out_specs=[_rows(tm, wd) for wd in widths],
        out_shape=[jax.ShapeDtypeStruct((n, wd), BF16) for wd in widths],
        compiler_params=_params("arbitrary"), name="proj_gate",
    )(hn, w)


def _segment_cumsum(x, seg):
    pos = lax.broadcasted_iota(jnp.int32, x.shape, 0) % seg
    d = 1
    while d < seg:
        x = x + jnp.where(pos >= d, pltpu.roll(x, d, 0), 0.0)
        d *= 2
    return x


def _proj_aux_kernel(hn_ref, w_ref, gb_ref, wal_ref, bal_ref, cu_ref, gm_ref, gmt_ref, gcum_ref, *,
                     m_chunk, g_chunk):
    r = _dot(hn_ref[...], w_ref[...])
    for j in range(S_BLOCKS):
        cu_ref[j] = r[:, j * LANES:(j + 1) * LANES]
    raw = r[:, S_W:]
    gates = raw + gb_ref[...]
    lane = lax.broadcasted_iota(jnp.int32, gates.shape, 1)
    fcum = _segment_cumsum(_log_sigmoid(gates), m_chunk)
    gm = jnp.where((lane >= M_HEADS) & (lane < 2 * M_HEADS), fcum, gates)
    gm_ref[...] = gm
    gmt_ref[...] = gm.T
    za = _dot(raw.astype(BF16), wal_ref[...]) + bal_ref[...]
    gcum_ref[...] = _segment_cumsum(_log_sigmoid(za) * (1.0 / G_TAU), g_chunk)


def _proj_aux(hn, l, w, gb, wal, bal, tm, m_chunk, g_chunk):
    n = hn.shape[0]
    return pl.pallas_call(
        functools.partial(_proj_aux_kernel, m_chunk=m_chunk, g_chunk=g_chunk), grid=(n // tm,),
        in_specs=[_rows(tm, D_MODEL), _of_layer(l, (D_MODEL, SMALL_W)), _of_layer(l, (1, LANES)),
                  _of_layer(l, (LANES, G_KWP)), _of_layer(l, (1, G_KWP))],
        out_specs=[pl.BlockSpec((S_BLOCKS, tm, LANES), lambda i: (0, i, 0)), _rows(tm, LANES),
                   pl.BlockSpec((LANES, tm), lambda i: (0, i)), _rows(tm, G_KWP)],
        out_shape=[jax.ShapeDtypeStruct((S_BLOCKS, n, LANES), F32),
                   jax.ShapeDtypeStruct((n, LANES), F32),
                   jax.ShapeDtypeStruct((LANES, n), F32),
                   jax.ShapeDtypeStruct((n, G_KWP), F32)],
        compiler_params=_params("arbitrary"), name="proj_aux",
    )(hn, w, gb, wal, bal)


def _mlstm_kernel(q_ref, k_ref, v_ref, oz_ref, gm_ref, gmt_ref, nw_ref, out_ref, c_scr, m_scr, *, tc):
    @pl.when(pl.program_id(1) == 0)
    def _():
        c_scr[...] = jnp.zeros_like(c_scr)
        m_scr[...] = jnp.zeros_like(m_scr)

    row = lax.broadcasted_iota(jnp.int32, (tc, tc), 0)
    col = lax.broadcasted_iota(jnp.int32, (tc, tc), 1)
    causal = col <= row
    real = lax.broadcasted_iota(jnp.int32, (tc, M_HDP), 1) < M_HD

    for h in range(M_HEADS):
        sl = slice(h * M_HDP, (h + 1) * M_HDP)
        qb = q_ref[:, sl]
        kb = k_ref[:, sl]
        vext = v_ref[:, sl]
        i_col = gm_ref[:, h:h + 1]
        bcum_col = gm_ref[:, M_HEADS + h:M_HEADS + h + 1]
        g_row = gmt_ref[h:h + 1, :] - gmt_ref[M_HEADS + h:M_HEADS + h + 1, :]
        m_prev = m_scr[h, 0:1, 0:1]

        inter = bcum_col + m_prev
        m_t = jnp.maximum(inter, bcum_col + jnp.max(jnp.where(causal, g_row, NEG), axis=1, keepdims=True))
        w_intra = jnp.exp(jnp.where(causal, g_row + (bcum_col - m_t), NEG))
        w_inter = jnp.exp(inter - m_t)
        s = _dot_nt(qb, kb) * w_intra
        c_mat = c_scr[h]
        num = _dot(s.astype(BF16), vext) + w_inter * _dot(qb, c_mat.astype(BF16))
        den = num[:, M_HD:M_HD + 1]
        rden = 1.0 / jnp.maximum(jnp.abs(den), jnp.exp(-m_t))

        b_end = bcum_col[tc - 1:tc, :]
        to_end = b_end - bcum_col + i_col
        m_new = jnp.maximum(b_end + m_prev, jnp.max(to_end, axis=0, keepdims=True))
        wk = jnp.exp(to_end - m_new)
        w_prev = jnp.exp(b_end + m_prev - m_new)
        c_scr[h] = w_prev * c_mat + _dot_tn((wk * kb.astype(F32)).astype(BF16), vext)
        m_scr[h] = jnp.broadcast_to(m_new, (8, LANES))

        mean = jnp.sum(jnp.where(real, num, 0.0), axis=1, keepdims=True) * (1.0 / M_HD)
        xc = jnp.where(real, num - mean, 0.0)
        var = jnp.sum(xc * xc, axis=1, keepdims=True) * (1.0 / M_HD)
        scale = rden * lax.rsqrt(rden * rden * var + EPS)
        out_ref[:, sl] = (xc * scale * nw_ref[:, sl] * oz_ref[:, sl].astype(F32)).astype(BF16)


def _mlstm(q, k, v, oz, gm, gmt, l, nw, batch, seq, tc):
    n = batch * seq
    nt = seq // tc
    blk = pl.BlockSpec((tc, M_WP), lambda b, t: (b * nt + t, 0))
    return pl.pallas_call(
        functools.partial(_mlstm_kernel, tc=tc),
        grid=(batch, nt),
        in_specs=[blk, blk, blk, blk, pl.BlockSpec((tc, LANES), lambda b, t: (b * nt + t, 0)),
                  pl.BlockSpec((LANES, tc), lambda b, t: (0, b * nt + t)),
                  _of_layer(l, (1, M_WP))],
        out_specs=blk,
        out_shape=jax.ShapeDtypeStruct((n, M_WP), BF16),
        scratch_shapes=[pltpu.VMEM((M_HEADS, M_HDP, M_HDP), F32),
                        pltpu.VMEM((M_HEADS, 8, LANES), F32)],
        compiler_params=_params("arbitrary", "arbitrary"),
        name="mlstm",
    )(q, k, v, oz, gm, gmt, nw)


def _gla_kernel(q_ref, k_ref, v_ref, z_ref, b_ref, nw_ref,
                out_ref, kbuf, bbuf, attn_scr, st_scr, *, t, cs):
    @pl.when(pl.program_id(1) == 0)
    def _():
        st_scr[...] = jnp.zeros_like(st_scr)

    row = lax.broadcasted_iota(jnp.int32, (t, t), 0)
    col = lax.broadcasted_iota(jnp.int32, (t, t), 1)
    causal = col <= row

    def operands(h):
        sl = slice(h * G_DKP, (h + 1) * G_DKP)
        return q_ref[:, sl].astype(F32), k_ref[:, sl].astype(F32), b_ref[:, sl]

    safe = jnp.min(b_ref[t - 1:t, :]) >= G_SAFE_LOG_DECAY

    @pl.when(safe)
    def _():
        for h in range(G_HEADS):
            qs, kk, b = operands(h)
            a = _dot_nt((qs * jnp.exp(b)).astype(BF16), (kk * jnp.exp(-b)).astype(BF16))
            attn_scr[h] = jnp.where(causal, a, 0.0)

    @pl.when(jnp.logical_not(safe))
    def _():
        rowv = lax.broadcasted_iota(jnp.int32, (t, G_DKP), 0)
        sub = rowv % cs
        lag = row - col
        kbuf[0:cs, :] = jnp.zeros((cs, G_DKP), F32)
        bbuf[0:cs, :] = jnp.zeros((cs, G_DKP), F32)
        for h in range(G_HEADS):
            qs, kk, b = operands(h)
            qparts, kparts = [], []
            for i in range(1, t // cs):
                r = b[i * cs - 1:i * cs, :]
                inblk = (rowv >= i * cs) & (rowv < (i + 1) * cs)
                qparts.append((qs * jnp.exp(jnp.where(inblk, b - r, NEG))).astype(BF16))
                kparts.append((kk * jnp.exp(jnp.where(rowv < i * cs, r - b, NEG))).astype(BF16))
            attn = _dot_nt(jnp.concatenate(qparts, axis=1), jnp.concatenate(kparts, axis=1))
            kbuf[cs:cs + t, :] = kk
            bbuf[cs:cs + t, :] = b
            for j in range(cs):
                ksh = kbuf[cs - j:cs - j + t, :]
                bsh = bbuf[cs - j:cs - j + t, :]
                e = jnp.exp(jnp.where(sub >= j, b - bsh, NEG))
                dj = jnp.sum(qs * ksh * e, axis=1, keepdims=True)
                attn = attn + jnp.where(lag == j, dj, 0.0)
            attn_scr[h] = attn

    for h in range(G_HEADS):
        slv = slice(h * G_DVP, (h + 1) * G_DVP)
        qs, kk, b = operands(h)
        v = v_ref[:, slv]
        blast = b[t - 1:t, :]
        st = st_scr[h]
        o = (_dot_nt((qs * jnp.exp(b)).astype(BF16), st.astype(BF16))
             + _dot(attn_scr[h].astype(BF16), v))
        kdec = (kk * jnp.exp(blast - b)).astype(BF16)
        st_scr[h] = jnp.exp(blast) * st + _dot_tn(v, kdec)

        var = jnp.sum(o * o, axis=1, keepdims=True) * (1.0 / G_DV)
        hb = o * lax.rsqrt(var + EPS) * nw_ref[:, slv]
        out_ref[:, slv] = (hb * z_ref[:, slv].astype(F32)).astype(BF16)


def _gla(q, k, v, z, bcum, l, nw, batch, seq, t, cs):
    n = batch * seq
    nt = seq // t

    def blk(width):
        return pl.BlockSpec((t, width), lambda b, c: (b * nt + c, 0))

    return pl.pallas_call(
        functools.partial(_gla_kernel, t=t, cs=cs),
        grid=(batch, nt),
        in_specs=[blk(G_KWP), blk(G_KWP), blk(G_VWP), blk(G_VWP), blk(G_KWP),
                  _of_layer(l, (1, G_VWP))],
        out_specs=blk(G_VWP),
        out_shape=jax.ShapeDtypeStruct((n, G_VWP), BF16),
        scratch_shapes=[pltpu.VMEM((t + cs, G_DKP), F32), pltpu.VMEM((t + cs, G_DKP), F32),
                        pltpu.VMEM((G_HEADS, t, t), F32),
                        pltpu.VMEM((G_HEADS, G_DVP, G_DKP), F32)],
        compiler_params=_params("arbitrary", "arbitrary"),
        name="gla",
    )(q, k, v, z, bcum, nw)


def _gelu_tanh(x):
    return 0.5 * x * (1.0 + jnp.tanh(math.sqrt(2.0 / math.pi) * (x + 0.044715 * (x * x * x))))


def _s5_kernel(u_ref, kms_ref, mo_ref, p_ref, d_ref, y_ref, carry, *, nch):
    @pl.when(pl.program_id(2) == 0)
    def _():
        carry[...] = jnp.zeros_like(carry)

    qw = S_Q * LANES
    ucat = jnp.concatenate([u_ref[0, pl.ds(s, nch, stride=S_Q), :] for s in range(S_Q)],
                           axis=1).astype(BF16)
    r = _dot(ucat, kms_ref[0])
    y = r[:, :qw]
    xr = r[:, qw:qw + S_SW]
    xi = r[:, qw + S_SW:]

    rowi = lax.broadcasted_iota(jnp.int32, (nch, S_SW), 0)
    d = 1
    while d < nch:
        pr = p_ref[0, d - 1:d, :S_SW]
        pi = p_ref[0, d - 1:d, S_SW:]
        keep = rowi >= d
        sr = jnp.where(keep, pltpu.roll(xr, d, 0), 0.0)
        si = jnp.where(keep, pltpu.roll(xi, d, 0), 0.0)
        xr, xi = xr + (pr * sr - pi * si), xi + (pr * si + pi * sr)
        d *= 2
    cr = carry[0:1, :S_SW]
    ci = carry[0:1, S_SW:]
    pr = p_ref[0, :, :S_SW]
    pi = p_ref[0, :, S_SW:]
    xr, xi = xr + (pr * cr - pi * ci), xi + (pr * ci + pi * cr)
    first = rowi == 0
    xpr = jnp.where(first, cr, pltpu.roll(xr, 1, 0))
    xpi = jnp.where(first, ci, pltpu.roll(xi, 1, 0))
    carry[0:1, :S_SW] = xr[nch - 1:nch, :]
    carry[0:1, S_SW:] = xi[nch - 1:nch, :]

    y = y + _dot(jnp.concatenate([xpr, xpi], axis=1).astype(BF16), mo_ref[0])
    for t in range(S_Q):
        y_ref[0, pl.ds(t, nch, stride=S_Q), :] = y[:, t * LANES:(t + 1) * LANES]
    y_ref[0] = _gelu_tanh(y_ref[0] + d_ref[0] * u_ref[0])


def _s5(cu, l, kms, mo, ptab, dskip, batch, seq, tm):
    n = batch * seq
    nt = seq // tm
    nch = tm // S_Q
    qw = S_Q * LANES

    def table(rows, cols):
        return pl.BlockSpec((None, 1, rows, cols), lambda j, b, t: (l, j, 0, 0))

    return pl.pallas_call(
        functools.partial(_s5_kernel, nch=nch),
        grid=(S_BLOCKS, batch, nt),
        in_specs=[pl.BlockSpec((1, tm, LANES), lambda j, b, t: (j, b * nt + t, 0)),
                  table(qw, qw + 2 * S_SW), table(2 * S_SW, qw), table(nch, 2 * S_SW), table(1, LANES)],
        out_specs=pl.BlockSpec((1, tm, LANES), lambda j, b, t: (j, b * nt + t, 0)),
        out_shape=jax.ShapeDtypeStruct((S_BLOCKS, n, LANES), F32),
        scratch_shapes=[pltpu.VMEM((8, 2 * S_SW), F32)],
        compiler_params=_params("arbitrary", "arbitrary", "arbitrary"),
        name="s5",
    )(cu, kms, mo, ptab, dskip)


def _s5_expand_kernel(kt_ref, msr_ref, msi_ref, mor_ref, moi_ref, kms_ref, mo_ref):
    q = S_Q
    qw = q * LANES

    def embed(m, rows, width):
        shape = (width, S_GPB * width)
        tile = (lax.broadcasted_iota(jnp.int32, shape, 1) % width
                == lax.broadcasted_iota(jnp.int32, shape, 0)).astype(BF16)
        oshape = (S_GPB * rows, S_GPB * width)
        diag = (lax.broadcasted_iota(jnp.int32, oshape, 0) // rows
                == lax.broadcasted_iota(jnp.int32, oshape, 1) // width)
        return jnp.where(diag, _dot(m, tile), 0.0).astype(BF16)

    kd = [embed(kt_ref[0, 0, tau], S_GROUP, S_GROUP) for tau in range(q)]
    zero = jnp.zeros((LANES, LANES), BF16)
    for s in range(q):
        rows = slice(s * LANES, (s + 1) * LANES)
        for t in range(q):
            kms_ref[0, 0, rows, t * LANES:(t + 1) * LANES] = kd[t - s] if t >= s else zero
        kms_ref[0, 0, rows, qw:qw + S_SW] = embed(msr_ref[0, 0, s], S_GROUP, S_STATE)
        kms_ref[0, 0, rows, qw + S_SW:] = embed(msi_ref[0, 0, s], S_GROUP, S_STATE)
    for t in range(q):
        cols = slice(t * LANES, (t + 1) * LANES)
        mo_ref[0, 0, 0:S_SW, cols] = embed(mor_ref[0, 0, t], S_STATE, S_GROUP)
        mo_ref[0, 0, S_SW:, cols] = embed(moi_ref[0, 0, t], S_STATE, S_GROUP)


def _s5_expand(kt, msr, msi, mor, moi):
    depth = kt.shape[0]
    qw = S_Q * LANES

    def blk(a):
        return pl.BlockSpec((1, 1) + a.shape[2:], lambda d, j: (d, j, 0, 0, 0))

    return pl.pallas_call(
        _s5_expand_kernel, grid=(depth, S_BLOCKS),
        in_specs=[blk(a) for a in (kt, msr, msi, mor, moi)],
        out_specs=[pl.BlockSpec((1, 1, qw, qw + 2 * S_SW), lambda d, j: (d, j, 0, 0)),
                   pl.BlockSpec((1, 1, 2 * S_SW, qw), lambda d, j: (d, j, 0, 0))],
        out_shape=[jax.ShapeDtypeStruct((depth, S_BLOCKS, qw, qw + 2 * S_SW), BF16),
                   jax.ShapeDtypeStruct((depth, S_BLOCKS, 2 * S_SW, qw), BF16)],
        compiler_params=_params("arbitrary", "arbitrary"), name="s5_expand",
    )(kt, msr, msi, mor, moi)


def _s5_tables(lam_re, lam_im, log_dt, b_re, b_im, c_re, c_im, nch):
    q = S_Q
    dep = lam_re.shape[0]
    lr = jnp.minimum(lam_re.astype(F32), -1e-4)
    li = lam_im.astype(F32)
    dt = jnp.exp(log_dt.astype(F32))[..., None]
    mag = jnp.exp(lr * dt)
    ab_re = mag * jnp.cos(li * dt)
    ab_im = mag * jnp.sin(li * dt)
    nr = ab_re - 1.0
    den = lr * lr + li * li
    coef_re = ((nr * lr + ab_im * li) / den)[:, :, None, :]
    coef_im = ((ab_im * lr - nr * li) / den)[:, :, None, :]
    brt = jnp.swapaxes(b_re.astype(F32), -1, -2)
    bit = jnp.swapaxes(b_im.astype(F32), -1, -2)
    bb_re = coef_re * brt - coef_im * bit
    bb_im = coef_re * bit + coef_im * brt

    def apow(nvals):
        e = jnp.asarray(nvals, F32)[:, None, None, None]
        m = jnp.exp(e * (lr * dt))
        ang = e * (li * dt)
        return m * jnp.cos(ang), m * jnp.sin(ang)

    def per_block(m, rows):
        m = m.reshape(q, dep, S_BLOCKS, S_GPB * rows, m.shape[-1])
        return m.transpose(1, 2, 0, 3, 4).astype(BF16)

    ar, ai = apow(list(range(q + 1)))
    cr, ci = c_re.astype(F32), c_im.astype(F32)
    ca_re = cr[None] * ar[:, :, :, None, :] - ci[None] * ai[:, :, :, None, :]
    ca_im = cr[None] * ai[:, :, :, None, :] + ci[None] * ar[:, :, :, None, :]
    kt = jnp.sum(ca_re[:q, :, :, None, :, :] * bb_re[None, :, :, :, None, :]
                 - ca_im[:q, :, :, None, :, :] * bb_im[None, :, :, :, None, :], axis=-1)
    pw_r = ar[:q][::-1][:, :, :, None, :]
    pw_i = ai[:q][::-1][:, :, :, None, :]
    kms, mo = _s5_expand(per_block(kt, S_GROUP),
                         per_block(pw_r * bb_re - pw_i * bb_im, S_GROUP),
                         per_block(pw_r * bb_im + pw_i * bb_re, S_GROUP),
                         per_block(jnp.swapaxes(ca_re[1:], -1, -2), S_STATE),
                         per_block(jnp.swapaxes(-ca_im[1:], -1, -2), S_STATE))
    pr, pi = apow([q * (c + 1) for c in range(nch)])
    pr = pr.reshape(nch, dep, S_BLOCKS, S_SW).transpose(1, 2, 0, 3)
    pi = pi.reshape(nch, dep, S_BLOCKS, S_SW).transpose(1, 2, 0, 3)
    ptab = jnp.concatenate([pr, pi], axis=3)
    return kms, mo, ptab


def _merge_kernel(ya_ref, yb_ref, yc_ref, cz_ref, g_ref, x_ref, wa_ref, wb_ref, wglu_ref, wc_ref,
                  wo_ref, nw_ref, *out_refs, final):
    yc0 = jnp.concatenate([yc_ref[j] for j in range(S_BLOCKS)], axis=1)
    glu = _dot(yc0.astype(BF16), wglu_ref[...])
    yc = yc0 * _sigmoid(glu) * cz_ref[...].astype(F32)
    merged = (g_ref[:, 0:D_MODEL].astype(F32) * _dot(ya_ref[...], wa_ref[...])
              + g_ref[:, D_MODEL:2 * D_MODEL].astype(F32) * _dot(yb_ref[...], wb_ref[...])
              + g_ref[:, 2 * D_MODEL:].astype(F32) * _dot(yc.astype(BF16), wc_ref[...]))
    out = x_ref[...] + _dot(merged.astype(BF16), wo_ref[...])
    ms = jnp.mean(out * out, axis=-1, keepdims=True)
    normed = out * lax.rsqrt(ms + EPS) * nw_ref[...]
    if final:
        out_refs[0][...] = normed
    else:
        out_refs[0][...] = out
        out_refs[1][...] = normed.astype(BF16)


def _merge(ya, yb, yc, cz, g, x2, l, wa, wb, wglu, wc, wo, nw, tm, final):
    n = x2.shape[0]
    x_out = jax.ShapeDtypeStruct((n, D_MODEL), F32)
    hn_out = jax.ShapeDtypeStruct((n, D_MODEL), BF16)
    return pl.pallas_call(
        functools.partial(_merge_kernel, final=final),
        grid=(n // tm,),
        in_specs=[_rows(tm, M_WP), _rows(tm, G_VWP),
                  pl.BlockSpec((S_BLOCKS, tm, LANES), lambda i: (0, i, 0)),
                  _rows(tm, S_W), _rows(tm, 3 * D_MODEL), _rows(tm, D_MODEL),
                  _of_layer(l, (M_WP, D_MODEL)), _of_layer(l, (G_VWP, D_MODEL)), _of_layer(l, (S_W, S_W)),
                  _of_layer(l, (S_W, D_MODEL)), _of_layer(l, (D_MODEL, D_MODEL)), _whole((1, D_MODEL))],
        out_specs=[_rows(tm, D_MODEL)] if final else [_rows(tm, D_MODEL), _rows(tm, D_MODEL)],
        out_shape=[x_out] if final else [x_out, hn_out],
        compiler_params=_params("arbitrary"),
        name="merge",
    )(ya, yb, yc, cz, g, x2, wa, wb, wglu, wc, wo, nw)


def _pad_heads(w, heads, d, dp):
    zeros = jnp.zeros(w.shape[:-1] + (dp - d,), w.dtype)
    parts = []
    for h in range(heads):
        parts += [w[..., h * d:(h + 1) * d], zeros]
    return jnp.concatenate(parts, axis=-1)


def _pad_head_rows(w, heads, d, dp):
    zeros = jnp.zeros((w.shape[0], dp - d, w.shape[2]), w.dtype)
    parts = []
    for h in range(heads):
        parts += [w[:, h * d:(h + 1) * d], zeros]
    return jnp.concatenate(parts, axis=1)


def _prepare(p, nch):
    depth = p["w_in"].shape[0]
    w_mlstm, w_plain, w_gate, w_small = _wprep(jnp.swapaxes(p["w_in"], 1, 2), LANES)
    mp = functools.partial(_pad_heads, heads=M_HEADS, d=M_HD, dp=M_HDP)
    gkp = functools.partial(_pad_heads, heads=G_HEADS, d=G_DK, dp=G_DKP)
    gvp = functools.partial(_pad_heads, heads=G_HEADS, d=G_DV, dp=G_DVP)
    kms, mo, ptab = _s5_tables(p["s5_lam_re"], p["s5_lam_im"], p["s5_log_dt"], p["s5_B_re"], p["s5_B_im"],
                               p["s5_C_re"], p["s5_C_im"], nch)
    gate_b = p["mlstm_gate_b"]
    return dict(
        w_mlstm=w_mlstm, w_plain=w_plain, w_gate=w_gate, w_small=w_small,
        cq=mp(p["mlstm_conv"][:, :, :M_W]), ck=mp(p["mlstm_conv"][:, :, M_W:]),
        gb=jnp.concatenate([gate_b[:, 0], gate_b[:, 1],
                            jnp.zeros((depth, LANES - 2 * M_HEADS), F32)], axis=1)[:, None, :],
        m_nw=mp(p["mlstm_norm"])[:, None, :],
        wal=jnp.concatenate([jnp.zeros((depth, 2 * M_HEADS, G_KWP), F32), gkp(p["gla_w_alpha"]),
                             jnp.zeros((depth, LANES - 2 * M_HEADS - G_RANK, G_KWP), F32)],
                            axis=1).astype(BF16),
        bal=gkp(p["gla_b_alpha"])[:, None, :],
        g_nw=gvp(p["gla_norm"])[:, None, :],
        kms=kms, mo=mo, ptab=ptab,
        dskip=p["s5_D"].reshape(depth, S_BLOCKS, 1, LANES),
        wa=_pad_head_rows(p["w_branch_mlstm"].astype(BF16), M_HEADS, M_HD, M_HDP),
        wb=_pad_head_rows(p["w_branch_gla"].astype(BF16), G_HEADS, G_DV, G_DVP),
        wglu=p["s5_w_glu"].astype(BF16), wc=p["w_branch_s5"].astype(BF16), wo=p["w_out"].astype(BF16),
    )


def _tiles(batch, seq):
    return dict(proj=min(1024, seq), m_chunk=min(256, seq), g_chunk=min(128, seq), g_sub=16,
                s5=min(4096, seq), merge=min(512, batch * seq))


def _layer(x2, hn, batch, seq, l, w, next_norm, final, tl):
    tm = tl["proj"]
    q, k, oz = _proj_mlstm(hn, l, w["w_mlstm"], w["cq"], w["ck"], seq, tm)
    mv, gv, gq, gk = _proj_plain(hn, l, w["w_plain"], tm)
    gz, czs, gs = _proj_gate(hn, l, w["w_gate"], tm)
    cu_s, gm, gmt, gcum = _proj_aux(hn, l, w["w_small"], w["gb"], w["wal"], w["bal"], tm,
                                    tl["m_chunk"], tl["g_chunk"])
    ya = _mlstm(q, k, mv, oz, gm, gmt, l, w["m_nw"], batch, seq, tl["m_chunk"])
    yb = _gla(gq, gk, gv, gz, gcum, l, w["g_nw"], batch, seq, tl["g_chunk"], tl["g_sub"])
    yc = _s5(cu_s, l, w["kms"], w["mo"], w["ptab"], w["dskip"], batch, seq, tl["s5"])
    return _merge(ya, yb, yc, czs, gs, x2, l, w["wa"], w["wb"], w["wglu"], w["wc"], w["wo"],
                  next_norm.reshape(1, D_MODEL), tl["merge"], final)


def kernel(x, norm_w, w_in, mlstm_conv, mlstm_gate_b, mlstm_norm, gla_w_alpha, gla_b_alpha, gla_norm,
           s5_lam_re, s5_lam_im, s5_log_dt, s5_B_re, s5_B_im, s5_C_re, s5_C_im, s5_D, s5_w_glu,
           w_branch_mlstm, w_branch_gla, w_branch_s5, w_out, final_norm):
    batch, seq, _ = x.shape
    depth = norm_w.shape[0]
    tl = _tiles(batch, seq)
    prep = _prepare(dict(w_in=w_in, mlstm_conv=mlstm_conv, mlstm_gate_b=mlstm_gate_b, mlstm_norm=mlstm_norm,
                         gla_w_alpha=gla_w_alpha, gla_b_alpha=gla_b_alpha, gla_norm=gla_norm,
                         s5_lam_re=s5_lam_re, s5_lam_im=s5_lam_im, s5_log_dt=s5_log_dt, s5_B_re=s5_B_re,
                         s5_B_im=s5_B_im, s5_C_re=s5_C_re, s5_C_im=s5_C_im, s5_D=s5_D, s5_w_glu=s5_w_glu,
                         w_branch_mlstm=w_branch_mlstm, w_branch_gla=w_branch_gla,
                         w_branch_s5=w_branch_s5, w_out=w_out), tl["s5"] // S_Q)
    x2 = x.reshape(batch * seq, D_MODEL)
    hn = _rms(x2, norm_w[0].reshape(1, D_MODEL), min(1024, batch * seq))
    for l in range(depth):
        final = l == depth - 1
        res = _layer(x2, hn, batch, seq, l, prep, final_norm if final else norm_w[l + 1], final, tl)
        if final:
            x2 = res[0]
        else:
            x2, hn = res
    return x2.reshape(batch, seq, D_MODEL)
```

```python
import functools
import math

import jax
import jax.numpy as jnp
from jax import lax
from jax.experimental import pallas as pl
from jax.experimental.pallas import tpu as pltpu

F32 = jnp.float32
BF16 = jnp.bfloat16
HIGHEST = lax.Precision.HIGHEST

EPS = 1e-6
D_MODEL = 1024
LANES = 128
M_HEADS = 4
M_HD = 192
M_HDP = 256
M_W = M_HEADS * M_HD
M_WP = M_HEADS * M_HDP
M_CONV = 4
G_HEADS = 4
G_DK = 96
G_DKP = 128
G_DV = 192
G_DVP = 256
G_KW = G_HEADS * G_DK
G_KWP = G_HEADS * G_DKP
G_VW = G_HEADS * G_DV
G_VWP = G_HEADS * G_DVP
G_RANK = 16
G_TAU = 16.0
S_W = 512
S_GROUP = 16
S_GROUPS = 32
S_STATE = 64
S_BLOCKS = S_W // LANES
S_GPB = LANES // S_GROUP
S_SW = S_GPB * S_STATE
S_Q = 8

IN_SIZES = (M_W, M_W, M_W, M_W, M_HEADS, M_HEADS, M_W,
            G_KW, G_KW, G_VW, G_RANK, G_VW,
            S_W, S_W, 3 * D_MODEL)
SMALL_W = S_W + LANES
CW = 512

NEG = -1e30
G_SAFE_LOG_DECAY = -60.0
VMEM_LIMIT = 56 * 1024 * 1024


def _sigmoid(x):
    return 0.5 * jnp.tanh(0.5 * x) + 0.5


def _silu(x):
    h = 0.5 * x
    return h * jnp.tanh(h) + h


def _log_sigmoid(x):
    return jnp.minimum(x, 0.0) - jnp.log1p(jnp.exp(-jnp.abs(x)))


def _dot(a, b):
    return jnp.dot(a, b, preferred_element_type=F32)


def _dot_nt(a, b):
    return lax.dot_general(a, b, (((1,), (1,)), ((), ())), preferred_element_type=F32)


def _dot_tn(a, b):
    return lax.dot_general(a, b, (((0,), (0,)), ((), ())), preferred_element_type=F32)


def _params(*sem):
    return pltpu.CompilerParams(dimension_semantics=sem, vmem_limit_bytes=VMEM_LIMIT)


def _rows(tm, width):
    return pl.BlockSpec((tm, width), lambda i: (i, 0))


def _whole(shape):
    return pl.BlockSpec(shape, lambda i: (0,) * len(shape))


def _of_layer(l, shape):
    return pl.BlockSpec((None,) + shape, lambda *_: (l,) + (0,) * len(shape))


def _rms_kernel(x_ref, nw_ref, o_ref):
    x = x_ref[...]
    ms = jnp.mean(x * x, axis=-1, keepdims=True)
    o_ref[...] = (x * lax.rsqrt(ms + EPS) * nw_ref[...]).astype(BF16)


def _rms(x2, nw, tm):
    n = x2.shape[0]
    return pl.pallas_call(
        _rms_kernel, grid=(n // tm,),
        in_specs=[_rows(tm, D_MODEL), _whole((1, D_MODEL))],
        out_specs=_rows(tm, D_MODEL),
        out_shape=jax.ShapeDtypeStruct((n, D_MODEL), BF16),
        compiler_params=_params("arbitrary"), name="rms",
    )(x2, nw)


W_OFF = {}
_o = 0
for _name, _size in zip(("aq", "ak", "av", "ao", "ai", "af", "az", "bq", "bk", "bv", "ba", "bz", "cu", "cz", "g"),
                        IN_SIZES):
    W_OFF[_name] = _o
    _o += _size
IN_WIDTH = _o
W_MLSTM, W_PLAIN, W_GATE = 4 * M_WP, M_WP + G_VWP + 2 * G_KWP, G_VWP + S_W + 3 * D_MODEL


def _wprep_kernel(w_ref, wm_ref, wp_ref, wg_ref, ws_ref):
    kb = w_ref.shape[2]

    def put(dst, dst_off, parts, width):
        xs = [w_ref[0, off:off + rows, :] for off, rows in parts]
        used = sum(rows for _, rows in parts)
        if used < width:
            xs.append(jnp.zeros((width - used, kb), F32))
        x = xs[0] if len(xs) == 1 else jnp.concatenate(xs, axis=0)
        dst[0, :, dst_off:dst_off + width] = x.T.astype(BF16)

    def heads(dst, dst_off, name, n_heads, d, dp):
        for h in range(n_heads):
            put(dst, dst_off + h * dp, [(W_OFF[name] + h * d, d)], dp)

    def plain(dst, dst_off, name, width):
        for c in range(0, width, CW):
            put(dst, dst_off + c, [(W_OFF[name] + c, CW)], CW)

    for i, name in enumerate(("aq", "ak", "ao", "az")):
        heads(wm_ref, i * M_WP, name, M_HEADS, M_HD, M_HDP)
    heads(wp_ref, 0, "av", M_HEADS, M_HD, M_HDP)
    heads(wp_ref, M_WP, "bv", G_HEADS, G_DV, G_DVP)
    heads(wp_ref, M_WP + G_VWP, "bq", G_HEADS, G_DK, G_DKP)
    heads(wp_ref, M_WP + G_VWP + G_KWP, "bk", G_HEADS, G_DK, G_DKP)
    heads(wg_ref, 0, "bz", G_HEADS, G_DV, G_DVP)
    plain(wg_ref, G_VWP, "cz", S_W)
    plain(wg_ref, G_VWP + S_W, "g", 3 * D_MODEL)
    plain(ws_ref, 0, "cu", S_W)
    put(ws_ref, S_W, [(W_OFF["ai"], 2 * M_HEADS), (W_OFF["ba"], G_RANK)], LANES)


def _wprep(w_t, kb):
    depth = w_t.shape[0]
    widths = (W_MLSTM, W_PLAIN, W_GATE, SMALL_W)
    return pl.pallas_call(
        _wprep_kernel, grid=(depth, D_MODEL // kb),
        in_specs=[pl.BlockSpec((1, IN_WIDTH, kb), lambda l, i: (l, 0, i))],
        out_specs=[pl.BlockSpec((1, kb, wd), lambda l, i: (l, i, 0)) for wd in widths],
        out_shape=[jax.ShapeDtypeStruct((depth, D_MODEL, wd), BF16) for wd in widths],
        compiler_params=_params("arbitrary", "arbitrary"), name="wprep",
    )(w_t)


def _proj_mlstm_kernel(hn_ref, w_ref, cq_ref, ck_ref, q_ref, k_ref, oz_ref, halo_x, halo_b, *, tm,
                       tiles_per_seq):
    @pl.when(pl.program_id(0) % tiles_per_seq == 0)
    def _():
        halo_x[...] = jnp.zeros_like(halo_x)
        halo_b[...] = jnp.zeros_like(halo_b)

    hn = hn_ref[...]
    row8 = lax.broadcasted_iota(jnp.int32, (8, CW), 0)

    def shift(x, prev8, d):
        rolled = pltpu.roll(x, d, 0)
        head = jnp.where(row8 < d, pltpu.roll(prev8, d, 0), rolled[0:8])
        return jnp.concatenate([head, rolled[8:]], axis=0)

    nchunk = M_WP // CW
    for c in range(2 * nchunk):
        cw_ref, dst, scale = (cq_ref, q_ref, 1.0) if c < nchunk else (ck_ref, k_ref, M_HD ** -0.5)
        lc = slice((c % nchunk) * CW, (c % nchunk + 1) * CW)
        x = _dot(hn, w_ref[:, c * CW:(c + 1) * CW])
        x1 = shift(x, halo_x[c], 1)
        pair = cw_ref[1:2, lc] * x + cw_ref[0:1, lc] * x1
        y = _silu(cw_ref[3:4, lc] * x + cw_ref[2:3, lc] * x1 + shift(pair, halo_b[c], 2))
        halo_x[c] = x[tm - 8:tm]
        halo_b[c] = pair[tm - 8:tm]
        dst[:, lc] = (y if scale == 1.0 else y * scale).astype(BF16)
    for c in range(nchunk):
        lc = slice(c * CW, (c + 1) * CW)
        o = _dot(hn, w_ref[:, 2 * M_WP + c * CW:2 * M_WP + (c + 1) * CW])
        z = _dot(hn, w_ref[:, 3 * M_WP + c * CW:3 * M_WP + (c + 1) * CW])
        oz_ref[:, lc] = (_sigmoid(o) * _silu(z)).astype(BF16)


def _proj_mlstm(hn, l, w, cq, ck, seq, tm):
    n = hn.shape[0]
    out = jax.ShapeDtypeStruct((n, M_WP), BF16)
    return pl.pallas_call(
        functools.partial(_proj_mlstm_kernel, tm=tm, tiles_per_seq=seq // tm),
        grid=(n // tm,),
        in_specs=[_rows(tm, D_MODEL), _of_layer(l, (D_MODEL, 4 * M_WP)), _of_layer(l, (M_CONV, M_WP)),
                  _of_layer(l, (M_CONV, M_WP))],
        out_specs=[_rows(tm, M_WP)] * 3,
        out_shape=[out] * 3,
        scratch_shapes=[pltpu.VMEM((2 * M_WP // CW, 8, CW), F32)] * 2,
        compiler_params=_params("arbitrary"), name="proj_mlstm",
    )(hn, w, cq, ck)


def _proj_plain_kernel(hn_ref, w_ref, mv_ref, gv_ref, gq_ref, gk_ref):
    hn = hn_ref[...]
    lane = lax.broadcasted_iota(jnp.int32, (hn.shape[0], CW), 1)
    ones_lane = lane % M_HDP >= M_HD
    off = 0
    for dst, width, kind in ((mv_ref, M_WP, "ones"), (gv_ref, G_VWP, None), (gq_ref, G_KWP, "scale"),
                             (gk_ref, G_KWP, None)):
        for c in range(width // CW):
            r = _dot(hn, w_ref[:, off + c * CW:off + (c + 1) * CW])
            if kind == "ones":
                r = jnp.where(ones_lane, 1.0, r)
            elif kind == "scale":
                r = r * (G_DK ** -0.5)
            dst[:, c * CW:(c + 1) * CW] = r.astype(BF16)
        off += width


def _proj_plain(hn, l, w, tm):
    n = hn.shape[0]
    widths = (M_WP, G_VWP, G_KWP, G_KWP)
    return pl.pallas_call(
        _proj_plain_kernel, grid=(n // tm,),
        in_specs=[_rows(tm, D_MODEL), _of_layer(l, (D_MODEL, sum(widths)))],
        out_specs=[_rows(tm, wd) for wd in widths],
        out_shape=[jax.ShapeDtypeStruct((n, wd), BF16) for wd in widths],
        compiler_params=_params("arbitrary"), name="proj_plain",
    )(hn, w)


def _proj_gate_kernel(hn_ref, w_ref, gz_ref, cz_ref, g_ref):
    hn = hn_ref[...]
    off = 0
    for dst, width, act in ((gz_ref, G_VWP, _silu), (cz_ref, S_W, _silu), (g_ref, 3 * D_MODEL, _sigmoid)):
        for c in range(width // CW):
            r = _dot(hn, w_ref[:, off + c * CW:off + (c + 1) * CW])
            dst[:, c * CW:(c + 1) * CW] = act(r).astype(BF16)
        off += width


def _proj_gate(hn, l, w, tm):
    n = hn.shape[0]
    widths = (G_VWP, S_W, 3 * D_MODEL)
    return pl.pallas_call(
        _proj_gate_kernel, grid=(n // tm,),
        in_specs=[_rows(tm, D_MODEL), _of_layer(l, (D_MODEL, sum(widths)))],
        out_specs=[_rows(tm, wd) for wd in widths],
        out_shape=[jax.ShapeDtypeStruct((n, wd), BF16) for wd in widths],
        compiler_params=_params("arbitrary"), name="proj_gate",
    )(hn, w)


def _segment_cumsum(x, seg):
    pos = lax.broadcasted_iota(jnp.int32, x.shape, 0) % seg
    d = 1
    while d < seg:
        x = x + jnp.where(pos >= d, pltpu.roll(x, d, 0), 0.0)
        d *= 2
    return x


def _proj_aux_kernel(hn_ref, w_ref, gb_ref, wal_ref, bal_ref, cu_ref, gmb_ref, gmt_ref, gcum_ref, *,
                     m_chunk, g_chunk):
    r = _dot(hn_ref[...], w_ref[...])
    for j in range(S_BLOCKS):
        cu_ref[j] = r[:, j * LANES:(j + 1) * LANES]
    raw = r[:, S_W:]
    gates = raw + gb_ref[...]
    lane = lax.broadcasted_iota(jnp.int32, gates.shape, 1)
    fcum = _segment_cumsum(_log_sigmoid(gates), m_chunk)
    gm = jnp.where((lane >= M_HEADS) & (lane < 2 * M_HEADS), fcum, gates)
    gmt_ref[...] = gm.T
    hi = gm.astype(BF16)
    r1 = gm - hi.astype(F32)
    mid = r1.astype(BF16)
    lo = (r1 - mid.astype(F32)).astype(BF16)
    slots = 2 * M_HEADS * LANES
    spread = (lax.broadcasted_iota(jnp.int32, (3 * LANES, slots), 0) % LANES
              == lax.broadcasted_iota(jnp.int32, (3 * LANES, slots), 1) // LANES).astype(BF16)
    gmb_ref[...] = _dot(jnp.concatenate([hi, mid, lo], axis=1), spread)
    za = _dot(raw.astype(BF16), wal_ref[...]) + bal_ref[...]
    gcum_ref[...] = _segment_cumsum(_log_sigmoid(za) * (1.0 / G_TAU), g_chunk)


def _proj_aux(hn, l, w, gb, wal, bal, tm, m_chunk, g_chunk):
    n = hn.shape[0]
    return pl.pallas_call(
        functools.partial(_proj_aux_kernel, m_chunk=m_chunk, g_chunk=g_chunk), grid=(n // tm,),
        in_specs=[_rows(tm, D_MODEL), _of_layer(l, (D_MODEL, SMALL_W)), _of_layer(l, (1, LANES)),
                  _of_layer(l, (LANES, G_KWP)), _of_layer(l, (1, G_KWP))],
        out_specs=[pl.BlockSpec((S_BLOCKS, tm, LANES), lambda i: (0, i, 0)),
                   _rows(tm, 2 * M_HEADS * LANES),
                   pl.BlockSpec((LANES, tm), lambda i: (0, i)), _rows(tm, G_KWP)],
        out_shape=[jax.ShapeDtypeStruct((S_BLOCKS, n, LANES), F32),
                   jax.ShapeDtypeStruct((n, 2 * M_HEADS * LANES), F32),
                   jax.ShapeDtypeStruct((LANES, n), F32),
                   jax.ShapeDtypeStruct((n, G_KWP), F32)],
        compiler_params=_params("arbitrary"), name="proj_aux",
    )(hn, w, gb, wal, bal)


def _mlstm_kernel(q_ref, k_ref, v_ref, oz_ref, gmb_ref, gmt_ref, nw_ref, out_ref, c_scr, m_scr, *, tc):
    @pl.when(pl.program_id(1) == 0)
    def _():
        c_scr[...] = jnp.zeros_like(c_scr)
        m_scr[...] = jnp.zeros_like(m_scr)

    row = lax.broadcasted_iota(jnp.int32, (tc, tc), 0)
    col = lax.broadcasted_iota(jnp.int32, (tc, tc), 1)
    causal = col <= row
    real = lax.broadcasted_iota(jnp.int32, (tc, M_HDP), 1) < M_HD
    ones_tile = jnp.ones((tc, LANES), BF16)

    def lanes(x, n):
        return jnp.concatenate([x] * n, axis=1)

    for h in range(M_HEADS):
        sl = slice(h * M_HDP, (h + 1) * M_HDP)
        qb = q_ref[:, sl]
        kb = k_ref[:, sl]
        vext = jnp.concatenate([v_ref[:, sl], ones_tile], axis=1)
        i_b = gmb_ref[:, h * LANES:(h + 1) * LANES]
        bcum = gmb_ref[:, (M_HEADS + h) * LANES:(M_HEADS + h + 1) * LANES]
        g_row = gmt_ref[h:h + 1, :] - gmt_ref[M_HEADS + h:M_HEADS + h + 1, :]
        m_prev = m_scr[h, 0:1, :]

        inter = bcum + m_prev
        m_t = jnp.maximum(inter, bcum + jnp.max(jnp.where(causal, g_row, NEG), axis=1, keepdims=True))
        w_intra = jnp.exp(jnp.where(causal, g_row + lanes(bcum - m_t, tc // LANES), NEG))
        w_inter = jnp.exp(inter - m_t)
        s = _dot_nt(qb, kb) * w_intra
        c_mat = c_scr[h]
        num = _dot(s.astype(BF16), vext) + lanes(w_inter, 3) * _dot(qb, c_mat.astype(BF16))
        den = num[:, M_HDP:]
        rden = 1.0 / jnp.maximum(jnp.abs(den), jnp.exp(-m_t))
        num = num[:, :M_HDP]

        b_end = bcum[tc - 1:tc, :]
        to_end = b_end - bcum + i_b
        m_new = jnp.maximum(b_end + m_prev, jnp.max(to_end, axis=0, keepdims=True))
        wk = jnp.exp(to_end - m_new)
        w_prev = jnp.exp(b_end + m_prev - m_new)
        c_scr[h] = lanes(w_prev, 3) * c_mat + _dot_tn((lanes(wk, 2) * kb.astype(F32)).astype(BF16), vext)
        m_scr[h] = jnp.broadcast_to(m_new, (8, LANES))

        mean = jnp.sum(jnp.where(real, num, 0.0), axis=1, keepdims=True) * (1.0 / M_HD)
        xc = jnp.where(real, num - mean, 0.0)
        var = jnp.sum(xc * xc, axis=1, keepdims=True) * (1.0 / M_HD)
        scale = rden * lax.rsqrt(rden * rden * var + EPS)
        out_ref[:, sl] = (xc * lanes(scale, 2) * nw_ref[:, sl] * oz_ref[:, sl].astype(F32)).astype(BF16)


def _mlstm(q, k, v, oz, gmb, gmt, l, nw, batch, seq, tc):
    n = batch * seq
    nt = seq // tc
    blk = pl.BlockSpec((tc, M_WP), lambda b, t: (b * nt + t, 0))
    return pl.pallas_call(
        functools.partial(_mlstm_kernel, tc=tc),
        grid=(batch, nt),
        in_specs=[blk, blk, blk, blk,
                  pl.BlockSpec((tc, 2 * M_HEADS * LANES), lambda b, t: (b * nt + t, 0)),
                  pl.BlockSpec((LANES, tc), lambda b, t: (0, b * nt + t)),
                  _of_layer(l, (1, M_WP))],
        out_specs=blk,
        out_shape=jax.ShapeDtypeStruct((n, M_WP), BF16),
        scratch_shapes=[pltpu.VMEM((M_HEADS, M_HDP, M_HDP + LANES), F32),
                        pltpu.VMEM((M_HEADS, 8, LANES), F32)],
        compiler_params=_params("arbitrary", "arbitrary"),
        name="mlstm",
    )(q, k, v, oz, gmb, gmt, nw)


def _gla_kernel(q_ref, k_ref, v_ref, z_ref, b_ref, nw_ref,
                out_ref, kbuf, bbuf, attn_scr, st_scr, *, t, cs):
    @pl.when(pl.program_id(1) == 0)
    def _():
        st_scr[...] = jnp.zeros_like(st_scr)

    row = lax.broadcasted_iota(jnp.int32, (t, t), 0)
    col = lax.broadcasted_iota(jnp.int32, (t, t), 1)
    causal = col <= row

    def operands(h):
        sl = slice(h * G_DKP, (h + 1) * G_DKP)
        return q_ref[:, sl].astype(F32), k_ref[:, sl].astype(F32), b_ref[:, sl]

    safe = jnp.min(b_ref[t - 1:t, :]) >= G_SAFE_LOG_DECAY

    @pl.when(safe)
    def _():
        for h in range(G_HEADS):
            qs, kk, b = operands(h)
            a = _dot_nt((qs * jnp.exp(b)).astype(BF16), (kk * jnp.exp(-b)).astype(BF16))
            attn_scr[h] = jnp.where(causal, a, 0.0)

    @pl.when(jnp.logical_not(safe))
    def _():
        rowv = lax.broadcasted_iota(jnp.int32, (t, G_DKP), 0)
        sub = rowv % cs
        lag = row - col
        kbuf[0:cs, :] = jnp.zeros((cs, G_DKP), F32)
        bbuf[0:cs, :] = jnp.zeros((cs, G_DKP), F32)
        for h in range(G_HEADS):
            qs, kk, b = operands(h)
            qparts, kparts = [], []
            for i in range(1, t // cs):
                r = b[i * cs - 1:i * cs, :]
                inblk = (rowv >= i * cs) & (rowv < (i + 1) * cs)
                qparts.append((qs * jnp.exp(jnp.where(inblk, b - r, NEG))).astype(BF16))
                kparts.append((kk * jnp.exp(jnp.where(rowv < i * cs, r - b, NEG))).astype(BF16))
            attn = _dot_nt(jnp.concatenate(qparts, axis=1), jnp.concatenate(kparts, axis=1))
            kbuf[cs:cs + t, :] = kk
            bbuf[cs:cs + t, :] = b
            for j in range(cs):
                ksh = kbuf[cs - j:cs - j + t, :]
                bsh = bbuf[cs - j:cs - j + t, :]
                e = jnp.exp(jnp.where(sub >= j, b - bsh, NEG))
                dj = jnp.sum(qs * ksh * e, axis=1, keepdims=True)
                attn = attn + jnp.where(lag == j, dj, 0.0)
            attn_scr[h] = attn

    for h in range(G_HEADS):
        slv = slice(h * G_DVP, (h + 1) * G_DVP)
        qs, kk, b = operands(h)
        v = v_ref[:, slv]
        blast = b[t - 1:t, :]
        st = st_scr[h]
        o = (_dot_nt((qs * jnp.exp(b)).astype(BF16), st.astype(BF16))
             + _dot(attn_scr[h].astype(BF16), v))
        kdec = (kk * jnp.exp(blast - b)).astype(BF16)
        st_scr[h] = jnp.exp(blast) * st + _dot_tn(v, kdec)

        var = jnp.sum(o * o, axis=1, keepdims=True) * (1.0 / G_DV)
        hb = o * lax.rsqrt(var + EPS) * nw_ref[:, slv]
        out_ref[:, slv] = (hb * z_ref[:, slv].astype(F32)).astype(BF16)


def _gla(q, k, v, z, bcum, l, nw, batch, seq, t, cs):
    n = batch * seq
    nt = seq // t

    def blk(width):
        return pl.BlockSpec((t, width), lambda b, c: (b * nt + c, 0))

    return pl.pallas_call(
        functools.partial(_gla_kernel, t=t, cs=cs),
        grid=(batch, nt),
        in_specs=[blk(G_KWP), blk(G_KWP), blk(G_VWP), blk(G_VWP), blk(G_KWP),
                  _of_layer(l, (1, G_VWP))],
        out_specs=blk(G_VWP),
        out_shape=jax.ShapeDtypeStruct((n, G_VWP), BF16),
        scratch_shapes=[pltpu.VMEM((t + cs, G_DKP), F32), pltpu.VMEM((t + cs, G_DKP), F32),
                        pltpu.VMEM((G_HEADS, t, t), F32),
                        pltpu.VMEM((G_HEADS, G_DVP, G_DKP), F32)],
        compiler_params=_params("arbitrary", "arbitrary"),
        name="gla",
    )(q, k, v, z, bcum, nw)


def _gelu_tanh(x):
    return 0.5 * x * (1.0 + jnp.tanh(math.sqrt(2.0 / math.pi) * (x + 0.044715 * (x * x * x))))


def _s5_kernel(u_ref, kms_ref, mo_ref, p_ref, d_ref, y_ref, carry, *, nch):
    @pl.when(pl.program_id(2) == 0)
    def _():
        carry[...] = jnp.zeros_like(carry)

    qw = S_Q * LANES
    ucat = jnp.concatenate([u_ref[0, pl.ds(s, nch, stride=S_Q), :] for s in range(S_Q)],
                           axis=1).astype(BF16)
    r = _dot(ucat, kms_ref[0])
    y = r[:, :qw]
    xr = r[:, qw:qw + S_SW]
    xi = r[:, qw + S_SW:]

    rowi = lax.broadcasted_iota(jnp.int32, (nch, S_SW), 0)
    d = 1
    while d < nch:
        pr = p_ref[0, d - 1:d, :S_SW]
        pi = p_ref[0, d - 1:d, S_SW:]
        keep = rowi >= d
        sr = jnp.where(keep, pltpu.roll(xr, d, 0), 0.0)
        si = jnp.where(keep, pltpu.roll(xi, d, 0), 0.0)
        xr, xi = xr + (pr * sr - pi * si), xi + (pr * si + pi * sr)
        d *= 2
    cr = carry[0:1, :S_SW]
    ci = carry[0:1, S_SW:]
    pr = p_ref[0, :, :S_SW]
    pi = p_ref[0, :, S_SW:]
    xr, xi = xr + (pr * cr - pi * ci), xi + (pr * ci + pi * cr)
    first = rowi == 0
    xpr = jnp.where(first, cr, pltpu.roll(xr, 1, 0))
    xpi = jnp.where(first, ci, pltpu.roll(xi, 1, 0))
    carry[0:1, :S_SW] = xr[nch - 1:nch, :]
    carry[0:1, S_SW:] = xi[nch - 1:nch, :]

    y = y + _dot(jnp.concatenate([xpr, xpi], axis=1).astype(BF16), mo_ref[0])
    for t in range(S_Q):
        y_ref[0, pl.ds(t, nch, stride=S_Q), :] = y[:, t * LANES:(t + 1) * LANES]
    y_ref[0] = _gelu_tanh(y_ref[0] + d_ref[0] * u_ref[0])


def _s5(cu, l, kms, mo, ptab, dskip, batch, seq, tm):
    n = batch * seq
    nt = seq // tm
    nch = tm // S_Q
    qw = S_Q * LANES

    def table(rows, cols):
        return pl.BlockSpec((None, 1, rows, cols), lambda j, b, t: (l, j, 0, 0))

    return pl.pallas_call(
        functools.partial(_s5_kernel, nch=nch),
        grid=(S_BLOCKS, batch, nt),
        in_specs=[pl.BlockSpec((1, tm, LANES), lambda j, b, t: (j, b * nt + t, 0)),
                  table(qw, qw + 2 * S_SW), table(2 * S_SW, qw), table(nch, 2 * S_SW), table(1, LANES)],
        out_specs=pl.BlockSpec((1, tm, LANES), lambda j, b, t: (j, b * nt + t, 0)),
        out_shape=jax.ShapeDtypeStruct((S_BLOCKS, n, LANES), F32),
        scratch_shapes=[pltpu.VMEM((8, 2 * S_SW), F32)],
        compiler_params=_params("arbitrary", "arbitrary", "arbitrary"),
        name="s5",
    )(cu, kms, mo, ptab, dskip)


def _s5_expand_kernel(kt_ref, msr_ref, msi_ref, mor_ref, moi_ref, kms_ref, mo_ref):
    q = S_Q
    qw = q * LANES

    def embed(m, rows, width):
        shape = (width, S_GPB * width)
        tile = (lax.broadcasted_iota(jnp.int32, shape, 1) % width
                == lax.broadcasted_iota(jnp.int32, shape, 0)).astype(BF16)
        oshape = (S_GPB * rows, S_GPB * width)
        diag = (lax.broadcasted_iota(jnp.int32, oshape, 0) // rows
                == lax.broadcasted_iota(jnp.int32, oshape, 1) // width)
        return jnp.where(diag, _dot(m, tile), 0.0).astype(BF16)

    kd = [embed(kt_ref[0, 0, tau], S_GROUP, S_GROUP) for tau in range(q)]
    zero = jnp.zeros((LANES, LANES), BF16)
    for s in range(q):
        rows = slice(s * LANES, (s + 1) * LANES)
        for t in range(q):
            kms_ref[0, 0, rows, t * LANES:(t + 1) * LANES] = kd[t - s] if t >= s else zero
        kms_ref[0, 0, rows, qw:qw + S_SW] = embed(msr_ref[0, 0, s], S_GROUP, S_STATE)
        kms_ref[0, 0, rows, qw + S_SW:] = embed(msi_ref[0, 0, s], S_GROUP, S_STATE)
    for t in range(q):
        cols = slice(t * LANES, (t + 1) * LANES)
        mo_ref[0, 0, 0:S_SW, cols] = embed(mor_ref[0, 0, t], S_STATE, S_GROUP)
        mo_ref[0, 0, S_SW:, cols] = embed(moi_ref[0, 0, t], S_STATE, S_GROUP)


def _s5_expand(kt, msr, msi, mor, moi):
    depth = kt.shape[0]
    qw = S_Q * LANES

    def blk(a):
        return pl.BlockSpec((1, 1) + a.shape[2:], lambda d, j: (d, j, 0, 0, 0))

    return pl.pallas_call(
        _s5_expand_kernel, grid=(depth, S_BLOCKS),
        in_specs=[blk(a) for a in (kt, msr, msi, mor, moi)],
        out_specs=[pl.BlockSpec((1, 1, qw, qw + 2 * S_SW), lambda d, j: (d, j, 0, 0)),
                   pl.BlockSpec((1, 1, 2 * S_SW, qw), lambda d, j: (d, j, 0, 0))],
        out_shape=[jax.ShapeDtypeStruct((depth, S_BLOCKS, qw, qw + 2 * S_SW), BF16),
                   jax.ShapeDtypeStruct((depth, S_BLOCKS, 2 * S_SW, qw), BF16)],
        compiler_params=_params("arbitrary", "arbitrary"), name="s5_expand",
    )(kt, msr, msi, mor, moi)


def _s5_tables(lam_re, lam_im, log_dt, b_re, b_im, c_re, c_im, nch):
    q = S_Q
    dep = lam_re.shape[0]
    lr = jnp.minimum(lam_re.astype(F32), -1e-4)
    li = lam_im.astype(F32)
    dt = jnp.exp(log_dt.astype(F32))[..., None]
    mag = jnp.exp(lr * dt)
    ab_re = mag * jnp.cos(li * dt)
    ab_im = mag * jnp.sin(li * dt)
    nr = ab_re - 1.0
    den = lr * lr + li * li
    coef_re = ((nr * lr + ab_im * li) / den)[:, :, None, :]
    coef_im = ((ab_im * lr - nr * li) / den)[:, :, None, :]
    brt = jnp.swapaxes(b_re.astype(F32), -1, -2)
    bit = jnp.swapaxes(b_im.astype(F32), -1, -2)
    bb_re = coef_re * brt - coef_im * bit
    bb_im = coef_re * bit + coef_im * brt

    def apow(nvals):
        e = jnp.asarray(nvals, F32)[:, None, None, None]
        m = jnp.exp(e * (lr * dt))
        ang = e * (li * dt)
        return m * jnp.cos(ang), m * jnp.sin(ang)

    def per_block(m, rows):
        m = m.reshape(q, dep, S_BLOCKS, S_GPB * rows, m.shape[-1])
        return m.transpose(1, 2, 0, 3, 4).astype(BF16)

    ar, ai = apow(list(range(q + 1)))
    cr, ci = c_re.astype(F32), c_im.astype(F32)
    ca_re = cr[None] * ar[:, :, :, None, :] - ci[None] * ai[:, :, :, None, :]
    ca_im = cr[None] * ai[:, :, :, None, :] + ci[None] * ar[:, :, :, None, :]
    kt = jnp.sum(ca_re[:q, :, :, None, :, :] * bb_re[None, :, :, :, None, :]
                 - ca_im[:q, :, :, None, :, :] * bb_im[None, :, :, :, None, :], axis=-1)
    pw_r = ar[:q][::-1][:, :, :, None, :]
    pw_i = ai[:q][::-1][:, :, :, None, :]
    kms, mo = _s5_expand(per_block(kt, S_GROUP),
                         per_block(pw_r * bb_re - pw_i * bb_im, S_GROUP),
                         per_block(pw_r * bb_im + pw_i * bb_re, S_GROUP),
                         per_block(jnp.swapaxes(ca_re[1:], -1, -2), S_STATE),
                         per_block(jnp.swapaxes(-ca_im[1:], -1, -2), S_STATE))
    pr, pi = apow([q * (c + 1) for c in range(nch)])
    pr = pr.reshape(nch, dep, S_BLOCKS, S_SW).transpose(1, 2, 0, 3)
    pi = pi.reshape(nch, dep, S_BLOCKS, S_SW).transpose(1, 2, 0, 3)
    ptab = jnp.concatenate([pr, pi], axis=3)
    return kms, mo, ptab


def _merge_kernel(ya_ref, yb_ref, yc_ref, cz_ref, g_ref, x_ref, wa_ref, wb_ref, wglu_ref, wc_ref,
                  wo_ref, nw_ref, *out_refs, final):
    yc0 = jnp.concatenate([yc_ref[j] for j in range(S_BLOCKS)], axis=1)
    glu = _dot(yc0.astype(BF16), wglu_ref[...])
    yc = yc0 * _sigmoid(glu) * cz_ref[...].astype(F32)
    merged = (g_ref[:, 0:D_MODEL].astype(F32) * _dot(ya_ref[...], wa_ref[...])
              + g_ref[:, D_MODEL:2 * D_MODEL].astype(F32) * _dot(yb_ref[...], wb_ref[...])
              + g_ref[:, 2 * D_MODEL:].astype(F32) * _dot(yc.astype(BF16), wc_ref[...]))
    out = x_ref[...] + _dot(merged.astype(BF16), wo_ref[...])
    ms = jnp.mean(out * out, axis=-1, keepdims=True)
    normed = out * lax.rsqrt(ms + EPS) * nw_ref[...]
    if final:
        out_refs[0][...] = normed
    else:
        out_refs[0][...] = out
        out_refs[1][...] = normed.astype(BF16)


def _merge(ya, yb, yc, cz, g, x2, l, wa, wb, wglu, wc, wo, nw, tm, final):
    n = x2.shape[0]
    x_out = jax.ShapeDtypeStruct((n, D_MODEL), F32)
    hn_out = jax.ShapeDtypeStruct((n, D_MODEL), BF16)
    return pl.pallas_call(
        functools.partial(_merge_kernel, final=final),
        grid=(n // tm,),
        in_specs=[_rows(tm, M_WP), _rows(tm, G_VWP),
                  pl.BlockSpec((S_BLOCKS, tm, LANES), lambda i: (0, i, 0)),
                  _rows(tm, S_W), _rows(tm, 3 * D_MODEL), _rows(tm, D_MODEL),
                  _of_layer(l, (M_WP, D_MODEL)), _of_layer(l, (G_VWP, D_MODEL)), _of_layer(l, (S_W, S_W)),
                  _of_layer(l, (S_W, D_MODEL)), _of_layer(l, (D_MODEL, D_MODEL)), _whole((1, D_MODEL))],
        out_specs=[_rows(tm, D_MODEL)] if final else [_rows(tm, D_MODEL), _rows(tm, D_MODEL)],
        out_shape=[x_out] if final else [x_out, hn_out],
        compiler_params=_params("arbitrary"),
        name="merge",
    )(ya, yb, yc, cz, g, x2, wa, wb, wglu, wc, wo, nw)


def _pad_heads(w, heads, d, dp):
    zeros = jnp.zeros(w.shape[:-1] + (dp - d,), w.dtype)
    parts = []
    for h in range(heads):
        parts += [w[..., h * d:(h + 1) * d], zeros]
    return jnp.concatenate(parts, axis=-1)


def _pad_head_rows(w, heads, d, dp):
    zeros = jnp.zeros((w.shape[0], dp - d, w.shape[2]), w.dtype)
    parts = []
    for h in range(heads):
        parts += [w[:, h * d:(h + 1) * d], zeros]
    return jnp.concatenate(parts, axis=1)


def _prepare(p, nch):
    depth = p["w_in"].shape[0]
    w_mlstm, w_plain, w_gate, w_small = _wprep(jnp.swapaxes(p["w_in"], 1, 2), LANES)
    mp = functools.partial(_pad_heads, heads=M_HEADS, d=M_HD, dp=M_HDP)
    gkp = functools.partial(_pad_heads, heads=G_HEADS, d=G_DK, dp=G_DKP)
    gvp = functools.partial(_pad_heads, heads=G_HEADS, d=G_DV, dp=G_DVP)
    kms, mo, ptab = _s5_tables(p["s5_lam_re"], p["s5_lam_im"], p["s5_log_dt"], p["s5_B_re"], p["s5_B_im"],
                               p["s5_C_re"], p["s5_C_im"], nch)
    gate_b = p["mlstm_gate_b"]
    return dict(
        w_mlstm=w_mlstm, w_plain=w_plain, w_gate=w_gate, w_small=w_small,
        cq=mp(p["mlstm_conv"][:, :, :M_W]), ck=mp(p["mlstm_conv"][:, :, M_W:]),
        gb=jnp.concatenate([gate_b[:, 0], gate_b[:, 1],
                            jnp.zeros((depth, LANES - 2 * M_HEADS), F32)], axis=1)[:, None, :],
        m_nw=mp(p["mlstm_norm"])[:, None, :],
        wal=jnp.concatenate([jnp.zeros((depth, 2 * M_HEADS, G_KWP), F32), gkp(p["gla_w_alpha"]),
                             jnp.zeros((depth, LANES - 2 * M_HEADS - G_RANK, G_KWP), F32)],
                            axis=1).astype(BF16),
        bal=gkp(p["gla_b_alpha"])[:, None, :],
        g_nw=gvp(p["gla_norm"])[:, None, :],
        kms=kms, mo=mo, ptab=ptab,
        dskip=p["s5_D"].reshape(depth, S_BLOCKS, 1, LANES),
        wa=_pad_head_rows(p["w_branch_mlstm"].astype(BF16), M_HEADS, M_HD, M_HDP),
        wb=_pad_head_rows(p["w_branch_gla"].astype(BF16), G_HEADS, G_DV, G_DVP),
        wglu=p["s5_w_glu"].astype(BF16), wc=p["w_branch_s5"].astype(BF16), wo=p["w_out"].astype(BF16),
    )


def _tiles(batch, seq):
    return dict(proj=min(1024, seq), m_chunk=min(256, seq), g_chunk=min(256, seq), g_sub=16,
                s5=min(4096, seq), merge=min(512, batch * seq))


def _layer(x2, hn, batch, seq, l, w, next_norm, final, tl):
    tm = tl["proj"]
    q, k, oz = _proj_mlstm(hn, l, w["w_mlstm"], w["cq"], w["ck"], seq, tm)
    mv, gv, gq, gk = _proj_plain(hn, l, w["w_plain"], tm)
    gz, czs, gs = _proj_gate(hn, l, w["w_gate"], tm)
    cu_s, gm, gmt, gcum = _proj_aux(hn, l, w["w_small"], w["gb"], w["wal"], w["bal"], tm,
                                    tl["m_chunk"], tl["g_chunk"])
    ya = _mlstm(q, k, mv, oz, gm, gmt, l, w["m_nw"], batch, seq, tl["m_chunk"])
    yb = _gla(gq, gk, gv, gz, gcum, l, w["g_nw"], batch, seq, tl["g_chunk"], tl["g_sub"])
    yc = _s5(cu_s, l, w["kms"], w["mo"], w["ptab"], w["dskip"], batch, seq, tl["s5"])
    return _merge(ya, yb, yc, czs, gs, x2, l, w["wa"], w["wb"], w["wglu"], w["wc"], w["wo"],
                  next_norm.reshape(1, D_MODEL), tl["merge"], final)


def kernel(x, norm_w, w_in, mlstm_conv, mlstm_gate_b, mlstm_norm, gla_w_alpha, gla_b_alpha, gla_norm,
           s5_lam_re, s5_lam_im, s5_log_dt, s5_B_re, s5_B_im, s5_C_re, s5_C_im, s5_D, s5_w_glu,
           w_branch_mlstm, w_branch_gla, w_branch_s5, w_out, final_norm):
    batch, seq, _ = x.shape
    depth = norm_w.shape[0]
    tl = _tiles(batch, seq)
    prep = _prepare(dict(w_in=w_in, mlstm_conv=mlstm_conv, mlstm_gate_b=mlstm_gate_b, mlstm_norm=mlstm_norm,
                         gla_w_alpha=gla_w_alpha, gla_b_alpha=gla_b_alpha, gla_norm=gla_norm,
                         s5_lam_re=s5_lam_re, s5_lam_im=s5_lam_im, s5_log_dt=s5_log_dt, s5_B_re=s5_B_re,
                         s5_B_im=s5_B_im, s5_C_re=s5_C_re, s5_C_im=s5_C_im, s5_D=s5_D, s5_w_glu=s5_w_glu,
                         w_branch_mlstm=w_branch_mlstm, w_branch_gla=w_branch_gla,
                         w_branch_s5=w_branch_s5, w_out=w_out), tl["s5"] // S_Q)
    x2 = x.reshape(batch * seq, D_MODEL)
    hn = _rms(x2, norm_w[0].reshape(1, D_MODEL), min(1024, batch * seq))
    for l in range(depth):
        final = l == depth - 1
        res = _layer(x2, hn, batch, seq, l, prep, final_norm if final else norm_w[l + 1], final, tl)
        if final:
            x2 = res[0]
        else:
            x2, hn = res
    return x2.reshape(batch, seq, D_MODEL)
```

```python
import functools
import math

import jax
import jax.numpy as jnp
from jax import lax
from jax.experimental import pallas as pl
from jax.experimental.pallas import tpu as pltpu

F32 = jnp.float32
BF16 = jnp.bfloat16
HIGHEST = lax.Precision.HIGHEST

EPS = 1e-6
D_MODEL = 1024
LANES = 128
M_HEADS = 4
M_HD = 192
M_HDP = 256
M_W = M_HEADS * M_HD
M_WP = M_HEADS * M_HDP
M_CONV = 4
G_HEADS = 4
G_DK = 96
G_DKP = 128
G_DV = 192
G_DVP = 256
G_KW = G_HEADS * G_DK
G_KWP = G_HEADS * G_DKP
G_VW = G_HEADS * G_DV
G_VWP = G_HEADS * G_DVP
G_RANK = 16
G_TAU = 16.0
S_W = 512
S_GROUP = 16
S_GROUPS = 32
S_STATE = 64
S_BLOCKS = S_W // LANES
S_GPB = LANES // S_GROUP
S_SW = S_GPB * S_STATE
S_Q = 8

IN_SIZES = (M_W, M_W, M_W, M_W, M_HEADS, M_HEADS, M_W,
            G_KW, G_KW, G_VW, G_RANK, G_VW,
            S_W, S_W, 3 * D_MODEL)
SMALL_W = S_W + LANES
CW = 512

NEG = -1e30
G_SAFE_LOG_DECAY = -60.0
VMEM_LIMIT = 56 * 1024 * 1024


def _sigmoid(x):
    return 0.5 * jnp.tanh(0.5 * x) + 0.5


def _silu(x):
    h = 0.5 * x
    return h * jnp.tanh(h) + h


def _log_sigmoid(x):
    return jnp.minimum(x, 0.0) - jnp.log1p(jnp.exp(-jnp.abs(x)))


def _dot(a, b):
    return jnp.dot(a, b, preferred_element_type=F32)


def _dot_nt(a, b):
    return lax.dot_general(a, b, (((1,), (1,)), ((), ())), preferred_element_type=F32)


def _dot_tn(a, b):
    return lax.dot_general(a, b, (((0,), (0,)), ((), ())), preferred_element_type=F32)


def _params(*sem):
    return pltpu.CompilerParams(dimension_semantics=sem, vmem_limit_bytes=VMEM_LIMIT)


def _rows(tm, width):
    return pl.BlockSpec((tm, width), lambda i: (i, 0))


def _whole(shape):
    return pl.BlockSpec(shape, lambda i: (0,) * len(shape))


def _of_layer(l, shape):
    return pl.BlockSpec((None,) + shape, lambda *_: (l,) + (0,) * len(shape))


def _rms_kernel(x_ref, nw_ref, o_ref):
    x = x_ref[...]
    ms = jnp.mean(x * x, axis=-1, keepdims=True)
    o_ref[...] = (x * lax.rsqrt(ms + EPS) * nw_ref[...]).astype(BF16)


def _rms(x2, nw, tm):
    n = x2.shape[0]
    return pl.pallas_call(
        _rms_kernel, grid=(n // tm,),
        in_specs=[_rows(tm, D_MODEL), _whole((1, D_MODEL))],
        out_specs=_rows(tm, D_MODEL),
        out_shape=jax.ShapeDtypeStruct((n, D_MODEL), BF16),
        compiler_params=_params("arbitrary"), name="rms",
    )(x2, nw)


W_OFF = {}
_o = 0
for _name, _size in zip(("aq", "ak", "av", "ao", "ai", "af", "az", "bq", "bk", "bv", "ba", "bz", "cu", "cz", "g"),
                        IN_SIZES):
    W_OFF[_name] = _o
    _o += _size
IN_WIDTH = _o
W_MLSTM, W_PLAIN, W_GATE = 4 * M_W, M_W + G_VW + 2 * G_KW, G_VW + S_W + 3 * D_MODEL


def _wprep_kernel(w_ref, wm_ref, wp_ref, wg_ref, ws_ref):
    kb = w_ref.shape[2]

    def put(dst, dst_off, parts, width):
        xs = [w_ref[0, off:off + rows, :] for off, rows in parts]
        used = sum(rows for _, rows in parts)
        if used < width:
            xs.append(jnp.zeros((width - used, kb), F32))
        x = xs[0] if len(xs) == 1 else jnp.concatenate(xs, axis=0)
        dst[0, :, dst_off:dst_off + width] = x.T.astype(BF16)

    def group(dst, names):
        off = 0
        for name in names:
            width = IN_SIZES[list(W_OFF).index(name)]
            for c in range(0, width, LANES):
                put(dst, off + c, [(W_OFF[name] + c, LANES)], LANES)
            off += width

    group(wm_ref, ("aq", "ak", "ao", "az"))
    group(wp_ref, ("av", "bv", "bq", "bk"))
    group(wg_ref, ("bz", "cz", "g"))
    group(ws_ref, ("cu",))
    put(ws_ref, S_W, [(W_OFF["ai"], 2 * M_HEADS), (W_OFF["ba"], G_RANK)], LANES)


def _wprep(w_t, kb):
    depth = w_t.shape[0]
    widths = (W_MLSTM, W_PLAIN, W_GATE, SMALL_W)
    return pl.pallas_call(
        _wprep_kernel, grid=(depth, D_MODEL // kb),
        in_specs=[pl.BlockSpec((1, IN_WIDTH, kb), lambda l, i: (l, 0, i))],
        out_specs=[pl.BlockSpec((1, kb, wd), lambda l, i: (l, i, 0)) for wd in widths],
        out_shape=[jax.ShapeDtypeStruct((depth, D_MODEL, wd), BF16) for wd in widths],
        compiler_params=_params("arbitrary", "arbitrary"), name="wprep",
    )(w_t)


def _scatter_heads(dst, y, heads, d, dp, fill):
    rows = y.shape[0]
    for h in range(heads):
        dst[:, h * dp:h * dp + d] = y[:, h * d:(h + 1) * d].astype(BF16)
        dst[:, h * dp + d:(h + 1) * dp] = jnp.full((rows, dp - d), fill, BF16)


def _proj_mlstm_kernel(hn_ref, w_ref, cq_ref, ck_ref, q_ref, k_ref, oz_ref, halo_x, halo_b, *, tm,
                       tiles_per_seq):
    @pl.when(pl.program_id(0) % tiles_per_seq == 0)
    def _():
        halo_x[...] = jnp.zeros_like(halo_x)
        halo_b[...] = jnp.zeros_like(halo_b)

    hn = hn_ref[...]
    row8 = lax.broadcasted_iota(jnp.int32, (8, M_W), 0)

    def shift(x, prev8, d):
        rolled = pltpu.roll(x, d, 0)
        head = jnp.where(row8 < d, pltpu.roll(prev8, d, 0), rolled[0:8])
        return jnp.concatenate([head, rolled[8:]], axis=0)

    for c, (cw_ref, dst, scale) in enumerate(((cq_ref, q_ref, 1.0), (ck_ref, k_ref, M_HD ** -0.5))):
        x = _dot(hn, w_ref[:, c * M_W:(c + 1) * M_W])
        x1 = shift(x, halo_x[c], 1)
        pair = cw_ref[1:2, :] * x + cw_ref[0:1, :] * x1
        y = _silu(cw_ref[3:4, :] * x + cw_ref[2:3, :] * x1 + shift(pair, halo_b[c], 2))
        halo_x[c] = x[tm - 8:tm]
        halo_b[c] = pair[tm - 8:tm]
        _scatter_heads(dst, y if scale == 1.0 else y * scale, M_HEADS, M_HD, M_HDP, 0.0)
    o = _dot(hn, w_ref[:, 2 * M_W:3 * M_W])
    z = _dot(hn, w_ref[:, 3 * M_W:])
    oz_ref[...] = (_sigmoid(o) * _silu(z)).astype(BF16)


def _proj_mlstm(hn, l, w, cq, ck, seq, tm):
    n = hn.shape[0]
    padded = jax.ShapeDtypeStruct((n, M_WP), BF16)
    return pl.pallas_call(
        functools.partial(_proj_mlstm_kernel, tm=tm, tiles_per_seq=seq // tm),
        grid=(n // tm,),
        in_specs=[_rows(tm, D_MODEL), _of_layer(l, (D_MODEL, W_MLSTM)), _of_layer(l, (M_CONV, M_W)),
                  _of_layer(l, (M_CONV, M_W))],
        out_specs=[_rows(tm, M_WP), _rows(tm, M_WP), _rows(tm, M_W)],
        out_shape=[padded, padded, jax.ShapeDtypeStruct((n, M_W), BF16)],
        scratch_shapes=[pltpu.VMEM((2, 8, M_W), F32)] * 2,
        compiler_params=_params("arbitrary"), name="proj_mlstm",
    )(hn, w, cq, ck)


def _proj_plain_kernel(hn_ref, w_ref, mv_ref, gv_ref, gq_ref, gk_ref):
    hn = hn_ref[...]
    _scatter_heads(mv_ref, _dot(hn, w_ref[:, 0:M_W]), M_HEADS, M_HD, M_HDP, 1.0)
    _scatter_heads(gv_ref, _dot(hn, w_ref[:, M_W:M_W + G_VW]), G_HEADS, G_DV, G_DVP, 0.0)
    off = M_W + G_VW
    _scatter_heads(gq_ref, _dot(hn, w_ref[:, off:off + G_KW]) * (G_DK ** -0.5), G_HEADS, G_DK, G_DKP, 0.0)
    _scatter_heads(gk_ref, _dot(hn, w_ref[:, off + G_KW:]), G_HEADS, G_DK, G_DKP, 0.0)


def _proj_plain(hn, l, w, tm):
    n = hn.shape[0]
    widths = (M_WP, G_VWP, G_KWP, G_KWP)
    return pl.pallas_call(
        _proj_plain_kernel, grid=(n // tm,),
        in_specs=[_rows(tm, D_MODEL), _of_layer(l, (D_MODEL, W_PLAIN))],
        out_specs=[_rows(tm, wd) for wd in widths],
        out_shape=[jax.ShapeDtypeStruct((n, wd), BF16) for wd in widths],
        compiler_params=_params("arbitrary"), name="proj_plain",
    )(hn, w)


def _proj_gate_kernel(hn_ref, w_ref, gz_ref, cz_ref, g_ref):
    hn = hn_ref[...]
    gz_ref[...] = _silu(_dot(hn, w_ref[:, 0:G_VW])).astype(BF16)
    cz_ref[...] = _silu(_dot(hn, w_ref[:, G_VW:G_VW + S_W])).astype(BF16)
    off = G_VW + S_W
    for c in range(3 * D_MODEL // CW):
        r = _dot(hn, w_ref[:, off + c * CW:off + (c + 1) * CW])
        g_ref[:, c * CW:(c + 1) * CW] = _sigmoid(r).astype(BF16)


def _proj_gate(hn, l, w, tm):
    n = hn.shape[0]
    widths = (G_VW, S_W, 3 * D_MODEL)
    return pl.pallas_call(
        _proj_gate_kernel, grid=(n // tm,),
        in_specs=[_rows(tm, D_MODEL), _of_layer(l, (D_MODEL, W_GATE))],
        out_specs=[_rows(tm, wd) for wd in widths],
        out_shape=[jax.ShapeDtypeStruct((n, wd), BF16) for wd in widths],
        compiler_params=_params("arbitrary"), name="proj_gate",
    )(hn, w)


def _segment_cumsum(x, seg):
    pos = lax.broadcasted_iota(jnp.int32, x.shape, 0) % seg
    d = 1
    while d < seg:
        x = x + jnp.where(pos >= d, pltpu.roll(x, d, 0), 0.0)
        d *= 2
    return x


def _proj_aux_kernel(hn_ref, w_ref, gb_ref, wal_ref, bal_ref, cu_ref, gmb_ref, gmt_ref, gcum_ref, *,
                     m_chunk, g_chunk):
    r = _dot(hn_ref[...], w_ref[...])
    for j in range(S_BLOCKS):
        cu_ref[j] = r[:, j * LANES:(j + 1) * LANES]
    raw = r[:, S_W:]
    gates = raw + gb_ref[...]
    lane = lax.broadcasted_iota(jnp.int32, gates.shape, 1)
    fcum = _segment_cumsum(_log_sigmoid(gates), m_chunk)
    gm = jnp.where((lane >= M_HEADS) & (lane < 2 * M_HEADS), fcum, gates)
    gmt_ref[...] = gm.T
    hi = gm.astype(BF16)
    r1 = gm - hi.astype(F32)
    mid = r1.astype(BF16)
    lo = (r1 - mid.astype(F32)).astype(BF16)
    slots = 2 * M_HEADS * LANES
    spread = (lax.broadcasted_iota(jnp.int32, (3 * LANES, slots), 0) % LANES
              == lax.broadcasted_iota(jnp.int32, (3 * LANES, slots), 1) // LANES).astype(BF16)
    gmb_ref[...] = _dot(jnp.concatenate([hi, mid, lo], axis=1), spread)
    za = _dot(raw.astype(BF16), wal_ref[...]) + bal_ref[...]
    gcum_ref[...] = _segment_cumsum(_log_sigmoid(za) * (1.0 / G_TAU), g_chunk)


def _proj_aux(hn, l, w, gb, wal, bal, tm, m_chunk, g_chunk):
    n = hn.shape[0]
    return pl.pallas_call(
        functools.partial(_proj_aux_kernel, m_chunk=m_chunk, g_chunk=g_chunk), grid=(n // tm,),
        in_specs=[_rows(tm, D_MODEL), _of_layer(l, (D_MODEL, SMALL_W)), _of_layer(l, (1, LANES)),
                  _of_layer(l, (LANES, G_KWP)), _of_layer(l, (1, G_KWP))],
        out_specs=[pl.BlockSpec((S_BLOCKS, tm, LANES), lambda i: (0, i, 0)),
                   _rows(tm, 2 * M_HEADS * LANES),
                   pl.BlockSpec((LANES, tm), lambda i: (0, i)), _rows(tm, G_KWP)],
        out_shape=[jax.ShapeDtypeStruct((S_BLOCKS, n, LANES), F32),
                   jax.ShapeDtypeStruct((n, 2 * M_HEADS * LANES), F32),
                   jax.ShapeDtypeStruct((LANES, n), F32),
                   jax.ShapeDtypeStruct((n, G_KWP), F32)],
        compiler_params=_params("arbitrary"), name="proj_aux",
    )(hn, w, gb, wal, bal)


def _mlstm_kernel(q_ref, k_ref, v_ref, oz_ref, gmb_ref, gmt_ref, nw_ref, out_ref, c_scr, m_scr, *, tc):
    @pl.when(pl.program_id(1) == 0)
    def _():
        c_scr[...] = jnp.zeros_like(c_scr)
        m_scr[...] = jnp.zeros_like(m_scr)

    row = lax.broadcasted_iota(jnp.int32, (tc, tc), 0)
    col = lax.broadcasted_iota(jnp.int32, (tc, tc), 1)
    causal = col <= row
    real = lax.broadcasted_iota(jnp.int32, (tc, M_HDP), 1) < M_HD
    ones_tile = jnp.ones((tc, LANES), BF16)

    def lanes(x, n):
        return jnp.concatenate([x] * n, axis=1)

    for h in range(M_HEADS):
        sl = slice(h * M_HDP, (h + 1) * M_HDP)
        qb = q_ref[:, sl]
        kb = k_ref[:, sl]
        vext = jnp.concatenate([v_ref[:, sl], ones_tile], axis=1)
        i_b = gmb_ref[:, h * LANES:(h + 1) * LANES]
        bcum = gmb_ref[:, (M_HEADS + h) * LANES:(M_HEADS + h + 1) * LANES]
        g_row = gmt_ref[h:h + 1, :] - gmt_ref[M_HEADS + h:M_HEADS + h + 1, :]
        m_prev = m_scr[h, 0:1, :]

        inter = bcum + m_prev
        m_t = jnp.maximum(inter, bcum + jnp.max(jnp.where(causal, g_row, NEG), axis=1, keepdims=True))
        w_intra = jnp.exp(jnp.where(causal, g_row + lanes(bcum - m_t, tc // LANES), NEG))
        w_inter = jnp.exp(inter - m_t)
        s = _dot_nt(qb, kb) * w_intra
        c_mat = c_scr[h]
        num = _dot(s.astype(BF16), vext) + lanes(w_inter, 3) * _dot(qb, c_mat.astype(BF16))
        den = num[:, M_HDP:]
        rden = 1.0 / jnp.maximum(jnp.abs(den), jnp.exp(-m_t))
        num = num[:, :M_HDP]

        b_end = bcum[tc - 1:tc, :]
        to_end = b_end - bcum + i_b
        m_new = jnp.maximum(b_end + m_prev, jnp.max(to_end, axis=0, keepdims=True))
        wk = jnp.exp(to_end - m_new)
        w_prev = jnp.exp(b_end + m_prev - m_new)
        c_scr[h] = lanes(w_prev, 3) * c_mat + _dot_tn((lanes(wk, 2) * kb.astype(F32)).astype(BF16), vext)
        m_scr[h] = jnp.broadcast_to(m_new, (8, LANES))

        sc = slice(h * M_HD, (h + 1) * M_HD)
        mean = jnp.sum(jnp.where(real, num, 0.0), axis=1, keepdims=True) * (1.0 / M_HD)
        xc = jnp.where(real, num - mean, 0.0)
        var = jnp.sum(xc * xc, axis=1, keepdims=True) * (1.0 / M_HD)
        scale = rden * lax.rsqrt(rden * rden * var + EPS)
        ha = (xc * lanes(scale, 2))[:, :M_HD]
        out_ref[:, sc] = (ha * nw_ref[:, sc] * oz_ref[:, sc].astype(F32)).astype(BF16)


def _mlstm(q, k, v, oz, gmb, gmt, l, nw, batch, seq, tc):
    n = batch * seq
    nt = seq // tc
    blk = pl.BlockSpec((tc, M_WP), lambda b, t: (b * nt + t, 0))
    compact = pl.BlockSpec((tc, M_W), lambda b, t: (b * nt + t, 0))
    return pl.pallas_call(
        functools.partial(_mlstm_kernel, tc=tc),
        grid=(batch, nt),
        in_specs=[blk, blk, blk, compact,
                  pl.BlockSpec((tc, 2 * M_HEADS * LANES), lambda b, t: (b * nt + t, 0)),
                  pl.BlockSpec((LANES, tc), lambda b, t: (0, b * nt + t)),
                  _of_layer(l, (1, M_W))],
        out_specs=compact,
        out_shape=jax.ShapeDtypeStruct((n, M_W), BF16),
        scratch_shapes=[pltpu.VMEM((M_HEADS, M_HDP, M_HDP + LANES), F32),
                        pltpu.VMEM((M_HEADS, 8, LANES), F32)],
        compiler_params=_params("arbitrary", "arbitrary"),
        name="mlstm",
    )(q, k, v, oz, gmb, gmt, nw)


def _gla_kernel(q_ref, k_ref, v_ref, z_ref, b_ref, nw_ref,
                out_ref, kbuf, bbuf, attn_scr, st_scr, *, t, cs):
    @pl.when(pl.program_id(1) == 0)
    def _():
        st_scr[...] = jnp.zeros_like(st_scr)

    row = lax.broadcasted_iota(jnp.int32, (t, t), 0)
    col = lax.broadcasted_iota(jnp.int32, (t, t), 1)
    causal = col <= row

    def operands(h):
        sl = slice(h * G_DKP, (h + 1) * G_DKP)
        return q_ref[:, sl].astype(F32), k_ref[:, sl].astype(F32), b_ref[:, sl]

    safe = jnp.min(b_ref[t - 1:t, :]) >= G_SAFE_LOG_DECAY

    @pl.when(safe)
    def _():
        for h in range(G_HEADS):
            qs, kk, b = operands(h)
            a = _dot_nt((qs * jnp.exp(b)).astype(BF16), (kk * jnp.exp(-b)).astype(BF16))
            attn_scr[h] = jnp.where(causal, a, 0.0)

    @pl.when(jnp.logical_not(safe))
    def _():
        rowv = lax.broadcasted_iota(jnp.int32, (t, G_DKP), 0)
        sub = rowv % cs
        lag = row - col
        kbuf[0:cs, :] = jnp.zeros((cs, G_DKP), F32)
        bbuf[0:cs, :] = jnp.zeros((cs, G_DKP), F32)
        for h in range(G_HEADS):
            qs, kk, b = operands(h)
            qparts, kparts = [], []
            for i in range(1, t // cs):
                r = b[i * cs - 1:i * cs, :]
                inblk = (rowv >= i * cs) & (rowv < (i + 1) * cs)
                qparts.append((qs * jnp.exp(jnp.where(inblk, b - r, NEG))).astype(BF16))
                kparts.append((kk * jnp.exp(jnp.where(rowv < i * cs, r - b, NEG))).astype(BF16))
            attn = _dot_nt(jnp.concatenate(qparts, axis=1), jnp.concatenate(kparts, axis=1))
            kbuf[cs:cs + t, :] = kk
            bbuf[cs:cs + t, :] = b
            for j in range(cs):
                ksh = kbuf[cs - j:cs - j + t, :]
                bsh = bbuf[cs - j:cs - j + t, :]
                e = jnp.exp(jnp.where(sub >= j, b - bsh, NEG))
                dj = jnp.sum(qs * ksh * e, axis=1, keepdims=True)
                attn = attn + jnp.where(lag == j, dj, 0.0)
            attn_scr[h] = attn

    for h in range(G_HEADS):
        slv = slice(h * G_DVP, (h + 1) * G_DVP)
        qs, kk, b = operands(h)
        v = v_ref[:, slv]
        blast = b[t - 1:t, :]
        st = st_scr[h]
        o = (_dot_nt((qs * jnp.exp(b)).astype(BF16), st.astype(BF16))
             + _dot(attn_scr[h].astype(BF16), v))
        kdec = (kk * jnp.exp(blast - b)).astype(BF16)
        st_scr[h] = jnp.exp(blast) * st + _dot_tn(v, kdec)

        sc = slice(h * G_DV, (h + 1) * G_DV)
        var = jnp.sum(o * o, axis=1, keepdims=True) * (1.0 / G_DV)
        hb = (o * lax.rsqrt(var + EPS))[:, :G_DV] * nw_ref[:, sc]
        out_ref[:, sc] = (hb * z_ref[:, sc].astype(F32)).astype(BF16)


def _gla(q, k, v, z, bcum, l, nw, batch, seq, t, cs):
    n = batch * seq
    nt = seq // t

    def blk(width):
        return pl.BlockSpec((t, width), lambda b, c: (b * nt + c, 0))

    return pl.pallas_call(
        functools.partial(_gla_kernel, t=t, cs=cs),
        grid=(batch, nt),
        in_specs=[blk(G_KWP), blk(G_KWP), blk(G_VWP), blk(G_VW), blk(G_KWP),
                  _of_layer(l, (1, G_VW))],
        out_specs=blk(G_VW),
        out_shape=jax.ShapeDtypeStruct((n, G_VW), BF16),
        scratch_shapes=[pltpu.VMEM((t + cs, G_DKP), F32), pltpu.VMEM((t + cs, G_DKP), F32),
                        pltpu.VMEM((G_HEADS, t, t), F32),
                        pltpu.VMEM((G_HEADS, G_DVP, G_DKP), F32)],
        compiler_params=_params("arbitrary", "arbitrary"),
        name="gla",
    )(q, k, v, z, bcum, nw)


def _gelu_tanh(x):
    return 0.5 * x * (1.0 + jnp.tanh(math.sqrt(2.0 / math.pi) * (x + 0.044715 * (x * x * x))))


def _s5_kernel(u_ref, kms_ref, mo_ref, p_ref, d_ref, y_ref, carry, *, nch):
    @pl.when(pl.program_id(2) == 0)
    def _():
        carry[...] = jnp.zeros_like(carry)

    qw = S_Q * LANES
    ucat = jnp.concatenate([u_ref[0, pl.ds(s, nch, stride=S_Q), :] for s in range(S_Q)],
                           axis=1).astype(BF16)
    r = _dot(ucat, kms_ref[0])
    y = r[:, :qw]
    xr = r[:, qw:qw + S_SW]
    xi = r[:, qw + S_SW:]

    rowi = lax.broadcasted_iota(jnp.int32, (nch, S_SW), 0)
    d = 1
    while d < nch:
        pr = p_ref[0, d - 1:d, :S_SW]
        pi = p_ref[0, d - 1:d, S_SW:]
        keep = rowi >= d
        sr = jnp.where(keep, pltpu.roll(xr, d, 0), 0.0)
        si = jnp.where(keep, pltpu.roll(xi, d, 0), 0.0)
        xr, xi = xr + (pr * sr - pi * si), xi + (pr * si + pi * sr)
        d *= 2
    cr = carry[0:1, :S_SW]
    ci = carry[0:1, S_SW:]
    pr = p_ref[0, :, :S_SW]
    pi = p_ref[0, :, S_SW:]
    xr, xi = xr + (pr * cr - pi * ci), xi + (pr * ci + pi * cr)
    first = rowi == 0
    xpr = jnp.where(first, cr, pltpu.roll(xr, 1, 0))
    xpi = jnp.where(first, ci, pltpu.roll(xi, 1, 0))
    carry[0:1, :S_SW] = xr[nch - 1:nch, :]
    carry[0:1, S_SW:] = xi[nch - 1:nch, :]

    y = y + _dot(jnp.concatenate([xpr, xpi], axis=1).astype(BF16), mo_ref[0])
    for t in range(S_Q):
        y_ref[0, pl.ds(t, nch, stride=S_Q), :] = y[:, t * LANES:(t + 1) * LANES]
    y_ref[0] = _gelu_tanh(y_ref[0] + d_ref[0] * u_ref[0])


def _s5(cu, l, kms, mo, ptab, dskip, batch, seq, tm):
    n = batch * seq
    nt = seq // tm
    nch = tm // S_Q
    qw = S_Q * LANES

    def table(rows, cols):
        return pl.BlockSpec((None, 1, rows, cols), lambda j, b, t: (l, j, 0, 0))

    return pl.pallas_call(
        functools.partial(_s5_kernel, nch=nch),
        grid=(S_BLOCKS, batch, nt),
        in_specs=[pl.BlockSpec((1, tm, LANES), lambda j, b, t: (j, b * nt + t, 0)),
                  table(qw, qw + 2 * S_SW), table(2 * S_SW, qw), table(nch, 2 * S_SW), table(1, LANES)],
        out_specs=pl.BlockSpec((1, tm, LANES), lambda j, b, t: (j, b * nt + t, 0)),
        out_shape=jax.ShapeDtypeStruct((S_BLOCKS, n, LANES), F32),
        scratch_shapes=[pltpu.VMEM((8, 2 * S_SW), F32)],
        compiler_params=_params("arbitrary", "arbitrary", "arbitrary"),
        name="s5",
    )(cu, kms, mo, ptab, dskip)


def _s5_expand_kernel(kt_ref, msr_ref, msi_ref, mor_ref, moi_ref, kms_ref, mo_ref):
    q = S_Q
    qw = q * LANES

    def embed(m, rows, width):
        shape = (width, S_GPB * width)
        tile = (lax.broadcasted_iota(jnp.int32, shape, 1) % width
                == lax.broadcasted_iota(jnp.int32, shape, 0)).astype(BF16)
        oshape = (S_GPB * rows, S_GPB * width)
        diag = (lax.broadcasted_iota(jnp.int32, oshape, 0) // rows
                == lax.broadcasted_iota(jnp.int32, oshape, 1) // width)
        return jnp.where(diag, _dot(m, tile), 0.0).astype(BF16)

    kd = [embed(kt_ref[0, 0, tau], S_GROUP, S_GROUP) for tau in range(q)]
    zero = jnp.zeros((LANES, LANES), BF16)
    for s in range(q):
        rows = slice(s * LANES, (s + 1) * LANES)
        for t in range(q):
            kms_ref[0, 0, rows, t * LANES:(t + 1) * LANES] = kd[t - s] if t >= s else zero
        kms_ref[0, 0, rows, qw:qw + S_SW] = embed(msr_ref[0, 0, s], S_GROUP, S_STATE)
        kms_ref[0, 0, rows, qw + S_SW:] = embed(msi_ref[0, 0, s], S_GROUP, S_STATE)
    for t in range(q):
        cols = slice(t * LANES, (t + 1) * LANES)
        mo_ref[0, 0, 0:S_SW, cols] = embed(mor_ref[0, 0, t], S_STATE, S_GROUP)
        mo_ref[0, 0, S_SW:, cols] = embed(moi_ref[0, 0, t], S_STATE, S_GROUP)


def _s5_expand(kt, msr, msi, mor, moi):
    depth = kt.shape[0]
    qw = S_Q * LANES

    def blk(a):
        return pl.BlockSpec((1, 1) + a.shape[2:], lambda d, j: (d, j, 0, 0, 0))

    return pl.pallas_call(
        _s5_expand_kernel, grid=(depth, S_BLOCKS),
        in_specs=[blk(a) for a in (kt, msr, msi, mor, moi)],
        out_specs=[pl.BlockSpec((1, 1, qw, qw + 2 * S_SW), lambda d, j: (d, j, 0, 0)),
                   pl.BlockSpec((1, 1, 2 * S_SW, qw), lambda d, j: (d, j, 0, 0))],
        out_shape=[jax.ShapeDtypeStruct((depth, S_BLOCKS, qw, qw + 2 * S_SW), BF16),
                   jax.ShapeDtypeStruct((depth, S_BLOCKS, 2 * S_SW, qw), BF16)],
        compiler_params=_params("arbitrary", "arbitrary"), name="s5_expand",
    )(kt, msr, msi, mor, moi)


def _s5_tables(lam_re, lam_im, log_dt, b_re, b_im, c_re, c_im, nch):
    q = S_Q
    dep = lam_re.shape[0]
    lr = jnp.minimum(lam_re.astype(F32), -1e-4)
    li = lam_im.astype(F32)
    dt = jnp.exp(log_dt.astype(F32))[..., None]
    mag = jnp.exp(lr * dt)
    ab_re = mag * jnp.cos(li * dt)
    ab_im = mag * jnp.sin(li * dt)
    nr = ab_re - 1.0
    den = lr * lr + li * li
    coef_re = ((nr * lr + ab_im * li) / den)[:, :, None, :]
    coef_im = ((ab_im * lr - nr * li) / den)[:, :, None, :]
    brt = jnp.swapaxes(b_re.astype(F32), -1, -2)
    bit = jnp.swapaxes(b_im.astype(F32), -1, -2)
    bb_re = coef_re * brt - coef_im * bit
    bb_im = coef_re * bit + coef_im * brt

    def apow(nvals):
        e = jnp.asarray(nvals, F32)[:, None, None, None]
        m = jnp.exp(e * (lr * dt))
        ang = e * (li * dt)
        return m * jnp.cos(ang), m * jnp.sin(ang)

    def per_block(m, rows):
        m = m.reshape(q, dep, S_BLOCKS, S_GPB * rows, m.shape[-1])
        return m.transpose(1, 2, 0, 3, 4).astype(BF16)

    ar, ai = apow(list(range(q + 1)))
    cr, ci = c_re.astype(F32), c_im.astype(F32)
    ca_re = cr[None] * ar[:, :, :, None, :] - ci[None] * ai[:, :, :, None, :]
    ca_im = cr[None] * ai[:, :, :, None, :] + ci[None] * ar[:, :, :, None, :]
    kt = jnp.sum(ca_re[:q, :, :, None, :, :] * bb_re[None, :, :, :, None, :]
                 - ca_im[:q, :, :, None, :, :] * bb_im[None, :, :, :, None, :], axis=-1)
    pw_r = ar[:q][::-1][:, :, :, None, :]
    pw_i = ai[:q][::-1][:, :, :, None, :]
    kms, mo = _s5_expand(per_block(kt, S_GROUP),
                         per_block(pw_r * bb_re - pw_i * bb_im, S_GROUP),
                         per_block(pw_r * bb_im + pw_i * bb_re, S_GROUP),
                         per_block(jnp.swapaxes(ca_re[1:], -1, -2), S_STATE),
                         per_block(jnp.swapaxes(-ca_im[1:], -1, -2), S_STATE))
    lo_n = min(32, nch)
    lr_, li_ = apow([q * (c + 1) for c in range(lo_n)])
    hr_, hi_ = apow([q * lo_n * c for c in range(nch // lo_n)])
    pr = (hr_[:, None] * lr_[None] - hi_[:, None] * li_[None]).reshape((nch,) + lr_.shape[1:])
    pi = (hr_[:, None] * li_[None] + hi_[:, None] * lr_[None]).reshape((nch,) + lr_.shape[1:])
    pr = pr.reshape(nch, dep, S_BLOCKS, S_SW).transpose(1, 2, 0, 3)
    pi = pi.reshape(nch, dep, S_BLOCKS, S_SW).transpose(1, 2, 0, 3)
    ptab = jnp.concatenate([pr, pi], axis=3)
    return kms, mo, ptab


def _merge_kernel(ya_ref, yb_ref, yc_ref, cz_ref, g_ref, x_ref, wa_ref, wb_ref, wglu_ref, wc_ref,
                  wo_ref, nw_ref, *out_refs, final):
    yc0 = jnp.concatenate([yc_ref[j] for j in range(S_BLOCKS)], axis=1)
    glu = _dot(yc0.astype(BF16), wglu_ref[...])
    yc = yc0 * _sigmoid(glu) * cz_ref[...].astype(F32)
    merged = (g_ref[:, 0:D_MODEL].astype(F32) * _dot(ya_ref[...], wa_ref[...])
              + g_ref[:, D_MODEL:2 * D_MODEL].astype(F32) * _dot(yb_ref[...], wb_ref[...])
              + g_ref[:, 2 * D_MODEL:].astype(F32) * _dot(yc.astype(BF16), wc_ref[...]))
    out = x_ref[...] + _dot(merged.astype(BF16), wo_ref[...])
    ms = jnp.mean(out * out, axis=-1, keepdims=True)
    normed = out * lax.rsqrt(ms + EPS) * nw_ref[...]
    if final:
        out_refs[0][...] = normed
    else:
        out_refs[0][...] = out
        out_refs[1][...] = normed.astype(BF16)


def _merge(ya, yb, yc, cz, g, x2, l, wa, wb, wglu, wc, wo, nw, tm, final):
    n = x2.shape[0]
    x_out = jax.ShapeDtypeStruct((n, D_MODEL), F32)
    hn_out = jax.ShapeDtypeStruct((n, D_MODEL), BF16)
    return pl.pallas_call(
        functools.partial(_merge_kernel, final=final),
        grid=(n // tm,),
        in_specs=[_rows(tm, M_W), _rows(tm, G_VW),
                  pl.BlockSpec((S_BLOCKS, tm, LANES), lambda i: (0, i, 0)),
                  _rows(tm, S_W), _rows(tm, 3 * D_MODEL), _rows(tm, D_MODEL),
                  _of_layer(l, (M_W, D_MODEL)), _of_layer(l, (G_VW, D_MODEL)), _of_layer(l, (S_W, S_W)),
                  _of_layer(l, (S_W, D_MODEL)), _of_layer(l, (D_MODEL, D_MODEL)), _whole((1, D_MODEL))],
        out_specs=[_rows(tm, D_MODEL)] if final else [_rows(tm, D_MODEL), _rows(tm, D_MODEL)],
        out_shape=[x_out] if final else [x_out, hn_out],
        compiler_params=_params("arbitrary"),
        name="merge",
    )(ya, yb, yc, cz, g, x2, wa, wb, wglu, wc, wo, nw)


def _pad_heads(w, heads, d, dp):
    zeros = jnp.zeros(w.shape[:-1] + (dp - d,), w.dtype)
    parts = []
    for h in range(heads):
        parts += [w[..., h * d:(h + 1) * d], zeros]
    return jnp.concatenate(parts, axis=-1)


def _prepare(p, nch):
    depth = p["w_in"].shape[0]
    w_mlstm, w_plain, w_gate, w_small = _wprep(jnp.swapaxes(p["w_in"], 1, 2), LANES)
    gkp = functools.partial(_pad_heads, heads=G_HEADS, d=G_DK, dp=G_DKP)
    kms, mo, ptab = _s5_tables(p["s5_lam_re"], p["s5_lam_im"], p["s5_log_dt"], p["s5_B_re"], p["s5_B_im"],
                               p["s5_C_re"], p["s5_C_im"], nch)
    gate_b = p["mlstm_gate_b"]
    return dict(
        w_mlstm=w_mlstm, w_plain=w_plain, w_gate=w_gate, w_small=w_small,
        cq=p["mlstm_conv"][:, :, :M_W], ck=p["mlstm_conv"][:, :, M_W:],
        gb=jnp.concatenate([gate_b[:, 0], gate_b[:, 1],
                            jnp.zeros((depth, LANES - 2 * M_HEADS), F32)], axis=1)[:, None, :],
        m_nw=p["mlstm_norm"][:, None, :],
        wal=jnp.concatenate([jnp.zeros((depth, 2 * M_HEADS, G_KWP), F32), gkp(p["gla_w_alpha"]),
                             jnp.zeros((depth, LANES - 2 * M_HEADS - G_RANK, G_KWP), F32)],
                            axis=1).astype(BF16),
        bal=gkp(p["gla_b_alpha"])[:, None, :],
        g_nw=p["gla_norm"][:, None, :],
        kms=kms, mo=mo, ptab=ptab,
        dskip=p["s5_D"].reshape(depth, S_BLOCKS, 1, LANES),
        wa=p["w_branch_mlstm"].astype(BF16), wb=p["w_branch_gla"].astype(BF16),
        wglu=p["s5_w_glu"].astype(BF16), wc=p["w_branch_s5"].astype(BF16), wo=p["w_out"].astype(BF16),
    )


def _tiles(batch, seq):
    return dict(proj=min(1024, seq), m_chunk=min(256, seq), g_chunk=min(256, seq), g_sub=16,
                s5=min(4096, seq), merge=min(512, batch * seq))


def _layer(x2, hn, batch, seq, l, w, next_norm, final, tl):
    tm = tl["proj"]
    q, k, oz = _proj_mlstm(hn, l, w["w_mlstm"], w["cq"], w["ck"], seq, tm)
    mv, gv, gq, gk = _proj_plain(hn, l, w["w_plain"], tm)
    gz, czs, gs = _proj_gate(hn, l, w["w_gate"], tm)
    cu_s, gm, gmt, gcum = _proj_aux(hn, l, w["w_small"], w["gb"], w["wal"], w["bal"], tm,
                                    tl["m_chunk"], tl["g_chunk"])
    ya = _mlstm(q, k, mv, oz, gm, gmt, l, w["m_nw"], batch, seq, tl["m_chunk"])
    yb = _gla(gq, gk, gv, gz, gcum, l, w["g_nw"], batch, seq, tl["g_chunk"], tl["g_sub"])
    yc = _s5(cu_s, l, w["kms"], w["mo"], w["ptab"], w["dskip"], batch, seq, tl["s5"])
    return _merge(ya, yb, yc, czs, gs, x2, l, w["wa"], w["wb"], w["wglu"], w["wc"], w["wo"],
                  next_norm.reshape(1, D_MODEL), tl["merge"], final)


def kernel(x, norm_w, w_in, mlstm_conv, mlstm_gate_b, mlstm_norm, gla_w_alpha, gla_b_alpha, gla_norm,
           s5_lam_re, s5_lam_im, s5_log_dt, s5_B_re, s5_B_im, s5_C_re, s5_C_im, s5_D, s5_w_glu,
           w_branch_mlstm, w_branch_gla, w_branch_s5, w_out, final_norm):
    batch, seq, _ = x.shape
    depth = norm_w.shape[0]
    tl = _tiles(batch, seq)
    prep = _prepare(dict(w_in=w_in, mlstm_conv=mlstm_conv, mlstm_gate_b=mlstm_gate_b, mlstm_norm=mlstm_norm,
                         gla_w_alpha=gla_w_alpha, gla_b_alpha=gla_b_alpha, gla_norm=gla_norm,
                         s5_lam_re=s5_lam_re, s5_lam_im=s5_lam_im, s5_log_dt=s5_log_dt, s5_B_re=s5_B_re,
                         s5_B_im=s5_B_im, s5_C_re=s5_C_re, s5_C_im=s5_C_im, s5_D=s5_D, s5_w_glu=s5_w_glu,
                         w_branch_mlstm=w_branch_mlstm, w_branch_gla=w_branch_gla,
                         w_branch_s5=w_branch_s5, w_out=w_out), tl["s5"] // S_Q)
    x2 = x.reshape(batch * seq, D_MODEL)
    hn = _rms(x2, norm_w[0].reshape(1, D_MODEL), min(1024, batch * seq))
    for l in range(depth):
        final = l == depth - 1
        res = _layer(x2, hn, batch, seq, l, prep, final_norm if final else norm_w[l + 1], final, tl)
        if final:
            x2 = res[0]
        else:
            x2, hn = res
    return x2.reshape(batch, seq, D_MODEL)
```

```python
import functools
import math

import jax
import jax.numpy as jnp
from jax import lax
from jax.experimental import pallas as pl
from jax.experimental.pallas import tpu as pltpu

F32 = jnp.float32
BF16 = jnp.bfloat16
HIGHEST = lax.Precision.HIGHEST

EPS = 1e-6
D_MODEL = 1024
LANES = 128
M_HEADS = 4
M_HD = 192
M_HDP = 256
M_W = M_HEADS * M_HD
M_WP = M_HEADS * M_HDP
M_CONV = 4
G_HEADS = 4
G_DK = 96
G_DKP = 128
G_DV = 192
G_DVP = 256
G_KW = G_HEADS * G_DK
G_KWP = G_HEADS * G_DKP
G_VW = G_HEADS * G_DV
G_VWP = G_HEADS * G_DVP
G_RANK = 16
G_TAU = 16.0
S_W = 512
S_GROUP = 16
S_GROUPS = 32
S_STATE = 64
S_BLOCKS = S_W // LANES
S_GPB = LANES // S_GROUP
S_SW = S_GPB * S_STATE
S_Q = 8
S_SUB = 8

IN_SIZES = (M_W, M_W, M_W, M_W, M_HEADS, M_HEADS, M_W,
            G_KW, G_KW, G_VW, G_RANK, G_VW,
            S_W, S_W, 3 * D_MODEL)
SMALL_W = S_W + LANES
CW = 512

NEG = -1e30
G_SAFE_LOG_DECAY = -60.0
VMEM_LIMIT = 56 * 1024 * 1024


def _sigmoid(x):
    return 0.5 * jnp.tanh(0.5 * x) + 0.5


def _silu(x):
    h = 0.5 * x
    return h * jnp.tanh(h) + h


def _log_sigmoid(x):
    return jnp.minimum(x, 0.0) - jnp.log1p(jnp.exp(-jnp.abs(x)))


def _dot(a, b):
    return jnp.dot(a, b, preferred_element_type=F32)


def _dot_nt(a, b):
    return lax.dot_general(a, b, (((1,), (1,)), ((), ())), preferred_element_type=F32)


def _dot_tn(a, b):
    return lax.dot_general(a, b, (((0,), (0,)), ((), ())), preferred_element_type=F32)


def _params(*sem):
    return pltpu.CompilerParams(dimension_semantics=sem, vmem_limit_bytes=VMEM_LIMIT)


def _rows(tm, width):
    return pl.BlockSpec((tm, width), lambda i: (i, 0))


def _whole(shape):
    return pl.BlockSpec(shape, lambda i: (0,) * len(shape))


def _of_layer(l, shape):
    return pl.BlockSpec((None,) + shape, lambda *_: (l,) + (0,) * len(shape))


def _rms_kernel(x_ref, nw_ref, o_ref):
    x = x_ref[...]
    ms = jnp.mean(x * x, axis=-1, keepdims=True)
    o_ref[...] = (x * lax.rsqrt(ms + EPS) * nw_ref[...]).astype(BF16)


def _rms(x2, nw, tm):
    n = x2.shape[0]
    return pl.pallas_call(
        _rms_kernel, grid=(n // tm,),
        in_specs=[_rows(tm, D_MODEL), _whole((1, D_MODEL))],
        out_specs=_rows(tm, D_MODEL),
        out_shape=jax.ShapeDtypeStruct((n, D_MODEL), BF16),
        compiler_params=_params("arbitrary"), name="rms",
    )(x2, nw)


W_OFF = {}
_o = 0
for _name, _size in zip(("aq", "ak", "av", "ao", "ai", "af", "az", "bq", "bk", "bv", "ba", "bz", "cu", "cz", "g"),
                        IN_SIZES):
    W_OFF[_name] = _o
    _o += _size
IN_WIDTH = _o
W_MLSTM, W_PLAIN, W_GATE = 4 * M_W, M_W + G_VW + 2 * G_KW, G_VW + S_W + 3 * D_MODEL


def _wprep_kernel(w_ref, wm_ref, wp_ref, wg_ref, ws_ref):
    kb = w_ref.shape[2]

    def put(dst, dst_off, parts, width):
        xs = [w_ref[0, off:off + rows, :] for off, rows in parts]
        used = sum(rows for _, rows in parts)
        if used < width:
            xs.append(jnp.zeros((width - used, kb), F32))
        x = xs[0] if len(xs) == 1 else jnp.concatenate(xs, axis=0)
        dst[0, :, dst_off:dst_off + width] = x.T.astype(BF16)

    def group(dst, names):
        off = 0
        for name in names:
            width = IN_SIZES[list(W_OFF).index(name)]
            for c in range(0, width, LANES):
                put(dst, off + c, [(W_OFF[name] + c, LANES)], LANES)
            off += width

    group(wm_ref, ("aq", "ak", "ao", "az"))
    group(wp_ref, ("av", "bv", "bq", "bk"))
    group(wg_ref, ("bz", "cz", "g"))
    group(ws_ref, ("cu",))
    put(ws_ref, S_W, [(W_OFF["ai"], 2 * M_HEADS), (W_OFF["ba"], G_RANK)], LANES)


def _wprep(w_t, kb):
    depth = w_t.shape[0]
    widths = (W_MLSTM, W_PLAIN, W_GATE, SMALL_W)
    return pl.pallas_call(
        _wprep_kernel, grid=(depth, D_MODEL // kb),
        in_specs=[pl.BlockSpec((1, IN_WIDTH, kb), lambda l, i: (l, 0, i))],
        out_specs=[pl.BlockSpec((1, kb, wd), lambda l, i: (l, i, 0)) for wd in widths],
        out_shape=[jax.ShapeDtypeStruct((depth, D_MODEL, wd), BF16) for wd in widths],
        compiler_params=_params("arbitrary", "arbitrary"), name="wprep",
    )(w_t)


def _scatter_heads(dst, y, heads, d, dp, fill):
    rows = y.shape[0]
    for h in range(heads):
        dst[:, h * dp:h * dp + d] = y[:, h * d:(h + 1) * d].astype(BF16)
        dst[:, h * dp + d:(h + 1) * dp] = jnp.full((rows, dp - d), fill, BF16)


def _proj_mlstm_kernel(hn_ref, w_ref, cq_ref, ck_ref, q_ref, k_ref, oz_ref, halo_x, halo_b, *, tm,
                       tiles_per_seq):
    @pl.when(pl.program_id(0) % tiles_per_seq == 0)
    def _():
        halo_x[...] = jnp.zeros_like(halo_x)
        halo_b[...] = jnp.zeros_like(halo_b)

    hn = hn_ref[...]
    row8 = lax.broadcasted_iota(jnp.int32, (8, M_W), 0)

    def shift(x, prev8, d):
        rolled = pltpu.roll(x, d, 0)
        head = jnp.where(row8 < d, pltpu.roll(prev8, d, 0), rolled[0:8])
        return jnp.concatenate([head, rolled[8:]], axis=0)

    for c, (cw_ref, dst, scale) in enumerate(((cq_ref, q_ref, 1.0), (ck_ref, k_ref, M_HD ** -0.5))):
        x = _dot(hn, w_ref[:, c * M_W:(c + 1) * M_W])
        x1 = shift(x, halo_x[c], 1)
        pair = cw_ref[1:2, :] * x + cw_ref[0:1, :] * x1
        y = _silu(cw_ref[3:4, :] * x + cw_ref[2:3, :] * x1 + shift(pair, halo_b[c], 2))
        halo_x[c] = x[tm - 8:tm]
        halo_b[c] = pair[tm - 8:tm]
        _scatter_heads(dst, y if scale == 1.0 else y * scale, M_HEADS, M_HD, M_HDP, 0.0)
    o = _dot(hn, w_ref[:, 2 * M_W:3 * M_W])
    z = _dot(hn, w_ref[:, 3 * M_W:])
    oz_ref[...] = (_sigmoid(o) * _silu(z)).astype(BF16)


def _proj_mlstm(hn, l, w, cq, ck, seq, tm):
    n = hn.shape[0]
    padded = jax.ShapeDtypeStruct((n, M_WP), BF16)
    return pl.pallas_call(
        functools.partial(_proj_mlstm_kernel, tm=tm, tiles_per_seq=seq // tm),
        grid=(n // tm,),
        in_specs=[_rows(tm, D_MODEL), _of_layer(l, (D_MODEL, W_MLSTM)), _of_layer(l, (M_CONV, M_W)),
                  _of_layer(l, (M_CONV, M_W))],
        out_specs=[_rows(tm, M_WP), _rows(tm, M_WP), _rows(tm, M_W)],
        out_shape=[padded, padded, jax.ShapeDtypeStruct((n, M_W), BF16)],
        scratch_shapes=[pltpu.VMEM((2, 8, M_W), F32)] * 2,
        compiler_params=_params("arbitrary"), name="proj_mlstm",
    )(hn, w, cq, ck)


def _proj_plain_kernel(hn_ref, w_ref, mv_ref, gv_ref, gq_ref, gk_ref):
    hn = hn_ref[...]
    _scatter_heads(mv_ref, _dot(hn, w_ref[:, 0:M_W]), M_HEADS, M_HD, M_HDP, 1.0)
    _scatter_heads(gv_ref, _dot(hn, w_ref[:, M_W:M_W + G_VW]), G_HEADS, G_DV, G_DVP, 0.0)
    off = M_W + G_VW
    _scatter_heads(gq_ref, _dot(hn, w_ref[:, off:off + G_KW]) * (G_DK ** -0.5), G_HEADS, G_DK, G_DKP, 0.0)
    _scatter_heads(gk_ref, _dot(hn, w_ref[:, off + G_KW:]), G_HEADS, G_DK, G_DKP, 0.0)


def _proj_plain(hn, l, w, tm):
    n = hn.shape[0]
    widths = (M_WP, G_VWP, G_KWP, G_KWP)
    return pl.pallas_call(
        _proj_plain_kernel, grid=(n // tm,),
        in_specs=[_rows(tm, D_MODEL), _of_layer(l, (D_MODEL, W_PLAIN))],
        out_specs=[_rows(tm, wd) for wd in widths],
        out_shape=[jax.ShapeDtypeStruct((n, wd), BF16) for wd in widths],
        compiler_params=_params("arbitrary"), name="proj_plain",
    )(hn, w)


def _proj_gate_kernel(hn_ref, w_ref, gz_ref, cz_ref, g_ref):
    hn = hn_ref[...]
    gz_ref[...] = _silu(_dot(hn, w_ref[:, 0:G_VW])).astype(BF16)
    cz_ref[...] = _silu(_dot(hn, w_ref[:, G_VW:G_VW + S_W])).astype(BF16)
    off = G_VW + S_W
    for c in range(3 * D_MODEL // CW):
        r = _dot(hn, w_ref[:, off + c * CW:off + (c + 1) * CW])
        g_ref[:, c * CW:(c + 1) * CW] = _sigmoid(r).astype(BF16)


def _proj_gate(hn, l, w, tm):
    n = hn.shape[0]
    widths = (G_VW, S_W, 3 * D_MODEL)
    return pl.pallas_call(
        _proj_gate_kernel, grid=(n // tm,),
        in_specs=[_rows(tm, D_MODEL), _of_layer(l, (D_MODEL, W_GATE))],
        out_specs=[_rows(tm, wd) for wd in widths],
        out_shape=[jax.ShapeDtypeStruct((n, wd), BF16) for wd in widths],
        compiler_params=_params("arbitrary"), name="proj_gate",
    )(hn, w)


def _split3(x):
    hi = x.astype(BF16)
    r1 = x - hi.astype(F32)
    mid = r1.astype(BF16)
    return hi, mid, (r1 - mid.astype(F32)).astype(BF16)


def _segment_cumsum(x, seg):
    rows, width = x.shape
    tril = (lax.broadcasted_iota(jnp.int32, (seg, seg), 1)
            <= lax.broadcasted_iota(jnp.int32, (seg, seg), 0)).astype(BF16)
    parts = jnp.concatenate(_split3(x), axis=1)
    out = []
    for s in range(rows // seg):
        acc = _dot(tril, parts[s * seg:(s + 1) * seg])
        out.append(acc[:, :width] + acc[:, width:2 * width] + acc[:, 2 * width:])
    return jnp.concatenate(out, axis=0)


def _proj_aux_kernel(hn_ref, w_ref, gb_ref, wal_ref, bal_ref, cu_ref, gmb_ref, gmt_ref, gcum_ref, *,
                     m_chunk, g_chunk):
    r = _dot(hn_ref[...], w_ref[...])
    for j in range(S_BLOCKS):
        cu_ref[j] = r[:, j * LANES:(j + 1) * LANES]
    slots = 2 * M_HEADS * LANES
    spread = (lax.broadcasted_iota(jnp.int32, (3 * LANES, slots), 0) % LANES
              == lax.broadcasted_iota(jnp.int32, (3 * LANES, slots), 1) // LANES).astype(BF16)
    raw = r[:, S_W:]
    gates = raw + gb_ref[...]
    lane = lax.broadcasted_iota(jnp.int32, gates.shape, 1)
    fgate = (lane >= M_HEADS) & (lane < 2 * M_HEADS)
    gm = jnp.where(fgate, _segment_cumsum(_log_sigmoid(gates), m_chunk), gates)
    gmt_ref[...] = gm.T
    gmb_ref[...] = _dot(jnp.concatenate(_split3(gm), axis=1), spread)
    za = _dot(raw.astype(BF16), wal_ref[...]) + bal_ref[...]
    gcum_ref[...] = _segment_cumsum(_log_sigmoid(za) * (1.0 / G_TAU), g_chunk)


def _proj_aux(hn, l, w, gb, wal, bal, tm, m_chunk, g_chunk):
    n = hn.shape[0]
    return pl.pallas_call(
        functools.partial(_proj_aux_kernel, m_chunk=m_chunk, g_chunk=g_chunk), grid=(n // tm,),
        in_specs=[_rows(tm, D_MODEL), _of_layer(l, (D_MODEL, SMALL_W)), _of_layer(l, (1, LANES)),
                  _of_layer(l, (LANES, G_KWP)), _of_layer(l, (1, G_KWP))],
        out_specs=[pl.BlockSpec((S_BLOCKS, tm, LANES), lambda i: (0, i, 0)),
                   _rows(tm, 2 * M_HEADS * LANES),
                   pl.BlockSpec((LANES, tm), lambda i: (0, i)), _rows(tm, G_KWP)],
        out_shape=[jax.ShapeDtypeStruct((S_BLOCKS, n, LANES), F32),
                   jax.ShapeDtypeStruct((n, 2 * M_HEADS * LANES), F32),
                   jax.ShapeDtypeStruct((LANES, n), F32),
                   jax.ShapeDtypeStruct((n, G_KWP), F32)],
        compiler_params=_params("arbitrary"), name="proj_aux",
    )(hn, w, gb, wal, bal)


def _mlstm_kernel(q_ref, k_ref, v_ref, oz_ref, gmb_ref, gmt_ref, nw_ref, out_ref, c_scr, m_scr, *, tc):
    @pl.when(pl.program_id(1) == 0)
    def _():
        c_scr[...] = jnp.zeros_like(c_scr)
        m_scr[...] = jnp.zeros_like(m_scr)

    row = lax.broadcasted_iota(jnp.int32, (tc, tc), 0)
    col = lax.broadcasted_iota(jnp.int32, (tc, tc), 1)
    causal = col <= row
    real = lax.broadcasted_iota(jnp.int32, (tc, M_HDP), 1) < M_HD
    ones_tile = jnp.ones((tc, LANES), BF16)

    def lanes(x, n):
        return jnp.concatenate([x] * n, axis=1)

    for h in range(M_HEADS):
        sl = slice(h * M_HDP, (h + 1) * M_HDP)
        qb = q_ref[:, sl]
        kb = k_ref[:, sl]
        vext = jnp.concatenate([v_ref[:, sl], ones_tile], axis=1)
        i_b = gmb_ref[:, h * LANES:(h + 1) * LANES]
        bcum = gmb_ref[:, (M_HEADS + h) * LANES:(M_HEADS + h + 1) * LANES]
        g_row = gmt_ref[h:h + 1, :] - gmt_ref[M_HEADS + h:M_HEADS + h + 1, :]
        m_prev = m_scr[h, 0:1, :]

        inter = bcum + m_prev
        m_t = jnp.maximum(inter, bcum + jnp.max(jnp.where(causal, g_row, NEG), axis=1, keepdims=True))
        w_intra = jnp.exp(jnp.where(causal, g_row + lanes(bcum - m_t, tc // LANES), NEG))
        w_inter = jnp.exp(inter - m_t)
        s = _dot_nt(qb, kb) * w_intra
        c_mat = c_scr[h]
        num = _dot(s.astype(BF16), vext) + lanes(w_inter, 3) * _dot(qb, c_mat.astype(BF16))
        den = num[:, M_HDP:]
        rden = 1.0 / jnp.maximum(jnp.abs(den), jnp.exp(-m_t))
        num = num[:, :M_HDP]

        b_end = bcum[tc - 1:tc, :]
        to_end = b_end - bcum + i_b
        m_new = jnp.maximum(b_end + m_prev, jnp.max(to_end, axis=0, keepdims=True))
        wk = jnp.exp(to_end - m_new)
        w_prev = jnp.exp(b_end + m_prev - m_new)
        c_scr[h] = lanes(w_prev, 3) * c_mat + _dot_tn((lanes(wk, 2) * kb.astype(F32)).astype(BF16), vext)
        m_scr[h] = jnp.broadcast_to(m_new, (8, LANES))

        sc = slice(h * M_HD, (h + 1) * M_HD)
        mean = jnp.sum(jnp.where(real, num, 0.0), axis=1, keepdims=True) * (1.0 / M_HD)
        xc = jnp.where(real, num - mean, 0.0)
        var = jnp.sum(xc * xc, axis=1, keepdims=True) * (1.0 / M_HD)
        scale = rden * lax.rsqrt(rden * rden * var + EPS)
        ha = (xc * lanes(scale, 2))[:, :M_HD]
        out_ref[:, sc] = (ha * nw_ref[:, sc] * oz_ref[:, sc].astype(F32)).astype(BF16)


def _mlstm(q, k, v, oz, gmb, gmt, l, nw, batch, seq, tc):
    n = batch * seq
    nt = seq // tc
    blk = pl.BlockSpec((tc, M_WP), lambda b, t: (b * nt + t, 0))
    compact = pl.BlockSpec((tc, M_W), lambda b, t: (b * nt + t, 0))
    return pl.pallas_call(
        functools.partial(_mlstm_kernel, tc=tc),
        grid=(batch, nt),
        in_specs=[blk, blk, blk, compact,
                  pl.BlockSpec((tc, 2 * M_HEADS * LANES), lambda b, t: (b * nt + t, 0)),
                  pl.BlockSpec((LANES, tc), lambda b, t: (0, b * nt + t)),
                  _of_layer(l, (1, M_W))],
        out_specs=compact,
        out_shape=jax.ShapeDtypeStruct((n, M_W), BF16),
        scratch_shapes=[pltpu.VMEM((M_HEADS, M_HDP, M_HDP + LANES), F32),
                        pltpu.VMEM((M_HEADS, 8, LANES), F32)],
        compiler_params=_params("arbitrary", "arbitrary"),
        name="mlstm",
    )(q, k, v, oz, gmb, gmt, nw)


def _gla_kernel(q_ref, k_ref, v_ref, z_ref, b_ref, nw_ref,
                out_ref, kbuf, bbuf, attn_scr, st_scr, *, t, cs):
    @pl.when(pl.program_id(1) == 0)
    def _():
        st_scr[...] = jnp.zeros_like(st_scr)

    row = lax.broadcasted_iota(jnp.int32, (t, t), 0)
    col = lax.broadcasted_iota(jnp.int32, (t, t), 1)
    causal = col <= row

    def operands(h):
        sl = slice(h * G_DKP, (h + 1) * G_DKP)
        return q_ref[:, sl].astype(F32), k_ref[:, sl].astype(F32), b_ref[:, sl]

    safe = jnp.min(b_ref[t - 1:t, :]) >= G_SAFE_LOG_DECAY

    @pl.when(safe)
    def _():
        for h in range(G_HEADS):
            qs, kk, b = operands(h)
            a = _dot_nt((qs * jnp.exp(b)).astype(BF16), (kk * jnp.exp(-b)).astype(BF16))
            attn_scr[h] = jnp.where(causal, a, 0.0)

    @pl.when(jnp.logical_not(safe))
    def _():
        rowv = lax.broadcasted_iota(jnp.int32, (t, G_DKP), 0)
        sub = rowv % cs
        lag = row - col
        kbuf[0:cs, :] = jnp.zeros((cs, G_DKP), F32)
        bbuf[0:cs, :] = jnp.zeros((cs, G_DKP), F32)
        for h in range(G_HEADS):
            qs, kk, b = operands(h)
            qparts, kparts = [], []
            for i in range(1, t // cs):
                r = b[i * cs - 1:i * cs, :]
                inblk = (rowv >= i * cs) & (rowv < (i + 1) * cs)
                qparts.append((qs * jnp.exp(jnp.where(inblk, b - r, NEG))).astype(BF16))
                kparts.append((kk * jnp.exp(jnp.where(rowv < i * cs, r - b, NEG))).astype(BF16))
            attn = _dot_nt(jnp.concatenate(qparts, axis=1), jnp.concatenate(kparts, axis=1))
            kbuf[cs:cs + t, :] = kk
            bbuf[cs:cs + t, :] = b
            for j in range(cs):
                ksh = kbuf[cs - j:cs - j + t, :]
                bsh = bbuf[cs - j:cs - j + t, :]
                e = jnp.exp(jnp.where(sub >= j, b - bsh, NEG))
                dj = jnp.sum(qs * ksh * e, axis=1, keepdims=True)
                attn = attn + jnp.where(lag == j, dj, 0.0)
            attn_scr[h] = attn

    for h in range(G_HEADS):
        slv = slice(h * G_DVP, (h + 1) * G_DVP)
        qs, kk, b = operands(h)
        v = v_ref[:, slv]
        blast = b[t - 1:t, :]
        st = st_scr[h]
        o = (_dot_nt((qs * jnp.exp(b)).astype(BF16), st.astype(BF16))
             + _dot(attn_scr[h].astype(BF16), v))
        kdec = (kk * jnp.exp(blast - b)).astype(BF16)
        st_scr[h] = jnp.exp(blast) * st + _dot_tn(v, kdec)

        sc = slice(h * G_DV, (h + 1) * G_DV)
        var = jnp.sum(o * o, axis=1, keepdims=True) * (1.0 / G_DV)
        hb = (o * lax.rsqrt(var + EPS))[:, :G_DV] * nw_ref[:, sc]
        out_ref[:, sc] = (hb * z_ref[:, sc].astype(F32)).astype(BF16)


def _gla(q, k, v, z, bcum, l, nw, batch, seq, t, cs):
    n = batch * seq
    nt = seq // t

    def blk(width):
        return pl.BlockSpec((t, width), lambda b, c: (b * nt + c, 0))

    return pl.pallas_call(
        functools.partial(_gla_kernel, t=t, cs=cs),
        grid=(batch, nt),
        in_specs=[blk(G_KWP), blk(G_KWP), blk(G_VWP), blk(G_VW), blk(G_KWP),
                  _of_layer(l, (1, G_VW))],
        out_specs=blk(G_VW),
        out_shape=jax.ShapeDtypeStruct((n, G_VW), BF16),
        scratch_shapes=[pltpu.VMEM((t + cs, G_DKP), F32), pltpu.VMEM((t + cs, G_DKP), F32),
                        pltpu.VMEM((G_HEADS, t, t), F32),
                        pltpu.VMEM((G_HEADS, G_DVP, G_DKP), F32)],
        compiler_params=_params("arbitrary", "arbitrary"),
        name="gla",
    )(q, k, v, z, bcum, nw)


def _gelu_tanh(x):
    return 0.5 * x * (1.0 + jnp.tanh(math.sqrt(2.0 / math.pi) * (x + 0.044715 * (x * x * x))))


def _s5_kernel(u_ref, kms_ref, mo_ref, pin_ref, pt_ref, d_ref, y_ref, carry, xs, *, nch):
    @pl.when(pl.program_id(2) == 0)
    def _():
        carry[...] = jnp.zeros_like(carry)

    qw = S_Q * LANES
    ngrp = nch // S_SUB
    ltiles = S_SW // LANES
    ucat = jnp.concatenate([u_ref[0, pl.ds(s, nch, stride=S_Q), :] for s in range(S_Q)],
                           axis=1).astype(BF16)
    r = _dot(ucat, kms_ref[0])
    y = r[:, :qw]
    xr = r[:, qw:qw + S_SW]
    xi = r[:, qw + S_SW:]

    def axpy(xr, xi, pr, pi, sr, si):
        return xr + (pr * sr - pi * si), xi + (pr * si + pi * sr)

    def scan(xr, xi, p_ref, pos, length):
        d = 1
        while d < length:
            keep = pos >= d
            sr = jnp.where(keep, pltpu.roll(xr, d, 0), 0.0)
            si = jnp.where(keep, pltpu.roll(xi, d, 0), 0.0)
            xr, xi = axpy(xr, xi, p_ref[0, d - 1:d, :S_SW], p_ref[0, d - 1:d, S_SW:], sr, si)
            d *= 2
        return xr, xi

    def slabs(part, arr):
        return [(part * ltiles + j, arr[:, j * LANES:(j + 1) * LANES]) for j in range(ltiles)]

    rowi = lax.broadcasted_iota(jnp.int32, (nch, S_SW), 0)
    xr, xi = scan(xr, xi, pin_ref, rowi % S_SUB, S_SUB)
    for part, arr in ((0, xr), (1, xi)):
        for j, piece in slabs(part, arr):
            xs[j] = piece

    def group_ends(part):
        return jnp.concatenate([xs[part * ltiles + j, pl.ds(S_SUB - 1, ngrp, stride=S_SUB), :]
                                for j in range(ltiles)], axis=1)

    rowg = lax.broadcasted_iota(jnp.int32, (ngrp, S_SW), 0)
    er, ei = scan(group_ends(0), group_ends(1), pt_ref, rowg, ngrp)
    cr = carry[0:1, :S_SW]
    ci = carry[0:1, S_SW:]
    er, ei = axpy(er, ei, pt_ref[0, :, :S_SW], pt_ref[0, :, S_SW:], cr, ci)
    first = rowg == 0
    gr = jnp.where(first, cr, pltpu.roll(er, 1, 0))
    gi = jnp.where(first, ci, pltpu.roll(ei, 1, 0))
    carry[0:1, :S_SW] = er[ngrp - 1:ngrp, :]
    carry[0:1, S_SW:] = ei[ngrp - 1:ngrp, :]
    for part, arr in ((0, gr), (1, gi)):
        for j, piece in slabs(part, arr):
            for k in range(S_SUB):
                xs[j, pl.ds(k, ngrp, stride=S_SUB), :] = piece
    sr = jnp.concatenate([xs[j] for j in range(ltiles)], axis=1)
    si = jnp.concatenate([xs[ltiles + j] for j in range(ltiles)], axis=1)
    pwr = jnp.concatenate([pin_ref[0, :, :S_SW]] * ngrp, axis=0)
    pwi = jnp.concatenate([pin_ref[0, :, S_SW:]] * ngrp, axis=0)
    xr, xi = axpy(xr, xi, pwr, pwi, sr, si)

    firstc = rowi == 0
    xpr = jnp.where(firstc, cr, pltpu.roll(xr, 1, 0))
    xpi = jnp.where(firstc, ci, pltpu.roll(xi, 1, 0))
    y = y + _dot(jnp.concatenate([xpr, xpi], axis=1).astype(BF16), mo_ref[0])
    for t in range(S_Q):
        y_ref[0, pl.ds(t, nch, stride=S_Q), :] = y[:, t * LANES:(t + 1) * LANES]
    y_ref[0] = _gelu_tanh(y_ref[0] + d_ref[0] * u_ref[0])


def _s5(cu, l, kms, mo, pin, pt, dskip, batch, seq, tm):
    n = batch * seq
    nt = seq // tm
    nch = tm // S_Q
    qw = S_Q * LANES

    def table(rows, cols):
        return pl.BlockSpec((None, 1, rows, cols), lambda j, b, t: (l, j, 0, 0))

    return pl.pallas_call(
        functools.partial(_s5_kernel, nch=nch),
        grid=(S_BLOCKS, batch, nt),
        in_specs=[pl.BlockSpec((1, tm, LANES), lambda j, b, t: (j, b * nt + t, 0)),
                  table(qw, qw + 2 * S_SW), table(2 * S_SW, qw), table(S_SUB, 2 * S_SW),
                  table(nch // S_SUB, 2 * S_SW), table(1, LANES)],
        out_specs=pl.BlockSpec((1, tm, LANES), lambda j, b, t: (j, b * nt + t, 0)),
        out_shape=jax.ShapeDtypeStruct((S_BLOCKS, n, LANES), F32),
        scratch_shapes=[pltpu.VMEM((8, 2 * S_SW), F32),
                        pltpu.VMEM((2 * S_SW // LANES, nch, LANES), F32)],
        compiler_params=_params("arbitrary", "arbitrary", "arbitrary"),
        name="s5",
    )(cu, kms, mo, pin, pt, dskip)


def _s5_expand_kernel(kt_ref, msr_ref, msi_ref, mor_ref, moi_ref, kms_ref, mo_ref):
    q = S_Q
    qw = q * LANES

    def embed(m, rows, width):
        shape = (width, S_GPB * width)
        tile = (lax.broadcasted_iota(jnp.int32, shape, 1) % width
                == lax.broadcasted_iota(jnp.int32, shape, 0)).astype(BF16)
        oshape = (S_GPB * rows, S_GPB * width)
        diag = (lax.broadcasted_iota(jnp.int32, oshape, 0) // rows
                == lax.broadcasted_iota(jnp.int32, oshape, 1) // width)
        return jnp.where(diag, _dot(m, tile), 0.0).astype(BF16)

    kd = [embed(kt_ref[0, 0, tau], S_GROUP, S_GROUP) for tau in range(q)]
    zero = jnp.zeros((LANES, LANES), BF16)
    for s in range(q):
        rows = slice(s * LANES, (s + 1) * LANES)
        for t in range(q):
            kms_ref[0, 0, rows, t * LANES:(t + 1) * LANES] = kd[t - s] if t >= s else zero
        kms_ref[0, 0, rows, qw:qw + S_SW] = embed(msr_ref[0, 0, s], S_GROUP, S_STATE)
        kms_ref[0, 0, rows, qw + S_SW:] = embed(msi_ref[0, 0, s], S_GROUP, S_STATE)
    for t in range(q):
        cols = slice(t * LANES, (t + 1) * LANES)
        mo_ref[0, 0, 0:S_SW, cols] = embed(mor_ref[0, 0, t], S_STATE, S_GROUP)
        mo_ref[0, 0, S_SW:, cols] = embed(moi_ref[0, 0, t], S_STATE, S_GROUP)


def _s5_expand(kt, msr, msi, mor, moi):
    depth = kt.shape[0]
    qw = S_Q * LANES

    def blk(a):
        return pl.BlockSpec((1, 1) + a.shape[2:], lambda d, j: (d, j, 0, 0, 0))

    return pl.pallas_call(
        _s5_expand_kernel, grid=(depth, S_BLOCKS),
        in_specs=[blk(a) for a in (kt, msr, msi, mor, moi)],
        out_specs=[pl.BlockSpec((1, 1, qw, qw + 2 * S_SW), lambda d, j: (d, j, 0, 0)),
                   pl.BlockSpec((1, 1, 2 * S_SW, qw), lambda d, j: (d, j, 0, 0))],
        out_shape=[jax.ShapeDtypeStruct((depth, S_BLOCKS, qw, qw + 2 * S_SW), BF16),
                   jax.ShapeDtypeStruct((depth, S_BLOCKS, 2 * S_SW, qw), BF16)],
        compiler_params=_params("arbitrary", "arbitrary"), name="s5_expand",
    )(kt, msr, msi, mor, moi)


def _s5_tables(lam_re, lam_im, log_dt, b_re, b_im, c_re, c_im, nch):
    q = S_Q
    dep = lam_re.shape[0]
    lr = jnp.minimum(lam_re.astype(F32), -1e-4)
    li = lam_im.astype(F32)
    dt = jnp.exp(log_dt.astype(F32))[..., None]
    mag = jnp.exp(lr * dt)
    ab_re = mag * jnp.cos(li * dt)
    ab_im = mag * jnp.sin(li * dt)
    nr = ab_re - 1.0
    den = lr * lr + li * li
    coef_re = ((nr * lr + ab_im * li) / den)[:, :, None, :]
    coef_im = ((ab_im * lr - nr * li) / den)[:, :, None, :]
    brt = jnp.swapaxes(b_re.astype(F32), -1, -2)
    bit = jnp.swapaxes(b_im.astype(F32), -1, -2)
    bb_re = coef_re * brt - coef_im * bit
    bb_im = coef_re * bit + coef_im * brt

    def apow(nvals):
        e = jnp.asarray(nvals, F32)[:, None, None, None]
        m = jnp.exp(e * (lr * dt))
        ang = e * (li * dt)
        return m * jnp.cos(ang), m * jnp.sin(ang)

    def per_block(m, rows):
        m = m.reshape(q, dep, S_BLOCKS, S_GPB * rows, m.shape[-1])
        return m.transpose(1, 2, 0, 3, 4).astype(BF16)

    ar, ai = apow(list(range(q + 1)))
    cr, ci = c_re.astype(F32), c_im.astype(F32)
    ca_re = cr[None] * ar[:, :, :, None, :] - ci[None] * ai[:, :, :, None, :]
    ca_im = cr[None] * ai[:, :, :, None, :] + ci[None] * ar[:, :, :, None, :]
    kt = jnp.sum(ca_re[:q, :, :, None, :, :] * bb_re[None, :, :, :, None, :]
                 - ca_im[:q, :, :, None, :, :] * bb_im[None, :, :, :, None, :], axis=-1)
    pw_r = ar[:q][::-1][:, :, :, None, :]
    pw_i = ai[:q][::-1][:, :, :, None, :]
    kms, mo = _s5_expand(per_block(kt, S_GROUP),
                         per_block(pw_r * bb_re - pw_i * bb_im, S_GROUP),
                         per_block(pw_r * bb_im + pw_i * bb_re, S_GROUP),
                         per_block(jnp.swapaxes(ca_re[1:], -1, -2), S_STATE),
                         per_block(jnp.swapaxes(-ca_im[1:], -1, -2), S_STATE))
    def power_table(nvals):
        pr, pi = apow(nvals)
        pr = pr.reshape(len(nvals), dep, S_BLOCKS, S_SW).transpose(1, 2, 0, 3)
        pi = pi.reshape(len(nvals), dep, S_BLOCKS, S_SW).transpose(1, 2, 0, 3)
        return jnp.concatenate([pr, pi], axis=3)

    pin = power_table([q * (r + 1) for r in range(S_SUB)])
    pt = power_table([q * S_SUB * (k + 1) for k in range(nch // S_SUB)])
    return kms, mo, pin, pt


def _merge_kernel(ya_ref, yb_ref, yc_ref, cz_ref, g_ref, x_ref, wa_ref, wb_ref, wglu_ref, wc_ref,
                  wo_ref, nw_ref, *out_refs, final):
    yc0 = jnp.concatenate([yc_ref[j] for j in range(S_BLOCKS)], axis=1)
    glu = _dot(yc0.astype(BF16), wglu_ref[...])
    yc = yc0 * _sigmoid(glu) * cz_ref[...].astype(F32)
    merged = (g_ref[:, 0:D_MODEL].astype(F32) * _dot(ya_ref[...], wa_ref[...])
              + g_ref[:, D_MODEL:2 * D_MODEL].astype(F32) * _dot(yb_ref[...], wb_ref[...])
              + g_ref[:, 2 * D_MODEL:].astype(F32) * _dot(yc.astype(BF16), wc_ref[...]))
    out = x_ref[...] + _dot(merged.astype(BF16), wo_ref[...])
    ms = jnp.mean(out * out, axis=-1, keepdims=True)
    normed = out * lax.rsqrt(ms + EPS) * nw_ref[...]
    if final:
        out_refs[0][...] = normed
    else:
        out_refs[0][...] = out
        out_refs[1][...] = normed.astype(BF16)


def _merge(ya, yb, yc, cz, g, x2, l, wa, wb, wglu, wc, wo, nw, tm, final):
    n = x2.shape[0]
    x_out = jax.ShapeDtypeStruct((n, D_MODEL), F32)
    hn_out = jax.ShapeDtypeStruct((n, D_MODEL), BF16)
    return pl.pallas_call(
        functools.partial(_merge_kernel, final=final),
        grid=(n // tm,),
        in_specs=[_rows(tm, M_W), _rows(tm, G_VW),
                  pl.BlockSpec((S_BLOCKS, tm, LANES), lambda i: (0, i, 0)),
                  _rows(tm, S_W), _rows(tm, 3 * D_MODEL), _rows(tm, D_MODEL),
                  _of_layer(l, (M_W, D_MODEL)), _of_layer(l, (G_VW, D_MODEL)), _of_layer(l, (S_W, S_W)),
                  _of_layer(l, (S_W, D_MODEL)), _of_layer(l, (D_MODEL, D_MODEL)), _whole((1, D_MODEL))],
        out_specs=[_rows(tm, D_MODEL)] if final else [_rows(tm, D_MODEL), _rows(tm, D_MODEL)],
        out_shape=[x_out] if final else [x_out, hn_out],
        compiler_params=_params("arbitrary"),
        name="merge",
    )(ya, yb, yc, cz, g, x2, wa, wb, wglu, wc, wo, nw)


def _pad_heads(w, heads, d, dp):
    zeros = jnp.zeros(w.shape[:-1] + (dp - d,), w.dtype)
    parts = []
    for h in range(heads):
        parts += [w[..., h * d:(h + 1) * d], zeros]
    return jnp.concatenate(parts, axis=-1)


def _prepare(p, nch):
    depth = p["w_in"].shape[0]
    w_mlstm, w_plain, w_gate, w_small = _wprep(jnp.swapaxes(p["w_in"], 1, 2), LANES)
    gkp = functools.partial(_pad_heads, heads=G_HEADS, d=G_DK, dp=G_DKP)
    kms, mo, pin, pt = _s5_tables(p["s5_lam_re"], p["s5_lam_im"], p["s5_log_dt"], p["s5_B_re"], p["s5_B_im"],
                               p["s5_C_re"], p["s5_C_im"], nch)
    gate_b = p["mlstm_gate_b"]
    return dict(
        w_mlstm=w_mlstm, w_plain=w_plain, w_gate=w_gate, w_small=w_small,
        cq=p["mlstm_conv"][:, :, :M_W], ck=p["mlstm_conv"][:, :, M_W:],
        gb=jnp.concatenate([gate_b[:, 0], gate_b[:, 1],
                            jnp.zeros((depth, LANES - 2 * M_HEADS), F32)], axis=1)[:, None, :],
        m_nw=p["mlstm_norm"][:, None, :],
        wal=jnp.concatenate([jnp.zeros((depth, 2 * M_HEADS, G_KWP), F32), gkp(p["gla_w_alpha"]),
                             jnp.zeros((depth, LANES - 2 * M_HEADS - G_RANK, G_KWP), F32)],
                            axis=1).astype(BF16),
        bal=gkp(p["gla_b_alpha"])[:, None, :],
        g_nw=p["gla_norm"][:, None, :],
        kms=kms, mo=mo, pin=pin, pt=pt,
        dskip=p["s5_D"].reshape(depth, S_BLOCKS, 1, LANES),
        wa=p["w_branch_mlstm"].astype(BF16), wb=p["w_branch_gla"].astype(BF16),
        wglu=p["s5_w_glu"].astype(BF16), wc=p["w_branch_s5"].astype(BF16), wo=p["w_out"].astype(BF16),
    )


def _tiles(batch, seq):
    return dict(proj=min(1024, seq), m_chunk=min(256, seq), g_chunk=min(256, seq), g_sub=16,
                s5=min(4096, seq), merge=min(512, batch * seq))


def _layer(x2, hn, batch, seq, l, w, next_norm, final, tl):
    tm = tl["proj"]
    q, k, oz = _proj_mlstm(hn, l, w["w_mlstm"], w["cq"], w["ck"], seq, tm)
    mv, gv, gq, gk = _proj_plain(hn, l, w["w_plain"], tm)
    gz, czs, gs = _proj_gate(hn, l, w["w_gate"], tm)
    cu_s, gm, gmt, gcum = _proj_aux(hn, l, w["w_small"], w["gb"], w["wal"], w["bal"], tm,
                                    tl["m_chunk"], tl["g_chunk"])
    ya = _mlstm(q, k, mv, oz, gm, gmt, l, w["m_nw"], batch, seq, tl["m_chunk"])
    yb = _gla(gq, gk, gv, gz, gcum, l, w["g_nw"], batch, seq, tl["g_chunk"], tl["g_sub"])
    yc = _s5(cu_s, l, w["kms"], w["mo"], w["pin"], w["pt"], w["dskip"], batch, seq, tl["s5"])
    return _merge(ya, yb, yc, czs, gs, x2, l, w["wa"], w["wb"], w["wglu"], w["wc"], w["wo"],
                  next_norm.reshape(1, D_MODEL), tl["merge"], final)


def kernel(x, norm_w, w_in, mlstm_conv, mlstm_gate_b, mlstm_norm, gla_w_alpha, gla_b_alpha, gla_norm,
           s5_lam_re, s5_lam_im, s5_log_dt, s5_B_re, s5_B_im, s5_C_re, s5_C_im, s5_D, s5_w_glu,
           w_branch_mlstm, w_branch_gla, w_branch_s5, w_out, final_norm):
    batch, seq, _ = x.shape
    depth = norm_w.shape[0]
    tl = _tiles(batch, seq)
    prep = _prepare(dict(w_in=w_in, mlstm_conv=mlstm_conv, mlstm_gate_b=mlstm_gate_b, mlstm_norm=mlstm_norm,
                         gla_w_alpha=gla_w_alpha, gla_b_alpha=gla_b_alpha, gla_norm=gla_norm,
                         s5_lam_re=s5_lam_re, s5_lam_im=s5_lam_im, s5_log_dt=s5_log_dt, s5_B_re=s5_B_re,
                         s5_B_im=s5_B_im, s5_C_re=s5_C_re, s5_C_im=s5_C_im, s5_D=s5_D, s5_w_glu=s5_w_glu,
                         w_branch_mlstm=w_branch_mlstm, w_branch_gla=w_branch_gla,
                         w_branch_s5=w_branch_s5, w_out=w_out), tl["s5"] // S_Q)
    x2 = x.reshape(batch * seq, D_MODEL)
    hn = _rms(x2, norm_w[0].reshape(1, D_MODEL), min(1024, batch * seq))
    for l in range(depth):
        final = l == depth - 1
        res = _layer(x2, hn, batch, seq, l, prep, final_norm if final else norm_w[l + 1], final, tl)
        if final:
            x2 = res[0]
        else:
            x2, hn = res
    return x2.reshape(batch, seq, D_MODEL)
```

```python
import functools
import math

import jax
import jax.numpy as jnp
from jax import lax
from jax.experimental import pallas as pl
from jax.experimental.pallas import tpu as pltpu

F32 = jnp.float32
BF16 = jnp.bfloat16
HIGHEST = lax.Precision.HIGHEST

EPS = 1e-6
D_MODEL = 1024
LANES = 128
M_HEADS = 4
M_HD = 192
M_HDP = 256
M_W = M_HEADS * M_HD
M_WP = M_HEADS * M_HDP
M_CONV = 4
G_HEADS = 4
G_DK = 96
G_DKP = 128
G_DV = 192
G_DVP = 256
G_KW = G_HEADS * G_DK
G_KWP = G_HEADS * G_DKP
G_VW = G_HEADS * G_DV
G_VWP = G_HEADS * G_DVP
G_RANK = 16
G_TAU = 16.0
S_W = 512
S_GROUP = 16
S_GROUPS = 32
S_STATE = 64
S_BLOCKS = S_W // LANES
S_GPB = LANES // S_GROUP
S_SW = S_GPB * S_STATE
S_Q = 8
S_SUB = 8

IN_SIZES = (M_W, M_W, M_W, M_W, M_HEADS, M_HEADS, M_W,
            G_KW, G_KW, G_VW, G_RANK, G_VW,
            S_W, S_W, 3 * D_MODEL)
SMALL_W = S_W + LANES
CW = 512

NEG = -1e30
G_SAFE_LOG_DECAY = -60.0
VMEM_LIMIT = 56 * 1024 * 1024


def _sigmoid(x):
    return 0.5 * jnp.tanh(0.5 * x) + 0.5


def _silu(x):
    h = 0.5 * x
    return h * jnp.tanh(h) + h


def _log_sigmoid(x):
    return jnp.minimum(x, 0.0) - jnp.log1p(jnp.exp(-jnp.abs(x)))


def _dot(a, b):
    return jnp.dot(a, b, preferred_element_type=F32)


def _dot_nt(a, b):
    return lax.dot_general(a, b, (((1,), (1,)), ((), ())), preferred_element_type=F32)


def _dot_tn(a, b):
    return lax.dot_general(a, b, (((0,), (0,)), ((), ())), preferred_element_type=F32)


def _params(*sem):
    return pltpu.CompilerParams(dimension_semantics=sem, vmem_limit_bytes=VMEM_LIMIT)


def _rows(tm, width):
    return pl.BlockSpec((tm, width), lambda i: (i, 0))


def _whole(shape):
    return pl.BlockSpec(shape, lambda i: (0,) * len(shape))


def _of_layer(l, shape):
    return pl.BlockSpec((None,) + shape, lambda *_: (l,) + (0,) * len(shape))


def _rms_kernel(x_ref, nw_ref, o_ref):
    x = x_ref[...]
    ms = jnp.mean(x * x, axis=-1, keepdims=True)
    o_ref[...] = (x * lax.rsqrt(ms + EPS) * nw_ref[...]).astype(BF16)


def _rms(x2, nw, tm):
    n = x2.shape[0]
    return pl.pallas_call(
        _rms_kernel, grid=(n // tm,),
        in_specs=[_rows(tm, D_MODEL), _whole((1, D_MODEL))],
        out_specs=_rows(tm, D_MODEL),
        out_shape=jax.ShapeDtypeStruct((n, D_MODEL), BF16),
        compiler_params=_params("arbitrary"), name="rms",
    )(x2, nw)


W_OFF = {}
_o = 0
for _name, _size in zip(("aq", "ak", "av", "ao", "ai", "af", "az", "bq", "bk", "bv", "ba", "bz", "cu", "cz", "g"),
                        IN_SIZES):
    W_OFF[_name] = _o
    _o += _size
IN_WIDTH = _o
W_MLSTM, W_PLAIN, W_GATE = 4 * M_W, M_W + G_VW + 2 * G_KW, G_VW + S_W + 3 * D_MODEL


def _wprep_kernel(w_ref, wm_ref, wp_ref, wg_ref, ws_ref):
    kb = w_ref.shape[2]

    def put(dst, dst_off, parts, width):
        xs = [w_ref[0, off:off + rows, :] for off, rows in parts]
        used = sum(rows for _, rows in parts)
        if used < width:
            xs.append(jnp.zeros((width - used, kb), F32))
        x = xs[0] if len(xs) == 1 else jnp.concatenate(xs, axis=0)
        dst[0, :, dst_off:dst_off + width] = x.T.astype(BF16)

    def group(dst, names):
        off = 0
        for name in names:
            width = IN_SIZES[list(W_OFF).index(name)]
            for c in range(0, width, LANES):
                put(dst, off + c, [(W_OFF[name] + c, LANES)], LANES)
            off += width

    group(wm_ref, ("aq", "ak", "ao", "az"))
    group(wp_ref, ("av", "bv", "bq", "bk"))
    group(wg_ref, ("bz", "cz", "g"))
    group(ws_ref, ("cu",))
    put(ws_ref, S_W, [(W_OFF["ai"], 2 * M_HEADS), (W_OFF["ba"], G_RANK)], LANES)


def _wprep(w_t, kb):
    depth = w_t.shape[0]
    widths = (W_MLSTM, W_PLAIN, W_GATE, SMALL_W)
    return pl.pallas_call(
        _wprep_kernel, grid=(depth, D_MODEL // kb),
        in_specs=[pl.BlockSpec((1, IN_WIDTH, kb), lambda l, i: (l, 0, i))],
        out_specs=[pl.BlockSpec((1, kb, wd), lambda l, i: (l, i, 0)) for wd in widths],
        out_shape=[jax.ShapeDtypeStruct((depth, D_MODEL, wd), BF16) for wd in widths],
        compiler_params=_params("arbitrary", "arbitrary"), name="wprep",
    )(w_t)


def _scatter_heads(dst, y, heads, d, dp, fill):
    rows = y.shape[0]
    for h in range(heads):
        dst[:, h * dp:h * dp + d] = y[:, h * d:(h + 1) * d].astype(BF16)
        dst[:, h * dp + d:(h + 1) * dp] = jnp.full((rows, dp - d), fill, BF16)


def _proj_mlstm_kernel(hn_ref, w_ref, cq_ref, ck_ref, q_ref, k_ref, oz_ref, halo_x, halo_b, *, tm,
                       tiles_per_seq):
    @pl.when(pl.program_id(0) % tiles_per_seq == 0)
    def _():
        halo_x[...] = jnp.zeros_like(halo_x)
        halo_b[...] = jnp.zeros_like(halo_b)

    hn = hn_ref[...]
    row8 = lax.broadcasted_iota(jnp.int32, (8, M_W), 0)

    def shift(x, prev8, d):
        rolled = pltpu.roll(x, d, 0)
        head = jnp.where(row8 < d, pltpu.roll(prev8, d, 0), rolled[0:8])
        return jnp.concatenate([head, rolled[8:]], axis=0)

    for c, (cw_ref, dst, scale) in enumerate(((cq_ref, q_ref, 1.0), (ck_ref, k_ref, M_HD ** -0.5))):
        x = _dot(hn, w_ref[:, c * M_W:(c + 1) * M_W])
        x1 = shift(x, halo_x[c], 1)
        pair = cw_ref[1:2, :] * x + cw_ref[0:1, :] * x1
        y = _silu(cw_ref[3:4, :] * x + cw_ref[2:3, :] * x1 + shift(pair, halo_b[c], 2))
        halo_x[c] = x[tm - 8:tm]
        halo_b[c] = pair[tm - 8:tm]
        _scatter_heads(dst, y if scale == 1.0 else y * scale, M_HEADS, M_HD, M_HDP, 0.0)
    o = _dot(hn, w_ref[:, 2 * M_W:3 * M_W])
    z = _dot(hn, w_ref[:, 3 * M_W:])
    oz_ref[...] = (_sigmoid(o) * _silu(z)).astype(BF16)


def _proj_mlstm(hn, l, w, cq, ck, seq, tm):
    n = hn.shape[0]
    padded = jax.ShapeDtypeStruct((n, M_WP), BF16)
    return pl.pallas_call(
        functools.partial(_proj_mlstm_kernel, tm=tm, tiles_per_seq=seq // tm),
        grid=(n // tm,),
        in_specs=[_rows(tm, D_MODEL), _of_layer(l, (D_MODEL, W_MLSTM)), _of_layer(l, (M_CONV, M_W)),
                  _of_layer(l, (M_CONV, M_W))],
        out_specs=[_rows(tm, M_WP), _rows(tm, M_WP), _rows(tm, M_W)],
        out_shape=[padded, padded, jax.ShapeDtypeStruct((n, M_W), BF16)],
        scratch_shapes=[pltpu.VMEM((2, 8, M_W), F32)] * 2,
        compiler_params=_params("arbitrary"), name="proj_mlstm",
    )(hn, w, cq, ck)


def _proj_plain_kernel(hn_ref, w_ref, mv_ref, gv_ref, gq_ref, gk_ref):
    hn = hn_ref[...]
    _scatter_heads(mv_ref, _dot(hn, w_ref[:, 0:M_W]), M_HEADS, M_HD, M_HDP, 0.0)
    _scatter_heads(gv_ref, _dot(hn, w_ref[:, M_W:M_W + G_VW]), G_HEADS, G_DV, G_DVP, 0.0)
    off = M_W + G_VW
    _scatter_heads(gq_ref, _dot(hn, w_ref[:, off:off + G_KW]) * (G_DK ** -0.5), G_HEADS, G_DK, G_DKP, 0.0)
    _scatter_heads(gk_ref, _dot(hn, w_ref[:, off + G_KW:]), G_HEADS, G_DK, G_DKP, 0.0)


def _proj_plain(hn, l, w, tm):
    n = hn.shape[0]
    widths = (M_WP, G_VWP, G_KWP, G_KWP)
    return pl.pallas_call(
        _proj_plain_kernel, grid=(n // tm,),
        in_specs=[_rows(tm, D_MODEL), _of_layer(l, (D_MODEL, W_PLAIN))],
        out_specs=[_rows(tm, wd) for wd in widths],
        out_shape=[jax.ShapeDtypeStruct((n, wd), BF16) for wd in widths],
        compiler_params=_params("arbitrary"), name="proj_plain",
    )(hn, w)


def _proj_gate_kernel(hn_ref, w_ref, gz_ref, cz_ref, g_ref):
    hn = hn_ref[...]
    gz_ref[...] = _silu(_dot(hn, w_ref[:, 0:G_VW])).astype(BF16)
    cz_ref[...] = _silu(_dot(hn, w_ref[:, G_VW:G_VW + S_W])).astype(BF16)
    off = G_VW + S_W
    for c in range(3 * D_MODEL // CW):
        r = _dot(hn, w_ref[:, off + c * CW:off + (c + 1) * CW])
        g_ref[:, c * CW:(c + 1) * CW] = _sigmoid(r).astype(BF16)


def _proj_gate(hn, l, w, tm):
    n = hn.shape[0]
    widths = (G_VW, S_W, 3 * D_MODEL)
    return pl.pallas_call(
        _proj_gate_kernel, grid=(n // tm,),
        in_specs=[_rows(tm, D_MODEL), _of_layer(l, (D_MODEL, W_GATE))],
        out_specs=[_rows(tm, wd) for wd in widths],
        out_shape=[jax.ShapeDtypeStruct((n, wd), BF16) for wd in widths],
        compiler_params=_params("arbitrary"), name="proj_gate",
    )(hn, w)


def _split3(x):
    hi = x.astype(BF16)
    r1 = x - hi.astype(F32)
    mid = r1.astype(BF16)
    return hi, mid, (r1 - mid.astype(F32)).astype(BF16)


def _segment_cumsum(x, seg):
    rows, width = x.shape
    tril = (lax.broadcasted_iota(jnp.int32, (seg, seg), 1)
            <= lax.broadcasted_iota(jnp.int32, (seg, seg), 0)).astype(BF16)
    parts = jnp.concatenate(_split3(x), axis=1)
    out = []
    for s in range(rows // seg):
        acc = _dot(tril, parts[s * seg:(s + 1) * seg])
        out.append(acc[:, :width] + acc[:, width:2 * width] + acc[:, 2 * width:])
    return jnp.concatenate(out, axis=0)


def _proj_aux_kernel(hn_ref, w_ref, gb_ref, wal_ref, bal_ref, cu_ref, gmb_ref, gmt_ref, gcum_ref, *,
                     m_chunk, g_chunk):
    r = _dot(hn_ref[...], w_ref[...])
    for j in range(S_BLOCKS):
        cu_ref[j] = r[:, j * LANES:(j + 1) * LANES]
    slots = 2 * M_HEADS * LANES
    spread = (lax.broadcasted_iota(jnp.int32, (3 * LANES, slots), 0) % LANES
              == lax.broadcasted_iota(jnp.int32, (3 * LANES, slots), 1) // LANES).astype(BF16)
    raw = r[:, S_W:]
    gates = raw + gb_ref[...]
    lane = lax.broadcasted_iota(jnp.int32, gates.shape, 1)
    fgate = (lane >= M_HEADS) & (lane < 2 * M_HEADS)
    gm = jnp.where(fgate, _segment_cumsum(_log_sigmoid(gates), m_chunk), gates)
    gmt_ref[...] = gm.T
    gmb_ref[...] = _dot(jnp.concatenate(_split3(gm), axis=1), spread)
    za = _dot(raw.astype(BF16), wal_ref[...]) + bal_ref[...]
    gcum_ref[...] = _segment_cumsum(_log_sigmoid(za) * (1.0 / G_TAU), g_chunk)


def _proj_aux(hn, l, w, gb, wal, bal, tm, m_chunk, g_chunk):
    n = hn.shape[0]
    return pl.pallas_call(
        functools.partial(_proj_aux_kernel, m_chunk=m_chunk, g_chunk=g_chunk), grid=(n // tm,),
        in_specs=[_rows(tm, D_MODEL), _of_layer(l, (D_MODEL, SMALL_W)), _of_layer(l, (1, LANES)),
                  _of_layer(l, (LANES, G_KWP)), _of_layer(l, (1, G_KWP))],
        out_specs=[pl.BlockSpec((S_BLOCKS, tm, LANES), lambda i: (0, i, 0)),
                   _rows(tm, 2 * M_HEADS * LANES),
                   pl.BlockSpec((LANES, tm), lambda i: (0, i)), _rows(tm, G_KWP)],
        out_shape=[jax.ShapeDtypeStruct((S_BLOCKS, n, LANES), F32),
                   jax.ShapeDtypeStruct((n, 2 * M_HEADS * LANES), F32),
                   jax.ShapeDtypeStruct((LANES, n), F32),
                   jax.ShapeDtypeStruct((n, G_KWP), F32)],
        compiler_params=_params("arbitrary"), name="proj_aux",
    )(hn, w, gb, wal, bal)


def _mlstm_body(q_ref, k_ref, v_ref, oz_ref, gmb_ref, gmt_ref, nw_ref, out_ref, c_scr, m_scr, *, tc):
    row = lax.broadcasted_iota(jnp.int32, (tc, tc), 0)
    col = lax.broadcasted_iota(jnp.int32, (tc, tc), 1)
    causal = col <= row
    real = lax.broadcasted_iota(jnp.int32, (tc, M_HDP), 1) < M_HD
    ones_tile = jnp.ones((tc, LANES), BF16)

    def lanes(x, n):
        return jnp.concatenate([x] * n, axis=1)

    for h in range(M_HEADS):
        sl = slice(h * M_HDP, (h + 1) * M_HDP)
        qb = q_ref[:, sl]
        kb = k_ref[:, sl]
        vext = jnp.concatenate([v_ref[:, sl], ones_tile], axis=1)
        i_b = gmb_ref[:, h * LANES:(h + 1) * LANES]
        bcum = gmb_ref[:, (M_HEADS + h) * LANES:(M_HEADS + h + 1) * LANES]
        g_row = gmt_ref[h:h + 1, :] - gmt_ref[M_HEADS + h:M_HEADS + h + 1, :]
        m_prev = m_scr[h, 0:1, :]

        inter = bcum + m_prev
        m_t = jnp.maximum(inter, bcum + jnp.max(jnp.where(causal, g_row, NEG), axis=1, keepdims=True))
        w_intra = jnp.exp(jnp.where(causal, g_row + lanes(bcum - m_t, tc // LANES), NEG))
        w_inter = jnp.exp(inter - m_t)
        s = _dot_nt(qb, kb) * w_intra
        c_mat = c_scr[h]
        num = _dot(s.astype(BF16), vext) + lanes(w_inter, 3) * _dot(qb, c_mat.astype(BF16))
        den = num[:, M_HDP:]
        rden = 1.0 / jnp.maximum(jnp.abs(den), jnp.exp(-m_t))
        num = num[:, :M_HDP]

        b_end = bcum[tc - 1:tc, :]
        to_end = b_end - bcum + i_b
        m_new = jnp.maximum(b_end + m_prev, jnp.max(to_end, axis=0, keepdims=True))
        wk = jnp.exp(to_end - m_new)
        w_prev = jnp.exp(b_end + m_prev - m_new)
        c_scr[h] = lanes(w_prev, 3) * c_mat + _dot_tn((lanes(wk, 2) * kb.astype(F32)).astype(BF16), vext)
        m_scr[h] = jnp.broadcast_to(m_new, (8, LANES))

        sc = slice(h * M_HD, (h + 1) * M_HD)
        mean = jnp.sum(num, axis=1, keepdims=True) * (1.0 / M_HD)
        xc = jnp.where(real, num - mean, 0.0)
        var = jnp.sum(xc * xc, axis=1, keepdims=True) * (1.0 / M_HD)
        scale = rden * lax.rsqrt(rden * rden * var + EPS)
        ha = (xc * lanes(scale, 2))[:, :M_HD]
        out_ref[:, sc] = (ha * nw_ref[:, sc] * oz_ref[:, sc].astype(F32)).astype(BF16)


def _gla_body(q_ref, k_ref, v_ref, z_ref, b_ref, nw_ref,
              out_ref, kbuf, bbuf, attn_scr, st_scr, *, t, cs):
    row = lax.broadcasted_iota(jnp.int32, (t, t), 0)
    col = lax.broadcasted_iota(jnp.int32, (t, t), 1)
    causal = col <= row

    def operands(h):
        sl = slice(h * G_DKP, (h + 1) * G_DKP)
        return q_ref[:, sl].astype(F32), k_ref[:, sl].astype(F32), b_ref[:, sl]

    safe = jnp.min(b_ref[t - 1:t, :]) >= G_SAFE_LOG_DECAY

    @pl.when(safe)
    def _():
        for h in range(G_HEADS):
            qs, kk, b = operands(h)
            a = _dot_nt((qs * jnp.exp(b)).astype(BF16), (kk * jnp.exp(-b)).astype(BF16))
            attn_scr[h] = jnp.where(causal, a, 0.0)

    @pl.when(jnp.logical_not(safe))
    def _():
        rowv = lax.broadcasted_iota(jnp.int32, (t, G_DKP), 0)
        sub = rowv % cs
        lag = row - col
        kbuf[0:cs, :] = jnp.zeros((cs, G_DKP), F32)
        bbuf[0:cs, :] = jnp.zeros((cs, G_DKP), F32)
        for h in range(G_HEADS):
            qs, kk, b = operands(h)
            qparts, kparts = [], []
            for i in range(1, t // cs):
                r = b[i * cs - 1:i * cs, :]
                inblk = (rowv >= i * cs) & (rowv < (i + 1) * cs)
                qparts.append((qs * jnp.exp(jnp.where(inblk, b - r, NEG))).astype(BF16))
                kparts.append((kk * jnp.exp(jnp.where(rowv < i * cs, r - b, NEG))).astype(BF16))
            attn = _dot_nt(jnp.concatenate(qparts, axis=1), jnp.concatenate(kparts, axis=1))
            kbuf[cs:cs + t, :] = kk
            bbuf[cs:cs + t, :] = b
            for j in range(cs):
                ksh = kbuf[cs - j:cs - j + t, :]
                bsh = bbuf[cs - j:cs - j + t, :]
                e = jnp.exp(jnp.where(sub >= j, b - bsh, NEG))
                dj = jnp.sum(qs * ksh * e, axis=1, keepdims=True)
                attn = attn + jnp.where(lag == j, dj, 0.0)
            attn_scr[h] = attn

    for h in range(G_HEADS):
        slv = slice(h * G_DVP, (h + 1) * G_DVP)
        qs, kk, b = operands(h)
        v = v_ref[:, slv]
        blast = b[t - 1:t, :]
        st = st_scr[h]
        o = (_dot_nt((qs * jnp.exp(b)).astype(BF16), st.astype(BF16))
             + _dot(attn_scr[h].astype(BF16), v))
        kdec = (kk * jnp.exp(blast - b)).astype(BF16)
        st_scr[h] = jnp.exp(blast) * st + _dot_tn(v, kdec)

        sc = slice(h * G_DV, (h + 1) * G_DV)
        var = jnp.sum(o * o, axis=1, keepdims=True) * (1.0 / G_DV)
        hb = (o * lax.rsqrt(var + EPS))[:, :G_DV] * nw_ref[:, sc]
        out_ref[:, sc] = (hb * z_ref[:, sc].astype(F32)).astype(BF16)


def _mixers_kernel(mq_ref, mk_ref, mv_ref, moz_ref, gmb_ref, gmt_ref, mnw_ref,
                   gq_ref, gk_ref, gv_ref, gz_ref, gb_ref, gnw_ref, ya_ref, yb_ref,
                   c_scr, m_scr, kbuf, bbuf, attn_scr, st_scr, *, t, cs):
    @pl.when(pl.program_id(1) == 0)
    def _():
        c_scr[...] = jnp.zeros_like(c_scr)
        m_scr[...] = jnp.zeros_like(m_scr)
        st_scr[...] = jnp.zeros_like(st_scr)

    _gla_body(gq_ref, gk_ref, gv_ref, gz_ref, gb_ref, gnw_ref, yb_ref, kbuf, bbuf, attn_scr, st_scr,
              t=t, cs=cs)
    _mlstm_body(mq_ref, mk_ref, mv_ref, moz_ref, gmb_ref, gmt_ref, mnw_ref, ya_ref, c_scr, m_scr, tc=t)


def _mixers(mq, mk, mv, moz, gmb, gmt, gq, gk, gv, gz, gcum, l, m_nw, g_nw, batch, seq, t, cs):
    n = batch * seq
    nt = seq // t

    def blk(width):
        return pl.BlockSpec((t, width), lambda b, c: (b * nt + c, 0))

    return pl.pallas_call(
        functools.partial(_mixers_kernel, t=t, cs=cs),
        grid=(batch, nt),
        in_specs=[blk(M_WP), blk(M_WP), blk(M_WP), blk(M_W), blk(2 * M_HEADS * LANES),
                  pl.BlockSpec((LANES, t), lambda b, c: (0, b * nt + c)), _of_layer(l, (1, M_W)),
                  blk(G_KWP), blk(G_KWP), blk(G_VWP), blk(G_VW), blk(G_KWP), _of_layer(l, (1, G_VW))],
        out_specs=[blk(M_W), blk(G_VW)],
        out_shape=[jax.ShapeDtypeStruct((n, M_W), BF16), jax.ShapeDtypeStruct((n, G_VW), BF16)],
        scratch_shapes=[pltpu.VMEM((M_HEADS, M_HDP, M_HDP + LANES), F32),
                        pltpu.VMEM((M_HEADS, 8, LANES), F32),
                        pltpu.VMEM((t + cs, G_DKP), F32), pltpu.VMEM((t + cs, G_DKP), F32),
                        pltpu.VMEM((G_HEADS, t, t), F32),
                        pltpu.VMEM((G_HEADS, G_DVP, G_DKP), F32)],
        compiler_params=_params("arbitrary", "arbitrary"),
        name="mixers",
    )(mq, mk, mv, moz, gmb, gmt, m_nw, gq, gk, gv, gz, gcum, g_nw)


def _gelu_tanh(x):
    return 0.5 * x * (1.0 + jnp.tanh(math.sqrt(2.0 / math.pi) * (x + 0.044715 * (x * x * x))))


def _s5_kernel(u_ref, kms_ref, mo_ref, pin_ref, pt_ref, d_ref, y_ref, carry, xs, *, nch):
    @pl.when(pl.program_id(2) == 0)
    def _():
        carry[...] = jnp.zeros_like(carry)

    qw = S_Q * LANES
    ngrp = nch // S_SUB
    ltiles = S_SW // LANES
    ucat = jnp.concatenate([u_ref[0, pl.ds(s, nch, stride=S_Q), :] for s in range(S_Q)],
                           axis=1).astype(BF16)
    r = _dot(ucat, kms_ref[0])
    y = r[:, :qw]
    xr = r[:, qw:qw + S_SW]
    xi = r[:, qw + S_SW:]

    def axpy(xr, xi, pr, pi, sr, si):
        return xr + (pr * sr - pi * si), xi + (pr * si + pi * sr)

    def scan(xr, xi, p_ref, pos, length):
        d = 1
        while d < length:
            keep = pos >= d
            sr = jnp.where(keep, pltpu.roll(xr, d, 0), 0.0)
            si = jnp.where(keep, pltpu.roll(xi, d, 0), 0.0)
            xr, xi = axpy(xr, xi, p_ref[0, d - 1:d, :S_SW], p_ref[0, d - 1:d, S_SW:], sr, si)
            d *= 2
        return xr, xi

    def slabs(part, arr):
        return [(part * ltiles + j, arr[:, j * LANES:(j + 1) * LANES]) for j in range(ltiles)]

    rowi = lax.broadcasted_iota(jnp.int32, (nch, S_SW), 0)
    xr, xi = scan(xr, xi, pin_ref, rowi % S_SUB, S_SUB)
    for part, arr in ((0, xr), (1, xi)):
        for j, piece in slabs(part, arr):
            xs[j] = piece

    def group_ends(part):
        return jnp.concatenate([xs[part * ltiles + j, pl.ds(S_SUB - 1, ngrp, stride=S_SUB), :]
                                for j in range(ltiles)], axis=1)

    rowg = lax.broadcasted_iota(jnp.int32, (ngrp, S_SW), 0)
    er, ei = scan(group_ends(0), group_ends(1), pt_ref, rowg, ngrp)
    cr = carry[0:1, :S_SW]
    ci = carry[0:1, S_SW:]
    er, ei = axpy(er, ei, pt_ref[0, :, :S_SW], pt_ref[0, :, S_SW:], cr, ci)
    first = rowg == 0
    gr = jnp.where(first, cr, pltpu.roll(er, 1, 0))
    gi = jnp.where(first, ci, pltpu.roll(ei, 1, 0))
    carry[0:1, :S_SW] = er[ngrp - 1:ngrp, :]
    carry[0:1, S_SW:] = ei[ngrp - 1:ngrp, :]
    for part, arr in ((0, gr), (1, gi)):
        for j, piece in slabs(part, arr):
            for k in range(S_SUB):
                xs[j, pl.ds(k, ngrp, stride=S_SUB), :] = piece
    sr = jnp.concatenate([xs[j] for j in range(ltiles)], axis=1)
    si = jnp.concatenate([xs[ltiles + j] for j in range(ltiles)], axis=1)
    pwr = jnp.concatenate([pin_ref[0, :, :S_SW]] * ngrp, axis=0)
    pwi = jnp.concatenate([pin_ref[0, :, S_SW:]] * ngrp, axis=0)
    xr, xi = axpy(xr, xi, pwr, pwi, sr, si)

    firstc = rowi == 0
    xpr = jnp.where(firstc, cr, pltpu.roll(xr, 1, 0))
    xpi = jnp.where(firstc, ci, pltpu.roll(xi, 1, 0))
    y = y + _dot(jnp.concatenate([xpr, xpi], axis=1).astype(BF16), mo_ref[0])
    for t in range(S_Q):
        y_ref[0, pl.ds(t, nch, stride=S_Q), :] = y[:, t * LANES:(t + 1) * LANES]
    y_ref[0] = _gelu_tanh(y_ref[0] + d_ref[0] * u_ref[0])


def _s5(cu, l, kms, mo, pin, pt, dskip, batch, seq, tm):
    n = batch * seq
    nt = seq // tm
    nch = tm // S_Q
    qw = S_Q * LANES

    def table(rows, cols):
        return pl.BlockSpec((None, 1, rows, cols), lambda j, b, t: (l, j, 0, 0))

    return pl.pallas_call(
        functools.partial(_s5_kernel, nch=nch),
        grid=(S_BLOCKS, batch, nt),
        in_specs=[pl.BlockSpec((1, tm, LANES), lambda j, b, t: (j, b * nt + t, 0)),
                  table(qw, qw + 2 * S_SW), table(2 * S_SW, qw), table(S_SUB, 2 * S_SW),
                  table(nch // S_SUB, 2 * S_SW), table(1, LANES)],
        out_specs=pl.BlockSpec((1, tm, LANES), lambda j, b, t: (j, b * nt + t, 0)),
        out_shape=jax.ShapeDtypeStruct((S_BLOCKS, n, LANES), F32),
        scratch_shapes=[pltpu.VMEM((8, 2 * S_SW), F32),
                        pltpu.VMEM((2 * S_SW // LANES, nch, LANES), F32)],
        compiler_params=_params("arbitrary", "arbitrary", "arbitrary"),
        name="s5",
    )(cu, kms, mo, pin, pt, dskip)


def _s5_expand_kernel(kt_ref, msr_ref, msi_ref, mor_ref, moi_ref, kms_ref, mo_ref):
    q = S_Q
    qw = q * LANES

    def embed(m, rows, width):
        shape = (width, S_GPB * width)
        tile = (lax.broadcasted_iota(jnp.int32, shape, 1) % width
                == lax.broadcasted_iota(jnp.int32, shape, 0)).astype(BF16)
        oshape = (S_GPB * rows, S_GPB * width)
        diag = (lax.broadcasted_iota(jnp.int32, oshape, 0) // rows
                == lax.broadcasted_iota(jnp.int32, oshape, 1) // width)
        return jnp.where(diag, _dot(m, tile), 0.0).astype(BF16)

    kd = [embed(kt_ref[0, 0, tau], S_GROUP, S_GROUP) for tau in range(q)]
    zero = jnp.zeros((LANES, LANES), BF16)
    for s in range(q):
        rows = slice(s * LANES, (s + 1) * LANES)
        for t in range(q):
            kms_ref[0, 0, rows, t * LANES:(t + 1) * LANES] = kd[t - s] if t >= s else zero
        kms_ref[0, 0, rows, qw:qw + S_SW] = embed(msr_ref[0, 0, s], S_GROUP, S_STATE)
        kms_ref[0, 0, rows, qw + S_SW:] = embed(msi_ref[0, 0, s], S_GROUP, S_STATE)
    for t in range(q):
        cols = slice(t * LANES, (t + 1) * LANES)
        mo_ref[0, 0, 0:S_SW, cols] = embed(mor_ref[0, 0, t], S_STATE, S_GROUP)
        mo_ref[0, 0, S_SW:, cols] = embed(moi_ref[0, 0, t], S_STATE, S_GROUP)


def _s5_expand(kt, msr, msi, mor, moi):
    depth = kt.shape[0]
    qw = S_Q * LANES

    def blk(a):
        return pl.BlockSpec((1, 1) + a.shape[2:], lambda d, j: (d, j, 0, 0, 0))

    return pl.pallas_call(
        _s5_expand_kernel, grid=(depth, S_BLOCKS),
        in_specs=[blk(a) for a in (kt, msr, msi, mor, moi)],
        out_specs=[pl.BlockSpec((1, 1, qw, qw + 2 * S_SW), lambda d, j: (d, j, 0, 0)),
                   pl.BlockSpec((1, 1, 2 * S_SW, qw), lambda d, j: (d, j, 0, 0))],
        out_shape=[jax.ShapeDtypeStruct((depth, S_BLOCKS, qw, qw + 2 * S_SW), BF16),
                   jax.ShapeDtypeStruct((depth, S_BLOCKS, 2 * S_SW, qw), BF16)],
        compiler_params=_params("arbitrary", "arbitrary"), name="s5_expand",
    )(kt, msr, msi, mor, moi)


def _s5_tables(lam_re, lam_im, log_dt, b_re, b_im, c_re, c_im, nch):
    q = S_Q
    dep = lam_re.shape[0]
    lr = jnp.minimum(lam_re.astype(F32), -1e-4)
    li = lam_im.astype(F32)
    dt = jnp.exp(log_dt.astype(F32))[..., None]
    mag = jnp.exp(lr * dt)
    ab_re = mag * jnp.cos(li * dt)
    ab_im = mag * jnp.sin(li * dt)
    nr = ab_re - 1.0
    den = lr * lr + li * li
    coef_re = ((nr * lr + ab_im * li) / den)[:, :, None, :]
    coef_im = ((ab_im * lr - nr * li) / den)[:, :, None, :]
    brt = jnp.swapaxes(b_re.astype(F32), -1, -2)
    bit = jnp.swapaxes(b_im.astype(F32), -1, -2)
    bb_re = coef_re * brt - coef_im * bit
    bb_im = coef_re * bit + coef_im * brt

    def apow(nvals):
        e = jnp.asarray(nvals, F32)[:, None, None, None]
        m = jnp.exp(e * (lr * dt))
        ang = e * (li * dt)
        return m * jnp.cos(ang), m * jnp.sin(ang)

    def per_block(m, rows):
        m = m.reshape(q, dep, S_BLOCKS, S_GPB * rows, m.shape[-1])
        return m.transpose(1, 2, 0, 3, 4).astype(BF16)

    ar, ai = apow(list(range(q + 1)))
    cr, ci = c_re.astype(F32), c_im.astype(F32)
    ca_re = cr[None] * ar[:, :, :, None, :] - ci[None] * ai[:, :, :, None, :]
    ca_im = cr[None] * ai[:, :, :, None, :] + ci[None] * ar[:, :, :, None, :]
    kt = jnp.sum(ca_re[:q, :, :, None, :, :] * bb_re[None, :, :, :, None, :]
                 - ca_im[:q, :, :, None, :, :] * bb_im[None, :, :, :, None, :], axis=-1)
    pw_r = ar[:q][::-1][:, :, :, None, :]
    pw_i = ai[:q][::-1][:, :, :, None, :]
    kms, mo = _s5_expand(per_block(kt, S_GROUP),
                         per_block(pw_r * bb_re - pw_i * bb_im, S_GROUP),
                         per_block(pw_r * bb_im + pw_i * bb_re, S_GROUP),
                         per_block(jnp.swapaxes(ca_re[1:], -1, -2), S_STATE),
                         per_block(jnp.swapaxes(-ca_im[1:], -1, -2), S_STATE))
    def power_table(nvals):
        pr, pi = apow(nvals)
        pr = pr.reshape(len(nvals), dep, S_BLOCKS, S_SW).transpose(1, 2, 0, 3)
        pi = pi.reshape(len(nvals), dep, S_BLOCKS, S_SW).transpose(1, 2, 0, 3)
        return jnp.concatenate([pr, pi], axis=3)

    pin = power_table([q * (r + 1) for r in range(S_SUB)])
    pt = power_table([q * S_SUB * (k + 1) for k in range(nch // S_SUB)])
    return kms, mo, pin, pt


def _merge_kernel(ya_ref, yb_ref, yc_ref, cz_ref, g_ref, x_ref, wa_ref, wb_ref, wglu_ref, wc_ref,
                  wo_ref, nw_ref, *out_refs, final):
    yc0 = jnp.concatenate([yc_ref[j] for j in range(S_BLOCKS)], axis=1)
    glu = _dot(yc0.astype(BF16), wglu_ref[...])
    yc = yc0 * _sigmoid(glu) * cz_ref[...].astype(F32)
    merged = (g_ref[:, 0:D_MODEL].astype(F32) * _dot(ya_ref[...], wa_ref[...])
              + g_ref[:, D_MODEL:2 * D_MODEL].astype(F32) * _dot(yb_ref[...], wb_ref[...])
              + g_ref[:, 2 * D_MODEL:].astype(F32) * _dot(yc.astype(BF16), wc_ref[...]))
    out = x_ref[...] + _dot(merged.astype(BF16), wo_ref[...])
    ms = jnp.mean(out * out, axis=-1, keepdims=True)
    normed = out * lax.rsqrt(ms + EPS) * nw_ref[...]
    if final:
        out_refs[0][...] = normed
    else:
        out_refs[0][...] = out
        out_refs[1][...] = normed.astype(BF16)


def _merge(ya, yb, yc, cz, g, x2, l, wa, wb, wglu, wc, wo, nw, tm, final):
    n = x2.shape[0]
    x_out = jax.ShapeDtypeStruct((n, D_MODEL), F32)
    hn_out = jax.ShapeDtypeStruct((n, D_MODEL), BF16)
    return pl.pallas_call(
        functools.partial(_merge_kernel, final=final),
        grid=(n // tm,),
        in_specs=[_rows(tm, M_W), _rows(tm, G_VW),
                  pl.BlockSpec((S_BLOCKS, tm, LANES), lambda i: (0, i, 0)),
                  _rows(tm, S_W), _rows(tm, 3 * D_MODEL), _rows(tm, D_MODEL),
                  _of_layer(l, (M_W, D_MODEL)), _of_layer(l, (G_VW, D_MODEL)), _of_layer(l, (S_W, S_W)),
                  _of_layer(l, (S_W, D_MODEL)), _of_layer(l, (D_MODEL, D_MODEL)), _whole((1, D_MODEL))],
        out_specs=[_rows(tm, D_MODEL)] if final else [_rows(tm, D_MODEL), _rows(tm, D_MODEL)],
        out_shape=[x_out] if final else [x_out, hn_out],
        compiler_params=_params("arbitrary"),
        name="merge",
    )(ya, yb, yc, cz, g, x2, wa, wb, wglu, wc, wo, nw)


def _pad_heads(w, heads, d, dp):
    zeros = jnp.zeros(w.shape[:-1] + (dp - d,), w.dtype)
    parts = []
    for h in range(heads):
        parts += [w[..., h * d:(h + 1) * d], zeros]
    return jnp.concatenate(parts, axis=-1)


def _prepare(p, nch):
    depth = p["w_in"].shape[0]
    w_mlstm, w_plain, w_gate, w_small = _wprep(jnp.swapaxes(p["w_in"], 1, 2), LANES)
    gkp = functools.partial(_pad_heads, heads=G_HEADS, d=G_DK, dp=G_DKP)
    kms, mo, pin, pt = _s5_tables(p["s5_lam_re"], p["s5_lam_im"], p["s5_log_dt"], p["s5_B_re"], p["s5_B_im"],
                               p["s5_C_re"], p["s5_C_im"], nch)
    gate_b = p["mlstm_gate_b"]
    return dict(
        w_mlstm=w_mlstm, w_plain=w_plain, w_gate=w_gate, w_small=w_small,
        cq=p["mlstm_conv"][:, :, :M_W], ck=p["mlstm_conv"][:, :, M_W:],
        gb=jnp.concatenate([gate_b[:, 0], gate_b[:, 1],
                            jnp.zeros((depth, LANES - 2 * M_HEADS), F32)], axis=1)[:, None, :],
        m_nw=p["mlstm_norm"][:, None, :],
        wal=jnp.concatenate([jnp.zeros((depth, 2 * M_HEADS, G_KWP), F32), gkp(p["gla_w_alpha"]),
                             jnp.zeros((depth, LANES - 2 * M_HEADS - G_RANK, G_KWP), F32)],
                            axis=1).astype(BF16),
        bal=gkp(p["gla_b_alpha"])[:, None, :],
        g_nw=p["gla_norm"][:, None, :],
        kms=kms, mo=mo, pin=pin, pt=pt,
        dskip=p["s5_D"].reshape(depth, S_BLOCKS, 1, LANES),
        wa=p["w_branch_mlstm"].astype(BF16), wb=p["w_branch_gla"].astype(BF16),
        wglu=p["s5_w_glu"].astype(BF16), wc=p["w_branch_s5"].astype(BF16), wo=p["w_out"].astype(BF16),
    )


def _tiles(batch, seq):
    return dict(proj=min(1024, seq), chunk=min(256, seq), g_sub=16,
                s5=min(4096, seq), merge=min(512, batch * seq))


def _layer(x2, hn, batch, seq, l, w, next_norm, final, tl):
    tm = tl["proj"]
    q, k, oz = _proj_mlstm(hn, l, w["w_mlstm"], w["cq"], w["ck"], seq, tm)
    mv, gv, gq, gk = _proj_plain(hn, l, w["w_plain"], tm)
    gz, czs, gs = _proj_gate(hn, l, w["w_gate"], tm)
    cu_s, gm, gmt, gcum = _proj_aux(hn, l, w["w_small"], w["gb"], w["wal"], w["bal"], tm,
                                    tl["chunk"], tl["chunk"])
    ya, yb = _mixers(q, k, mv, oz, gm, gmt, gq, gk, gv, gz, gcum, l, w["m_nw"], w["g_nw"], batch, seq,
                     tl["chunk"], tl["g_sub"])
    yc = _s5(cu_s, l, w["kms"], w["mo"], w["pin"], w["pt"], w["dskip"], batch, seq, tl["s5"])
    return _merge(ya, yb, yc, czs, gs, x2, l, w["wa"], w["wb"], w["wglu"], w["wc"], w["wo"],
                  next_norm.reshape(1, D_MODEL), tl["merge"], final)


def kernel(x, norm_w, w_in, mlstm_conv, mlstm_gate_b, mlstm_norm, gla_w_alpha, gla_b_alpha, gla_norm,
           s5_lam_re, s5_lam_im, s5_log_dt, s5_B_re, s5_B_im, s5_C_re, s5_C_im, s5_D, s5_w_glu,
           w_branch_mlstm, w_branch_gla, w_branch_s5, w_out, final_norm):
    batch, seq, _ = x.shape
    depth = norm_w.shape[0]
    tl = _tiles(batch, seq)
    prep = _prepare(dict(w_in=w_in, mlstm_conv=mlstm_conv, mlstm_gate_b=mlstm_gate_b, mlstm_norm=mlstm_norm,
                         gla_w_alpha=gla_w_alpha, gla_b_alpha=gla_b_alpha, gla_norm=gla_norm,
                         s5_lam_re=s5_lam_re, s5_lam_im=s5_lam_im, s5_log_dt=s5_log_dt, s5_B_re=s5_B_re,
                         s5_B_im=s5_B_im, s5_C_re=s5_C_re, s5_C_im=s5_C_im, s5_D=s5_D, s5_w_glu=s5_w_glu,
                         w_branch_mlstm=w_branch_mlstm, w_branch_gla=w_branch_gla,
                         w_branch_s5=w_branch_s5, w_out=w_out), tl["s5"] // S_Q)
    x2 = x.reshape(batch * seq, D_MODEL)
    hn = _rms(x2, norm_w[0].reshape(1, D_MODEL), min(1024, batch * seq))
    for l in range(depth):
        final = l == depth - 1
        res = _layer(x2, hn, batch, seq, l, prep, final_norm if final else norm_w[l + 1], final, tl)
        if final:
            x2 = res[0]
        else:
            x2, hn = res
    return x2.reshape(batch, seq, D_MODEL)
```

```python
import functools
import math

import jax
import jax.numpy as jnp
from jax import lax
from jax.experimental import pallas as pl
from jax.experimental.pallas import tpu as pltpu

F32 = jnp.float32
BF16 = jnp.bfloat16
HIGHEST = lax.Precision.HIGHEST

EPS = 1e-6
D_MODEL = 1024
LANES = 128
M_HEADS = 4
M_HD = 192
M_HDP = 256
M_W = M_HEADS * M_HD
M_WP = M_HEADS * M_HDP
M_CONV = 4
G_HEADS = 4
G_DK = 96
G_DKP = 128
G_DV = 192
G_DVP = 256
G_KW = G_HEADS * G_DK
G_KWP = G_HEADS * G_DKP
G_VW = G_HEADS * G_DV
G_VWP = G_HEADS * G_DVP
G_RANK = 16
G_TAU = 16.0
S_W = 512
S_GROUP = 16
S_GROUPS = 32
S_STATE = 64
S_BLOCKS = S_W // LANES
S_GPB = LANES // S_GROUP
S_SW = S_GPB * S_STATE
S_Q = 8
S_SUB = 8

IN_SIZES = (M_W, M_W, M_W, M_W, M_HEADS, M_HEADS, M_W,
            G_KW, G_KW, G_VW, G_RANK, G_VW,
            S_W, S_W, 3 * D_MODEL)
SMALL_W = S_W + LANES
CW = 512

NEG = -1e30
G_SAFE_LOG_DECAY = -60.0
VMEM_LIMIT = 56 * 1024 * 1024


def _sigmoid(x):
    return 0.5 * jnp.tanh(0.5 * x) + 0.5


def _silu(x):
    h = 0.5 * x
    return h * jnp.tanh(h) + h


def _log_sigmoid(x):
    return jnp.minimum(x, 0.0) - jnp.log1p(jnp.exp(-jnp.abs(x)))


def _dot(a, b):
    return jnp.dot(a, b, preferred_element_type=F32)


def _dot_nt(a, b):
    return lax.dot_general(a, b, (((1,), (1,)), ((), ())), preferred_element_type=F32)


def _dot_tn(a, b):
    return lax.dot_general(a, b, (((0,), (0,)), ((), ())), preferred_element_type=F32)


def _params(*sem):
    return pltpu.CompilerParams(dimension_semantics=sem, vmem_limit_bytes=VMEM_LIMIT)


def _rows(tm, width):
    return pl.BlockSpec((tm, width), lambda i: (i, 0))


def _whole(shape):
    return pl.BlockSpec(shape, lambda i: (0,) * len(shape))


def _of_layer(l, shape):
    return pl.BlockSpec((None,) + shape, lambda *_: (l,) + (0,) * len(shape))


def _rms_kernel(x_ref, nw_ref, o_ref):
    x = x_ref[...]
    ms = jnp.mean(x * x, axis=-1, keepdims=True)
    o_ref[...] = (x * lax.rsqrt(ms + EPS) * nw_ref[...]).astype(BF16)


def _rms(x2, nw, tm):
    n = x2.shape[0]
    return pl.pallas_call(
        _rms_kernel, grid=(n // tm,),
        in_specs=[_rows(tm, D_MODEL), _whole((1, D_MODEL))],
        out_specs=_rows(tm, D_MODEL),
        out_shape=jax.ShapeDtypeStruct((n, D_MODEL), BF16),
        compiler_params=_params("arbitrary"), name="rms",
    )(x2, nw)


W_OFF = {}
_o = 0
for _name, _size in zip(("aq", "ak", "av", "ao", "ai", "af", "az", "bq", "bk", "bv", "ba", "bz", "cu", "cz", "g"),
                        IN_SIZES):
    W_OFF[_name] = _o
    _o += _size
IN_WIDTH = _o
W_MLSTM, W_PLAIN, W_GATE = 4 * M_W, M_W + G_VW + 2 * G_KW, G_VW + S_W + 3 * D_MODEL


def _wprep_kernel(w_ref, wm_ref, wp_ref, wg_ref, ws_ref):
    kb = w_ref.shape[2]

    def put(dst, dst_off, parts, width):
        xs = [w_ref[0, off:off + rows, :] for off, rows in parts]
        used = sum(rows for _, rows in parts)
        if used < width:
            xs.append(jnp.zeros((width - used, kb), F32))
        x = xs[0] if len(xs) == 1 else jnp.concatenate(xs, axis=0)
        dst[0, :, dst_off:dst_off + width] = x.T.astype(BF16)

    def group(dst, names):
        off = 0
        for name in names:
            width = IN_SIZES[list(W_OFF).index(name)]
            for c in range(0, width, LANES):
                put(dst, off + c, [(W_OFF[name] + c, LANES)], LANES)
            off += width

    group(wm_ref, ("aq", "ak", "ao", "az"))
    group(wp_ref, ("av", "bv", "bq", "bk"))
    group(wg_ref, ("bz", "cz", "g"))
    group(ws_ref, ("cu",))
    put(ws_ref, S_W, [(W_OFF["ai"], 2 * M_HEADS), (W_OFF["ba"], G_RANK)], LANES)


def _wprep(w_t, kb):
    depth = w_t.shape[0]
    widths = (W_MLSTM, W_PLAIN, W_GATE, SMALL_W)
    return pl.pallas_call(
        _wprep_kernel, grid=(depth, D_MODEL // kb),
        in_specs=[pl.BlockSpec((1, IN_WIDTH, kb), lambda l, i: (l, 0, i))],
        out_specs=[pl.BlockSpec((1, kb, wd), lambda l, i: (l, i, 0)) for wd in widths],
        out_shape=[jax.ShapeDtypeStruct((depth, D_MODEL, wd), BF16) for wd in widths],
        compiler_params=_params("arbitrary", "arbitrary"), name="wprep",
    )(w_t)


def _scatter_heads(dst, y, heads, d, dp, fill):
    rows = y.shape[0]
    for h in range(heads):
        dst[:, h * dp:h * dp + d] = y[:, h * d:(h + 1) * d].astype(BF16)
        dst[:, h * dp + d:(h + 1) * dp] = jnp.full((rows, dp - d), fill, BF16)


def _proj_mlstm_kernel(hn_ref, w_ref, cq_ref, ck_ref, q_ref, k_ref, oz_ref, halo_x, halo_b, *, tm,
                       tiles_per_seq):
    @pl.when(pl.program_id(0) % tiles_per_seq == 0)
    def _():
        halo_x[...] = jnp.zeros_like(halo_x)
        halo_b[...] = jnp.zeros_like(halo_b)

    hn = hn_ref[...]
    row8 = lax.broadcasted_iota(jnp.int32, (8, M_W), 0)

    def shift(x, prev8, d):
        rolled = pltpu.roll(x, d, 0)
        head = jnp.where(row8 < d, pltpu.roll(prev8, d, 0), rolled[0:8])
        return jnp.concatenate([head, rolled[8:]], axis=0)

    def conv_act(c, x, cw_ref, dst, scale):
        x1 = shift(x, halo_x[c], 1)
        pair = cw_ref[1:2, :] * x + cw_ref[0:1, :] * x1
        y = _silu(cw_ref[3:4, :] * x + cw_ref[2:3, :] * x1 + shift(pair, halo_b[c], 2))
        halo_x[c] = x[tm - 8:tm]
        halo_b[c] = pair[tm - 8:tm]
        _scatter_heads(dst, y if scale == 1.0 else y * scale, M_HEADS, M_HD, M_HDP, 0.0)

    xq = _dot(hn, w_ref[:, 0:M_W])
    xk = _dot(hn, w_ref[:, M_W:2 * M_W])
    conv_act(0, xq, cq_ref, q_ref, 1.0)
    o = _dot(hn, w_ref[:, 2 * M_W:3 * M_W])
    conv_act(1, xk, ck_ref, k_ref, M_HD ** -0.5)
    z = _dot(hn, w_ref[:, 3 * M_W:])
    oz_ref[...] = (_sigmoid(o) * _silu(z)).astype(BF16)


def _proj_mlstm(hn, l, w, cq, ck, seq, tm):
    n = hn.shape[0]
    padded = jax.ShapeDtypeStruct((n, M_WP), BF16)
    return pl.pallas_call(
        functools.partial(_proj_mlstm_kernel, tm=tm, tiles_per_seq=seq // tm),
        grid=(n // tm,),
        in_specs=[_rows(tm, D_MODEL), _of_layer(l, (D_MODEL, W_MLSTM)), _of_layer(l, (M_CONV, M_W)),
                  _of_layer(l, (M_CONV, M_W))],
        out_specs=[_rows(tm, M_WP), _rows(tm, M_WP), _rows(tm, M_W)],
        out_shape=[padded, padded, jax.ShapeDtypeStruct((n, M_W), BF16)],
        scratch_shapes=[pltpu.VMEM((2, 8, M_W), F32)] * 2,
        compiler_params=_params("arbitrary"), name="proj_mlstm",
    )(hn, w, cq, ck)


def _proj_gate_kernel(hn_ref, w_ref, gz_ref, cz_ref, g_ref):
    hn = hn_ref[...]
    gz_ref[...] = _silu(_dot(hn, w_ref[:, 0:G_VW])).astype(BF16)
    cz_ref[...] = _silu(_dot(hn, w_ref[:, G_VW:G_VW + S_W])).astype(BF16)
    off = G_VW + S_W
    for c in range(3 * D_MODEL // CW):
        r = _dot(hn, w_ref[:, off + c * CW:off + (c + 1) * CW])
        g_ref[:, c * CW:(c + 1) * CW] = _sigmoid(r).astype(BF16)


def _proj_gate(hn, l, w, tm):
    n = hn.shape[0]
    widths = (G_VW, S_W, 3 * D_MODEL)
    return pl.pallas_call(
        _proj_gate_kernel, grid=(n // tm,),
        in_specs=[_rows(tm, D_MODEL), _of_layer(l, (D_MODEL, W_GATE))],
        out_specs=[_rows(tm, wd) for wd in widths],
        out_shape=[jax.ShapeDtypeStruct((n, wd), BF16) for wd in widths],
        compiler_params=_params("arbitrary"), name="proj_gate",
    )(hn, w)


def _split3(x):
    hi = x.astype(BF16)
    r1 = x - hi.astype(F32)
    mid = r1.astype(BF16)
    return hi, mid, (r1 - mid.astype(F32)).astype(BF16)


def _segment_cumsum(x, seg):
    rows, width = x.shape
    tril = (lax.broadcasted_iota(jnp.int32, (seg, seg), 1)
            <= lax.broadcasted_iota(jnp.int32, (seg, seg), 0)).astype(BF16)
    parts = jnp.concatenate(_split3(x), axis=1)
    out = []
    for s in range(rows // seg):
        acc = _dot(tril, parts[s * seg:(s + 1) * seg])
        out.append(acc[:, :width] + acc[:, width:2 * width] + acc[:, 2 * width:])
    return jnp.concatenate(out, axis=0)


def _proj_va_kernel(hn_ref, wp_ref, ws_ref, gb_ref, wal_ref, bal_ref,
                    mv_ref, gv_ref, gq_ref, gk_ref, cu_ref, gmb_ref, gmt_ref, gcum_ref, *, chunk):
    hn = hn_ref[...]
    r = _dot(hn, ws_ref[...])
    _scatter_heads(mv_ref, _dot(hn, wp_ref[:, 0:M_W]), M_HEADS, M_HD, M_HDP, 0.0)
    for j in range(S_BLOCKS):
        cu_ref[j] = r[:, j * LANES:(j + 1) * LANES]
    raw = r[:, S_W:]
    gates = raw + gb_ref[...]
    lane = lax.broadcasted_iota(jnp.int32, gates.shape, 1)
    fgate = (lane >= M_HEADS) & (lane < 2 * M_HEADS)
    za = _dot(raw.astype(BF16), wal_ref[...]) + bal_ref[...]
    gm = jnp.where(fgate, _segment_cumsum(_log_sigmoid(gates), chunk), gates)
    _scatter_heads(gv_ref, _dot(hn, wp_ref[:, M_W:M_W + G_VW]), G_HEADS, G_DV, G_DVP, 0.0)
    gmt_ref[...] = gm.T
    slots = 2 * M_HEADS * LANES
    spread = (lax.broadcasted_iota(jnp.int32, (3 * LANES, slots), 0) % LANES
              == lax.broadcasted_iota(jnp.int32, (3 * LANES, slots), 1) // LANES).astype(BF16)
    gmb_ref[...] = _dot(jnp.concatenate(_split3(gm), axis=1), spread)
    off = M_W + G_VW
    _scatter_heads(gq_ref, _dot(hn, wp_ref[:, off:off + G_KW]) * (G_DK ** -0.5), G_HEADS, G_DK, G_DKP, 0.0)
    gcum_ref[...] = _segment_cumsum(_log_sigmoid(za) * (1.0 / G_TAU), chunk)
    _scatter_heads(gk_ref, _dot(hn, wp_ref[:, off + G_KW:]), G_HEADS, G_DK, G_DKP, 0.0)


def _proj_va(hn, l, wp, ws, gb, wal, bal, tm, chunk):
    n = hn.shape[0]
    widths = (M_WP, G_VWP, G_KWP, G_KWP)
    return pl.pallas_call(
        functools.partial(_proj_va_kernel, chunk=chunk), grid=(n // tm,),
        in_specs=[_rows(tm, D_MODEL), _of_layer(l, (D_MODEL, W_PLAIN)), _of_layer(l, (D_MODEL, SMALL_W)),
                  _of_layer(l, (1, LANES)), _of_layer(l, (LANES, G_KWP)), _of_layer(l, (1, G_KWP))],
        out_specs=[_rows(tm, wd) for wd in widths] + [
            pl.BlockSpec((S_BLOCKS, tm, LANES), lambda i: (0, i, 0)), _rows(tm, 2 * M_HEADS * LANES),
            pl.BlockSpec((LANES, tm), lambda i: (0, i)), _rows(tm, G_KWP)],
        out_shape=[jax.ShapeDtypeStruct((n, wd), BF16) for wd in widths] + [
            jax.ShapeDtypeStruct((S_BLOCKS, n, LANES), F32),
            jax.ShapeDtypeStruct((n, 2 * M_HEADS * LANES), F32),
            jax.ShapeDtypeStruct((LANES, n), F32),
            jax.ShapeDtypeStruct((n, G_KWP), F32)],
        compiler_params=_params("arbitrary"), name="proj_va",
    )(hn, wp, ws, gb, wal, bal)


def _mlstm_body(q_ref, k_ref, v_ref, oz_ref, gmb_ref, gmt_ref, nw_ref, out_ref, c_scr, m_scr, *, tc):
    row = lax.broadcasted_iota(jnp.int32, (tc, tc), 0)
    col = lax.broadcasted_iota(jnp.int32, (tc, tc), 1)
    causal = col <= row
    real = lax.broadcasted_iota(jnp.int32, (tc, M_HDP), 1) < M_HD
    ones_tile = jnp.ones((tc, LANES), BF16)

    def lanes(x, n):
        return jnp.concatenate([x] * n, axis=1)

    for h in range(M_HEADS):
        sl = slice(h * M_HDP, (h + 1) * M_HDP)
        qb = q_ref[:, sl]
        kb = k_ref[:, sl]
        vext = jnp.concatenate([v_ref[:, sl], ones_tile], axis=1)
        i_b = gmb_ref[:, h * LANES:(h + 1) * LANES]
        bcum = gmb_ref[:, (M_HEADS + h) * LANES:(M_HEADS + h + 1) * LANES]
        g_row = gmt_ref[h:h + 1, :] - gmt_ref[M_HEADS + h:M_HEADS + h + 1, :]
        m_prev = m_scr[h, 0:1, :]

        inter = bcum + m_prev
        m_t = jnp.maximum(inter, bcum + jnp.max(jnp.where(causal, g_row, NEG), axis=1, keepdims=True))
        w_intra = jnp.exp(jnp.where(causal, g_row + lanes(bcum - m_t, tc // LANES), NEG))
        w_inter = jnp.exp(inter - m_t)
        s = _dot_nt(qb, kb) * w_intra
        c_mat = c_scr[h]
        num = _dot(s.astype(BF16), vext) + lanes(w_inter, 3) * _dot(qb, c_mat.astype(BF16))
        den = num[:, M_HDP:]
        rden = 1.0 / jnp.maximum(jnp.abs(den), jnp.exp(-m_t))
        num = num[:, :M_HDP]

        b_end = bcum[tc - 1:tc, :]
        to_end = b_end - bcum + i_b
        m_new = jnp.maximum(b_end + m_prev, jnp.max(to_end, axis=0, keepdims=True))
        wk = jnp.exp(to_end - m_new)
        w_prev = jnp.exp(b_end + m_prev - m_new)
        c_scr[h] = lanes(w_prev, 3) * c_mat + _dot_tn((lanes(wk, 2) * kb.astype(F32)).astype(BF16), vext)
        m_scr[h] = jnp.broadcast_to(m_new, (8, LANES))

        sc = slice(h * M_HD, (h + 1) * M_HD)
        mean = jnp.sum(num, axis=1, keepdims=True) * (1.0 / M_HD)
        xc = jnp.where(real, num - mean, 0.0)
        var = jnp.sum(xc * xc, axis=1, keepdims=True) * (1.0 / M_HD)
        scale = rden * lax.rsqrt(rden * rden * var + EPS)
        ha = (xc * lanes(scale, 2))[:, :M_HD]
        out_ref[:, sc] = (ha * nw_ref[:, sc] * oz_ref[:, sc].astype(F32)).astype(BF16)


def _gla_body(q_ref, k_ref, v_ref, z_ref, b_ref, nw_ref,
              out_ref, kbuf, bbuf, attn_scr, st_scr, *, t, cs):
    row = lax.broadcasted_iota(jnp.int32, (t, t), 0)
    col = lax.broadcasted_iota(jnp.int32, (t, t), 1)
    causal = col <= row

    def operands(h):
        sl = slice(h * G_DKP, (h + 1) * G_DKP)
        return q_ref[:, sl].astype(F32), k_ref[:, sl].astype(F32), b_ref[:, sl]

    safe = jnp.min(b_ref[t - 1:t, :]) >= G_SAFE_LOG_DECAY

    @pl.when(safe)
    def _():
        for h in range(G_HEADS):
            qs, kk, b = operands(h)
            a = _dot_nt((qs * jnp.exp(b)).astype(BF16), (kk * jnp.exp(-b)).astype(BF16))
            attn_scr[h] = jnp.where(causal, a, 0.0)

    @pl.when(jnp.logical_not(safe))
    def _():
        rowv = lax.broadcasted_iota(jnp.int32, (t, G_DKP), 0)
        sub = rowv % cs
        lag = row - col
        kbuf[0:cs, :] = jnp.zeros((cs, G_DKP), F32)
        bbuf[0:cs, :] = jnp.zeros((cs, G_DKP), F32)
        for h in range(G_HEADS):
            qs, kk, b = operands(h)
            qparts, kparts = [], []
            for i in range(1, t // cs):
                r = b[i * cs - 1:i * cs, :]
                inblk = (rowv >= i * cs) & (rowv < (i + 1) * cs)
                qparts.append((qs * jnp.exp(jnp.where(inblk, b - r, NEG))).astype(BF16))
                kparts.append((kk * jnp.exp(jnp.where(rowv < i * cs, r - b, NEG))).astype(BF16))
            attn = _dot_nt(jnp.concatenate(qparts, axis=1), jnp.concatenate(kparts, axis=1))
            kbuf[cs:cs + t, :] = kk
            bbuf[cs:cs + t, :] = b
            for j in range(cs):
                ksh = kbuf[cs - j:cs - j + t, :]
                bsh = bbuf[cs - j:cs - j + t, :]
                e = jnp.exp(jnp.where(sub >= j, b - bsh, NEG))
                dj = jnp.sum(qs * ksh * e, axis=1, keepdims=True)
                attn = attn + jnp.where(lag == j, dj, 0.0)
            attn_scr[h] = attn

    for h in range(G_HEADS):
        slv = slice(h * G_DVP, (h + 1) * G_DVP)
        qs, kk, b = operands(h)
        v = v_ref[:, slv]
        blast = b[t - 1:t, :]
        st = st_scr[h]
        o = (_dot_nt((qs * jnp.exp(b)).astype(BF16), st.astype(BF16))
             + _dot(attn_scr[h].astype(BF16), v))
        kdec = (kk * jnp.exp(blast - b)).astype(BF16)
        st_scr[h] = jnp.exp(blast) * st + _dot_tn(v, kdec)

        sc = slice(h * G_DV, (h + 1) * G_DV)
        var = jnp.sum(o * o, axis=1, keepdims=True) * (1.0 / G_DV)
        hb = (o * lax.rsqrt(var + EPS))[:, :G_DV] * nw_ref[:, sc]
        out_ref[:, sc] = (hb * z_ref[:, sc].astype(F32)).astype(BF16)


def _mixers_kernel(mq_ref, mk_ref, mv_ref, moz_ref, gmb_ref, gmt_ref, mnw_ref,
                   gq_ref, gk_ref, gv_ref, gz_ref, gb_ref, gnw_ref, ya_ref, yb_ref,
                   c_scr, m_scr, kbuf, bbuf, attn_scr, st_scr, *, t, cs):
    @pl.when(pl.program_id(1) == 0)
    def _():
        c_scr[...] = jnp.zeros_like(c_scr)
        m_scr[...] = jnp.zeros_like(m_scr)
        st_scr[...] = jnp.zeros_like(st_scr)

    _gla_body(gq_ref, gk_ref, gv_ref, gz_ref, gb_ref, gnw_ref, yb_ref, kbuf, bbuf, attn_scr, st_scr,
              t=t, cs=cs)
    _mlstm_body(mq_ref, mk_ref, mv_ref, moz_ref, gmb_ref, gmt_ref, mnw_ref, ya_ref, c_scr, m_scr, tc=t)


def _mixers(mq, mk, mv, moz, gmb, gmt, gq, gk, gv, gz, gcum, l, m_nw, g_nw, batch, seq, t, cs):
    n = batch * seq
    nt = seq // t

    def blk(width):
        return pl.BlockSpec((t, width), lambda b, c: (b * nt + c, 0))

    return pl.pallas_call(
        functools.partial(_mixers_kernel, t=t, cs=cs),
        grid=(batch, nt),
        in_specs=[blk(M_WP), blk(M_WP), blk(M_WP), blk(M_W), blk(2 * M_HEADS * LANES),
                  pl.BlockSpec((LANES, t), lambda b, c: (0, b * nt + c)), _of_layer(l, (1, M_W)),
                  blk(G_KWP), blk(G_KWP), blk(G_VWP), blk(G_VW), blk(G_KWP), _of_layer(l, (1, G_VW))],
        out_specs=[blk(M_W), blk(G_VW)],
        out_shape=[jax.ShapeDtypeStruct((n, M_W), BF16), jax.ShapeDtypeStruct((n, G_VW), BF16)],
        scratch_shapes=[pltpu.VMEM((M_HEADS, M_HDP, M_HDP + LANES), F32),
                        pltpu.VMEM((M_HEADS, 8, LANES), F32),
                        pltpu.VMEM((t + cs, G_DKP), F32), pltpu.VMEM((t + cs, G_DKP), F32),
                        pltpu.VMEM((G_HEADS, t, t), F32),
                        pltpu.VMEM((G_HEADS, G_DVP, G_DKP), F32)],
        compiler_params=_params("arbitrary", "arbitrary"),
        name="mixers",
    )(mq, mk, mv, moz, gmb, gmt, m_nw, gq, gk, gv, gz, gcum, g_nw)


def _gelu_tanh(x):
    return 0.5 * x * (1.0 + jnp.tanh(math.sqrt(2.0 / math.pi) * (x + 0.044715 * (x * x * x))))


def _s5_kernel(u_ref, kms_ref, mo_ref, pin_ref, pt_ref, d_ref, y_ref, carry, xs, *, nch):
    @pl.when(pl.program_id(2) == 0)
    def _():
        carry[...] = jnp.zeros_like(carry)

    qw = S_Q * LANES
    ngrp = nch // S_SUB
    ltiles = S_SW // LANES
    ucat = jnp.concatenate([u_ref[0, pl.ds(s, nch, stride=S_Q), :] for s in range(S_Q)],
                           axis=1).astype(BF16)
    r = _dot(ucat, kms_ref[0])
    y = r[:, :qw]
    xr = r[:, qw:qw + S_SW]
    xi = r[:, qw + S_SW:]

    def axpy(xr, xi, pr, pi, sr, si):
        return xr + (pr * sr - pi * si), xi + (pr * si + pi * sr)

    def scan(xr, xi, p_ref, pos, length):
        d = 1
        while d < length:
            keep = pos >= d
            sr = jnp.where(keep, pltpu.roll(xr, d, 0), 0.0)
            si = jnp.where(keep, pltpu.roll(xi, d, 0), 0.0)
            xr, xi = axpy(xr, xi, p_ref[0, d - 1:d, :S_SW], p_ref[0, d - 1:d, S_SW:], sr, si)
            d *= 2
        return xr, xi

    def slabs(part, arr):
        return [(part * ltiles + j, arr[:, j * LANES:(j + 1) * LANES]) for j in range(ltiles)]

    rowi = lax.broadcasted_iota(jnp.int32, (nch, S_SW), 0)
    xr, xi = scan(xr, xi, pin_ref, rowi % S_SUB, S_SUB)
    for part, arr in ((0, xr), (1, xi)):
        for j, piece in slabs(part, arr):
            xs[j] = piece

    def group_ends(part):
        return jnp.concatenate([xs[part * ltiles + j, pl.ds(S_SUB - 1, ngrp, stride=S_SUB), :]
                                for j in range(ltiles)], axis=1)

    rowg = lax.broadcasted_iota(jnp.int32, (ngrp, S_SW), 0)
    er, ei = scan(group_ends(0), group_ends(1), pt_ref, rowg, ngrp)
    cr = carry[0:1, :S_SW]
    ci = carry[0:1, S_SW:]
    er, ei = axpy(er, ei, pt_ref[0, :, :S_SW], pt_ref[0, :, S_SW:], cr, ci)
    first = rowg == 0
    gr = jnp.where(first, cr, pltpu.roll(er, 1, 0))
    gi = jnp.where(first, ci, pltpu.roll(ei, 1, 0))
    carry[0:1, :S_SW] = er[ngrp - 1:ngrp, :]
    carry[0:1, S_SW:] = ei[ngrp - 1:ngrp, :]
    for part, arr in ((0, gr), (1, gi)):
        for j, piece in slabs(part, arr):
            for k in range(S_SUB):
                xs[j, pl.ds(k, ngrp, stride=S_SUB), :] = piece
    sr = jnp.concatenate([xs[j] for j in range(ltiles)], axis=1)
    si = jnp.concatenate([xs[ltiles + j] for j in range(ltiles)], axis=1)
    pwr = jnp.concatenate([pin_ref[0, :, :S_SW]] * ngrp, axis=0)
    pwi = jnp.concatenate([pin_ref[0, :, S_SW:]] * ngrp, axis=0)
    xr, xi = axpy(xr, xi, pwr, pwi, sr, si)

    firstc = rowi == 0
    xpr = jnp.where(firstc, cr, pltpu.roll(xr, 1, 0))
    xpi = jnp.where(firstc, ci, pltpu.roll(xi, 1, 0))
    y = y + _dot(jnp.concatenate([xpr, xpi], axis=1).astype(BF16), mo_ref[0])
    for t in range(S_Q):
        y_ref[0, pl.ds(t, nch, stride=S_Q), :] = y[:, t * LANES:(t + 1) * LANES]
    y_ref[0] = _gelu_tanh(y_ref[0] + d_ref[0] * u_ref[0])


def _s5(cu, l, kms, mo, pin, pt, dskip, batch, seq, tm):
    n = batch * seq
    nt = seq // tm
    nch = tm // S_Q
    qw = S_Q * LANES

    def table(rows, cols):
        return pl.BlockSpec((None, 1, rows, cols), lambda j, b, t: (l, j, 0, 0))

    return pl.pallas_call(
        functools.partial(_s5_kernel, nch=nch),
        grid=(S_BLOCKS, batch, nt),
        in_specs=[pl.BlockSpec((1, tm, LANES), lambda j, b, t: (j, b * nt + t, 0)),
                  table(qw, qw + 2 * S_SW), table(2 * S_SW, qw), table(S_SUB, 2 * S_SW),
                  table(nch // S_SUB, 2 * S_SW), table(1, LANES)],
        out_specs=pl.BlockSpec((1, tm, LANES), lambda j, b, t: (j, b * nt + t, 0)),
        out_shape=jax.ShapeDtypeStruct((S_BLOCKS, n, LANES), F32),
        scratch_shapes=[pltpu.VMEM((8, 2 * S_SW), F32),
                        pltpu.VMEM((2 * S_SW // LANES, nch, LANES), F32)],
        compiler_params=_params("arbitrary", "arbitrary", "arbitrary"),
        name="s5",
    )(cu, kms, mo, pin, pt, dskip)


def _s5_expand_kernel(kt_ref, msr_ref, msi_ref, mor_ref, moi_ref, kms_ref, mo_ref):
    q = S_Q
    qw = q * LANES

    def embed(m, rows, width):
        shape = (width, S_GPB * width)
        tile = (lax.broadcasted_iota(jnp.int32, shape, 1) % width
                == lax.broadcasted_iota(jnp.int32, shape, 0)).astype(BF16)
        oshape = (S_GPB * rows, S_GPB * width)
        diag = (lax.broadcasted_iota(jnp.int32, oshape, 0) // rows
                == lax.broadcasted_iota(jnp.int32, oshape, 1) // width)
        return jnp.where(diag, _dot(m, tile), 0.0).astype(BF16)

    kd = [embed(kt_ref[0, 0, tau], S_GROUP, S_GROUP) for tau in range(q)]
    zero = jnp.zeros((LANES, LANES), BF16)
    for s in range(q):
        rows = slice(s * LANES, (s + 1) * LANES)
        for t in range(q):
            kms_ref[0, 0, rows, t * LANES:(t + 1) * LANES] = kd[t - s] if t >= s else zero
        kms_ref[0, 0, rows, qw:qw + S_SW] = embed(msr_ref[0, 0, s], S_GROUP, S_STATE)
        kms_ref[0, 0, rows, qw + S_SW:] = embed(msi_ref[0, 0, s], S_GROUP, S_STATE)
    for t in range(q):
        cols = slice(t * LANES, (t + 1) * LANES)
        mo_ref[0, 0, 0:S_SW, cols] = embed(mor_ref[0, 0, t], S_STATE, S_GROUP)
        mo_ref[0, 0, S_SW:, cols] = embed(moi_ref[0, 0, t], S_STATE, S_GROUP)


def _s5_expand(kt, msr, msi, mor, moi):
    depth = kt.shape[0]
    qw = S_Q * LANES

    def blk(a):
        return pl.BlockSpec((1, 1) + a.shape[2:], lambda d, j: (d, j, 0, 0, 0))

    return pl.pallas_call(
        _s5_expand_kernel, grid=(depth, S_BLOCKS),
        in_specs=[blk(a) for a in (kt, msr, msi, mor, moi)],
        out_specs=[pl.BlockSpec((1, 1, qw, qw + 2 * S_SW), lambda d, j: (d, j, 0, 0)),
                   pl.BlockSpec((1, 1, 2 * S_SW, qw), lambda d, j: (d, j, 0, 0))],
        out_shape=[jax.ShapeDtypeStruct((depth, S_BLOCKS, qw, qw + 2 * S_SW), BF16),
                   jax.ShapeDtypeStruct((depth, S_BLOCKS, 2 * S_SW, qw), BF16)],
        compiler_params=_params("arbitrary", "arbitrary"), name="s5_expand",
    )(kt, msr, msi, mor, moi)


def _s5_tables(lam_re, lam_im, log_dt, b_re, b_im, c_re, c_im, nch):
    q = S_Q
    dep = lam_re.shape[0]
    lr = jnp.minimum(lam_re.astype(F32), -1e-4)
    li = lam_im.astype(F32)
    dt = jnp.exp(log_dt.astype(F32))[..., None]
    mag = jnp.exp(lr * dt)
    ab_re = mag * jnp.cos(li * dt)
    ab_im = mag * jnp.sin(li * dt)
    nr = ab_re - 1.0
    den = lr * lr + li * li
    coef_re = ((nr * lr + ab_im * li) / den)[:, :, None, :]
    coef_im = ((ab_im * lr - nr * li) / den)[:, :, None, :]
    brt = jnp.swapaxes(b_re.astype(F32), -1, -2)
    bit = jnp.swapaxes(b_im.astype(F32), -1, -2)
    bb_re = coef_re * brt - coef_im * bit
    bb_im = coef_re * bit + coef_im * brt

    def apow(nvals):
        e = jnp.asarray(nvals, F32)[:, None, None, None]
        m = jnp.exp(e * (lr * dt))
        ang = e * (li * dt)
        return m * jnp.cos(ang), m * jnp.sin(ang)

    def per_block(m, rows):
        m = m.reshape(q, dep, S_BLOCKS, S_GPB * rows, m.shape[-1])
        return m.transpose(1, 2, 0, 3, 4).astype(BF16)

    ar, ai = apow(list(range(q + 1)))
    cr, ci = c_re.astype(F32), c_im.astype(F32)
    ca_re = cr[None] * ar[:, :, :, None, :] - ci[None] * ai[:, :, :, None, :]
    ca_im = cr[None] * ai[:, :, :, None, :] + ci[None] * ar[:, :, :, None, :]
    kt = jnp.sum(ca_re[:q, :, :, None, :, :] * bb_re[None, :, :, :, None, :]
                 - ca_im[:q, :, :, None, :, :] * bb_im[None, :, :, :, None, :], axis=-1)
    pw_r = ar[:q][::-1][:, :, :, None, :]
    pw_i = ai[:q][::-1][:, :, :, None, :]
    kms, mo = _s5_expand(per_block(kt, S_GROUP),
                         per_block(pw_r * bb_re - pw_i * bb_im, S_GROUP),
                         per_block(pw_r * bb_im + pw_i * bb_re, S_GROUP),
                         per_block(jnp.swapaxes(ca_re[1:], -1, -2), S_STATE),
                         per_block(jnp.swapaxes(-ca_im[1:], -1, -2), S_STATE))
    def power_table(nvals):
        pr, pi = apow(nvals)
        pr = pr.reshape(len(nvals), dep, S_BLOCKS, S_SW).transpose(1, 2, 0, 3)
        pi = pi.reshape(len(nvals), dep, S_BLOCKS, S_SW).transpose(1, 2, 0, 3)
        return jnp.concatenate([pr, pi], axis=3)

    pin = power_table([q * (r + 1) for r in range(S_SUB)])
    pt = power_table([q * S_SUB * (k + 1) for k in range(nch // S_SUB)])
    return kms, mo, pin, pt


def _merge_kernel(ya_ref, yb_ref, yc_ref, cz_ref, g_ref, x_ref, wa_ref, wb_ref, wglu_ref, wc_ref,
                  wo_ref, nw_ref, *out_refs, final):
    yc0 = jnp.concatenate([yc_ref[j] for j in range(S_BLOCKS)], axis=1)
    glu = _dot(yc0.astype(BF16), wglu_ref[...])
    yc = yc0 * _sigmoid(glu) * cz_ref[...].astype(F32)
    merged = (g_ref[:, 0:D_MODEL].astype(F32) * _dot(ya_ref[...], wa_ref[...])
              + g_ref[:, D_MODEL:2 * D_MODEL].astype(F32) * _dot(yb_ref[...], wb_ref[...])
              + g_ref[:, 2 * D_MODEL:].astype(F32) * _dot(yc.astype(BF16), wc_ref[...]))
    out = x_ref[...] + _dot(merged.astype(BF16), wo_ref[...])
    ms = jnp.mean(out * out, axis=-1, keepdims=True)
    normed = out * lax.rsqrt(ms + EPS) * nw_ref[...]
    if final:
        out_refs[0][...] = normed
    else:
        out_refs[0][...] = out
        out_refs[1][...] = normed.astype(BF16)


def _merge(ya, yb, yc, cz, g, x2, l, wa, wb, wglu, wc, wo, nw, tm, final):
    n = x2.shape[0]
    x_out = jax.ShapeDtypeStruct((n, D_MODEL), F32)
    hn_out = jax.ShapeDtypeStruct((n, D_MODEL), BF16)
    return pl.pallas_call(
        functools.partial(_merge_kernel, final=final),
        grid=(n // tm,),
        in_specs=[_rows(tm, M_W), _rows(tm, G_VW),
                  pl.BlockSpec((S_BLOCKS, tm, LANES), lambda i: (0, i, 0)),
                  _rows(tm, S_W), _rows(tm, 3 * D_MODEL), _rows(tm, D_MODEL),
                  _of_layer(l, (M_W, D_MODEL)), _of_layer(l, (G_VW, D_MODEL)), _of_layer(l, (S_W, S_W)),
                  _of_layer(l, (S_W, D_MODEL)), _of_layer(l, (D_MODEL, D_MODEL)), _whole((1, D_MODEL))],
        out_specs=[_rows(tm, D_MODEL)] if final else [_rows(tm, D_MODEL), _rows(tm, D_MODEL)],
        out_shape=[x_out] if final else [x_out, hn_out],
        compiler_params=_params("arbitrary"),
        name="merge",
    )(ya, yb, yc, cz, g, x2, wa, wb, wglu, wc, wo, nw)


def _pad_heads(w, heads, d, dp):
    zeros = jnp.zeros(w.shape[:-1] + (dp - d,), w.dtype)
    parts = []
    for h in range(heads):
        parts += [w[..., h * d:(h + 1) * d], zeros]
    return jnp.concatenate(parts, axis=-1)


def _prepare(p, nch):
    depth = p["w_in"].shape[0]
    w_mlstm, w_plain, w_gate, w_small = _wprep(jnp.swapaxes(p["w_in"], 1, 2), LANES)
    gkp = functools.partial(_pad_heads, heads=G_HEADS, d=G_DK, dp=G_DKP)
    kms, mo, pin, pt = _s5_tables(p["s5_lam_re"], p["s5_lam_im"], p["s5_log_dt"], p["s5_B_re"], p["s5_B_im"],
                               p["s5_C_re"], p["s5_C_im"], nch)
    gate_b = p["mlstm_gate_b"]
    return dict(
        w_mlstm=w_mlstm, w_plain=w_plain, w_gate=w_gate, w_small=w_small,
        cq=p["mlstm_conv"][:, :, :M_W], ck=p["mlstm_conv"][:, :, M_W:],
        gb=jnp.concatenate([gate_b[:, 0], gate_b[:, 1],
                            jnp.zeros((depth, LANES - 2 * M_HEADS), F32)], axis=1)[:, None, :],
        m_nw=p["mlstm_norm"][:, None, :],
        wal=jnp.concatenate([jnp.zeros((depth, 2 * M_HEADS, G_KWP), F32), gkp(p["gla_w_alpha"]),
                             jnp.zeros((depth, LANES - 2 * M_HEADS - G_RANK, G_KWP), F32)],
                            axis=1).astype(BF16),
        bal=gkp(p["gla_b_alpha"])[:, None, :],
        g_nw=p["gla_norm"][:, None, :],
        kms=kms, mo=mo, pin=pin, pt=pt,
        dskip=p["s5_D"].reshape(depth, S_BLOCKS, 1, LANES),
        wa=p["w_branch_mlstm"].astype(BF16), wb=p["w_branch_gla"].astype(BF16),
        wglu=p["s5_w_glu"].astype(BF16), wc=p["w_branch_s5"].astype(BF16), wo=p["w_out"].astype(BF16),
    )


def _tiles(batch, seq):
    return dict(proj=min(1024, seq), chunk=min(256, seq), g_sub=16,
                s5=min(4096, seq), merge=min(512, batch * seq))


def _layer(x2, hn, batch, seq, l, w, next_norm, final, tl):
    tm = tl["proj"]
    q, k, oz = _proj_mlstm(hn, l, w["w_mlstm"], w["cq"], w["ck"], seq, tm)
    gz, czs, gs = _proj_gate(hn, l, w["w_gate"], tm)
    mv, gv, gq, gk, cu_s, gm, gmt, gcum = _proj_va(hn, l, w["w_plain"], w["w_small"], w["gb"], w["wal"],
                                                   w["bal"], tm, tl["chunk"])
    ya, yb = _mixers(q, k, mv, oz, gm, gmt, gq, gk, gv, gz, gcum, l, w["m_nw"], w["g_nw"], batch, seq,
                     tl["chunk"], tl["g_sub"])
    yc = _s5(cu_s, l, w["kms"], w["mo"], w["pin"], w["pt"], w["dskip"], batch, seq, tl["s5"])
    return _merge(ya, yb, yc, czs, gs, x2, l, w["wa"], w["wb"], w["wglu"], w["wc"], w["wo"],
                  next_norm.reshape(1, D_MODEL), tl["merge"], final)


def kernel(x, norm_w, w_in, mlstm_conv, mlstm_gate_b, mlstm_norm, gla_w_alpha, gla_b_alpha, gla_norm,
           s5_lam_re, s5_lam_im, s5_log_dt, s5_B_re, s5_B_im, s5_C_re, s5_C_im, s5_D, s5_w_glu,
           w_branch_mlstm, w_branch_gla, w_branch_s5, w_out, final_norm):
    batch, seq, _ = x.shape
    depth = norm_w.shape[0]
    tl = _tiles(batch, seq)
    prep = _prepare(dict(w_in=w_in, mlstm_conv=mlstm_conv, mlstm_gate_b=mlstm_gate_b, mlstm_norm=mlstm_norm,
                         gla_w_alpha=gla_w_alpha, gla_b_alpha=gla_b_alpha, gla_norm=gla_norm,
                         s5_lam_re=s5_lam_re, s5_lam_im=s5_lam_im, s5_log_dt=s5_log_dt, s5_B_re=s5_B_re,
                         s5_B_im=s5_B_im, s5_C_re=s5_C_re, s5_C_im=s5_C_im, s5_D=s5_D, s5_w_glu=s5_w_glu,
                         w_branch_mlstm=w_branch_mlstm, w_branch_gla=w_branch_gla,
                         w_branch_s5=w_branch_s5, w_out=w_out), tl["s5"] // S_Q)
    x2 = x.reshape(batch * seq, D_MODEL)
    hn = _rms(x2, norm_w[0].reshape(1, D_MODEL), min(1024, batch * seq))
    for l in range(depth):
        final = l == depth - 1
        res = _layer(x2, hn, batch, seq, l, prep, final_norm if final else norm_w[l + 1], final, tl)
        if final:
            x2 = res[0]
        else:
            x2, hn = res
    return x2.reshape(batch, seq, D_MODEL)
```

```python
import functools
import math

import jax
import jax.numpy as jnp
from jax import lax
from jax.experimental import pallas as pl
from jax.experimental.pallas import tpu as pltpu

F32 = jnp.float32
BF16 = jnp.bfloat16
HIGHEST = lax.Precision.HIGHEST

EPS = 1e-6
D_MODEL = 1024
LANES = 128
M_HEADS = 4
M_HD = 192
M_HDP = 256
M_W = M_HEADS * M_HD
M_WP = M_HEADS * M_HDP
M_CONV = 4
G_HEADS = 4
G_DK = 96
G_DKP = 128
G_DV = 192
G_DVP = 256
G_KW = G_HEADS * G_DK
G_KWP = G_HEADS * G_DKP
G_VW = G_HEADS * G_DV
G_VWP = G_HEADS * G_DVP
G_RANK = 16
G_TAU = 16.0
S_W = 512
S_GROUP = 16
S_GROUPS = 32
S_STATE = 64
S_BLOCKS = S_W // LANES
S_GPB = LANES // S_GROUP
S_SW = S_GPB * S_STATE
S_Q = 8
S_SUB = 8

IN_SIZES = (M_W, M_W, M_W, M_W, M_HEADS, M_HEADS, M_W,
            G_KW, G_KW, G_VW, G_RANK, G_VW,
            S_W, S_W, 3 * D_MODEL)
SMALL_W = S_W + LANES
CW = 512

NEG = -1e30
G_SAFE_LOG_DECAY = -60.0
VMEM_LIMIT = 56 * 1024 * 1024


def _sigmoid(x):
    return 0.5 * jnp.tanh(0.5 * x) + 0.5


def _silu(x):
    h = 0.5 * x
    return h * jnp.tanh(h) + h


def _log_sigmoid(x):
    return jnp.minimum(x, 0.0) - jnp.log1p(jnp.exp(-jnp.abs(x)))


def _dot(a, b):
    return jnp.dot(a, b, preferred_element_type=F32)


def _dot_nt(a, b):
    return lax.dot_general(a, b, (((1,), (1,)), ((), ())), preferred_element_type=F32)


def _dot_tn(a, b):
    return lax.dot_general(a, b, (((0,), (0,)), ((), ())), preferred_element_type=F32)


def _params(*sem):
    return pltpu.CompilerParams(dimension_semantics=sem, vmem_limit_bytes=VMEM_LIMIT)


def _rows(tm, width):
    return pl.BlockSpec((tm, width), lambda i: (i, 0))


def _whole(shape):
    return pl.BlockSpec(shape, lambda i: (0,) * len(shape))


def _of_layer(l, shape):
    return pl.BlockSpec((None,) + shape, lambda *_: (l,) + (0,) * len(shape))


def _rms_kernel(x_ref, nw_ref, o_ref):
    x = x_ref[...]
    ms = jnp.mean(x * x, axis=-1, keepdims=True)
    o_ref[...] = (x * lax.rsqrt(ms + EPS) * nw_ref[...]).astype(BF16)


def _rms(x2, nw, tm):
    n = x2.shape[0]
    return pl.pallas_call(
        _rms_kernel, grid=(n // tm,),
        in_specs=[_rows(tm, D_MODEL), _whole((1, D_MODEL))],
        out_specs=_rows(tm, D_MODEL),
        out_shape=jax.ShapeDtypeStruct((n, D_MODEL), BF16),
        compiler_params=_params("arbitrary"), name="rms",
    )(x2, nw)


W_OFF = {}
_o = 0
for _name, _size in zip(("aq", "ak", "av", "ao", "ai", "af", "az", "bq", "bk", "bv", "ba", "bz", "cu", "cz", "g"),
                        IN_SIZES):
    W_OFF[_name] = _o
    _o += _size
IN_WIDTH = _o
W_MLSTM, W_PLAIN, W_GATE = 4 * M_W, M_W + G_VW + 2 * G_KW, G_VW + S_W + 3 * D_MODEL


def _wprep_kernel(w_ref, wm_ref, wp_ref, wg_ref, ws_ref):
    kb = w_ref.shape[2]

    def put(dst, dst_off, parts, width):
        xs = [w_ref[0, off:off + rows, :] for off, rows in parts]
        used = sum(rows for _, rows in parts)
        if used < width:
            xs.append(jnp.zeros((width - used, kb), F32))
        x = xs[0] if len(xs) == 1 else jnp.concatenate(xs, axis=0)
        dst[0, :, dst_off:dst_off + width] = x.T.astype(BF16)

    def group(dst, names):
        off = 0
        for name in names:
            width = IN_SIZES[list(W_OFF).index(name)]
            for c in range(0, width, LANES):
                put(dst, off + c, [(W_OFF[name] + c, LANES)], LANES)
            off += width

    group(wm_ref, ("aq", "ak", "ao", "az"))
    group(wp_ref, ("av", "bv", "bq", "bk"))
    group(wg_ref, ("bz", "cz", "g"))
    group(ws_ref, ("cu",))
    put(ws_ref, S_W, [(W_OFF["ai"], 2 * M_HEADS), (W_OFF["ba"], G_RANK)], LANES)


def _wprep(w_t, kb):
    depth = w_t.shape[0]
    widths = (W_MLSTM, W_PLAIN, W_GATE, SMALL_W)
    return pl.pallas_call(
        _wprep_kernel, grid=(depth, D_MODEL // kb),
        in_specs=[pl.BlockSpec((1, IN_WIDTH, kb), lambda l, i: (l, 0, i))],
        out_specs=[pl.BlockSpec((1, kb, wd), lambda l, i: (l, i, 0)) for wd in widths],
        out_shape=[jax.ShapeDtypeStruct((depth, D_MODEL, wd), BF16) for wd in widths],
        compiler_params=_params("arbitrary", "arbitrary"), name="wprep",
    )(w_t)


def _scatter_heads(dst, y, heads, d, dp, fill):
    rows = y.shape[0]
    for h in range(heads):
        dst[:, h * dp:h * dp + d] = y[:, h * d:(h + 1) * d].astype(BF16)
        dst[:, h * dp + d:(h + 1) * dp] = jnp.full((rows, dp - d), fill, BF16)


def _proj_mlstm_kernel(hn_ref, w_ref, cq_ref, ck_ref, q_ref, k_ref, oz_ref, halo_x, halo_b, *, tm,
                       tiles_per_seq):
    @pl.when(pl.program_id(0) % tiles_per_seq == 0)
    def _():
        halo_x[...] = jnp.zeros_like(halo_x)
        halo_b[...] = jnp.zeros_like(halo_b)

    hn = hn_ref[...]
    row8 = lax.broadcasted_iota(jnp.int32, (8, M_W), 0)

    def shift(x, prev8, d):
        rolled = pltpu.roll(x, d, 0)
        head = jnp.where(row8 < d, pltpu.roll(prev8, d, 0), rolled[0:8])
        return jnp.concatenate([head, rolled[8:]], axis=0)

    def conv_act(c, x, cw_ref, dst, scale):
        x1 = shift(x, halo_x[c], 1)
        pair = cw_ref[1:2, :] * x + cw_ref[0:1, :] * x1
        y = _silu(cw_ref[3:4, :] * x + cw_ref[2:3, :] * x1 + shift(pair, halo_b[c], 2))
        halo_x[c] = x[tm - 8:tm]
        halo_b[c] = pair[tm - 8:tm]
        _scatter_heads(dst, y if scale == 1.0 else y * scale, M_HEADS, M_HD, M_HDP, 0.0)

    xq = _dot(hn, w_ref[:, 0:M_W])
    xk = _dot(hn, w_ref[:, M_W:2 * M_W])
    conv_act(0, xq, cq_ref, q_ref, 1.0)
    o = _dot(hn, w_ref[:, 2 * M_W:3 * M_W])
    conv_act(1, xk, ck_ref, k_ref, M_HD ** -0.5)
    z = _dot(hn, w_ref[:, 3 * M_W:])
    oz_ref[...] = (_sigmoid(o) * _silu(z)).astype(BF16)


def _proj_mlstm(hn, l, w, cq, ck, seq, tm):
    n = hn.shape[0]
    padded = jax.ShapeDtypeStruct((n, M_WP), BF16)
    return pl.pallas_call(
        functools.partial(_proj_mlstm_kernel, tm=tm, tiles_per_seq=seq // tm),
        grid=(n // tm,),
        in_specs=[_rows(tm, D_MODEL), _of_layer(l, (D_MODEL, W_MLSTM)), _of_layer(l, (M_CONV, M_W)),
                  _of_layer(l, (M_CONV, M_W))],
        out_specs=[_rows(tm, M_WP), _rows(tm, M_WP), _rows(tm, M_W)],
        out_shape=[padded, padded, jax.ShapeDtypeStruct((n, M_W), BF16)],
        scratch_shapes=[pltpu.VMEM((2, 8, M_W), F32)] * 2,
        compiler_params=_params("arbitrary"), name="proj_mlstm",
    )(hn, w, cq, ck)


def _proj_gate_kernel(hn_ref, w_ref, gz_ref, cz_ref, g_ref):
    hn = hn_ref[...]
    gz_ref[...] = _silu(_dot(hn, w_ref[:, 0:G_VW])).astype(BF16)
    cz_ref[...] = _silu(_dot(hn, w_ref[:, G_VW:G_VW + S_W])).astype(BF16)
    off = G_VW + S_W
    for c in range(3 * D_MODEL // CW):
        r = _dot(hn, w_ref[:, off + c * CW:off + (c + 1) * CW])
        g_ref[:, c * CW:(c + 1) * CW] = _sigmoid(r).astype(BF16)


def _proj_gate(hn, l, w, tm):
    n = hn.shape[0]
    widths = (G_VW, S_W, 3 * D_MODEL)
    return pl.pallas_call(
        _proj_gate_kernel, grid=(n // tm,),
        in_specs=[_rows(tm, D_MODEL), _of_layer(l, (D_MODEL, W_GATE))],
        out_specs=[_rows(tm, wd) for wd in widths],
        out_shape=[jax.ShapeDtypeStruct((n, wd), BF16) for wd in widths],
        compiler_params=_params("arbitrary"), name="proj_gate",
    )(hn, w)


def _split3(x):
    hi = x.astype(BF16)
    r1 = x - hi.astype(F32)
    mid = r1.astype(BF16)
    return hi, mid, (r1 - mid.astype(F32)).astype(BF16)


def _segment_cumsum(x, seg):
    rows, width = x.shape
    tril = (lax.broadcasted_iota(jnp.int32, (seg, seg), 1)
            <= lax.broadcasted_iota(jnp.int32, (seg, seg), 0)).astype(BF16)
    parts = jnp.concatenate(_split3(x), axis=1)
    out = []
    for s in range(rows // seg):
        acc = _dot(tril, parts[s * seg:(s + 1) * seg])
        out.append(acc[:, :width] + acc[:, width:2 * width] + acc[:, 2 * width:])
    return jnp.concatenate(out, axis=0)


def _proj_va_kernel(hn_ref, wp_ref, ws_ref, gb_ref, wal_ref, bal_ref,
                    mv_ref, gv_ref, gq_ref, gk_ref, cu_ref, gmb_ref, gmt_ref, gcum_ref, *, chunk):
    hn = hn_ref[...]
    r = _dot(hn, ws_ref[...])
    _scatter_heads(mv_ref, _dot(hn, wp_ref[:, 0:M_W]), M_HEADS, M_HD, M_HDP, 0.0)
    for j in range(S_BLOCKS):
        cu_ref[j] = r[:, j * LANES:(j + 1) * LANES]
    raw = r[:, S_W:]
    gates = raw + gb_ref[...]
    lane = lax.broadcasted_iota(jnp.int32, gates.shape, 1)
    fgate = (lane >= M_HEADS) & (lane < 2 * M_HEADS)
    za = _dot(raw.astype(BF16), wal_ref[...]) + bal_ref[...]
    gm = jnp.where(fgate, _segment_cumsum(_log_sigmoid(gates), chunk), gates)
    _scatter_heads(gv_ref, _dot(hn, wp_ref[:, M_W:M_W + G_VW]), G_HEADS, G_DV, G_DVP, 0.0)
    gmt_ref[...] = gm.T
    slots = 2 * M_HEADS * LANES
    spread = (lax.broadcasted_iota(jnp.int32, (3 * LANES, slots), 0) % LANES
              == lax.broadcasted_iota(jnp.int32, (3 * LANES, slots), 1) // LANES).astype(BF16)
    gmb_ref[...] = _dot(jnp.concatenate(_split3(gm), axis=1), spread)
    off = M_W + G_VW
    _scatter_heads(gq_ref, _dot(hn, wp_ref[:, off:off + G_KW]) * (G_DK ** -0.5), G_HEADS, G_DK, G_DKP, 0.0)
    gcum_ref[...] = _segment_cumsum(_log_sigmoid(za) * (1.0 / G_TAU), chunk)
    _scatter_heads(gk_ref, _dot(hn, wp_ref[:, off + G_KW:]), G_HEADS, G_DK, G_DKP, 0.0)


def _proj_va(hn, l, wp, ws, gb, wal, bal, tm, chunk):
    n = hn.shape[0]
    widths = (M_WP, G_VWP, G_KWP, G_KWP)
    return pl.pallas_call(
        functools.partial(_proj_va_kernel, chunk=chunk), grid=(n // tm,),
        in_specs=[_rows(tm, D_MODEL), _of_layer(l, (D_MODEL, W_PLAIN)), _of_layer(l, (D_MODEL, SMALL_W)),
                  _of_layer(l, (1, LANES)), _of_layer(l, (LANES, G_KWP)), _of_layer(l, (1, G_KWP))],
        out_specs=[_rows(tm, wd) for wd in widths] + [
            pl.BlockSpec((S_BLOCKS, tm, LANES), lambda i: (0, i, 0)), _rows(tm, 2 * M_HEADS * LANES),
            pl.BlockSpec((LANES, tm), lambda i: (0, i)), _rows(tm, G_KWP)],
        out_shape=[jax.ShapeDtypeStruct((n, wd), BF16) for wd in widths] + [
            jax.ShapeDtypeStruct((S_BLOCKS, n, LANES), F32),
            jax.ShapeDtypeStruct((n, 2 * M_HEADS * LANES), F32),
            jax.ShapeDtypeStruct((LANES, n), F32),
            jax.ShapeDtypeStruct((n, G_KWP), F32)],
        compiler_params=_params("arbitrary"), name="proj_va",
    )(hn, wp, ws, gb, wal, bal)


def _mlstm_body(q_ref, k_ref, v_ref, oz_ref, gmb_ref, gmt_ref, nw_ref, out_ref, c_scr, m_scr, *, tc):
    row = lax.broadcasted_iota(jnp.int32, (tc, tc), 0)
    col = lax.broadcasted_iota(jnp.int32, (tc, tc), 1)
    causal = col <= row
    real = lax.broadcasted_iota(jnp.int32, (tc, M_HDP), 1) < M_HD
    ones_tile = jnp.ones((tc, LANES), BF16)

    def lanes(x, n):
        return jnp.concatenate([x] * n, axis=1)

    for h in range(M_HEADS):
        sl = slice(h * M_HDP, (h + 1) * M_HDP)
        qb = q_ref[:, sl]
        kb = k_ref[:, sl]
        vext = jnp.concatenate([v_ref[:, sl], ones_tile], axis=1)
        i_b = gmb_ref[:, h * LANES:(h + 1) * LANES]
        bcum = gmb_ref[:, (M_HEADS + h) * LANES:(M_HEADS + h + 1) * LANES]
        g_row = gmt_ref[h:h + 1, :] - gmt_ref[M_HEADS + h:M_HEADS + h + 1, :]
        m_prev = m_scr[h, 0:1, :]

        inter = bcum + m_prev
        m_t = jnp.maximum(inter, bcum + jnp.max(jnp.where(causal, g_row, NEG), axis=1, keepdims=True))
        w_intra = jnp.exp(jnp.where(causal, g_row + lanes(bcum - m_t, tc // LANES), NEG))
        w_inter = jnp.exp(inter - m_t)
        s = _dot_nt(qb, kb) * w_intra
        c_mat = c_scr[h]
        num = _dot(s.astype(BF16), vext) + lanes(w_inter, 3) * _dot(qb, c_mat.astype(BF16))
        den = num[:, M_HDP:]
        rden = 1.0 / jnp.maximum(jnp.abs(den), jnp.exp(-m_t))
        num = num[:, :M_HDP]

        b_end = bcum[tc - 1:tc, :]
        to_end = b_end - bcum + i_b
        m_new = jnp.maximum(b_end + m_prev, jnp.max(to_end, axis=0, keepdims=True))
        wk = jnp.exp(to_end - m_new)
        w_prev = jnp.exp(b_end + m_prev - m_new)
        c_scr[h] = lanes(w_prev, 3) * c_mat + _dot_tn((lanes(wk, 2) * kb.astype(F32)).astype(BF16), vext)
        m_scr[h] = jnp.broadcast_to(m_new, (8, LANES))

        sc = slice(h * M_HD, (h + 1) * M_HD)
        mean = jnp.sum(num, axis=1, keepdims=True) * (1.0 / M_HD)
        xc = jnp.where(real, num - mean, 0.0)
        var = jnp.sum(xc * xc, axis=1, keepdims=True) * (1.0 / M_HD)
        scale = rden * lax.rsqrt(rden * rden * var + EPS)
        ha = (xc * lanes(scale, 2))[:, :M_HD]
        out_ref[:, sc] = (ha * nw_ref[:, sc] * oz_ref[:, sc].astype(F32)).astype(BF16)


def _gla_body(safe, q_ref, k_ref, v_ref, z_ref, b_ref, nw_ref,
              out_ref, kbuf, bbuf, attn_scr, st_scr, *, t, cs):
    row = lax.broadcasted_iota(jnp.int32, (t, t), 0)
    col = lax.broadcasted_iota(jnp.int32, (t, t), 1)
    causal = col <= row

    def operands(h):
        sl = slice(h * G_DKP, (h + 1) * G_DKP)
        return q_ref[:, sl].astype(F32), k_ref[:, sl].astype(F32), b_ref[:, sl]

    @pl.when(safe)
    def _():
        for h in range(G_HEADS):
            qs, kk, b = operands(h)
            a = _dot_nt((qs * jnp.exp(b)).astype(BF16), (kk * jnp.exp(-b)).astype(BF16))
            attn_scr[h] = jnp.where(causal, a, 0.0)

    @pl.when(jnp.logical_not(safe))
    def _():
        rowv = lax.broadcasted_iota(jnp.int32, (t, G_DKP), 0)
        sub = rowv % cs
        lag = row - col
        kbuf[0:cs, :] = jnp.zeros((cs, G_DKP), F32)
        bbuf[0:cs, :] = jnp.zeros((cs, G_DKP), F32)
        for h in range(G_HEADS):
            qs, kk, b = operands(h)
            qparts, kparts = [], []
            for i in range(1, t // cs):
                r = b[i * cs - 1:i * cs, :]
                inblk = (rowv >= i * cs) & (rowv < (i + 1) * cs)
                qparts.append((qs * jnp.exp(jnp.where(inblk, b - r, NEG))).astype(BF16))
                kparts.append((kk * jnp.exp(jnp.where(rowv < i * cs, r - b, NEG))).astype(BF16))
            attn = _dot_nt(jnp.concatenate(qparts, axis=1), jnp.concatenate(kparts, axis=1))
            kbuf[cs:cs + t, :] = kk
            bbuf[cs:cs + t, :] = b
            for j in range(cs):
                ksh = kbuf[cs - j:cs - j + t, :]
                bsh = bbuf[cs - j:cs - j + t, :]
                e = jnp.exp(jnp.where(sub >= j, b - bsh, NEG))
                dj = jnp.sum(qs * ksh * e, axis=1, keepdims=True)
                attn = attn + jnp.where(lag == j, dj, 0.0)
            attn_scr[h] = attn

    for h in range(G_HEADS):
        slv = slice(h * G_DVP, (h + 1) * G_DVP)
        qs, kk, b = operands(h)
        v = v_ref[:, slv]
        blast = b[t - 1:t, :]
        st = st_scr[h]
        o = (_dot_nt((qs * jnp.exp(b)).astype(BF16), st.astype(BF16))
             + _dot(attn_scr[h].astype(BF16), v))
        kdec = (kk * jnp.exp(blast - b)).astype(BF16)
        st_scr[h] = jnp.exp(blast) * st + _dot_tn(v, kdec)

        sc = slice(h * G_DV, (h + 1) * G_DV)
        var = jnp.sum(o * o, axis=1, keepdims=True) * (1.0 / G_DV)
        hb = (o * lax.rsqrt(var + EPS))[:, :G_DV] * nw_ref[:, sc]
        out_ref[:, sc] = (hb * z_ref[:, sc].astype(F32)).astype(BF16)


def _mixers_kernel(safe_ref, mq_ref, mk_ref, mv_ref, moz_ref, gmb_ref, gmt_ref, mnw_ref,
                   gq_ref, gk_ref, gv_ref, gz_ref, gb_ref, gnw_ref, ya_ref, yb_ref,
                   c_scr, m_scr, kbuf, bbuf, attn_scr, st_scr, *, t, cs):
    @pl.when(pl.program_id(1) == 0)
    def _():
        c_scr[...] = jnp.zeros_like(c_scr)
        m_scr[...] = jnp.zeros_like(m_scr)
        st_scr[...] = jnp.zeros_like(st_scr)

    safe = safe_ref[pl.program_id(0) * pl.num_programs(1) + pl.program_id(1)] != 0
    _gla_body(safe, gq_ref, gk_ref, gv_ref, gz_ref, gb_ref, gnw_ref, yb_ref, kbuf, bbuf, attn_scr, st_scr,
              t=t, cs=cs)
    _mlstm_body(mq_ref, mk_ref, mv_ref, moz_ref, gmb_ref, gmt_ref, mnw_ref, ya_ref, c_scr, m_scr, tc=t)


def _mixers(mq, mk, mv, moz, gmb, gmt, gq, gk, gv, gz, gcum, l, m_nw, g_nw, batch, seq, t, cs):
    n = batch * seq
    nt = seq // t

    def blk(width):
        return pl.BlockSpec((t, width), lambda b, c: (b * nt + c, 0))

    safe = (jnp.min(gcum.reshape(n // t, t, G_KWP)[:, t - 1, :], axis=-1) >= G_SAFE_LOG_DECAY).astype(jnp.int32)
    return pl.pallas_call(
        functools.partial(_mixers_kernel, t=t, cs=cs),
        grid=(batch, nt),
        in_specs=[pl.BlockSpec(memory_space=pltpu.SMEM),
                  blk(M_WP), blk(M_WP), blk(M_WP), blk(M_W), blk(2 * M_HEADS * LANES),
                  pl.BlockSpec((LANES, t), lambda b, c: (0, b * nt + c)), _of_layer(l, (1, M_W)),
                  blk(G_KWP), blk(G_KWP), blk(G_VWP), blk(G_VW), blk(G_KWP), _of_layer(l, (1, G_VW))],
        out_specs=[blk(M_W), blk(G_VW)],
        out_shape=[jax.ShapeDtypeStruct((n, M_W), BF16), jax.ShapeDtypeStruct((n, G_VW), BF16)],
        scratch_shapes=[pltpu.VMEM((M_HEADS, M_HDP, M_HDP + LANES), F32),
                        pltpu.VMEM((M_HEADS, 8, LANES), F32),
                        pltpu.VMEM((t + cs, G_DKP), F32), pltpu.VMEM((t + cs, G_DKP), F32),
                        pltpu.VMEM((G_HEADS, t, t), F32),
                        pltpu.VMEM((G_HEADS, G_DVP, G_DKP), F32)],
        compiler_params=_params("arbitrary", "arbitrary"),
        name="mixers",
    )(safe, mq, mk, mv, moz, gmb, gmt, m_nw, gq, gk, gv, gz, gcum, g_nw)


def _gelu_tanh(x):
    return 0.5 * x * (1.0 + jnp.tanh(math.sqrt(2.0 / math.pi) * (x + 0.044715 * (x * x * x))))


def _s5_kernel(u_ref, kms_ref, mo_ref, pin_ref, pt_ref, d_ref, y_ref, carry, xs, *, nch):
    @pl.when(pl.program_id(2) == 0)
    def _():
        carry[...] = jnp.zeros_like(carry)

    qw = S_Q * LANES
    ngrp = nch // S_SUB
    ltiles = S_SW // LANES
    ucat = jnp.concatenate([u_ref[0, pl.ds(s, nch, stride=S_Q), :] for s in range(S_Q)],
                           axis=1).astype(BF16)
    r = _dot(ucat, kms_ref[0])
    y = r[:, :qw]
    xr = r[:, qw:qw + S_SW]
    xi = r[:, qw + S_SW:]

    def axpy(xr, xi, pr, pi, sr, si):
        return xr + (pr * sr - pi * si), xi + (pr * si + pi * sr)

    def scan(xr, xi, p_ref, pos, length):
        d = 1
        while d < length:
            keep = pos >= d
            sr = jnp.where(keep, pltpu.roll(xr, d, 0), 0.0)
            si = jnp.where(keep, pltpu.roll(xi, d, 0), 0.0)
            xr, xi = axpy(xr, xi, p_ref[0, d - 1:d, :S_SW], p_ref[0, d - 1:d, S_SW:], sr, si)
            d *= 2
        return xr, xi

    def slabs(part, arr):
        return [(part * ltiles + j, arr[:, j * LANES:(j + 1) * LANES]) for j in range(ltiles)]

    rowi = lax.broadcasted_iota(jnp.int32, (nch, S_SW), 0)
    xr, xi = scan(xr, xi, pin_ref, rowi % S_SUB, S_SUB)
    for part, arr in ((0, xr), (1, xi)):
        for j, piece in slabs(part, arr):
            xs[j] = piece

    def group_ends(part):
        return jnp.concatenate([xs[part * ltiles + j, pl.ds(S_SUB - 1, ngrp, stride=S_SUB), :]
                                for j in range(ltiles)], axis=1)

    rowg = lax.broadcasted_iota(jnp.int32, (ngrp, S_SW), 0)
    er, ei = scan(group_ends(0), group_ends(1), pt_ref, rowg, ngrp)
    cr = carry[0:1, :S_SW]
    ci = carry[0:1, S_SW:]
    er, ei = axpy(er, ei, pt_ref[0, :, :S_SW], pt_ref[0, :, S_SW:], cr, ci)
    first = rowg == 0
    gr = jnp.where(first, cr, pltpu.roll(er, 1, 0))
    gi = jnp.where(first, ci, pltpu.roll(ei, 1, 0))
    carry[0:1, :S_SW] = er[ngrp - 1:ngrp, :]
    carry[0:1, S_SW:] = ei[ngrp - 1:ngrp, :]
    for part, arr in ((0, gr), (1, gi)):
        for j, piece in slabs(part, arr):
            for k in range(S_SUB):
                xs[j, pl.ds(k, ngrp, stride=S_SUB), :] = piece
    sr = jnp.concatenate([xs[j] for j in range(ltiles)], axis=1)
    si = jnp.concatenate([xs[ltiles + j] for j in range(ltiles)], axis=1)
    pwr = jnp.concatenate([pin_ref[0, :, :S_SW]] * ngrp, axis=0)
    pwi = jnp.concatenate([pin_ref[0, :, S_SW:]] * ngrp, axis=0)
    xr, xi = axpy(xr, xi, pwr, pwi, sr, si)

    firstc = rowi == 0
    xpr = jnp.where(firstc, cr, pltpu.roll(xr, 1, 0))
    xpi = jnp.where(firstc, ci, pltpu.roll(xi, 1, 0))
    y = y + _dot(jnp.concatenate([xpr, xpi], axis=1).astype(BF16), mo_ref[0])
    for t in range(S_Q):
        y_ref[0, pl.ds(t, nch, stride=S_Q), :] = y[:, t * LANES:(t + 1) * LANES]
    y_ref[0] = _gelu_tanh(y_ref[0] + d_ref[0] * u_ref[0])


def _s5(cu, l, kms, mo, pin, pt, dskip, batch, seq, tm):
    n = batch * seq
    nt = seq // tm
    nch = tm // S_Q
    qw = S_Q * LANES

    def table(rows, cols):
        return pl.BlockSpec((None, 1, rows, cols), lambda j, b, t: (l, j, 0, 0))

    return pl.pallas_call(
        functools.partial(_s5_kernel, nch=nch),
        grid=(S_BLOCKS, batch, nt),
        in_specs=[pl.BlockSpec((1, tm, LANES), lambda j, b, t: (j, b * nt + t, 0)),
                  table(qw, qw + 2 * S_SW), table(2 * S_SW, qw), table(S_SUB, 2 * S_SW),
                  table(nch // S_SUB, 2 * S_SW), table(1, LANES)],
        out_specs=pl.BlockSpec((1, tm, LANES), lambda j, b, t: (j, b * nt + t, 0)),
        out_shape=jax.ShapeDtypeStruct((S_BLOCKS, n, LANES), F32),
        scratch_shapes=[pltpu.VMEM((8, 2 * S_SW), F32),
                        pltpu.VMEM((2 * S_SW // LANES, nch, LANES), F32)],
        compiler_params=_params("arbitrary", "arbitrary", "arbitrary"),
        name="s5",
    )(cu, kms, mo, pin, pt, dskip)


def _s5_expand_kernel(msr_ref, msi_ref, mor_ref, moi_ref, kms_ref, mo_ref):
    q = S_Q
    qw = q * LANES

    def embed(m, rows, width):
        shape = (width, S_GPB * width)
        tile = (lax.broadcasted_iota(jnp.int32, shape, 1) % width
                == lax.broadcasted_iota(jnp.int32, shape, 0)).astype(BF16)
        oshape = (S_GPB * rows, S_GPB * width)
        diag = (lax.broadcasted_iota(jnp.int32, oshape, 0) // rows
                == lax.broadcasted_iota(jnp.int32, oshape, 1) // width)
        return jnp.where(diag, _dot(m, tile), 0.0).astype(BF16)

    ms = [(embed(msr_ref[0, 0, s], S_GROUP, S_STATE), embed(msi_ref[0, 0, s], S_GROUP, S_STATE))
          for s in range(q)]
    mo = [(embed(mor_ref[0, 0, tau], S_STATE, S_GROUP), embed(moi_ref[0, 0, tau], S_STATE, S_GROUP))
          for tau in range(q + 1)]
    bbr, bbi = ms[q - 1]
    kd = [(_dot(bbr, mo[tau][0]) + _dot(bbi, mo[tau][1])).astype(BF16) for tau in range(q)]
    zero = jnp.zeros((LANES, LANES), BF16)
    for s in range(q):
        rows = slice(s * LANES, (s + 1) * LANES)
        for t in range(q):
            kms_ref[0, 0, rows, t * LANES:(t + 1) * LANES] = kd[t - s] if t >= s else zero
        kms_ref[0, 0, rows, qw:qw + S_SW] = ms[s][0]
        kms_ref[0, 0, rows, qw + S_SW:] = ms[s][1]
    for t in range(q):
        cols = slice(t * LANES, (t + 1) * LANES)
        mo_ref[0, 0, 0:S_SW, cols] = mo[t + 1][0]
        mo_ref[0, 0, S_SW:, cols] = mo[t + 1][1]


def _s5_expand(msr, msi, mor, moi):
    depth = msr.shape[0]
    qw = S_Q * LANES

    def blk(a):
        return pl.BlockSpec((1, 1) + a.shape[2:], lambda d, j: (d, j, 0, 0, 0))

    return pl.pallas_call(
        _s5_expand_kernel, grid=(depth, S_BLOCKS),
        in_specs=[blk(a) for a in (msr, msi, mor, moi)],
        out_specs=[pl.BlockSpec((1, 1, qw, qw + 2 * S_SW), lambda d, j: (d, j, 0, 0)),
                   pl.BlockSpec((1, 1, 2 * S_SW, qw), lambda d, j: (d, j, 0, 0))],
        out_shape=[jax.ShapeDtypeStruct((depth, S_BLOCKS, qw, qw + 2 * S_SW), BF16),
                   jax.ShapeDtypeStruct((depth, S_BLOCKS, 2 * S_SW, qw), BF16)],
        compiler_params=_params("arbitrary", "arbitrary"), name="s5_expand",
    )(msr, msi, mor, moi)


def _s5_tables(lam_re, lam_im, log_dt, b_re, b_im, c_re, c_im, nch):
    q = S_Q
    dep = lam_re.shape[0]
    lr = jnp.minimum(lam_re.astype(F32), -1e-4)
    li = lam_im.astype(F32)
    dt = jnp.exp(log_dt.astype(F32))[..., None]
    mag = jnp.exp(lr * dt)
    ab_re = mag * jnp.cos(li * dt)
    ab_im = mag * jnp.sin(li * dt)
    nr = ab_re - 1.0
    den = lr * lr + li * li
    coef_re = ((nr * lr + ab_im * li) / den)[:, :, None, :]
    coef_im = ((ab_im * lr - nr * li) / den)[:, :, None, :]
    brt = jnp.swapaxes(b_re.astype(F32), -1, -2)
    bit = jnp.swapaxes(b_im.astype(F32), -1, -2)
    bb_re = coef_re * brt - coef_im * bit
    bb_im = coef_re * bit + coef_im * brt

    def apow(nvals):
        e = jnp.asarray(nvals, F32)[:, None, None, None]
        m = jnp.exp(e * (lr * dt))
        ang = e * (li * dt)
        return m * jnp.cos(ang), m * jnp.sin(ang)

    def per_block(m, rows):
        m = m.reshape(m.shape[0], dep, S_BLOCKS, S_GPB * rows, m.shape[-1])
        return m.transpose(1, 2, 0, 3, 4).astype(BF16)

    ar, ai = apow(list(range(q + 1)))
    crt = jnp.swapaxes(c_re.astype(F32), -1, -2)
    cit = jnp.swapaxes(c_im.astype(F32), -1, -2)
    ca_re = crt[None] * ar[..., None] - cit[None] * ai[..., None]
    ca_im = crt[None] * ai[..., None] + cit[None] * ar[..., None]
    pw_r = ar[:q][::-1][:, :, :, None, :]
    pw_i = ai[:q][::-1][:, :, :, None, :]
    kms, mo = _s5_expand(per_block(pw_r * bb_re - pw_i * bb_im, S_GROUP),
                         per_block(pw_r * bb_im + pw_i * bb_re, S_GROUP),
                         per_block(ca_re, S_STATE), per_block(-ca_im, S_STATE))
    def power_table(nvals):
        pr, pi = apow(nvals)
        pr = pr.reshape(len(nvals), dep, S_BLOCKS, S_SW).transpose(1, 2, 0, 3)
        pi = pi.reshape(len(nvals), dep, S_BLOCKS, S_SW).transpose(1, 2, 0, 3)
        return jnp.concatenate([pr, pi], axis=3)

    pin = power_table([q * (r + 1) for r in range(S_SUB)])
    pt = power_table([q * S_SUB * (k + 1) for k in range(nch // S_SUB)])
    return kms, mo, pin, pt


def _merge_kernel(ya_ref, yb_ref, yc_ref, cz_ref, g_ref, x_ref, wa_ref, wb_ref, wglu_ref, wc_ref,
                  wo_ref, nw_ref, *out_refs, final):
    yc0 = jnp.concatenate([yc_ref[j] for j in range(S_BLOCKS)], axis=1)
    glu = _dot(yc0.astype(BF16), wglu_ref[...])
    yc = yc0 * _sigmoid(glu) * cz_ref[...].astype(F32)
    merged = (g_ref[:, 0:D_MODEL].astype(F32) * _dot(ya_ref[...], wa_ref[...])
              + g_ref[:, D_MODEL:2 * D_MODEL].astype(F32) * _dot(yb_ref[...], wb_ref[...])
              + g_ref[:, 2 * D_MODEL:].astype(F32) * _dot(yc.astype(BF16), wc_ref[...]))
    out = x_ref[...] + _dot(merged.astype(BF16), wo_ref[...])
    ms = jnp.mean(out * out, axis=-1, keepdims=True)
    normed = out * lax.rsqrt(ms + EPS) * nw_ref[...]
    if final:
        out_refs[0][...] = normed
    else:
        out_refs[0][...] = out
        out_refs[1][...] = normed.astype(BF16)


def _merge(ya, yb, yc, cz, g, x2, l, wa, wb, wglu, wc, wo, nw, tm, final):
    n = x2.shape[0]
    x_out = jax.ShapeDtypeStruct((n, D_MODEL), F32)
    hn_out = jax.ShapeDtypeStruct((n, D_MODEL), BF16)
    return pl.pallas_call(
        functools.partial(_merge_kernel, final=final),
        grid=(n // tm,),
        in_specs=[_rows(tm, M_W), _rows(tm, G_VW),
                  pl.BlockSpec((S_BLOCKS, tm, LANES), lambda i: (0, i, 0)),
                  _rows(tm, S_W), _rows(tm, 3 * D_MODEL), _rows(tm, D_MODEL),
                  _of_layer(l, (M_W, D_MODEL)), _of_layer(l, (G_VW, D_MODEL)), _of_layer(l, (S_W, S_W)),
                  _of_layer(l, (S_W, D_MODEL)), _of_layer(l, (D_MODEL, D_MODEL)), _whole((1, D_MODEL))],
        out_specs=[_rows(tm, D_MODEL)] if final else [_rows(tm, D_MODEL), _rows(tm, D_MODEL)],
        out_shape=[x_out] if final else [x_out, hn_out],
        compiler_params=_params("arbitrary"),
        name="merge",
    )(ya, yb, yc, cz, g, x2, wa, wb, wglu, wc, wo, nw)


def _pad_heads(w, heads, d, dp):
    zeros = jnp.zeros(w.shape[:-1] + (dp - d,), w.dtype)
    parts = []
    for h in range(heads):
        parts += [w[..., h * d:(h + 1) * d], zeros]
    return jnp.concatenate(parts, axis=-1)


def _prepare(p, nch):
    depth = p["w_in"].shape[0]
    w_mlstm, w_plain, w_gate, w_small = _wprep(jnp.swapaxes(p["w_in"], 1, 2), LANES)
    gkp = functools.partial(_pad_heads, heads=G_HEADS, d=G_DK, dp=G_DKP)
    kms, mo, pin, pt = _s5_tables(p["s5_lam_re"], p["s5_lam_im"], p["s5_log_dt"], p["s5_B_re"], p["s5_B_im"],
                               p["s5_C_re"], p["s5_C_im"], nch)
    gate_b = p["mlstm_gate_b"]
    return dict(
        w_mlstm=w_mlstm, w_plain=w_plain, w_gate=w_gate, w_small=w_small,
        cq=p["mlstm_conv"][:, :, :M_W], ck=p["mlstm_conv"][:, :, M_W:],
        gb=jnp.concatenate([gate_b[:, 0], gate_b[:, 1],
                            jnp.zeros((depth, LANES - 2 * M_HEADS), F32)], axis=1)[:, None, :],
        m_nw=p["mlstm_norm"][:, None, :],
        wal=jnp.concatenate([jnp.zeros((depth, 2 * M_HEADS, G_KWP), F32), gkp(p["gla_w_alpha"]),
                             jnp.zeros((depth, LANES - 2 * M_HEADS - G_RANK, G_KWP), F32)],
                            axis=1).astype(BF16),
        bal=gkp(p["gla_b_alpha"])[:, None, :],
        g_nw=p["gla_norm"][:, None, :],
        kms=kms, mo=mo, pin=pin, pt=pt,
        dskip=p["s5_D"].reshape(depth, S_BLOCKS, 1, LANES),
        wa=p["w_branch_mlstm"].astype(BF16), wb=p["w_branch_gla"].astype(BF16),
        wglu=p["s5_w_glu"].astype(BF16), wc=p["w_branch_s5"].astype(BF16), wo=p["w_out"].astype(BF16),
    )


def _tiles(batch, seq):
    return dict(proj=min(1024, seq), chunk=min(256, seq), g_sub=16,
                s5=min(4096, seq), merge=min(512, batch * seq))


def _layer(x2, hn, batch, seq, l, w, next_norm, final, tl):
    tm = tl["proj"]
    q, k, oz = _proj_mlstm(hn, l, w["w_mlstm"], w["cq"], w["ck"], seq, tm)
    gz, czs, gs = _proj_gate(hn, l, w["w_gate"], tm)
    mv, gv, gq, gk, cu_s, gm, gmt, gcum = _proj_va(hn, l, w["w_plain"], w["w_small"], w["gb"], w["wal"],
                                                   w["bal"], tm, tl["chunk"])
    ya, yb = _mixers(q, k, mv, oz, gm, gmt, gq, gk, gv, gz, gcum, l, w["m_nw"], w["g_nw"], batch, seq,
                     tl["chunk"], tl["g_sub"])
    yc = _s5(cu_s, l, w["kms"], w["mo"], w["pin"], w["pt"], w["dskip"], batch, seq, tl["s5"])
    return _merge(ya, yb, yc, czs, gs, x2, l, w["wa"], w["wb"], w["wglu"], w["wc"], w["wo"],
                  next_norm.reshape(1, D_MODEL), tl["merge"], final)


def kernel(x, norm_w, w_in, mlstm_conv, mlstm_gate_b, mlstm_norm, gla_w_alpha, gla_b_alpha, gla_norm,
           s5_lam_re, s5_lam_im, s5_log_dt, s5_B_re, s5_B_im, s5_C_re, s5_C_im, s5_D, s5_w_glu,
           w_branch_mlstm, w_branch_gla, w_branch_s5, w_out, final_norm):
    batch, seq, _ = x.shape
    depth = norm_w.shape[0]
    tl = _tiles(batch, seq)
    prep = _prepare(dict(w_in=w_in, mlstm_conv=mlstm_conv, mlstm_gate_b=mlstm_gate_b, mlstm_norm=mlstm_norm,
                         gla_w_alpha=gla_w_alpha, gla_b_alpha=gla_b_alpha, gla_norm=gla_norm,
                         s5_lam_re=s5_lam_re, s5_lam_im=s5_lam_im, s5_log_dt=s5_log_dt, s5_B_re=s5_B_re,
                         s5_B_im=s5_B_im, s5_C_re=s5_C_re, s5_C_im=s5_C_im, s5_D=s5_D, s5_w_glu=s5_w_glu,
                         w_branch_mlstm=w_branch_mlstm, w_branch_gla=w_branch_gla,
                         w_branch_s5=w_branch_s5, w_out=w_out), tl["s5"] // S_Q)
    x2 = x.reshape(batch * seq, D_MODEL)
    hn = _rms(x2, norm_w[0].reshape(1, D_MODEL), min(1024, batch * seq))
    for l in range(depth):
        final = l == depth - 1
        res = _layer(x2, hn, batch, seq, l, prep, final_norm if final else norm_w[l + 1], final, tl)
        if final:
            x2 = res[0]
        else:
            x2, hn = res
    return x2.reshape(batch, seq, D_MODEL)
```

```python
import functools
import math

import jax
import jax.numpy as jnp
from jax import lax
from jax.experimental import pallas as pl
from jax.experimental.pallas import tpu as pltpu

F32 = jnp.float32
BF16 = jnp.bfloat16

EPS = 1e-6
D_MODEL = 1024
LANES = 128
SUBLANES = 8
V7X_VMEM_BYTES = 64 * 1024 * 1024
M_HEADS = 4
M_HD = 192
M_HDP = 256
M_W = M_HEADS * M_HD
M_WP = M_HEADS * M_HDP
M_CONV = 4
G_HEADS = 4
G_DK = 96
G_DKP = 128
G_DV = 192
G_DVP = 256
G_KW = G_HEADS * G_DK
G_KWP = G_HEADS * G_DKP
G_VW = G_HEADS * G_DV
G_VWP = G_HEADS * G_DVP
G_RANK = 16
G_TAU = 16.0
S_W = 512
S_GROUP = 16
S_STATE = 64
S_BLOCKS = S_W // LANES
S_GPB = LANES // S_GROUP
S_SW = S_GPB * S_STATE
S_Q = 8
S_SUB = 8

IN_SIZES = (M_W, M_W, M_W, M_W, M_HEADS, M_HEADS, M_W,
            G_KW, G_KW, G_VW, G_RANK, G_VW,
            S_W, S_W, 3 * D_MODEL)
SMALL_W = S_W + LANES
CW = 512

NEG = -1e30
G_SAFE_LOG_DECAY = -60.0
VMEM_LIMIT = V7X_VMEM_BYTES * 7 // 8


def _sigmoid(x):
    return 0.5 * jnp.tanh(0.5 * x) + 0.5


def _silu(x):
    h = 0.5 * x
    return h * jnp.tanh(h) + h


def _log_sigmoid(x):
    return jnp.minimum(x, 0.0) - jnp.log1p(jnp.exp(-jnp.abs(x)))


def _dot(a, b):
    return jnp.dot(a, b, preferred_element_type=F32)


def _dot_nt(a, b):
    return lax.dot_general(a, b, (((1,), (1,)), ((), ())), preferred_element_type=F32)


def _dot_tn(a, b):
    return lax.dot_general(a, b, (((0,), (0,)), ((), ())), preferred_element_type=F32)


def _params(*sem):
    return pltpu.CompilerParams(dimension_semantics=sem, vmem_limit_bytes=VMEM_LIMIT)


def _rows(tm, width):
    return pl.BlockSpec((tm, width), lambda i: (i, 0))


def _whole(shape):
    return pl.BlockSpec(shape, lambda i: (0,) * len(shape))


def _of_layer(l, shape):
    return pl.BlockSpec((None,) + shape, lambda *_: (l,) + (0,) * len(shape))


def _rms_kernel(x_ref, nw_ref, o_ref):
    x = x_ref[...]
    ms = jnp.mean(x * x, axis=-1, keepdims=True)
    o_ref[...] = (x * lax.rsqrt(ms + EPS) * nw_ref[...]).astype(BF16)


def _rms(x2, nw, tm):
    n = x2.shape[0]
    return pl.pallas_call(
        _rms_kernel, grid=(n // tm,),
        in_specs=[_rows(tm, D_MODEL), _whole((1, D_MODEL))],
        out_specs=_rows(tm, D_MODEL),
        out_shape=jax.ShapeDtypeStruct((n, D_MODEL), BF16),
        compiler_params=_params("arbitrary"), name="rms",
    )(x2, nw)


IN_NAMES = ("aq", "ak", "av", "ao", "ai", "af", "az", "bq", "bk", "bv", "ba", "bz", "cu", "cz", "g")
W_OFF = {name: sum(IN_SIZES[:i]) for i, name in enumerate(IN_NAMES)}
IN_WIDTH = sum(IN_SIZES)
W_MLSTM, W_PLAIN, W_GATE = 4 * M_W, M_W + G_VW + 2 * G_KW, G_VW + S_W + 3 * D_MODEL


def _wprep_kernel(w_ref, wm_ref, wp_ref, wg_ref, ws_ref):
    kb = w_ref.shape[2]

    def put(dst, dst_off, parts, width):
        xs = [w_ref[0, off:off + rows, :] for off, rows in parts]
        used = sum(rows for _, rows in parts)
        if used < width:
            xs.append(jnp.zeros((width - used, kb), F32))
        x = xs[0] if len(xs) == 1 else jnp.concatenate(xs, axis=0)
        dst[0, :, dst_off:dst_off + width] = x.T.astype(BF16)

    def group(dst, names):
        off = 0
        for name in names:
            width = IN_SIZES[IN_NAMES.index(name)]
            for c in range(0, width, LANES):
                put(dst, off + c, [(W_OFF[name] + c, LANES)], LANES)
            off += width

    group(wm_ref, ("aq", "ak", "ao", "az"))
    group(wp_ref, ("av", "bv", "bq", "bk"))
    group(wg_ref, ("bz", "cz", "g"))
    group(ws_ref, ("cu",))
    put(ws_ref, S_W, [(W_OFF["ai"], 2 * M_HEADS), (W_OFF["ba"], G_RANK)], LANES)


def _wprep(w_t, kb):
    depth = w_t.shape[0]
    widths = (W_MLSTM, W_PLAIN, W_GATE, SMALL_W)
    return pl.pallas_call(
        _wprep_kernel, grid=(depth, D_MODEL // kb),
        in_specs=[pl.BlockSpec((1, IN_WIDTH, kb), lambda l, i: (l, 0, i))],
        out_specs=[pl.BlockSpec((1, kb, wd), lambda l, i: (l, i, 0)) for wd in widths],
        out_shape=[jax.ShapeDtypeStruct((depth, D_MODEL, wd), BF16) for wd in widths],
        compiler_params=_params("arbitrary", "arbitrary"), name="wprep",
    )(w_t)


def _scatter_heads(dst, y, heads, d, dp, fill):
    rows = y.shape[0]
    for h in range(heads):
        dst[:, h * dp:h * dp + d] = y[:, h * d:(h + 1) * d].astype(BF16)
        dst[:, h * dp + d:(h + 1) * dp] = jnp.full((rows, dp - d), fill, BF16)


def _proj_mlstm_kernel(hn_ref, w_ref, cq_ref, ck_ref, q_ref, k_ref, oz_ref, halo_x, halo_b, *, tm,
                       tiles_per_seq):
    @pl.when(pl.program_id(0) % tiles_per_seq == 0)
    def _():
        halo_x[...] = jnp.zeros_like(halo_x)
        halo_b[...] = jnp.zeros_like(halo_b)

    hn = hn_ref[...]
    row8 = lax.broadcasted_iota(jnp.int32, (SUBLANES, M_W), 0)

    def shift(x, prev8, d):
        rolled = pltpu.roll(x, d, 0)
        head = jnp.where(row8 < d, pltpu.roll(prev8, d, 0), rolled[0:SUBLANES])
        return jnp.concatenate([head, rolled[SUBLANES:]], axis=0)

    def conv_act(c, x, cw_ref, dst, scale):
        x1 = shift(x, halo_x[c], 1)
        pair = cw_ref[1:2, :] * x + cw_ref[0:1, :] * x1
        y = _silu(cw_ref[3:4, :] * x + cw_ref[2:3, :] * x1 + shift(pair, halo_b[c], 2))
        halo_x[c] = x[tm - SUBLANES:tm]
        halo_b[c] = pair[tm - SUBLANES:tm]
        _scatter_heads(dst, y if scale == 1.0 else y * scale, M_HEADS, M_HD, M_HDP, 0.0)

    xq = _dot(hn, w_ref[:, 0:M_W])
    xk = _dot(hn, w_ref[:, M_W:2 * M_W])
    conv_act(0, xq, cq_ref, q_ref, 1.0)
    o = _dot(hn, w_ref[:, 2 * M_W:3 * M_W])
    conv_act(1, xk, ck_ref, k_ref, M_HD ** -0.5)
    z = _dot(hn, w_ref[:, 3 * M_W:])
    oz_ref[...] = (_sigmoid(o) * _silu(z)).astype(BF16)


def _proj_mlstm(hn, l, w, cq, ck, seq, tm):
    n = hn.shape[0]
    padded = jax.ShapeDtypeStruct((n, M_WP), BF16)
    return pl.pallas_call(
        functools.partial(_proj_mlstm_kernel, tm=tm, tiles_per_seq=seq // tm),
        grid=(n // tm,),
        in_specs=[_rows(tm, D_MODEL), _of_layer(l, (D_MODEL, W_MLSTM)), _of_layer(l, (M_CONV, M_W)),
                  _of_layer(l, (M_CONV, M_W))],
        out_specs=[_rows(tm, M_WP), _rows(tm, M_WP), _rows(tm, M_W)],
        out_shape=[padded, padded, jax.ShapeDtypeStruct((n, M_W), BF16)],
        scratch_shapes=[pltpu.VMEM((2, SUBLANES, M_W), F32)] * 2,
        compiler_params=_params("arbitrary"), name="proj_mlstm",
    )(hn, w, cq, ck)


def _proj_gate_kernel(hn_ref, w_ref, gz_ref, cz_ref, g_ref):
    hn = hn_ref[...]
    gz_ref[...] = _silu(_dot(hn, w_ref[:, 0:G_VW])).astype(BF16)
    cz_ref[...] = _silu(_dot(hn, w_ref[:, G_VW:G_VW + S_W])).astype(BF16)
    off = G_VW + S_W
    for c in range(3 * D_MODEL // CW):
        r = _dot(hn, w_ref[:, off + c * CW:off + (c + 1) * CW])
        g_ref[:, c * CW:(c + 1) * CW] = _sigmoid(r).astype(BF16)


def _proj_gate(hn, l, w, tm):
    n = hn.shape[0]
    widths = (G_VW, S_W, 3 * D_MODEL)
    return pl.pallas_call(
        _proj_gate_kernel, grid=(n // tm,),
        in_specs=[_rows(tm, D_MODEL), _of_layer(l, (D_MODEL, W_GATE))],
        out_specs=[_rows(tm, wd) for wd in widths],
        out_shape=[jax.ShapeDtypeStruct((n, wd), BF16) for wd in widths],
        compiler_params=_params("arbitrary"), name="proj_gate",
    )(hn, w)


def _split3(x):
    hi = x.astype(BF16)
    r1 = x - hi.astype(F32)
    mid = r1.astype(BF16)
    return hi, mid, (r1 - mid.astype(F32)).astype(BF16)


def _segment_cumsum(x, seg):
    rows, width = x.shape
    tril = (lax.broadcasted_iota(jnp.int32, (seg, seg), 1)
            <= lax.broadcasted_iota(jnp.int32, (seg, seg), 0)).astype(BF16)
    parts = jnp.concatenate(_split3(x), axis=1)
    out = []
    for s in range(rows // seg):
        acc = _dot(tril, parts[s * seg:(s + 1) * seg])
        out.append(acc[:, :width] + acc[:, width:2 * width] + acc[:, 2 * width:])
    return jnp.concatenate(out, axis=0)


def _proj_va_kernel(hn_ref, wp_ref, ws_ref, gb_ref, wal_ref, bal_ref,
                    mv_ref, gv_ref, gq_ref, gk_ref, cu_ref, gmb_ref, gmt_ref, gcum_ref, *, chunk):
    hn = hn_ref[...]
    r = _dot(hn, ws_ref[...])
    _scatter_heads(mv_ref, _dot(hn, wp_ref[:, 0:M_W]), M_HEADS, M_HD, M_HDP, 0.0)
    for j in range(S_BLOCKS):
        cu_ref[j] = r[:, j * LANES:(j + 1) * LANES]
    raw = r[:, S_W:]
    gates = raw + gb_ref[...]
    lane = lax.broadcasted_iota(jnp.int32, gates.shape, 1)
    fgate = (lane >= M_HEADS) & (lane < 2 * M_HEADS)
    za = _dot(raw.astype(BF16), wal_ref[...]) + bal_ref[...]
    gm = jnp.where(fgate, _segment_cumsum(_log_sigmoid(gates), chunk), gates)
    _scatter_heads(gv_ref, _dot(hn, wp_ref[:, M_W:M_W + G_VW]), G_HEADS, G_DV, G_DVP, 0.0)
    gmt_ref[...] = gm.T
    slots = 2 * M_HEADS * LANES
    spread = (lax.broadcasted_iota(jnp.int32, (3 * LANES, slots), 0) % LANES
              == lax.broadcasted_iota(jnp.int32, (3 * LANES, slots), 1) // LANES).astype(BF16)
    gmb_ref[...] = _dot(jnp.concatenate(_split3(gm), axis=1), spread)
    off = M_W + G_VW
    _scatter_heads(gq_ref, _dot(hn, wp_ref[:, off:off + G_KW]) * (G_DK ** -0.5), G_HEADS, G_DK, G_DKP, 0.0)
    gcum_ref[...] = _segment_cumsum(_log_sigmoid(za) * (1.0 / G_TAU), chunk)
    _scatter_heads(gk_ref, _dot(hn, wp_ref[:, off + G_KW:]), G_HEADS, G_DK, G_DKP, 0.0)


def _proj_va(hn, l, wp, ws, gb, wal, bal, tm, chunk):
    n = hn.shape[0]
    widths = (M_WP, G_VWP, G_KWP, G_KWP)
    return pl.pallas_call(
        functools.partial(_proj_va_kernel, chunk=chunk), grid=(n // tm,),
        in_specs=[_rows(tm, D_MODEL), _of_layer(l, (D_MODEL, W_PLAIN)), _of_layer(l, (D_MODEL, SMALL_W)),
                  _of_layer(l, (1, LANES)), _of_layer(l, (LANES, G_KWP)), _of_layer(l, (1, G_KWP))],
        out_specs=[_rows(tm, wd) for wd in widths] + [
            pl.BlockSpec((S_BLOCKS, tm, LANES), lambda i: (0, i, 0)), _rows(tm, 2 * M_HEADS * LANES),
            pl.BlockSpec((LANES, tm), lambda i: (0, i)), _rows(tm, G_KWP)],
        out_shape=[jax.ShapeDtypeStruct((n, wd), BF16) for wd in widths] + [
            jax.ShapeDtypeStruct((S_BLOCKS, n, LANES), F32),
            jax.ShapeDtypeStruct((n, 2 * M_HEADS * LANES), F32),
            jax.ShapeDtypeStruct((LANES, n), F32),
            jax.ShapeDtypeStruct((n, G_KWP), F32)],
        compiler_params=_params("arbitrary"), name="proj_va",
    )(hn, wp, ws, gb, wal, bal)


def _mlstm_body(q_ref, k_ref, v_ref, oz_ref, gmb_ref, gmt_ref, nw_ref, out_ref, c_scr, m_scr, *, tc):
    row = lax.broadcasted_iota(jnp.int32, (tc, tc), 0)
    col = lax.broadcasted_iota(jnp.int32, (tc, tc), 1)
    causal = col <= row
    real = lax.broadcasted_iota(jnp.int32, (tc, M_HDP), 1) < M_HD
    ones_tile = jnp.ones((tc, LANES), BF16)

    def lanes(x, n):
        return jnp.concatenate([x] * n, axis=1)

    def head(h):
        sl = slice(h * M_HDP, (h + 1) * M_HDP)
        qb = q_ref[:, sl]
        kb = k_ref[:, sl]
        vext = jnp.concatenate([v_ref[:, sl], ones_tile], axis=1)
        i_b = gmb_ref[:, h * LANES:(h + 1) * LANES]
        bcum = gmb_ref[:, (M_HEADS + h) * LANES:(M_HEADS + h + 1) * LANES]
        g_row = gmt_ref[h:h + 1, :] - gmt_ref[M_HEADS + h:M_HEADS + h + 1, :]
        m_prev = m_scr[h, 0:1, :]

        inter = bcum + m_prev
        m_t = jnp.maximum(inter, bcum + jnp.max(jnp.where(causal, g_row, NEG), axis=1, keepdims=True))
        w_intra = jnp.exp(jnp.where(causal, g_row + lanes(bcum - m_t, tc // LANES), NEG))
        w_inter = jnp.exp(inter - m_t)
        s = _dot_nt(qb, kb) * w_intra
        c_mat = c_scr[h]
        num = _dot(s.astype(BF16), vext) + lanes(w_inter, 3) * _dot(qb, c_mat.astype(BF16))
        den = num[:, M_HDP:]
        rden = 1.0 / jnp.maximum(jnp.abs(den), jnp.exp(-m_t))
        num = num[:, :M_HDP]

        b_end = bcum[tc - 1:tc, :]
        to_end = b_end - bcum + i_b
        m_new = jnp.maximum(b_end + m_prev, jnp.max(to_end, axis=0, keepdims=True))
        wk = jnp.exp(to_end - m_new)
        w_prev = jnp.exp(b_end + m_prev - m_new)
        c_scr[h] = lanes(w_prev, 3) * c_mat + _dot_tn((lanes(wk, 2) * kb.astype(F32)).astype(BF16), vext)
        m_scr[h] = jnp.broadcast_to(m_new, (SUBLANES, LANES))

        sc = slice(h * M_HD, (h + 1) * M_HD)
        mean = jnp.sum(num, axis=1, keepdims=True) * (1.0 / M_HD)
        xc = jnp.where(real, num - mean, 0.0)
        var = jnp.sum(xc * xc, axis=1, keepdims=True) * (1.0 / M_HD)
        scale = rden * lax.rsqrt(rden * rden * var + EPS)
        ha = (xc * lanes(scale, 2))[:, :M_HD]
        out_ref[:, sc] = (ha * nw_ref[:, sc] * oz_ref[:, sc].astype(F32)).astype(BF16)

    return [functools.partial(head, h) for h in range(M_HEADS)]


def _gla_body(safe, q_ref, k_ref, v_ref, z_ref, b_ref, nw_ref,
              out_ref, kbuf, bbuf, attn_scr, st_scr, *, t, cs):
    row = lax.broadcasted_iota(jnp.int32, (t, t), 0)
    col = lax.broadcasted_iota(jnp.int32, (t, t), 1)
    causal = col <= row

    def operands(h):
        sl = slice(h * G_DKP, (h + 1) * G_DKP)
        return q_ref[:, sl].astype(F32), k_ref[:, sl].astype(F32), b_ref[:, sl]

    @pl.when(safe)
    def _():
        for h in range(G_HEADS):
            qs, kk, b = operands(h)
            a = _dot_nt((qs * jnp.exp(b)).astype(BF16), (kk * jnp.exp(-b)).astype(BF16))
            attn_scr[h] = jnp.where(causal, a, 0.0)

    @pl.when(jnp.logical_not(safe))
    def _():
        rowv = lax.broadcasted_iota(jnp.int32, (t, G_DKP), 0)
        sub = rowv % cs
        lag = row - col
        kbuf[0:cs, :] = jnp.zeros((cs, G_DKP), F32)
        bbuf[0:cs, :] = jnp.zeros((cs, G_DKP), F32)
        for h in range(G_HEADS):
            qs, kk, b = operands(h)
            qparts, kparts = [], []
            for i in range(1, t // cs):
                r = b[i * cs - 1:i * cs, :]
                inblk = (rowv >= i * cs) & (rowv < (i + 1) * cs)
                qparts.append((qs * jnp.exp(jnp.where(inblk, b - r, NEG))).astype(BF16))
                kparts.append((kk * jnp.exp(jnp.where(rowv < i * cs, r - b, NEG))).astype(BF16))
            attn = _dot_nt(jnp.concatenate(qparts, axis=1), jnp.concatenate(kparts, axis=1))
            kbuf[cs:cs + t, :] = kk
            bbuf[cs:cs + t, :] = b
            for j in range(cs):
                ksh = kbuf[cs - j:cs - j + t, :]
                bsh = bbuf[cs - j:cs - j + t, :]
                e = jnp.exp(jnp.where(sub >= j, b - bsh, NEG))
                dj = jnp.sum(qs * ksh * e, axis=1, keepdims=True)
                attn = attn + jnp.where(lag == j, dj, 0.0)
            attn_scr[h] = attn

    def tail(h):
        slv = slice(h * G_DVP, (h + 1) * G_DVP)
        qs, kk, b = operands(h)
        v = v_ref[:, slv]
        blast = b[t - 1:t, :]
        st = st_scr[h]
        o = (_dot_nt((qs * jnp.exp(b)).astype(BF16), st.astype(BF16))
             + _dot(attn_scr[h].astype(BF16), v))
        kdec = (kk * jnp.exp(blast - b)).astype(BF16)
        st_scr[h] = jnp.exp(blast) * st + _dot_tn(v, kdec)

        sc = slice(h * G_DV, (h + 1) * G_DV)
        var = jnp.sum(o * o, axis=1, keepdims=True) * (1.0 / G_DV)
        hb = (o * lax.rsqrt(var + EPS))[:, :G_DV] * nw_ref[:, sc]
        out_ref[:, sc] = (hb * z_ref[:, sc].astype(F32)).astype(BF16)

    return [functools.partial(tail, h) for h in range(G_HEADS)]


def _mixers_kernel(safe_ref, mq_ref, mk_ref, mv_ref, moz_ref, gmb_ref, gmt_ref, mnw_ref,
                   gq_ref, gk_ref, gv_ref, gz_ref, gb_ref, gnw_ref, ya_ref, yb_ref,
                   c_scr, m_scr, kbuf, bbuf, attn_scr, st_scr, *, t, cs):
    @pl.when(pl.program_id(1) == 0)
    def _():
        c_scr[...] = jnp.zeros_like(c_scr)
        m_scr[...] = jnp.zeros_like(m_scr)
        st_scr[...] = jnp.zeros_like(st_scr)

    safe = safe_ref[pl.program_id(0) * pl.num_programs(1) + pl.program_id(1)] != 0
    gla_heads = _gla_body(safe, gq_ref, gk_ref, gv_ref, gz_ref, gb_ref, gnw_ref, yb_ref, kbuf, bbuf,
                          attn_scr, st_scr, t=t, cs=cs)
    mlstm_heads = _mlstm_body(mq_ref, mk_ref, mv_ref, moz_ref, gmb_ref, gmt_ref, mnw_ref, ya_ref,
                              c_scr, m_scr, tc=t)
    for head in gla_heads + mlstm_heads:
        head()


def _mixers(mq, mk, mv, moz, gmb, gmt, gq, gk, gv, gz, gcum, l, m_nw, g_nw, batch, seq, t, cs):
    n = batch * seq
    nt = seq // t

    def blk(width):
        return pl.BlockSpec((t, width), lambda b, c: (b * nt + c, 0))

    safe = (jnp.min(gcum.reshape(n // t, t, G_KWP)[:, t - 1, :], axis=-1) >= G_SAFE_LOG_DECAY).astype(jnp.int32)
    return pl.pallas_call(
        functools.partial(_mixers_kernel, t=t, cs=cs),
        grid=(batch, nt),
        in_specs=[pl.BlockSpec(memory_space=pltpu.SMEM),
                  blk(M_WP), blk(M_WP), blk(M_WP), blk(M_W), blk(2 * M_HEADS * LANES),
                  pl.BlockSpec((LANES, t), lambda b, c: (0, b * nt + c)), _of_layer(l, (1, M_W)),
                  blk(G_KWP), blk(G_KWP), blk(G_VWP), blk(G_VW), blk(G_KWP), _of_layer(l, (1, G_VW))],
        out_specs=[blk(M_W), blk(G_VW)],
        out_shape=[jax.ShapeDtypeStruct((n, M_W), BF16), jax.ShapeDtypeStruct((n, G_VW), BF16)],
        scratch_shapes=[pltpu.VMEM((M_HEADS, M_HDP, M_HDP + LANES), F32),
                        pltpu.VMEM((M_HEADS, SUBLANES, LANES), F32),
                        pltpu.VMEM((t + cs, G_DKP), F32), pltpu.VMEM((t + cs, G_DKP), F32),
                        pltpu.VMEM((G_HEADS, t, t), F32),
                        pltpu.VMEM((G_HEADS, G_DVP, G_DKP), F32)],
        compiler_params=_params("arbitrary", "arbitrary"),
        name="mixers",
    )(safe, mq, mk, mv, moz, gmb, gmt, m_nw, gq, gk, gv, gz, gcum, g_nw)


def _gelu_tanh(x):
    return 0.5 * x * (1.0 + jnp.tanh(math.sqrt(2.0 / math.pi) * (x + 0.044715 * (x * x * x))))


def _s5_kernel(u_ref, kms_ref, mo_ref, pin_ref, pt_ref, d_ref, y_ref, carry, xs, *, nch):
    @pl.when(pl.program_id(2) == 0)
    def _():
        carry[...] = jnp.zeros_like(carry)

    qw = S_Q * LANES
    ngrp = nch // S_SUB
    ltiles = S_SW // LANES
    ucat = jnp.concatenate([u_ref[0, pl.ds(s, nch, stride=S_Q), :] for s in range(S_Q)],
                           axis=1).astype(BF16)
    r = _dot(ucat, kms_ref[0])
    y = r[:, :qw]
    xr = r[:, qw:qw + S_SW]
    xi = r[:, qw + S_SW:]

    def axpy(xr, xi, pr, pi, sr, si):
        return xr + (pr * sr - pi * si), xi + (pr * si + pi * sr)

    def scan(xr, xi, p_ref, pos, length):
        d = 1
        while d < length:
            keep = pos >= d
            sr = jnp.where(keep, pltpu.roll(xr, d, 0), 0.0)
            si = jnp.where(keep, pltpu.roll(xi, d, 0), 0.0)
            xr, xi = axpy(xr, xi, p_ref[0, d - 1:d, :S_SW], p_ref[0, d - 1:d, S_SW:], sr, si)
            d *= 2
        return xr, xi

    def slabs(part, arr):
        return [(part * ltiles + j, arr[:, j * LANES:(j + 1) * LANES]) for j in range(ltiles)]

    rowi = lax.broadcasted_iota(jnp.int32, (nch, S_SW), 0)
    xr, xi = scan(xr, xi, pin_ref, rowi % S_SUB, S_SUB)
    for part, arr in ((0, xr), (1, xi)):
        for j, piece in slabs(part, arr):
            xs[j] = piece

    def group_ends(part):
        return jnp.concatenate([xs[part * ltiles + j, pl.ds(S_SUB - 1, ngrp, stride=S_SUB), :]
                                for j in range(ltiles)], axis=1)

    rowg = lax.broadcasted_iota(jnp.int32, (ngrp, S_SW), 0)
    er, ei = scan(group_ends(0), group_ends(1), pt_ref, rowg, ngrp)
    cr = carry[0:1, :S_SW]
    ci = carry[0:1, S_SW:]
    er, ei = axpy(er, ei, pt_ref[0, :, :S_SW], pt_ref[0, :, S_SW:], cr, ci)
    first = rowg == 0
    gr = jnp.where(first, cr, pltpu.roll(er, 1, 0))
    gi = jnp.where(first, ci, pltpu.roll(ei, 1, 0))
    carry[0:1, :S_SW] = er[ngrp - 1:ngrp, :]
    carry[0:1, S_SW:] = ei[ngrp - 1:ngrp, :]
    for part, arr in ((0, gr), (1, gi)):
        for j, piece in slabs(part, arr):
            for k in range(S_SUB):
                xs[j, pl.ds(k, ngrp, stride=S_SUB), :] = piece
    sr = jnp.concatenate([xs[j] for j in range(ltiles)], axis=1)
    si = jnp.concatenate([xs[ltiles + j] for j in range(ltiles)], axis=1)
    pwr = jnp.concatenate([pin_ref[0, :, :S_SW]] * ngrp, axis=0)
    pwi = jnp.concatenate([pin_ref[0, :, S_SW:]] * ngrp, axis=0)
    xr, xi = axpy(xr, xi, pwr, pwi, sr, si)

    firstc = rowi == 0
    xpr = jnp.where(firstc, cr, pltpu.roll(xr, 1, 0))
    xpi = jnp.where(firstc, ci, pltpu.roll(xi, 1, 0))
    y = y + _dot(jnp.concatenate([xpr, xpi], axis=1).astype(BF16), mo_ref[0])
    for t in range(S_Q):
        y_ref[0, pl.ds(t, nch, stride=S_Q), :] = y[:, t * LANES:(t + 1) * LANES]
    y_ref[0] = _gelu_tanh(y_ref[0] + d_ref[0] * u_ref[0])


def _s5(cu, l, kms, mo, pin, pt, dskip, batch, seq, tm):
    n = batch * seq
    nt = seq // tm
    nch = tm // S_Q
    qw = S_Q * LANES

    def table(rows, cols):
        return pl.BlockSpec((None, 1, rows, cols), lambda j, b, t: (l, j, 0, 0))

    return pl.pallas_call(
        functools.partial(_s5_kernel, nch=nch),
        grid=(S_BLOCKS, batch, nt),
        in_specs=[pl.BlockSpec((1, tm, LANES), lambda j, b, t: (j, b * nt + t, 0)),
                  table(qw, qw + 2 * S_SW), table(2 * S_SW, qw), table(S_SUB, 2 * S_SW),
                  table(nch // S_SUB, 2 * S_SW), table(1, LANES)],
        out_specs=pl.BlockSpec((1, tm, LANES), lambda j, b, t: (j, b * nt + t, 0)),
        out_shape=jax.ShapeDtypeStruct((S_BLOCKS, n, LANES), F32),
        scratch_shapes=[pltpu.VMEM((SUBLANES, 2 * S_SW), F32),
                        pltpu.VMEM((2 * S_SW // LANES, nch, LANES), F32)],
        compiler_params=_params("arbitrary", "arbitrary", "arbitrary"),
        name="s5",
    )(cu, kms, mo, pin, pt, dskip)


def _s5_expand_kernel(msr_ref, msi_ref, mor_ref, moi_ref, kms_ref, mo_ref):
    q = S_Q
    qw = q * LANES

    def embed(m, rows, width):
        shape = (width, S_GPB * width)
        tile = (lax.broadcasted_iota(jnp.int32, shape, 1) % width
                == lax.broadcasted_iota(jnp.int32, shape, 0)).astype(BF16)
        oshape = (S_GPB * rows, S_GPB * width)
        diag = (lax.broadcasted_iota(jnp.int32, oshape, 0) // rows
                == lax.broadcasted_iota(jnp.int32, oshape, 1) // width)
        return jnp.where(diag, _dot(m, tile), 0.0).astype(BF16)

    ms = [(embed(msr_ref[0, 0, s], S_GROUP, S_STATE), embed(msi_ref[0, 0, s], S_GROUP, S_STATE))
          for s in range(q)]
    mo = [(embed(mor_ref[0, 0, tau], S_STATE, S_GROUP), embed(moi_ref[0, 0, tau], S_STATE, S_GROUP))
          for tau in range(q + 1)]
    bbr, bbi = ms[q - 1]
    kd = [(_dot(bbr, mo[tau][0]) + _dot(bbi, mo[tau][1])).astype(BF16) for tau in range(q)]
    zero = jnp.zeros((LANES, LANES), BF16)
    for s in range(q):
        rows = slice(s * LANES, (s + 1) * LANES)
        for t in range(q):
            kms_ref[0, 0, rows, t * LANES:(t + 1) * LANES] = kd[t - s] if t >= s else zero
        kms_ref[0, 0, rows, qw:qw + S_SW] = ms[s][0]
        kms_ref[0, 0, rows, qw + S_SW:] = ms[s][1]
    for t in range(q):
        cols = slice(t * LANES, (t + 1) * LANES)
        mo_ref[0, 0, 0:S_SW, cols] = mo[t + 1][0]
        mo_ref[0, 0, S_SW:, cols] = mo[t + 1][1]


def _s5_expand(msr, msi, mor, moi):
    depth = msr.shape[0]
    qw = S_Q * LANES

    def blk(a):
        return pl.BlockSpec((1, 1) + a.shape[2:], lambda d, j: (d, j, 0, 0, 0))

    return pl.pallas_call(
        _s5_expand_kernel, grid=(depth, S_BLOCKS),
        in_specs=[blk(a) for a in (msr, msi, mor, moi)],
        out_specs=[pl.BlockSpec((1, 1, qw, qw + 2 * S_SW), lambda d, j: (d, j, 0, 0)),
                   pl.BlockSpec((1, 1, 2 * S_SW, qw), lambda d, j: (d, j, 0, 0))],
        out_shape=[jax.ShapeDtypeStruct((depth, S_BLOCKS, qw, qw + 2 * S_SW), BF16),
                   jax.ShapeDtypeStruct((depth, S_BLOCKS, 2 * S_SW, qw), BF16)],
        compiler_params=_params("arbitrary", "arbitrary"), name="s5_expand",
    )(msr, msi, mor, moi)


def _s5_tables(lam_re, lam_im, log_dt, b_re, b_im, c_re, c_im, nch):
    q = S_Q
    dep = lam_re.shape[0]
    lr = jnp.minimum(lam_re.astype(F32), -1e-4)
    li = lam_im.astype(F32)
    dt = jnp.exp(log_dt.astype(F32))[..., None]
    mag = jnp.exp(lr * dt)
    ab_re = mag * jnp.cos(li * dt)
    ab_im = mag * jnp.sin(li * dt)
    nr = ab_re - 1.0
    den = lr * lr + li * li
    coef_re = ((nr * lr + ab_im * li) / den)[:, :, None, :]
    coef_im = ((ab_im * lr - nr * li) / den)[:, :, None, :]
    brt = jnp.swapaxes(b_re.astype(F32), -1, -2)
    bit = jnp.swapaxes(b_im.astype(F32), -1, -2)
    bb_re = coef_re * brt - coef_im * bit
    bb_im = coef_re * bit + coef_im * brt

    def apow(nvals):
        e = jnp.asarray(nvals, F32)[:, None, None, None]
        m = jnp.exp(e * (lr * dt))
        ang = e * (li * dt)
        return m * jnp.cos(ang), m * jnp.sin(ang)

    def per_block(m, rows):
        m = m.reshape(m.shape[0], dep, S_BLOCKS, S_GPB * rows, m.shape[-1])
        return m.transpose(1, 2, 0, 3, 4).astype(BF16)

    ar, ai = apow(list(range(q + 1)))
    crt = jnp.swapaxes(c_re.astype(F32), -1, -2)
    cit = jnp.swapaxes(c_im.astype(F32), -1, -2)
    ca_re = crt[None] * ar[..., None] - cit[None] * ai[..., None]
    ca_im = crt[None] * ai[..., None] + cit[None] * ar[..., None]
    pw_r = ar[:q][::-1][:, :, :, None, :]
    pw_i = ai[:q][::-1][:, :, :, None, :]
    kms, mo = _s5_expand(per_block(pw_r * bb_re - pw_i * bb_im, S_GROUP),
                         per_block(pw_r * bb_im + pw_i * bb_re, S_GROUP),
                         per_block(ca_re, S_STATE), per_block(-ca_im, S_STATE))
    def power_table(nvals):
        pr, pi = apow(nvals)
        pr = pr.reshape(len(nvals), dep, S_BLOCKS, S_SW).transpose(1, 2, 0, 3)
        pi = pi.reshape(len(nvals), dep, S_BLOCKS, S_SW).transpose(1, 2, 0, 3)
        return jnp.concatenate([pr, pi], axis=3)

    pin = power_table([q * (r + 1) for r in range(S_SUB)])
    pt = power_table([q * S_SUB * (k + 1) for k in range(nch // S_SUB)])
    return kms, mo, pin, pt


def _merge_kernel(ya_ref, yb_ref, yc_ref, cz_ref, g_ref, x_ref, wa_ref, wb_ref, wglu_ref, wc_ref,
                  wo_ref, nw_ref, *out_refs, final):
    yc0 = jnp.concatenate([yc_ref[j] for j in range(S_BLOCKS)], axis=1)
    glu = _dot(yc0.astype(BF16), wglu_ref[...])
    yc = yc0 * _sigmoid(glu) * cz_ref[...].astype(F32)
    merged = (g_ref[:, 0:D_MODEL].astype(F32) * _dot(ya_ref[...], wa_ref[...])
              + g_ref[:, D_MODEL:2 * D_MODEL].astype(F32) * _dot(yb_ref[...], wb_ref[...])
              + g_ref[:, 2 * D_MODEL:].astype(F32) * _dot(yc.astype(BF16), wc_ref[...]))
    out = x_ref[...] + _dot(merged.astype(BF16), wo_ref[...])
    ms = jnp.mean(out * out, axis=-1, keepdims=True)
    normed = out * lax.rsqrt(ms + EPS) * nw_ref[...]
    if final:
        out_refs[0][...] = normed
    else:
        out_refs[0][...] = out
        out_refs[1][...] = normed.astype(BF16)


def _merge(ya, yb, yc, cz, g, x2, l, wa, wb, wglu, wc, wo, nw, tm, final):
    n = x2.shape[0]
    x_out = jax.ShapeDtypeStruct((n, D_MODEL), F32)
    hn_out = jax.ShapeDtypeStruct((n, D_MODEL), BF16)
    return pl.pallas_call(
        functools.partial(_merge_kernel, final=final),
        grid=(n // tm,),
        in_specs=[_rows(tm, M_W), _rows(tm, G_VW),
                  pl.BlockSpec((S_BLOCKS, tm, LANES), lambda i: (0, i, 0)),
                  _rows(tm, S_W), _rows(tm, 3 * D_MODEL), _rows(tm, D_MODEL),
                  _of_layer(l, (M_W, D_MODEL)), _of_layer(l, (G_VW, D_MODEL)), _of_layer(l, (S_W, S_W)),
                  _of_layer(l, (S_W, D_MODEL)), _of_layer(l, (D_MODEL, D_MODEL)), _whole((1, D_MODEL))],
        out_specs=[_rows(tm, D_MODEL)] if final else [_rows(tm, D_MODEL), _rows(tm, D_MODEL)],
        out_shape=[x_out] if final else [x_out, hn_out],
        compiler_params=_params("arbitrary"),
        name="merge",
    )(ya, yb, yc, cz, g, x2, wa, wb, wglu, wc, wo, nw)


def _pad_heads(w, heads, d, dp):
    zeros = jnp.zeros(w.shape[:-1] + (dp - d,), w.dtype)
    parts = []
    for h in range(heads):
        parts += [w[..., h * d:(h + 1) * d], zeros]
    return jnp.concatenate(parts, axis=-1)


def _prepare(p, nch):
    depth = p["w_in"].shape[0]
    w_mlstm, w_plain, w_gate, w_small = _wprep(jnp.swapaxes(p["w_in"], 1, 2), LANES)
    gkp = functools.partial(_pad_heads, heads=G_HEADS, d=G_DK, dp=G_DKP)
    kms, mo, pin, pt = _s5_tables(p["s5_lam_re"], p["s5_lam_im"], p["s5_log_dt"], p["s5_B_re"], p["s5_B_im"],
                               p["s5_C_re"], p["s5_C_im"], nch)
    gate_b = p["mlstm_gate_b"]
    return dict(
        w_mlstm=w_mlstm, w_plain=w_plain, w_gate=w_gate, w_small=w_small,
        cq=p["mlstm_conv"][:, :, :M_W], ck=p["mlstm_conv"][:, :, M_W:],
        gb=jnp.concatenate([gate_b[:, 0], gate_b[:, 1],
                            jnp.zeros((depth, LANES - 2 * M_HEADS), F32)], axis=1)[:, None, :],
        m_nw=p["mlstm_norm"][:, None, :],
        wal=jnp.concatenate([jnp.zeros((depth, 2 * M_HEADS, G_KWP), F32), gkp(p["gla_w_alpha"]),
                             jnp.zeros((depth, LANES - 2 * M_HEADS - G_RANK, G_KWP), F32)],
                            axis=1).astype(BF16),
        bal=gkp(p["gla_b_alpha"])[:, None, :],
        g_nw=p["gla_norm"][:, None, :],
        kms=kms, mo=mo, pin=pin, pt=pt,
        dskip=p["s5_D"].reshape(depth, S_BLOCKS, 1, LANES),
        wa=p["w_branch_mlstm"].astype(BF16), wb=p["w_branch_gla"].astype(BF16),
        wglu=p["s5_w_glu"].astype(BF16), wc=p["w_branch_s5"].astype(BF16), wo=p["w_out"].astype(BF16),
    )


def _tiles(batch, seq):
    return dict(proj=min(1024, seq), chunk=min(256, seq), g_sub=16,
                s5=min(4096, seq), merge=min(512, batch * seq), rms=min(1024, batch * seq))


def _layer(x2, hn, batch, seq, l, w, next_norm, final, tl):
    tm = tl["proj"]
    q, k, oz = _proj_mlstm(hn, l, w["w_mlstm"], w["cq"], w["ck"], seq, tm)
    gz, czs, gs = _proj_gate(hn, l, w["w_gate"], tm)
    mv, gv, gq, gk, cu_s, gm, gmt, gcum = _proj_va(hn, l, w["w_plain"], w["w_small"], w["gb"], w["wal"],
                                                   w["bal"], tm, tl["chunk"])
    ya, yb = _mixers(q, k, mv, oz, gm, gmt, gq, gk, gv, gz, gcum, l, w["m_nw"], w["g_nw"], batch, seq,
                     tl["chunk"], tl["g_sub"])
    yc = _s5(cu_s, l, w["kms"], w["mo"], w["pin"], w["pt"], w["dskip"], batch, seq, tl["s5"])
    return _merge(ya, yb, yc, czs, gs, x2, l, w["wa"], w["wb"], w["wglu"], w["wc"], w["wo"],
                  next_norm.reshape(1, D_MODEL), tl["merge"], final)


def kernel(x, norm_w, w_in, mlstm_conv, mlstm_gate_b, mlstm_norm, gla_w_alpha, gla_b_alpha, gla_norm,
           s5_lam_re, s5_lam_im, s5_log_dt, s5_B_re, s5_B_im, s5_C_re, s5_C_im, s5_D, s5_w_glu,
           w_branch_mlstm, w_branch_gla, w_branch_s5, w_out, final_norm):
    batch, seq, _ = x.shape
    depth = norm_w.shape[0]
    tl = _tiles(batch, seq)
    prep = _prepare(dict(w_in=w_in, mlstm_conv=mlstm_conv, mlstm_gate_b=mlstm_gate_b, mlstm_norm=mlstm_norm,
                         gla_w_alpha=gla_w_alpha, gla_b_alpha=gla_b_alpha, gla_norm=gla_norm,
                         s5_lam_re=s5_lam_re, s5_lam_im=s5_lam_im, s5_log_dt=s5_log_dt, s5_B_re=s5_B_re,
                         s5_B_im=s5_B_im, s5_C_re=s5_C_re, s5_C_im=s5_C_im, s5_D=s5_D, s5_w_glu=s5_w_glu,
                         w_branch_mlstm=w_branch_mlstm, w_branch_gla=w_branch_gla,
                         w_branch_s5=w_branch_s5, w_out=w_out), tl["s5"] // S_Q)
    x2 = x.reshape(batch * seq, D_MODEL)
    hn = _rms(x2, norm_w[0].reshape(1, D_MODEL), tl["rms"])
    for l in range(depth):
        final = l == depth - 1
        res = _layer(x2, hn, batch, seq, l, prep, final_norm if final else norm_w[l + 1], final, tl)
        if final:
            x2 = res[0]
        else:
            x2, hn = res
    return x2.reshape(batch, seq, D_MODEL)
```

```python
import functools
import math

import jax
import jax.numpy as jnp
from jax import lax
from jax.experimental import pallas as pl
from jax.experimental.pallas import tpu as pltpu

F32 = jnp.float32
BF16 = jnp.bfloat16

EPS = 1e-6
D_MODEL = 1024
LANES = 128
SUBLANES = 8
V7X_VMEM_BYTES = 64 * 1024 * 1024
M_HEADS = 4
M_HD = 192
M_HDP = 256
M_W = M_HEADS * M_HD
M_WP = M_HEADS * M_HDP
M_CONV = 4
G_HEADS = 4
G_DK = 96
G_DKP = 128
G_DV = 192
G_DVP = 256
G_KW = G_HEADS * G_DK
G_KWP = G_HEADS * G_DKP
G_VW = G_HEADS * G_DV
G_VWP = G_HEADS * G_DVP
G_RANK = 16
G_TAU = 16.0
S_W = 512
S_GROUP = 16
S_STATE = 64
S_BLOCKS = S_W // LANES
S_GPB = LANES // S_GROUP
S_SW = S_GPB * S_STATE
S_Q = 8
S_SUB = 8

IN_SIZES = (M_W, M_W, M_W, M_W, M_HEADS, M_HEADS, M_W,
            G_KW, G_KW, G_VW, G_RANK, G_VW,
            S_W, S_W, 3 * D_MODEL)
SMALL_W = S_W + LANES
CW = 512

NEG = -1e30
G_SAFE_LOG_DECAY = -60.0
VMEM_LIMIT = V7X_VMEM_BYTES * 7 // 8


def _sigmoid(x):
    return 0.5 * jnp.tanh(0.5 * x) + 0.5


def _silu(x):
    h = 0.5 * x
    return h * jnp.tanh(h) + h


def _log_sigmoid(x):
    return jnp.minimum(x, 0.0) - jnp.log1p(jnp.exp(-jnp.abs(x)))


def _dot(a, b):
    return jnp.dot(a, b, preferred_element_type=F32)


def _dot_nt(a, b):
    return lax.dot_general(a, b, (((1,), (1,)), ((), ())), preferred_element_type=F32)


def _dot_tn(a, b):
    return lax.dot_general(a, b, (((0,), (0,)), ((), ())), preferred_element_type=F32)


def _params(*sem):
    return pltpu.CompilerParams(dimension_semantics=sem, vmem_limit_bytes=VMEM_LIMIT)


def _rows(tm, width):
    return pl.BlockSpec((tm, width), lambda i: (i, 0))


def _whole(shape):
    return pl.BlockSpec(shape, lambda i: (0,) * len(shape))


def _of_layer(l, shape):
    return pl.BlockSpec((None,) + shape, lambda *_: (l,) + (0,) * len(shape))


IN_NAMES = ("aq", "ak", "av", "ao", "ai", "af", "az", "bq", "bk", "bv", "ba", "bz", "cu", "cz", "g")
W_OFF = {name: sum(IN_SIZES[:i]) for i, name in enumerate(IN_NAMES)}
IN_WIDTH = sum(IN_SIZES)
W_MLSTM, W_PLAIN, W_GATE = 4 * M_W, M_W + G_VW + 2 * G_KW, G_VW + S_W + 3 * D_MODEL


def _wprep_kernel(w_ref, wm_ref, wp_ref, wg_ref, ws_ref):
    kb = w_ref.shape[2]

    def put(dst, dst_off, parts, width):
        xs = [w_ref[0, off:off + rows, :] for off, rows in parts]
        used = sum(rows for _, rows in parts)
        if used < width:
            xs.append(jnp.zeros((width - used, kb), F32))
        x = xs[0] if len(xs) == 1 else jnp.concatenate(xs, axis=0)
        dst[0, :, dst_off:dst_off + width] = x.T.astype(BF16)

    def group(dst, names):
        off = 0
        for name in names:
            width = IN_SIZES[IN_NAMES.index(name)]
            for c in range(0, width, LANES):
                put(dst, off + c, [(W_OFF[name] + c, LANES)], LANES)
            off += width

    group(wm_ref, ("aq", "ak", "ao", "az"))
    group(wp_ref, ("av", "bv", "bq", "bk"))
    group(wg_ref, ("bz", "cz", "g"))
    group(ws_ref, ("cu",))
    put(ws_ref, S_W, [(W_OFF["ai"], 2 * M_HEADS), (W_OFF["ba"], G_RANK)], LANES)


def _wprep(w_t, kb):
    depth = w_t.shape[0]
    widths = (W_MLSTM, W_PLAIN, W_GATE, SMALL_W)
    return pl.pallas_call(
        _wprep_kernel, grid=(depth, D_MODEL // kb),
        in_specs=[pl.BlockSpec((1, IN_WIDTH, kb), lambda l, i: (l, 0, i))],
        out_specs=[pl.BlockSpec((1, kb, wd), lambda l, i: (l, i, 0)) for wd in widths],
        out_shape=[jax.ShapeDtypeStruct((depth, D_MODEL, wd), BF16) for wd in widths],
        compiler_params=_params("arbitrary", "arbitrary"), name="wprep",
    )(w_t)


def _scatter_heads(dst, y, heads, d, dp, fill):
    rows = y.shape[0]
    for h in range(heads):
        dst[:, h * dp:h * dp + d] = y[:, h * d:(h + 1) * d].astype(BF16)
        dst[:, h * dp + d:(h + 1) * dp] = jnp.full((rows, dp - d), fill, BF16)


def _proj_mlstm_kernel(hn_ref, w_ref, cq_ref, ck_ref, q_ref, k_ref, oz_ref, halo_x, halo_b, *, tm,
                       tiles_per_seq):
    @pl.when(pl.program_id(0) % tiles_per_seq == 0)
    def _():
        halo_x[...] = jnp.zeros_like(halo_x)
        halo_b[...] = jnp.zeros_like(halo_b)

    hn = hn_ref[...]
    row8 = lax.broadcasted_iota(jnp.int32, (SUBLANES, M_W), 0)

    def shift(x, prev8, d):
        rolled = pltpu.roll(x, d, 0)
        head = jnp.where(row8 < d, pltpu.roll(prev8, d, 0), rolled[0:SUBLANES])
        return jnp.concatenate([head, rolled[SUBLANES:]], axis=0)

    def conv_act(c, x, cw_ref, dst, scale):
        x1 = shift(x, halo_x[c], 1)
        pair = cw_ref[1:2, :] * x + cw_ref[0:1, :] * x1
        y = _silu(cw_ref[3:4, :] * x + cw_ref[2:3, :] * x1 + shift(pair, halo_b[c], 2))
        halo_x[c] = x[tm - SUBLANES:tm]
        halo_b[c] = pair[tm - SUBLANES:tm]
        _scatter_heads(dst, y if scale == 1.0 else y * scale, M_HEADS, M_HD, M_HDP, 0.0)

    xq = _dot(hn, w_ref[:, 0:M_W])
    xk = _dot(hn, w_ref[:, M_W:2 * M_W])
    conv_act(0, xq, cq_ref, q_ref, 1.0)
    o = _dot(hn, w_ref[:, 2 * M_W:3 * M_W])
    conv_act(1, xk, ck_ref, k_ref, M_HD ** -0.5)
    z = _dot(hn, w_ref[:, 3 * M_W:])
    oz_ref[...] = (_sigmoid(o) * _silu(z)).astype(BF16)


def _proj_mlstm(hn, l, w, cq, ck, seq, tm):
    n = hn.shape[0]
    padded = jax.ShapeDtypeStruct((n, M_WP), BF16)
    return pl.pallas_call(
        functools.partial(_proj_mlstm_kernel, tm=tm, tiles_per_seq=seq // tm),
        grid=(n // tm,),
        in_specs=[_rows(tm, D_MODEL), _of_layer(l, (D_MODEL, W_MLSTM)), _of_layer(l, (M_CONV, M_W)),
                  _of_layer(l, (M_CONV, M_W))],
        out_specs=[_rows(tm, M_WP), _rows(tm, M_WP), _rows(tm, M_W)],
        out_shape=[padded, padded, jax.ShapeDtypeStruct((n, M_W), BF16)],
        scratch_shapes=[pltpu.VMEM((2, SUBLANES, M_W), F32)] * 2,
        compiler_params=_params("arbitrary"), name="proj_mlstm",
    )(hn, w, cq, ck)


def _proj_gate_kernel(src_ref, nw_ref, w_ref, gz_ref, cz_ref, g_ref, *hn_out, normalize):
    if normalize:
        x = src_ref[...]
        ms = jnp.mean(x * x, axis=-1, keepdims=True)
        hn = (x * lax.rsqrt(ms + EPS) * nw_ref[...]).astype(BF16)
        hn_out[0][...] = hn
    else:
        hn = src_ref[...]
    gz_ref[...] = _silu(_dot(hn, w_ref[:, 0:G_VW])).astype(BF16)
    cz_ref[...] = _silu(_dot(hn, w_ref[:, G_VW:G_VW + S_W])).astype(BF16)
    off = G_VW + S_W
    for c in range(3 * D_MODEL // CW):
        r = _dot(hn, w_ref[:, off + c * CW:off + (c + 1) * CW])
        g_ref[:, c * CW:(c + 1) * CW] = _sigmoid(r).astype(BF16)


def _proj_gate(src, nw, l, w, tm, normalize):
    n = src.shape[0]
    widths = (G_VW, S_W, 3 * D_MODEL) + ((D_MODEL,) if normalize else ())
    return pl.pallas_call(
        functools.partial(_proj_gate_kernel, normalize=normalize), grid=(n // tm,),
        in_specs=[_rows(tm, D_MODEL), _whole((1, D_MODEL)), _of_layer(l, (D_MODEL, W_GATE))],
        out_specs=[_rows(tm, wd) for wd in widths],
        out_shape=[jax.ShapeDtypeStruct((n, wd), BF16) for wd in widths],
        compiler_params=_params("arbitrary"), name="proj_gate",
    )(src, nw, w)


def _split3(x):
    hi = x.astype(BF16)
    r1 = x - hi.astype(F32)
    mid = r1.astype(BF16)
    return hi, mid, (r1 - mid.astype(F32)).astype(BF16)


def _segment_cumsum(x, seg):
    rows, width = x.shape
    tril = (lax.broadcasted_iota(jnp.int32, (seg, seg), 1)
            <= lax.broadcasted_iota(jnp.int32, (seg, seg), 0)).astype(BF16)
    parts = jnp.concatenate(_split3(x), axis=1)
    out = []
    for s in range(rows // seg):
        acc = _dot(tril, parts[s * seg:(s + 1) * seg])
        out.append(acc[:, :width] + acc[:, width:2 * width] + acc[:, 2 * width:])
    return jnp.concatenate(out, axis=0)


def _proj_va_kernel(hn_ref, wp_ref, ws_ref, gb_ref, wal_ref, bal_ref,
                    mv_ref, gv_ref, gq_ref, gk_ref, cu_ref, gmb_ref, gmt_ref, gcum_ref, *, chunk):
    hn = hn_ref[...]
    r = _dot(hn, ws_ref[...])
    _scatter_heads(mv_ref, _dot(hn, wp_ref[:, 0:M_W]), M_HEADS, M_HD, M_HDP, 0.0)
    for j in range(S_BLOCKS):
        cu_ref[j] = r[:, j * LANES:(j + 1) * LANES]
    raw = r[:, S_W:]
    gates = raw + gb_ref[...]
    lane = lax.broadcasted_iota(jnp.int32, gates.shape, 1)
    fgate = (lane >= M_HEADS) & (lane < 2 * M_HEADS)
    za = _dot(raw.astype(BF16), wal_ref[...]) + bal_ref[...]
    gm = jnp.where(fgate, _segment_cumsum(_log_sigmoid(gates), chunk), gates)
    _scatter_heads(gv_ref, _dot(hn, wp_ref[:, M_W:M_W + G_VW]), G_HEADS, G_DV, G_DVP, 0.0)
    gmt_ref[...] = gm.T
    slots = 2 * M_HEADS * LANES
    spread = (lax.broadcasted_iota(jnp.int32, (3 * LANES, slots), 0) % LANES
              == lax.broadcasted_iota(jnp.int32, (3 * LANES, slots), 1) // LANES).astype(BF16)
    gmb_ref[...] = _dot(jnp.concatenate(_split3(gm), axis=1), spread)
    off = M_W + G_VW
    _scatter_heads(gq_ref, _dot(hn, wp_ref[:, off:off + G_KW]) * (G_DK ** -0.5), G_HEADS, G_DK, G_DKP, 0.0)
    gcum_ref[...] = _segment_cumsum(_log_sigmoid(za) * (1.0 / G_TAU), chunk)
    _scatter_heads(gk_ref, _dot(hn, wp_ref[:, off + G_KW:]), G_HEADS, G_DK, G_DKP, 0.0)


def _proj_va(hn, l, wp, ws, gb, wal, bal, tm, chunk):
    n = hn.shape[0]
    widths = (M_WP, G_VWP, G_KWP, G_KWP)
    return pl.pallas_call(
        functools.partial(_proj_va_kernel, chunk=chunk), grid=(n // tm,),
        in_specs=[_rows(tm, D_MODEL), _of_layer(l, (D_MODEL, W_PLAIN)), _of_layer(l, (D_MODEL, SMALL_W)),
                  _of_layer(l, (1, LANES)), _of_layer(l, (LANES, G_KWP)), _of_layer(l, (1, G_KWP))],
        out_specs=[_rows(tm, wd) for wd in widths] + [
            pl.BlockSpec((S_BLOCKS, tm, LANES), lambda i: (0, i, 0)), _rows(tm, 2 * M_HEADS * LANES),
            pl.BlockSpec((LANES, tm), lambda i: (0, i)), _rows(tm, G_KWP)],
        out_shape=[jax.ShapeDtypeStruct((n, wd), BF16) for wd in widths] + [
            jax.ShapeDtypeStruct((S_BLOCKS, n, LANES), F32),
            jax.ShapeDtypeStruct((n, 2 * M_HEADS * LANES), F32),
            jax.ShapeDtypeStruct((LANES, n), F32),
            jax.ShapeDtypeStruct((n, G_KWP), F32)],
        compiler_params=_params("arbitrary"), name="proj_va",
    )(hn, wp, ws, gb, wal, bal)


def _mlstm_body(q_ref, k_ref, v_ref, oz_ref, gmb_ref, gmt_ref, nw_ref, out_ref, c_scr, m_scr, *, tc):
    row = lax.broadcasted_iota(jnp.int32, (tc, tc), 0)
    col = lax.broadcasted_iota(jnp.int32, (tc, tc), 1)
    causal = col <= row
    real = lax.broadcasted_iota(jnp.int32, (tc, M_HDP), 1) < M_HD
    ones_tile = jnp.ones((tc, LANES), BF16)

    def lanes(x, n):
        return jnp.concatenate([x] * n, axis=1)

    def head(h):
        sl = slice(h * M_HDP, (h + 1) * M_HDP)
        qb = q_ref[:, sl]
        kb = k_ref[:, sl]
        vext = jnp.concatenate([v_ref[:, sl], ones_tile], axis=1)
        i_b = gmb_ref[:, h * LANES:(h + 1) * LANES]
        bcum = gmb_ref[:, (M_HEADS + h) * LANES:(M_HEADS + h + 1) * LANES]
        g_row = gmt_ref[h:h + 1, :] - gmt_ref[M_HEADS + h:M_HEADS + h + 1, :]
        m_prev = m_scr[h, 0:1, :]

        inter = bcum + m_prev
        m_t = jnp.maximum(inter, bcum + jnp.max(jnp.where(causal, g_row, NEG), axis=1, keepdims=True))
        w_intra = jnp.exp(jnp.where(causal, g_row + lanes(bcum - m_t, tc // LANES), NEG))
        w_inter = jnp.exp(inter - m_t)
        s = _dot_nt(qb, kb) * w_intra
        c_mat = c_scr[h]
        num = _dot(s.astype(BF16), vext) + lanes(w_inter, 3) * _dot(qb, c_mat.astype(BF16))
        den = num[:, M_HDP:]
        rden = 1.0 / jnp.maximum(jnp.abs(den), jnp.exp(-m_t))
        num = num[:, :M_HDP]

        b_end = bcum[tc - 1:tc, :]
        to_end = b_end - bcum + i_b
        m_new = jnp.maximum(b_end + m_prev, jnp.max(to_end, axis=0, keepdims=True))
        wk = jnp.exp(to_end - m_new)
        w_prev = jnp.exp(b_end + m_prev - m_new)
        c_scr[h] = lanes(w_prev, 3) * c_mat + _dot_tn((lanes(wk, 2) * kb.astype(F32)).astype(BF16), vext)
        m_scr[h] = jnp.broadcast_to(m_new, (SUBLANES, LANES))

        sc = slice(h * M_HD, (h + 1) * M_HD)
        mean = jnp.sum(num, axis=1, keepdims=True) * (1.0 / M_HD)
        xc = jnp.where(real, num - mean, 0.0)
        var = jnp.sum(xc * xc, axis=1, keepdims=True) * (1.0 / M_HD)
        scale = rden * lax.rsqrt(rden * rden * var + EPS)
        ha = (xc * lanes(scale, 2))[:, :M_HD]
        out_ref[:, sc] = (ha * nw_ref[:, sc] * oz_ref[:, sc].astype(F32)).astype(BF16)

    return [functools.partial(head, h) for h in range(M_HEADS)]


def _gla_body(safe, q_ref, k_ref, v_ref, z_ref, b_ref, nw_ref,
              out_ref, kbuf, bbuf, attn_scr, st_scr, *, t, cs):
    row = lax.broadcasted_iota(jnp.int32, (t, t), 0)
    col = lax.broadcasted_iota(jnp.int32, (t, t), 1)
    causal = col <= row

    def operands(h):
        sl = slice(h * G_DKP, (h + 1) * G_DKP)
        return q_ref[:, sl].astype(F32), k_ref[:, sl].astype(F32), b_ref[:, sl]

    @pl.when(safe)
    def _():
        for h in range(G_HEADS):
            qs, kk, b = operands(h)
            a = _dot_nt((qs * jnp.exp(b)).astype(BF16), (kk * jnp.exp(-b)).astype(BF16))
            attn_scr[h] = jnp.where(causal, a, 0.0)

    @pl.when(jnp.logical_not(safe))
    def _():
        rowv = lax.broadcasted_iota(jnp.int32, (t, G_DKP), 0)
        sub = rowv % cs
        lag = row - col
        kbuf[0:cs, :] = jnp.zeros((cs, G_DKP), F32)
        bbuf[0:cs, :] = jnp.zeros((cs, G_DKP), F32)
        for h in range(G_HEADS):
            qs, kk, b = operands(h)
            qparts, kparts = [], []
            for i in range(1, t // cs):
                r = b[i * cs - 1:i * cs, :]
                inblk = (rowv >= i * cs) & (rowv < (i + 1) * cs)
                qparts.append((qs * jnp.exp(jnp.where(inblk, b - r, NEG))).astype(BF16))
                kparts.append((kk * jnp.exp(jnp.where(rowv < i * cs, r - b, NEG))).astype(BF16))
            attn = _dot_nt(jnp.concatenate(qparts, axis=1), jnp.concatenate(kparts, axis=1))
            kbuf[cs:cs + t, :] = kk
            bbuf[cs:cs + t, :] = b
            for j in range(cs):
                ksh = kbuf[cs - j:cs - j + t, :]
                bsh = bbuf[cs - j:cs - j + t, :]
                e = jnp.exp(jnp.where(sub >= j, b - bsh, NEG))
                dj = jnp.sum(qs * ksh * e, axis=1, keepdims=True)
                attn = attn + jnp.where(lag == j, dj, 0.0)
            attn_scr[h] = attn

    def tail(h):
        slv = slice(h * G_DVP, (h + 1) * G_DVP)
        qs, kk, b = operands(h)
        v = v_ref[:, slv]
        blast = b[t - 1:t, :]
        st = st_scr[h]
        o = (_dot_nt((qs * jnp.exp(b)).astype(BF16), st.astype(BF16))
             + _dot(attn_scr[h].astype(BF16), v))
        kdec = (kk * jnp.exp(blast - b)).astype(BF16)
        st_scr[h] = jnp.exp(blast) * st + _dot_tn(v, kdec)

        sc = slice(h * G_DV, (h + 1) * G_DV)
        var = jnp.sum(o * o, axis=1, keepdims=True) * (1.0 / G_DV)
        hb = (o * lax.rsqrt(var + EPS))[:, :G_DV] * nw_ref[:, sc]
        out_ref[:, sc] = (hb * z_ref[:, sc].astype(F32)).astype(BF16)

    return [functools.partial(tail, h) for h in range(G_HEADS)]


def _mixers_kernel(safe_ref, mq_ref, mk_ref, mv_ref, moz_ref, gmb_ref, gmt_ref, mnw_ref,
                   gq_ref, gk_ref, gv_ref, gz_ref, gb_ref, gnw_ref, ya_ref, yb_ref,
                   c_scr, m_scr, kbuf, bbuf, attn_scr, st_scr, *, t, cs):
    @pl.when(pl.program_id(1) == 0)
    def _():
        c_scr[...] = jnp.zeros_like(c_scr)
        m_scr[...] = jnp.zeros_like(m_scr)
        st_scr[...] = jnp.zeros_like(st_scr)

    safe = safe_ref[pl.program_id(0) * pl.num_programs(1) + pl.program_id(1)] != 0
    gla_heads = _gla_body(safe, gq_ref, gk_ref, gv_ref, gz_ref, gb_ref, gnw_ref, yb_ref, kbuf, bbuf,
                          attn_scr, st_scr, t=t, cs=cs)
    mlstm_heads = _mlstm_body(mq_ref, mk_ref, mv_ref, moz_ref, gmb_ref, gmt_ref, mnw_ref, ya_ref,
                              c_scr, m_scr, tc=t)
    for head in gla_heads + mlstm_heads:
        head()


def _mixers(mq, mk, mv, moz, gmb, gmt, gq, gk, gv, gz, gcum, l, m_nw, g_nw, batch, seq, t, cs):
    n = batch * seq
    nt = seq // t

    def blk(width):
        return pl.BlockSpec((t, width), lambda b, c: (b * nt + c, 0))

    safe = (jnp.min(gcum.reshape(n // t, t, G_KWP)[:, t - 1, :], axis=-1) >= G_SAFE_LOG_DECAY).astype(jnp.int32)
    return pl.pallas_call(
        functools.partial(_mixers_kernel, t=t, cs=cs),
        grid=(batch, nt),
        in_specs=[pl.BlockSpec(memory_space=pltpu.SMEM),
                  blk(M_WP), blk(M_WP), blk(M_WP), blk(M_W), blk(2 * M_HEADS * LANES),
                  pl.BlockSpec((LANES, t), lambda b, c: (0, b * nt + c)), _of_layer(l, (1, M_W)),
                  blk(G_KWP), blk(G_KWP), blk(G_VWP), blk(G_VW), blk(G_KWP), _of_layer(l, (1, G_VW))],
        out_specs=[blk(M_W), blk(G_VW)],
        out_shape=[jax.ShapeDtypeStruct((n, M_W), BF16), jax.ShapeDtypeStruct((n, G_VW), BF16)],
        scratch_shapes=[pltpu.VMEM((M_HEADS, M_HDP, M_HDP + LANES), F32),
                        pltpu.VMEM((M_HEADS, SUBLANES, LANES), F32),
                        pltpu.VMEM((t + cs, G_DKP), F32), pltpu.VMEM((t + cs, G_DKP), F32),
                        pltpu.VMEM((G_HEADS, t, t), F32),
                        pltpu.VMEM((G_HEADS, G_DVP, G_DKP), F32)],
        compiler_params=_params("arbitrary", "arbitrary"),
        name="mixers",
    )(safe, mq, mk, mv, moz, gmb, gmt, m_nw, gq, gk, gv, gz, gcum, g_nw)


def _gelu_tanh(x):
    return 0.5 * x * (1.0 + jnp.tanh(math.sqrt(2.0 / math.pi) * (x + 0.044715 * (x * x * x))))


def _s5_kernel(u_ref, kms_ref, mo_ref, pin_ref, pt_ref, d_ref, y_ref, carry, xs, *, nch):
    @pl.when(pl.program_id(2) == 0)
    def _():
        carry[...] = jnp.zeros_like(carry)

    qw = S_Q * LANES
    ngrp = nch // S_SUB
    ltiles = S_SW // LANES
    ucat = jnp.concatenate([u_ref[0, pl.ds(s, nch, stride=S_Q), :] for s in range(S_Q)],
                           axis=1).astype(BF16)
    r = _dot(ucat, kms_ref[0])
    y = r[:, :qw]
    xr = r[:, qw:qw + S_SW]
    xi = r[:, qw + S_SW:]

    def axpy(xr, xi, pr, pi, sr, si):
        return xr + (pr * sr - pi * si), xi + (pr * si + pi * sr)

    def scan(xr, xi, p_ref, pos, length):
        d = 1
        while d < length:
            keep = pos >= d
            sr = jnp.where(keep, pltpu.roll(xr, d, 0), 0.0)
            si = jnp.where(keep, pltpu.roll(xi, d, 0), 0.0)
            xr, xi = axpy(xr, xi, p_ref[0, d - 1:d, :S_SW], p_ref[0, d - 1:d, S_SW:], sr, si)
            d *= 2
        return xr, xi

    def slabs(part, arr):
        return [(part * ltiles + j, arr[:, j * LANES:(j + 1) * LANES]) for j in range(ltiles)]

    rowi = lax.broadcasted_iota(jnp.int32, (nch, S_SW), 0)
    xr, xi = scan(xr, xi, pin_ref, rowi % S_SUB, S_SUB)
    for part, arr in ((0, xr), (1, xi)):
        for j, piece in slabs(part, arr):
            xs[j] = piece

    def group_ends(part):
        return jnp.concatenate([xs[part * ltiles + j, pl.ds(S_SUB - 1, ngrp, stride=S_SUB), :]
                                for j in range(ltiles)], axis=1)

    rowg = lax.broadcasted_iota(jnp.int32, (ngrp, S_SW), 0)
    er, ei = scan(group_ends(0), group_ends(1), pt_ref, rowg, ngrp)
    cr = carry[0:1, :S_SW]
    ci = carry[0:1, S_SW:]
    er, ei = axpy(er, ei, pt_ref[0, :, :S_SW], pt_ref[0, :, S_SW:], cr, ci)
    first = rowg == 0
    gr = jnp.where(first, cr, pltpu.roll(er, 1, 0))
    gi = jnp.where(first, ci, pltpu.roll(ei, 1, 0))
    carry[0:1, :S_SW] = er[ngrp - 1:ngrp, :]
    carry[0:1, S_SW:] = ei[ngrp - 1:ngrp, :]
    for part, arr in ((0, gr), (1, gi)):
        for j, piece in slabs(part, arr):
            for k in range(S_SUB):
                xs[j, pl.ds(k, ngrp, stride=S_SUB), :] = piece
    sr = jnp.concatenate([xs[j] for j in range(ltiles)], axis=1)
    si = jnp.concatenate([xs[ltiles + j] for j in range(ltiles)], axis=1)
    pwr = jnp.concatenate([pin_ref[0, :, :S_SW]] * ngrp, axis=0)
    pwi = jnp.concatenate([pin_ref[0, :, S_SW:]] * ngrp, axis=0)
    xr, xi = axpy(xr, xi, pwr, pwi, sr, si)

    firstc = rowi == 0
    xpr = jnp.where(firstc, cr, pltpu.roll(xr, 1, 0))
    xpi = jnp.where(firstc, ci, pltpu.roll(xi, 1, 0))
    y = y + _dot(jnp.concatenate([xpr, xpi], axis=1).astype(BF16), mo_ref[0])
    for t in range(S_Q):
        y_ref[0, pl.ds(t, nch, stride=S_Q), :] = y[:, t * LANES:(t + 1) * LANES]
    y_ref[0] = _gelu_tanh(y_ref[0] + d_ref[0] * u_ref[0])


def _s5(cu, l, kms, mo, pin, pt, dskip, batch, seq, tm):
    n = batch * seq
    nt = seq // tm
    nch = tm // S_Q
    qw = S_Q * LANES

    def table(rows, cols):
        return pl.BlockSpec((None, 1, rows, cols), lambda j, b, t: (l, j, 0, 0))

    return pl.pallas_call(
        functools.partial(_s5_kernel, nch=nch),
        grid=(S_BLOCKS, batch, nt),
        in_specs=[pl.BlockSpec((1, tm, LANES), lambda j, b, t: (j, b * nt + t, 0)),
                  table(qw, qw + 2 * S_SW), table(2 * S_SW, qw), table(S_SUB, 2 * S_SW),
                  table(nch // S_SUB, 2 * S_SW), table(1, LANES)],
        out_specs=pl.BlockSpec((1, tm, LANES), lambda j, b, t: (j, b * nt + t, 0)),
        out_shape=jax.ShapeDtypeStruct((S_BLOCKS, n, LANES), F32),
        scratch_shapes=[pltpu.VMEM((SUBLANES, 2 * S_SW), F32),
                        pltpu.VMEM((2 * S_SW // LANES, nch, LANES), F32)],
        compiler_params=_params("arbitrary", "arbitrary", "arbitrary"),
        name="s5",
    )(cu, kms, mo, pin, pt, dskip)


def _s5_expand_kernel(msr_ref, msi_ref, mor_ref, moi_ref, kms_ref, mo_ref):
    q = S_Q
    qw = q * LANES

    def embed(m, rows, width):
        shape = (width, S_GPB * width)
        tile = (lax.broadcasted_iota(jnp.int32, shape, 1) % width
                == lax.broadcasted_iota(jnp.int32, shape, 0)).astype(BF16)
        oshape = (S_GPB * rows, S_GPB * width)
        diag = (lax.broadcasted_iota(jnp.int32, oshape, 0) // rows
                == lax.broadcasted_iota(jnp.int32, oshape, 1) // width)
        return jnp.where(diag, _dot(m, tile), 0.0).astype(BF16)

    ms = [(embed(msr_ref[0, 0, s], S_GROUP, S_STATE), embed(msi_ref[0, 0, s], S_GROUP, S_STATE))
          for s in range(q)]
    mo = [(embed(mor_ref[0, 0, tau], S_STATE, S_GROUP), embed(moi_ref[0, 0, tau], S_STATE, S_GROUP))
          for tau in range(q + 1)]
    bbr, bbi = ms[q - 1]
    kd = [(_dot(bbr, mo[tau][0]) + _dot(bbi, mo[tau][1])).astype(BF16) for tau in range(q)]
    zero = jnp.zeros((LANES, LANES), BF16)
    for s in range(q):
        rows = slice(s * LANES, (s + 1) * LANES)
        for t in range(q):
            kms_ref[0, 0, rows, t * LANES:(t + 1) * LANES] = kd[t - s] if t >= s else zero
        kms_ref[0, 0, rows, qw:qw + S_SW] = ms[s][0]
        kms_ref[0, 0, rows, qw + S_SW:] = ms[s][1]
    for t in range(q):
        cols = slice(t * LANES, (t + 1) * LANES)
        mo_ref[0, 0, 0:S_SW, cols] = mo[t + 1][0]
        mo_ref[0, 0, S_SW:, cols] = mo[t + 1][1]


def _s5_expand(msr, msi, mor, moi):
    depth = msr.shape[0]
    qw = S_Q * LANES

    def blk(a):
        return pl.BlockSpec((1, 1) + a.shape[2:], lambda d, j: (d, j, 0, 0, 0))

    return pl.pallas_call(
        _s5_expand_kernel, grid=(depth, S_BLOCKS),
        in_specs=[blk(a) for a in (msr, msi, mor, moi)],
        out_specs=[pl.BlockSpec((1, 1, qw, qw + 2 * S_SW), lambda d, j: (d, j, 0, 0)),
                   pl.BlockSpec((1, 1, 2 * S_SW, qw), lambda d, j: (d, j, 0, 0))],
        out_shape=[jax.ShapeDtypeStruct((depth, S_BLOCKS, qw, qw + 2 * S_SW), BF16),
                   jax.ShapeDtypeStruct((depth, S_BLOCKS, 2 * S_SW, qw), BF16)],
        compiler_params=_params("arbitrary", "arbitrary"), name="s5_expand",
    )(msr, msi, mor, moi)


def _s5_tables(lam_re, lam_im, log_dt, b_re, b_im, c_re, c_im, nch):
    q = S_Q
    dep = lam_re.shape[0]
    lr = jnp.minimum(lam_re.astype(F32), -1e-4)
    li = lam_im.astype(F32)
    dt = jnp.exp(log_dt.astype(F32))[..., None]
    mag = jnp.exp(lr * dt)
    ab_re = mag * jnp.cos(li * dt)
    ab_im = mag * jnp.sin(li * dt)
    nr = ab_re - 1.0
    den = lr * lr + li * li
    coef_re = ((nr * lr + ab_im * li) / den)[:, :, None, :]
    coef_im = ((ab_im * lr - nr * li) / den)[:, :, None, :]
    brt = jnp.swapaxes(b_re.astype(F32), -1, -2)
    bit = jnp.swapaxes(b_im.astype(F32), -1, -2)
    bb_re = coef_re * brt - coef_im * bit
    bb_im = coef_re * bit + coef_im * brt

    def apow(nvals):
        e = jnp.asarray(nvals, F32)[:, None, None, None]
        m = jnp.exp(e * (lr * dt))
        ang = e * (li * dt)
        return m * jnp.cos(ang), m * jnp.sin(ang)

    def per_block(m, rows):
        m = m.reshape(m.shape[0], dep, S_BLOCKS, S_GPB * rows, m.shape[-1])
        return m.transpose(1, 2, 0, 3, 4).astype(BF16)

    ar, ai = apow(list(range(q + 1)))
    crt = jnp.swapaxes(c_re.astype(F32), -1, -2)
    cit = jnp.swapaxes(c_im.astype(F32), -1, -2)
    ca_re = crt[None] * ar[..., None] - cit[None] * ai[..., None]
    ca_im = crt[None] * ai[..., None] + cit[None] * ar[..., None]
    pw_r = ar[:q][::-1][:, :, :, None, :]
    pw_i = ai[:q][::-1][:, :, :, None, :]
    kms, mo = _s5_expand(per_block(pw_r * bb_re - pw_i * bb_im, S_GROUP),
                         per_block(pw_r * bb_im + pw_i * bb_re, S_GROUP),
                         per_block(ca_re, S_STATE), per_block(-ca_im, S_STATE))
    def power_table(nvals):
        pr, pi = apow(nvals)
        pr = pr.reshape(len(nvals), dep, S_BLOCKS, S_SW).transpose(1, 2, 0, 3)
        pi = pi.reshape(len(nvals), dep, S_BLOCKS, S_SW).transpose(1, 2, 0, 3)
        return jnp.concatenate([pr, pi], axis=3)

    pin = power_table([q * (r + 1) for r in range(S_SUB)])
    pt = power_table([q * S_SUB * (k + 1) for k in range(nch // S_SUB)])
    return kms, mo, pin, pt


def _merge_kernel(ya_ref, yb_ref, yc_ref, cz_ref, g_ref, x_ref, wa_ref, wb_ref, wglu_ref, wc_ref,
                  wo_ref, nw_ref, *out_refs, final):
    yc0 = jnp.concatenate([yc_ref[j] for j in range(S_BLOCKS)], axis=1)
    glu = _dot(yc0.astype(BF16), wglu_ref[...])
    yc = yc0 * _sigmoid(glu) * cz_ref[...].astype(F32)
    merged = (g_ref[:, 0:D_MODEL].astype(F32) * _dot(ya_ref[...], wa_ref[...])
              + g_ref[:, D_MODEL:2 * D_MODEL].astype(F32) * _dot(yb_ref[...], wb_ref[...])
              + g_ref[:, 2 * D_MODEL:].astype(F32) * _dot(yc.astype(BF16), wc_ref[...]))
    out = x_ref[...] + _dot(merged.astype(BF16), wo_ref[...])
    ms = jnp.mean(out * out, axis=-1, keepdims=True)
    normed = out * lax.rsqrt(ms + EPS) * nw_ref[...]
    if final:
        out_refs[0][...] = normed
    else:
        out_refs[0][...] = out
        out_refs[1][...] = normed.astype(BF16)


def _merge(ya, yb, yc, cz, g, x2, l, wa, wb, wglu, wc, wo, nw, tm, final):
    n = x2.shape[0]
    x_out = jax.ShapeDtypeStruct((n, D_MODEL), F32)
    hn_out = jax.ShapeDtypeStruct((n, D_MODEL), BF16)
    return pl.pallas_call(
        functools.partial(_merge_kernel, final=final),
        grid=(n // tm,),
        in_specs=[_rows(tm, M_W), _rows(tm, G_VW),
                  pl.BlockSpec((S_BLOCKS, tm, LANES), lambda i: (0, i, 0)),
                  _rows(tm, S_W), _rows(tm, 3 * D_MODEL), _rows(tm, D_MODEL),
                  _of_layer(l, (M_W, D_MODEL)), _of_layer(l, (G_VW, D_MODEL)), _of_layer(l, (S_W, S_W)),
                  _of_layer(l, (S_W, D_MODEL)), _of_layer(l, (D_MODEL, D_MODEL)), _whole((1, D_MODEL))],
        out_specs=[_rows(tm, D_MODEL)] if final else [_rows(tm, D_MODEL), _rows(tm, D_MODEL)],
        out_shape=[x_out] if final else [x_out, hn_out],
        compiler_params=_params("arbitrary"),
        name="merge",
    )(ya, yb, yc, cz, g, x2, wa, wb, wglu, wc, wo, nw)


def _pad_heads(w, heads, d, dp):
    zeros = jnp.zeros(w.shape[:-1] + (dp - d,), w.dtype)
    parts = []
    for h in range(heads):
        parts += [w[..., h * d:(h + 1) * d], zeros]
    return jnp.concatenate(parts, axis=-1)


def _prepare(p, nch):
    depth = p["w_in"].shape[0]
    w_mlstm, w_plain, w_gate, w_small = _wprep(jnp.swapaxes(p["w_in"], 1, 2), LANES)
    gkp = functools.partial(_pad_heads, heads=G_HEADS, d=G_DK, dp=G_DKP)
    kms, mo, pin, pt = _s5_tables(p["s5_lam_re"], p["s5_lam_im"], p["s5_log_dt"], p["s5_B_re"], p["s5_B_im"],
                               p["s5_C_re"], p["s5_C_im"], nch)
    gate_b = p["mlstm_gate_b"]
    return dict(
        w_mlstm=w_mlstm, w_plain=w_plain, w_gate=w_gate, w_small=w_small,
        cq=p["mlstm_conv"][:, :, :M_W], ck=p["mlstm_conv"][:, :, M_W:],
        gb=jnp.concatenate([gate_b[:, 0], gate_b[:, 1],
                            jnp.zeros((depth, LANES - 2 * M_HEADS), F32)], axis=1)[:, None, :],
        m_nw=p["mlstm_norm"][:, None, :],
        wal=jnp.concatenate([jnp.zeros((depth, 2 * M_HEADS, G_KWP), F32), gkp(p["gla_w_alpha"]),
                             jnp.zeros((depth, LANES - 2 * M_HEADS - G_RANK, G_KWP), F32)],
                            axis=1).astype(BF16),
        bal=gkp(p["gla_b_alpha"])[:, None, :],
        g_nw=p["gla_norm"][:, None, :],
        kms=kms, mo=mo, pin=pin, pt=pt,
        dskip=p["s5_D"].reshape(depth, S_BLOCKS, 1, LANES),
        wa=p["w_branch_mlstm"].astype(BF16), wb=p["w_branch_gla"].astype(BF16),
        wglu=p["s5_w_glu"].astype(BF16), wc=p["w_branch_s5"].astype(BF16), wo=p["w_out"].astype(BF16),
    )


def _tiles(batch, seq):
    return dict(proj=min(1024, seq), chunk=min(256, seq), g_sub=16,
                s5=min(4096, seq), merge=min(512, batch * seq))


def _layer(x2, hn, batch, seq, l, w, this_norm, next_norm, final, tl):
    tm = tl["proj"]
    nw = this_norm.reshape(1, D_MODEL)
    if hn is None:
        gz, czs, gs, hn = _proj_gate(x2, nw, l, w["w_gate"], tm, True)
    else:
        gz, czs, gs = _proj_gate(hn, nw, l, w["w_gate"], tm, False)
    q, k, oz = _proj_mlstm(hn, l, w["w_mlstm"], w["cq"], w["ck"], seq, tm)
    mv, gv, gq, gk, cu_s, gm, gmt, gcum = _proj_va(hn, l, w["w_plain"], w["w_small"], w["gb"], w["wal"],
                                                   w["bal"], tm, tl["chunk"])
    ya, yb = _mixers(q, k, mv, oz, gm, gmt, gq, gk, gv, gz, gcum, l, w["m_nw"], w["g_nw"], batch, seq,
                     tl["chunk"], tl["g_sub"])
    yc = _s5(cu_s, l, w["kms"], w["mo"], w["pin"], w["pt"], w["dskip"], batch, seq, tl["s5"])
    return _merge(ya, yb, yc, czs, gs, x2, l, w["wa"], w["wb"], w["wglu"], w["wc"], w["wo"],
                  next_norm.reshape(1, D_MODEL), tl["merge"], final)


def kernel(x, norm_w, w_in, mlstm_conv, mlstm_gate_b, mlstm_norm, gla_w_alpha, gla_b_alpha, gla_norm,
           s5_lam_re, s5_lam_im, s5_log_dt, s5_B_re, s5_B_im, s5_C_re, s5_C_im, s5_D, s5_w_glu,
           w_branch_mlstm, w_branch_gla, w_branch_s5, w_out, final_norm):
    batch, seq, _ = x.shape
    depth = norm_w.shape[0]
    tl = _tiles(batch, seq)
    prep = _prepare(dict(w_in=w_in, mlstm_conv=mlstm_conv, mlstm_gate_b=mlstm_gate_b, mlstm_norm=mlstm_norm,
                         gla_w_alpha=gla_w_alpha, gla_b_alpha=gla_b_alpha, gla_norm=gla_norm,
                         s5_lam_re=s5_lam_re, s5_lam_im=s5_lam_im, s5_log_dt=s5_log_dt, s5_B_re=s5_B_re,
                         s5_B_im=s5_B_im, s5_C_re=s5_C_re, s5_C_im=s5_C_im, s5_D=s5_D, s5_w_glu=s5_w_glu,
                         w_branch_mlstm=w_branch_mlstm, w_branch_gla=w_branch_gla,
                         w_branch_s5=w_branch_s5, w_out=w_out), tl["s5"] // S_Q)
    x2 = x.reshape(batch * seq, D_MODEL)
    hn = None
    for l in range(depth):
        final = l == depth - 1
        res = _layer(x2, hn, batch, seq, l, prep, norm_w[l], final_norm if final else norm_w[l + 1], final, tl)
        if final:
            x2 = res[0]
        else:
            x2, hn = res
    return x2.reshape(batch, seq, D_MODEL)
```

```python
import functools
import math

import jax
import jax.numpy as jnp
from jax import lax
from jax.experimental import pallas as pl
from jax.experimental.pallas import tpu as pltpu

F32 = jnp.float32
BF16 = jnp.bfloat16

EPS = 1e-6
D_MODEL = 1024
LANES = 128
SUBLANES = 8
V7X_VMEM_BYTES = 64 * 1024 * 1024
M_HEADS = 4
M_HD = 192
M_HDP = 256
M_W = M_HEADS * M_HD
M_WP = M_HEADS * M_HDP
M_CONV = 4
G_HEADS = 4
G_DK = 96
G_DKP = 128
G_DV = 192
G_DVP = 256
G_KW = G_HEADS * G_DK
G_KWP = G_HEADS * G_DKP
G_VW = G_HEADS * G_DV
G_VWP = G_HEADS * G_DVP
G_RANK = 16
G_TAU = 16.0
S_W = 512
S_GROUP = 16
S_STATE = 64
S_BLOCKS = S_W // LANES
S_GPB = LANES // S_GROUP
S_SW = S_GPB * S_STATE
S_Q = 8
S_SUB = 8
S5_PARTS = 2

IN_SIZES = (M_W, M_W, M_W, M_W, M_HEADS, M_HEADS, M_W,
            G_KW, G_KW, G_VW, G_RANK, G_VW,
            S_W, S_W, 3 * D_MODEL)
SMALL_W = S_W + LANES
CW = 512

NEG = -1e30
G_SAFE_LOG_DECAY = -60.0
VMEM_LIMIT = V7X_VMEM_BYTES * 7 // 8


def _sigmoid(x):
    return 0.5 * jnp.tanh(0.5 * x) + 0.5


def _silu(x):
    h = 0.5 * x
    return h * jnp.tanh(h) + h


def _log_sigmoid(x):
    return jnp.minimum(x, 0.0) - jnp.log1p(jnp.exp(-jnp.abs(x)))


def _dot(a, b):
    return jnp.dot(a, b, preferred_element_type=F32)


def _dot_nt(a, b):
    return lax.dot_general(a, b, (((1,), (1,)), ((), ())), preferred_element_type=F32)


def _dot_tn(a, b):
    return lax.dot_general(a, b, (((0,), (0,)), ((), ())), preferred_element_type=F32)


def _params(*sem):
    return pltpu.CompilerParams(dimension_semantics=sem, vmem_limit_bytes=VMEM_LIMIT)


def _rows(tm, width):
    return pl.BlockSpec((tm, width), lambda i: (i, 0))


def _whole(shape):
    return pl.BlockSpec(shape, lambda i: (0,) * len(shape))


def _of_layer(l, shape):
    return pl.BlockSpec((None,) + shape, lambda *_: (l,) + (0,) * len(shape))


IN_NAMES = ("aq", "ak", "av", "ao", "ai", "af", "az", "bq", "bk", "bv", "ba", "bz", "cu", "cz", "g")
W_OFF = {name: sum(IN_SIZES[:i]) for i, name in enumerate(IN_NAMES)}
IN_WIDTH = sum(IN_SIZES)
W_MLSTM, W_PLAIN, W_GATE = 4 * M_W, M_W + G_VW + 2 * G_KW, G_VW + S_W + 3 * D_MODEL


def _wprep_kernel(w_ref, wm_ref, wp_ref, wg_ref, ws_ref):
    kb = w_ref.shape[2]

    def put(dst, dst_off, parts, width):
        xs = [w_ref[0, off:off + rows, :] for off, rows in parts]
        used = sum(rows for _, rows in parts)
        if used < width:
            xs.append(jnp.zeros((width - used, kb), F32))
        x = xs[0] if len(xs) == 1 else jnp.concatenate(xs, axis=0)
        dst[0, :, dst_off:dst_off + width] = x.T.astype(BF16)

    def group(dst, names):
        off = 0
        for name in names:
            width = IN_SIZES[IN_NAMES.index(name)]
            for c in range(0, width, LANES):
                put(dst, off + c, [(W_OFF[name] + c, LANES)], LANES)
            off += width

    group(wm_ref, ("aq", "ak", "ao", "az"))
    group(wp_ref, ("av", "bv", "bq", "bk"))
    group(wg_ref, ("bz", "cz", "g"))
    group(ws_ref, ("cu",))
    put(ws_ref, S_W, [(W_OFF["ai"], 2 * M_HEADS), (W_OFF["ba"], G_RANK)], LANES)


def _wprep(w_t, kb):
    depth = w_t.shape[0]
    widths = (W_MLSTM, W_PLAIN, W_GATE, SMALL_W)
    return pl.pallas_call(
        _wprep_kernel, grid=(depth, D_MODEL // kb),
        in_specs=[pl.BlockSpec((1, IN_WIDTH, kb), lambda l, i: (l, 0, i))],
        out_specs=[pl.BlockSpec((1, kb, wd), lambda l, i: (l, i, 0)) for wd in widths],
        out_shape=[jax.ShapeDtypeStruct((depth, D_MODEL, wd), BF16) for wd in widths],
        compiler_params=_params("arbitrary", "arbitrary"), name="wprep",
    )(w_t)


def _scatter_heads(dst, y, heads, d, dp, fill):
    rows = y.shape[0]
    for h in range(heads):
        dst[:, h * dp:h * dp + d] = y[:, h * d:(h + 1) * d].astype(BF16)
        dst[:, h * dp + d:(h + 1) * dp] = jnp.full((rows, dp - d), fill, BF16)


def _proj_mlstm_kernel(hn_ref, w_ref, cq_ref, ck_ref, q_ref, k_ref, oz_ref, halo_x, halo_b, *, tm,
                       tiles_per_seq):
    @pl.when(pl.program_id(0) % tiles_per_seq == 0)
    def _():
        halo_x[...] = jnp.zeros_like(halo_x)
        halo_b[...] = jnp.zeros_like(halo_b)

    hn = hn_ref[...]
    row8 = lax.broadcasted_iota(jnp.int32, (SUBLANES, M_W), 0)

    def shift(x, prev8, d):
        rolled = pltpu.roll(x, d, 0)
        head = jnp.where(row8 < d, pltpu.roll(prev8, d, 0), rolled[0:SUBLANES])
        return jnp.concatenate([head, rolled[SUBLANES:]], axis=0)

    def conv_act(c, x, cw_ref, dst, scale):
        x1 = shift(x, halo_x[c], 1)
        pair = cw_ref[1:2, :] * x + cw_ref[0:1, :] * x1
        y = _silu(cw_ref[3:4, :] * x + cw_ref[2:3, :] * x1 + shift(pair, halo_b[c], 2))
        halo_x[c] = x[tm - SUBLANES:tm]
        halo_b[c] = pair[tm - SUBLANES:tm]
        _scatter_heads(dst, y if scale == 1.0 else y * scale, M_HEADS, M_HD, M_HDP, 0.0)

    xq = _dot(hn, w_ref[:, 0:M_W])
    xk = _dot(hn, w_ref[:, M_W:2 * M_W])
    conv_act(0, xq, cq_ref, q_ref, 1.0)
    o = _dot(hn, w_ref[:, 2 * M_W:3 * M_W])
    conv_act(1, xk, ck_ref, k_ref, M_HD ** -0.5)
    z = _dot(hn, w_ref[:, 3 * M_W:])
    oz_ref[...] = (_sigmoid(o) * _silu(z)).astype(BF16)


def _proj_mlstm(hn, l, w, cq, ck, seq, tm):
    n = hn.shape[0]
    padded = jax.ShapeDtypeStruct((n, M_WP), BF16)
    return pl.pallas_call(
        functools.partial(_proj_mlstm_kernel, tm=tm, tiles_per_seq=seq // tm),
        grid=(n // tm,),
        in_specs=[_rows(tm, D_MODEL), _of_layer(l, (D_MODEL, W_MLSTM)), _of_layer(l, (M_CONV, M_W)),
                  _of_layer(l, (M_CONV, M_W))],
        out_specs=[_rows(tm, M_WP), _rows(tm, M_WP), _rows(tm, M_W)],
        out_shape=[padded, padded, jax.ShapeDtypeStruct((n, M_W), BF16)],
        scratch_shapes=[pltpu.VMEM((2, SUBLANES, M_W), F32)] * 2,
        compiler_params=_params("arbitrary"), name="proj_mlstm",
    )(hn, w, cq, ck)


def _proj_gate_kernel(src_ref, nw_ref, w_ref, gz_ref, cz_ref, g_ref, *hn_out, normalize):
    if normalize:
        x = src_ref[...]
        ms = jnp.mean(x * x, axis=-1, keepdims=True)
        hn = (x * lax.rsqrt(ms + EPS) * nw_ref[...]).astype(BF16)
        hn_out[0][...] = hn
    else:
        hn = src_ref[...]
    gz_ref[...] = _silu(_dot(hn, w_ref[:, 0:G_VW])).astype(BF16)
    cz_ref[...] = _silu(_dot(hn, w_ref[:, G_VW:G_VW + S_W])).astype(BF16)
    off = G_VW + S_W
    for c in range(3 * D_MODEL // CW):
        r = _dot(hn, w_ref[:, off + c * CW:off + (c + 1) * CW])
        g_ref[:, c * CW:(c + 1) * CW] = _sigmoid(r).astype(BF16)


def _proj_gate(src, nw, l, w, tm, normalize):
    n = src.shape[0]
    widths = (G_VW, S_W, 3 * D_MODEL) + ((D_MODEL,) if normalize else ())
    return pl.pallas_call(
        functools.partial(_proj_gate_kernel, normalize=normalize), grid=(n // tm,),
        in_specs=[_rows(tm, D_MODEL), _whole((1, D_MODEL)), _of_layer(l, (D_MODEL, W_GATE))],
        out_specs=[_rows(tm, wd) for wd in widths],
        out_shape=[jax.ShapeDtypeStruct((n, wd), BF16) for wd in widths],
        compiler_params=_params("arbitrary"), name="proj_gate",
    )(src, nw, w)


def _split3(x):
    hi = x.astype(BF16)
    r1 = x - hi.astype(F32)
    mid = r1.astype(BF16)
    return hi, mid, (r1 - mid.astype(F32)).astype(BF16)


def _segment_cumsum(x, seg):
    rows, width = x.shape
    tril = (lax.broadcasted_iota(jnp.int32, (seg, seg), 1)
            <= lax.broadcasted_iota(jnp.int32, (seg, seg), 0)).astype(BF16)
    parts = jnp.concatenate(_split3(x), axis=1)
    out = []
    for s in range(rows // seg):
        acc = _dot(tril, parts[s * seg:(s + 1) * seg])
        out.append(acc[:, :width] + acc[:, width:2 * width] + acc[:, 2 * width:])
    return jnp.concatenate(out, axis=0)


def _proj_va_kernel(hn_ref, wp_ref, ws_ref, gb_ref, wal_ref, bal_ref,
                    mv_ref, gv_ref, gq_ref, gk_ref, cu_ref, gmb_ref, gmt_ref, gcum_ref, *, chunk):
    hn = hn_ref[...]
    r = _dot(hn, ws_ref[...])
    _scatter_heads(mv_ref, _dot(hn, wp_ref[:, 0:M_W]), M_HEADS, M_HD, M_HDP, 0.0)
    for j in range(S_BLOCKS):
        cu_ref[j] = r[:, j * LANES:(j + 1) * LANES]
    raw = r[:, S_W:]
    gates = raw + gb_ref[...]
    lane = lax.broadcasted_iota(jnp.int32, gates.shape, 1)
    fgate = (lane >= M_HEADS) & (lane < 2 * M_HEADS)
    za = _dot(raw.astype(BF16), wal_ref[...]) + bal_ref[...]
    gm = jnp.where(fgate, _segment_cumsum(_log_sigmoid(gates), chunk), gates)
    _scatter_heads(gv_ref, _dot(hn, wp_ref[:, M_W:M_W + G_VW]), G_HEADS, G_DV, G_DVP, 0.0)
    gmt_ref[...] = gm.T
    slots = 2 * M_HEADS * LANES
    spread = (lax.broadcasted_iota(jnp.int32, (3 * LANES, slots), 0) % LANES
              == lax.broadcasted_iota(jnp.int32, (3 * LANES, slots), 1) // LANES).astype(BF16)
    gmb_ref[...] = _dot(jnp.concatenate(_split3(gm), axis=1), spread)
    off = M_W + G_VW
    _scatter_heads(gq_ref, _dot(hn, wp_ref[:, off:off + G_KW]) * (G_DK ** -0.5), G_HEADS, G_DK, G_DKP, 0.0)
    gcum_ref[...] = _segment_cumsum(_log_sigmoid(za) * (1.0 / G_TAU), chunk)
    _scatter_heads(gk_ref, _dot(hn, wp_ref[:, off + G_KW:]), G_HEADS, G_DK, G_DKP, 0.0)


def _proj_va(hn, l, wp, ws, gb, wal, bal, tm, chunk):
    n = hn.shape[0]
    widths = (M_WP, G_VWP, G_KWP, G_KWP)
    return pl.pallas_call(
        functools.partial(_proj_va_kernel, chunk=chunk), grid=(n // tm,),
        in_specs=[_rows(tm, D_MODEL), _of_layer(l, (D_MODEL, W_PLAIN)), _of_layer(l, (D_MODEL, SMALL_W)),
                  _of_layer(l, (1, LANES)), _of_layer(l, (LANES, G_KWP)), _of_layer(l, (1, G_KWP))],
        out_specs=[_rows(tm, wd) for wd in widths] + [
            pl.BlockSpec((S_BLOCKS, tm, LANES), lambda i: (0, i, 0)), _rows(tm, 2 * M_HEADS * LANES),
            pl.BlockSpec((LANES, tm), lambda i: (0, i)), _rows(tm, G_KWP)],
        out_shape=[jax.ShapeDtypeStruct((n, wd), BF16) for wd in widths] + [
            jax.ShapeDtypeStruct((S_BLOCKS, n, LANES), F32),
            jax.ShapeDtypeStruct((n, 2 * M_HEADS * LANES), F32),
            jax.ShapeDtypeStruct((LANES, n), F32),
            jax.ShapeDtypeStruct((n, G_KWP), F32)],
        compiler_params=_params("arbitrary"), name="proj_va",
    )(hn, wp, ws, gb, wal, bal)


def _mlstm_body(q_ref, k_ref, v_ref, oz_ref, gmb_ref, gmt_ref, nw_ref, out_ref, c_scr, m_scr, *, tc):
    row = lax.broadcasted_iota(jnp.int32, (tc, tc), 0)
    col = lax.broadcasted_iota(jnp.int32, (tc, tc), 1)
    causal = col <= row
    real = lax.broadcasted_iota(jnp.int32, (tc, M_HDP), 1) < M_HD
    ones_tile = jnp.ones((tc, LANES), BF16)

    def lanes(x, n):
        return jnp.concatenate([x] * n, axis=1)

    def head(h):
        sl = slice(h * M_HDP, (h + 1) * M_HDP)
        qb = q_ref[:, sl]
        kb = k_ref[:, sl]
        vext = jnp.concatenate([v_ref[:, sl], ones_tile], axis=1)
        i_b = gmb_ref[:, h * LANES:(h + 1) * LANES]
        bcum = gmb_ref[:, (M_HEADS + h) * LANES:(M_HEADS + h + 1) * LANES]
        g_row = gmt_ref[h:h + 1, :] - gmt_ref[M_HEADS + h:M_HEADS + h + 1, :]
        m_prev = m_scr[h, 0:1, :]

        inter = bcum + m_prev
        m_t = jnp.maximum(inter, bcum + jnp.max(jnp.where(causal, g_row, NEG), axis=1, keepdims=True))
        w_intra = jnp.exp(jnp.where(causal, g_row + lanes(bcum - m_t, tc // LANES), NEG))
        w_inter = jnp.exp(inter - m_t)
        s = _dot_nt(qb, kb) * w_intra
        c_mat = c_scr[h]
        num = _dot(s.astype(BF16), vext) + lanes(w_inter, 3) * _dot(qb, c_mat.astype(BF16))
        den = num[:, M_HDP:]
        rden = 1.0 / jnp.maximum(jnp.abs(den), jnp.exp(-m_t))
        num = num[:, :M_HDP]

        b_end = bcum[tc - 1:tc, :]
        to_end = b_end - bcum + i_b
        m_new = jnp.maximum(b_end + m_prev, jnp.max(to_end, axis=0, keepdims=True))
        wk = jnp.exp(to_end - m_new)
        w_prev = jnp.exp(b_end + m_prev - m_new)
        c_scr[h] = lanes(w_prev, 3) * c_mat + _dot_tn((lanes(wk, 2) * kb.astype(F32)).astype(BF16), vext)
        m_scr[h] = jnp.broadcast_to(m_new, (SUBLANES, LANES))

        sc = slice(h * M_HD, (h + 1) * M_HD)
        mean = jnp.sum(num, axis=1, keepdims=True) * (1.0 / M_HD)
        xc = jnp.where(real, num - mean, 0.0)
        var = jnp.sum(xc * xc, axis=1, keepdims=True) * (1.0 / M_HD)
        scale = rden * lax.rsqrt(rden * rden * var + EPS)
        ha = (xc * lanes(scale, 2))[:, :M_HD]
        out_ref[:, sc] = (ha * nw_ref[:, sc] * oz_ref[:, sc].astype(F32)).astype(BF16)

    return [functools.partial(head, h) for h in range(M_HEADS)]


def _gla_body(safe, q_ref, k_ref, v_ref, z_ref, b_ref, nw_ref,
              out_ref, kbuf, bbuf, attn_scr, st_scr, *, t, cs):
    row = lax.broadcasted_iota(jnp.int32, (t, t), 0)
    col = lax.broadcasted_iota(jnp.int32, (t, t), 1)
    causal = col <= row

    def operands(h):
        sl = slice(h * G_DKP, (h + 1) * G_DKP)
        return q_ref[:, sl].astype(F32), k_ref[:, sl].astype(F32), b_ref[:, sl]

    @pl.when(safe)
    def _():
        for h in range(G_HEADS):
            qs, kk, b = operands(h)
            a = _dot_nt((qs * jnp.exp(b)).astype(BF16), (kk * jnp.exp(-b)).astype(BF16))
            attn_scr[h] = jnp.where(causal, a, 0.0)

    @pl.when(jnp.logical_not(safe))
    def _():
        rowv = lax.broadcasted_iota(jnp.int32, (t, G_DKP), 0)
        sub = rowv % cs
        lag = row - col
        kbuf[0:cs, :] = jnp.zeros((cs, G_DKP), F32)
        bbuf[0:cs, :] = jnp.zeros((cs, G_DKP), F32)
        for h in range(G_HEADS):
            qs, kk, b = operands(h)
            qparts, kparts = [], []
            for i in range(1, t // cs):
                r = b[i * cs - 1:i * cs, :]
                inblk = (rowv >= i * cs) & (rowv < (i + 1) * cs)
                qparts.append((qs * jnp.exp(jnp.where(inblk, b - r, NEG))).astype(BF16))
                kparts.append((kk * jnp.exp(jnp.where(rowv < i * cs, r - b, NEG))).astype(BF16))
            attn = _dot_nt(jnp.concatenate(qparts, axis=1), jnp.concatenate(kparts, axis=1))
            kbuf[cs:cs + t, :] = kk
            bbuf[cs:cs + t, :] = b
            for j in range(cs):
                ksh = kbuf[cs - j:cs - j + t, :]
                bsh = bbuf[cs - j:cs - j + t, :]
                e = jnp.exp(jnp.where(sub >= j, b - bsh, NEG))
                dj = jnp.sum(qs * ksh * e, axis=1, keepdims=True)
                attn = attn + jnp.where(lag == j, dj, 0.0)
            attn_scr[h] = attn

    def tail(h):
        slv = slice(h * G_DVP, (h + 1) * G_DVP)
        qs, kk, b = operands(h)
        v = v_ref[:, slv]
        blast = b[t - 1:t, :]
        st = st_scr[h]
        o = (_dot_nt((qs * jnp.exp(b)).astype(BF16), st.astype(BF16))
             + _dot(attn_scr[h].astype(BF16), v))
        kdec = (kk * jnp.exp(blast - b)).astype(BF16)
        st_scr[h] = jnp.exp(blast) * st + _dot_tn(v, kdec)

        sc = slice(h * G_DV, (h + 1) * G_DV)
        var = jnp.sum(o * o, axis=1, keepdims=True) * (1.0 / G_DV)
        hb = (o * lax.rsqrt(var + EPS))[:, :G_DV] * nw_ref[:, sc]
        out_ref[:, sc] = (hb * z_ref[:, sc].astype(F32)).astype(BF16)

    return [functools.partial(tail, h) for h in range(G_HEADS)]


def _mixers_kernel(safe_ref, mq_ref, mk_ref, mv_ref, moz_ref, gmb_ref, gmt_ref, mnw_ref,
                   gq_ref, gk_ref, gv_ref, gz_ref, gb_ref, gnw_ref, ya_ref, yb_ref,
                   c_scr, m_scr, kbuf, bbuf, attn_scr, st_scr, *, t, cs):
    @pl.when(pl.program_id(1) == 0)
    def _():
        c_scr[...] = jnp.zeros_like(c_scr)
        m_scr[...] = jnp.zeros_like(m_scr)
        st_scr[...] = jnp.zeros_like(st_scr)

    safe = safe_ref[pl.program_id(0) * pl.num_programs(1) + pl.program_id(1)] != 0
    gla_heads = _gla_body(safe, gq_ref, gk_ref, gv_ref, gz_ref, gb_ref, gnw_ref, yb_ref, kbuf, bbuf,
                          attn_scr, st_scr, t=t, cs=cs)
    mlstm_heads = _mlstm_body(mq_ref, mk_ref, mv_ref, moz_ref, gmb_ref, gmt_ref, mnw_ref, ya_ref,
                              c_scr, m_scr, tc=t)
    for head in gla_heads + mlstm_heads:
        head()


def _mixers(mq, mk, mv, moz, gmb, gmt, gq, gk, gv, gz, gcum, l, m_nw, g_nw, batch, seq, t, cs):
    n = batch * seq
    nt = seq // t

    def blk(width):
        return pl.BlockSpec((t, width), lambda b, c: (b * nt + c, 0))

    safe = (jnp.min(gcum.reshape(n // t, t, G_KWP)[:, t - 1, :], axis=-1) >= G_SAFE_LOG_DECAY).astype(jnp.int32)
    return pl.pallas_call(
        functools.partial(_mixers_kernel, t=t, cs=cs),
        grid=(batch, nt),
        in_specs=[pl.BlockSpec(memory_space=pltpu.SMEM),
                  blk(M_WP), blk(M_WP), blk(M_WP), blk(M_W), blk(2 * M_HEADS * LANES),
                  pl.BlockSpec((LANES, t), lambda b, c: (0, b * nt + c)), _of_layer(l, (1, M_W)),
                  blk(G_KWP), blk(G_KWP), blk(G_VWP), blk(G_VW), blk(G_KWP), _of_layer(l, (1, G_VW))],
        out_specs=[blk(M_W), blk(G_VW)],
        out_shape=[jax.ShapeDtypeStruct((n, M_W), BF16), jax.ShapeDtypeStruct((n, G_VW), BF16)],
        scratch_shapes=[pltpu.VMEM((M_HEADS, M_HDP, M_HDP + LANES), F32),
                        pltpu.VMEM((M_HEADS, SUBLANES, LANES), F32),
                        pltpu.VMEM((t + cs, G_DKP), F32), pltpu.VMEM((t + cs, G_DKP), F32),
                        pltpu.VMEM((G_HEADS, t, t), F32),
                        pltpu.VMEM((G_HEADS, G_DVP, G_DKP), F32)],
        compiler_params=_params("arbitrary", "arbitrary"),
        name="mixers",
    )(safe, mq, mk, mv, moz, gmb, gmt, m_nw, gq, gk, gv, gz, gcum, g_nw)


def _gelu_tanh(x):
    return 0.5 * x * (1.0 + jnp.tanh(math.sqrt(2.0 / math.pi) * (x + 0.044715 * (x * x * x))))


def _s5_kernel(u_ref, kms_ref, mo_ref, pin_ref, pt_ref, d_ref, y_ref, carry, xs, *, nch):
    @pl.when(pl.program_id(2) == 0)
    def _():
        carry[...] = jnp.zeros_like(carry)

    qw = S_Q * LANES
    ltiles = S_SW // LANES
    parts = S5_PARTS if nch % (S5_PARTS * S_SUB) == 0 else 1
    pch = nch // parts
    ngrp = pch // S_SUB
    ucat = jnp.concatenate([u_ref[0, pl.ds(s, nch, stride=S_Q), :] for s in range(S_Q)],
                           axis=1).astype(BF16)
    x = _dot(ucat, kms_ref[0, :, qw:])
    y_in = _dot(ucat, kms_ref[0, :, :qw])

    def axpy(xr, xi, pr, pi, sr, si):
        return xr + (pr * sr - pi * si), xi + (pr * si + pi * sr)

    def scan(xr, xi, p_ref, pos, length):
        d = 1
        while d < length:
            keep = pos >= d
            sr = jnp.where(keep, pltpu.roll(xr, d, 0), 0.0)
            si = jnp.where(keep, pltpu.roll(xi, d, 0), 0.0)
            xr, xi = axpy(xr, xi, p_ref[0, d - 1:d, :S_SW], p_ref[0, d - 1:d, S_SW:], sr, si)
            d *= 2
        return xr, xi

    def slabs(part, arr):
        return [(part * ltiles + j, arr[:, j * LANES:(j + 1) * LANES]) for j in range(ltiles)]

    rowi = lax.broadcasted_iota(jnp.int32, (pch, S_SW), 0)
    rowg = lax.broadcasted_iota(jnp.int32, (ngrp, S_SW), 0)
    pwr = jnp.concatenate([pin_ref[0, :, :S_SW]] * ngrp, axis=0)
    pwi = jnp.concatenate([pin_ref[0, :, S_SW:]] * ngrp, axis=0)
    cr = carry[0:1, :S_SW]
    ci = carry[0:1, S_SW:]
    for p in range(parts):
        c0 = p * pch
        xr, xi = scan(x[c0:c0 + pch, :S_SW], x[c0:c0 + pch, S_SW:], pin_ref, rowi % S_SUB, S_SUB)
        for part, arr in ((0, xr), (1, xi)):
            for j, piece in slabs(part, arr):
                xs[j, c0:c0 + pch, :] = piece

        def group_ends(part):
            return jnp.concatenate([xs[part * ltiles + j, pl.ds(c0 + S_SUB - 1, ngrp, stride=S_SUB), :]
                                    for j in range(ltiles)], axis=1)

        er, ei = scan(group_ends(0), group_ends(1), pt_ref, rowg, ngrp)
        er, ei = axpy(er, ei, pt_ref[0, 0:ngrp, :S_SW], pt_ref[0, 0:ngrp, S_SW:], cr, ci)
        first = rowg == 0
        gr = jnp.where(first, cr, pltpu.roll(er, 1, 0))
        gi = jnp.where(first, ci, pltpu.roll(ei, 1, 0))
        for part, arr in ((0, gr), (1, gi)):
            for j, piece in slabs(part, arr):
                for k in range(S_SUB):
                    xs[j, pl.ds(c0 + k, ngrp, stride=S_SUB), :] = piece
        sr = jnp.concatenate([xs[j, c0:c0 + pch, :] for j in range(ltiles)], axis=1)
        si = jnp.concatenate([xs[ltiles + j, c0:c0 + pch, :] for j in range(ltiles)], axis=1)
        xr, xi = axpy(xr, xi, pwr, pwi, sr, si)

        firstc = rowi == 0
        xpr = jnp.where(firstc, cr, pltpu.roll(xr, 1, 0))
        xpi = jnp.where(firstc, ci, pltpu.roll(xi, 1, 0))
        y = y_in[c0:c0 + pch] + _dot(jnp.concatenate([xpr, xpi], axis=1).astype(BF16), mo_ref[0])
        r0 = c0 * S_Q
        for t in range(S_Q):
            y_ref[0, pl.ds(r0 + t, pch, stride=S_Q), :] = y[:, t * LANES:(t + 1) * LANES]
        rows = slice(r0, r0 + pch * S_Q)
        y_ref[0, rows, :] = _gelu_tanh(y_ref[0, rows, :] + d_ref[0] * u_ref[0, rows, :])
        cr, ci = er[ngrp - 1:ngrp, :], ei[ngrp - 1:ngrp, :]
    carry[0:1, :S_SW] = cr
    carry[0:1, S_SW:] = ci


def _s5(cu, l, kms, mo, pin, pt, dskip, batch, seq, tm):
    n = batch * seq
    nt = seq // tm
    nch = tm // S_Q
    qw = S_Q * LANES

    def table(rows, cols):
        return pl.BlockSpec((None, 1, rows, cols), lambda j, b, t: (l, j, 0, 0))

    return pl.pallas_call(
        functools.partial(_s5_kernel, nch=nch),
        grid=(S_BLOCKS, batch, nt),
        in_specs=[pl.BlockSpec((1, tm, LANES), lambda j, b, t: (j, b * nt + t, 0)),
                  table(qw, qw + 2 * S_SW), table(2 * S_SW, qw), table(S_SUB, 2 * S_SW),
                  table(nch // S_SUB, 2 * S_SW), table(1, LANES)],
        out_specs=pl.BlockSpec((1, tm, LANES), lambda j, b, t: (j, b * nt + t, 0)),
        out_shape=jax.ShapeDtypeStruct((S_BLOCKS, n, LANES), F32),
        scratch_shapes=[pltpu.VMEM((SUBLANES, 2 * S_SW), F32),
                        pltpu.VMEM((2 * S_SW // LANES, nch, LANES), F32)],
        compiler_params=_params("arbitrary", "arbitrary", "arbitrary"),
        name="s5",
    )(cu, kms, mo, pin, pt, dskip)


def _s5_expand_kernel(msr_ref, msi_ref, mor_ref, moi_ref, kms_ref, mo_ref):
    q = S_Q
    qw = q * LANES

    def embed(m, rows, width):
        shape = (width, S_GPB * width)
        tile = (lax.broadcasted_iota(jnp.int32, shape, 1) % width
                == lax.broadcasted_iota(jnp.int32, shape, 0)).astype(BF16)
        oshape = (S_GPB * rows, S_GPB * width)
        diag = (lax.broadcasted_iota(jnp.int32, oshape, 0) // rows
                == lax.broadcasted_iota(jnp.int32, oshape, 1) // width)
        return jnp.where(diag, _dot(m, tile), 0.0).astype(BF16)

    ms = [(embed(msr_ref[0, 0, s], S_GROUP, S_STATE), embed(msi_ref[0, 0, s], S_GROUP, S_STATE))
          for s in range(q)]
    mo = [(embed(mor_ref[0, 0, tau], S_STATE, S_GROUP), embed(moi_ref[0, 0, tau], S_STATE, S_GROUP))
          for tau in range(q + 1)]
    bbr, bbi = ms[q - 1]
    kd = [(_dot(bbr, mo[tau][0]) + _dot(bbi, mo[tau][1])).astype(BF16) for tau in range(q)]
    zero = jnp.zeros((LANES, LANES), BF16)
    for s in range(q):
        rows = slice(s * LANES, (s + 1) * LANES)
        for t in range(q):
            kms_ref[0, 0, rows, t * LANES:(t + 1) * LANES] = kd[t - s] if t >= s else zero
        kms_ref[0, 0, rows, qw:qw + S_SW] = ms[s][0]
        kms_ref[0, 0, rows, qw + S_SW:] = ms[s][1]
    for t in range(q):
        cols = slice(t * LANES, (t + 1) * LANES)
        mo_ref[0, 0, 0:S_SW, cols] = mo[t + 1][0]
        mo_ref[0, 0, S_SW:, cols] = mo[t + 1][1]


def _s5_expand(msr, msi, mor, moi):
    depth = msr.shape[0]
    qw = S_Q * LANES

    def blk(a):
        return pl.BlockSpec((1, 1) + a.shape[2:], lambda d, j: (d, j, 0, 0, 0))

    return pl.pallas_call(
        _s5_expand_kernel, grid=(depth, S_BLOCKS),
        in_specs=[blk(a) for a in (msr, msi, mor, moi)],
        out_specs=[pl.BlockSpec((1, 1, qw, qw + 2 * S_SW), lambda d, j: (d, j, 0, 0)),
                   pl.BlockSpec((1, 1, 2 * S_SW, qw), lambda d, j: (d, j, 0, 0))],
        out_shape=[jax.ShapeDtypeStruct((depth, S_BLOCKS, qw, qw + 2 * S_SW), BF16),
                   jax.ShapeDtypeStruct((depth, S_BLOCKS, 2 * S_SW, qw), BF16)],
        compiler_params=_params("arbitrary", "arbitrary"), name="s5_expand",
    )(msr, msi, mor, moi)


def _s5_tables(lam_re, lam_im, log_dt, b_re, b_im, c_re, c_im, nch):
    q = S_Q
    dep = lam_re.shape[0]
    lr = jnp.minimum(lam_re.astype(F32), -1e-4)
    li = lam_im.astype(F32)
    dt = jnp.exp(log_dt.astype(F32))[..., None]
    mag = jnp.exp(lr * dt)
    ab_re = mag * jnp.cos(li * dt)
    ab_im = mag * jnp.sin(li * dt)
    nr = ab_re - 1.0
    den = lr * lr + li * li
    coef_re = ((nr * lr + ab_im * li) / den)[:, :, None, :]
    coef_im = ((ab_im * lr - nr * li) / den)[:, :, None, :]
    brt = jnp.swapaxes(b_re.astype(F32), -1, -2)
    bit = jnp.swapaxes(b_im.astype(F32), -1, -2)
    bb_re = coef_re * brt - coef_im * bit
    bb_im = coef_re * bit + coef_im * brt

    def apow(nvals):
        e = jnp.asarray(nvals, F32)[:, None, None, None]
        m = jnp.exp(e * (lr * dt))
        ang = e * (li * dt)
        return m * jnp.cos(ang), m * jnp.sin(ang)

    def per_block(m, rows):
        m = m.reshape(m.shape[0], dep, S_BLOCKS, S_GPB * rows, m.shape[-1])
        return m.transpose(1, 2, 0, 3, 4).astype(BF16)

    ar, ai = apow(list(range(q + 1)))
    crt = jnp.swapaxes(c_re.astype(F32), -1, -2)
    cit = jnp.swapaxes(c_im.astype(F32), -1, -2)
    ca_re = crt[None] * ar[..., None] - cit[None] * ai[..., None]
    ca_im = crt[None] * ai[..., None] + cit[None] * ar[..., None]
    pw_r = ar[:q][::-1][:, :, :, None, :]
    pw_i = ai[:q][::-1][:, :, :, None, :]
    kms, mo = _s5_expand(per_block(pw_r * bb_re - pw_i * bb_im, S_GROUP),
                         per_block(pw_r * bb_im + pw_i * bb_re, S_GROUP),
                         per_block(ca_re, S_STATE), per_block(-ca_im, S_STATE))
    def power_table(nvals):
        pr, pi = apow(nvals)
        pr = pr.reshape(len(nvals), dep, S_BLOCKS, S_SW).transpose(1, 2, 0, 3)
        pi = pi.reshape(len(nvals), dep, S_BLOCKS, S_SW).transpose(1, 2, 0, 3)
        return jnp.concatenate([pr, pi], axis=3)

    pin = power_table([q * (r + 1) for r in range(S_SUB)])
    pt = power_table([q * S_SUB * (k + 1) for k in range(nch // S_SUB)])
    return kms, mo, pin, pt


def _merge_kernel(ya_ref, yb_ref, yc_ref, cz_ref, g_ref, x_ref, wa_ref, wb_ref, wglu_ref, wc_ref,
                  wo_ref, nw_ref, *out_refs, final):
    yc0 = jnp.concatenate([yc_ref[j] for j in range(S_BLOCKS)], axis=1)
    glu = _dot(yc0.astype(BF16), wglu_ref[...])
    yc = yc0 * _sigmoid(glu) * cz_ref[...].astype(F32)
    merged = (g_ref[:, 0:D_MODEL].astype(F32) * _dot(ya_ref[...], wa_ref[...])
              + g_ref[:, D_MODEL:2 * D_MODEL].astype(F32) * _dot(yb_ref[...], wb_ref[...])
              + g_ref[:, 2 * D_MODEL:].astype(F32) * _dot(yc.astype(BF16), wc_ref[...]))
    out = x_ref[...] + _dot(merged.astype(BF16), wo_ref[...])
    ms = jnp.mean(out * out, axis=-1, keepdims=True)
    normed = out * lax.rsqrt(ms + EPS) * nw_ref[...]
    if final:
        out_refs[0][...] = normed
    else:
        out_refs[0][...] = out
        out_refs[1][...] = normed.astype(BF16)


def _merge(ya, yb, yc, cz, g, x2, l, wa, wb, wglu, wc, wo, nw, tm, final):
    n = x2.shape[0]
    x_out = jax.ShapeDtypeStruct((n, D_MODEL), F32)
    hn_out = jax.ShapeDtypeStruct((n, D_MODEL), BF16)
    return pl.pallas_call(
        functools.partial(_merge_kernel, final=final),
        grid=(n // tm,),
        in_specs=[_rows(tm, M_W), _rows(tm, G_VW),
                  pl.BlockSpec((S_BLOCKS, tm, LANES), lambda i: (0, i, 0)),
                  _rows(tm, S_W), _rows(tm, 3 * D_MODEL), _rows(tm, D_MODEL),
                  _of_layer(l, (M_W, D_MODEL)), _of_layer(l, (G_VW, D_MODEL)), _of_layer(l, (S_W, S_W)),
                  _of_layer(l, (S_W, D_MODEL)), _of_layer(l, (D_MODEL, D_MODEL)), _whole((1, D_MODEL))],
        out_specs=[_rows(tm, D_MODEL)] if final else [_rows(tm, D_MODEL), _rows(tm, D_MODEL)],
        out_shape=[x_out] if final else [x_out, hn_out],
        compiler_params=_params("arbitrary"),
        name="merge",
    )(ya, yb, yc, cz, g, x2, wa, wb, wglu, wc, wo, nw)


def _pad_heads(w, heads, d, dp):
    zeros = jnp.zeros(w.shape[:-1] + (dp - d,), w.dtype)
    parts = []
    for h in range(heads):
        parts += [w[..., h * d:(h + 1) * d], zeros]
    return jnp.concatenate(parts, axis=-1)


def _prepare(p, nch):
    depth = p["w_in"].shape[0]
    w_mlstm, w_plain, w_gate, w_small = _wprep(jnp.swapaxes(p["w_in"], 1, 2), LANES)
    gkp = functools.partial(_pad_heads, heads=G_HEADS, d=G_DK, dp=G_DKP)
    kms, mo, pin, pt = _s5_tables(p["s5_lam_re"], p["s5_lam_im"], p["s5_log_dt"], p["s5_B_re"], p["s5_B_im"],
                               p["s5_C_re"], p["s5_C_im"], nch)
    gate_b = p["mlstm_gate_b"]
    return dict(
        w_mlstm=w_mlstm, w_plain=w_plain, w_gate=w_gate, w_small=w_small,
        cq=p["mlstm_conv"][:, :, :M_W], ck=p["mlstm_conv"][:, :, M_W:],
        gb=jnp.concatenate([gate_b[:, 0], gate_b[:, 1],
                            jnp.zeros((depth, LANES - 2 * M_HEADS), F32)], axis=1)[:, None, :],
        m_nw=p["mlstm_norm"][:, None, :],
        wal=jnp.concatenate([jnp.zeros((depth, 2 * M_HEADS, G_KWP), F32), gkp(p["gla_w_alpha"]),
                             jnp.zeros((depth, LANES - 2 * M_HEADS - G_RANK, G_KWP), F32)],
                            axis=1).astype(BF16),
        bal=gkp(p["gla_b_alpha"])[:, None, :],
        g_nw=p["gla_norm"][:, None, :],
        kms=kms, mo=mo, pin=pin, pt=pt,
        dskip=p["s5_D"].reshape(depth, S_BLOCKS, 1, LANES),
        wa=p["w_branch_mlstm"].astype(BF16), wb=p["w_branch_gla"].astype(BF16),
        wglu=p["s5_w_glu"].astype(BF16), wc=p["w_branch_s5"].astype(BF16), wo=p["w_out"].astype(BF16),
    )


def _tiles(batch, seq):
    return dict(proj=min(1024, seq), chunk=min(256, seq), g_sub=16,
                s5=min(4096, seq), merge=min(512, batch * seq))


def _layer(x2, hn, batch, seq, l, w, this_norm, next_norm, final, tl):
    tm = tl["proj"]
    nw = this_norm.reshape(1, D_MODEL)
    if hn is None:
        gz, czs, gs, hn = _proj_gate(x2, nw, l, w["w_gate"], tm, True)
    else:
        gz, czs, gs = _proj_gate(hn, nw, l, w["w_gate"], tm, False)
    q, k, oz = _proj_mlstm(hn, l, w["w_mlstm"], w["cq"], w["ck"], seq, tm)
    mv, gv, gq, gk, cu_s, gm, gmt, gcum = _proj_va(hn, l, w["w_plain"], w["w_small"], w["gb"], w["wal"],
                                                   w["bal"], tm, tl["chunk"])
    ya, yb = _mixers(q, k, mv, oz, gm, gmt, gq, gk, gv, gz, gcum, l, w["m_nw"], w["g_nw"], batch, seq,
                     tl["chunk"], tl["g_sub"])
    yc = _s5(cu_s, l, w["kms"], w["mo"], w["pin"], w["pt"], w["dskip"], batch, seq, tl["s5"])
    return _merge(ya, yb, yc, czs, gs, x2, l, w["wa"], w["wb"], w["wglu"], w["wc"], w["wo"],
                  next_norm.reshape(1, D_MODEL), tl["merge"], final)


def kernel(x, norm_w, w_in, mlstm_conv, mlstm_gate_b, mlstm_norm, gla_w_alpha, gla_b_alpha, gla_norm,
           s5_lam_re, s5_lam_im, s5_log_dt, s5_B_re, s5_B_im, s5_C_re, s5_C_im, s5_D, s5_w_glu,
           w_branch_mlstm, w_branch_gla, w_branch_s5, w_out, final_norm):
    batch, seq, _ = x.shape
    depth = norm_w.shape[0]
    tl = _tiles(batch, seq)
    prep = _prepare(dict(w_in=w_in, mlstm_conv=mlstm_conv, mlstm_gate_b=mlstm_gate_b, mlstm_norm=mlstm_norm,
                         gla_w_alpha=gla_w_alpha, gla_b_alpha=gla_b_alpha, gla_norm=gla_norm,
                         s5_lam_re=s5_lam_re, s5_lam_im=s5_lam_im, s5_log_dt=s5_log_dt, s5_B_re=s5_B_re,
                         s5_B_im=s5_B_im, s5_C_re=s5_C_re, s5_C_im=s5_C_im, s5_D=s5_D, s5_w_glu=s5_w_glu,
                         w_branch_mlstm=w_branch_mlstm, w_branch_gla=w_branch_gla,
                         w_branch_s5=w_branch_s5, w_out=w_out), tl["s5"] // S_Q)
    x2 = x.reshape(batch * seq, D_MODEL)
    hn = None
    for l in range(depth):
        final = l == depth - 1
        res = _layer(x2, hn, batch, seq, l, prep, norm_w[l], final_norm if final else norm_w[l + 1], final, tl)
        if final:
            x2 = res[0]
        else:
            x2, hn = res
    return x2.reshape(batch, seq, D_MODEL)
```

```python
import functools
import math

import jax
import jax.numpy as jnp
from jax import lax
from jax.experimental import pallas as pl
from jax.experimental.pallas import tpu as pltpu

F32 = jnp.float32
BF16 = jnp.bfloat16

EPS = 1e-6
D_MODEL = 1024
LANES = 128
SUBLANES = 8
V7X_VMEM_BYTES = 64 * 1024 * 1024
M_HEADS = 4
M_HD = 192
M_HDP = 256
M_W = M_HEADS * M_HD
M_WP = M_HEADS * M_HDP
M_CONV = 4
G_HEADS = 4
G_DK = 96
G_DKP = 128
G_DV = 192
G_DVP = 256
G_KW = G_HEADS * G_DK
G_KWP = G_HEADS * G_DKP
G_VW = G_HEADS * G_DV
G_VWP = G_HEADS * G_DVP
G_RANK = 16
G_TAU = 16.0
S_W = 512
S_GROUP = 16
S_STATE = 64
S_BLOCKS = S_W // LANES
S_GPB = LANES // S_GROUP
S_SW = S_GPB * S_STATE
S_Q = 8
S_SUB = 8
S5_PARTS = 2

IN_SIZES = (M_W, M_W, M_W, M_W, M_HEADS, M_HEADS, M_W,
            G_KW, G_KW, G_VW, G_RANK, G_VW,
            S_W, S_W, 3 * D_MODEL)
SMALL_W = S_W + LANES
CW = 512

NEG = -1e30
G_SAFE_LOG_DECAY = -60.0
VMEM_LIMIT = V7X_VMEM_BYTES * 7 // 8


def _sigmoid(x):
    return 0.5 * jnp.tanh(0.5 * x) + 0.5


def _silu(x):
    h = 0.5 * x
    return h * jnp.tanh(h) + h


def _log_sigmoid(x):
    return jnp.minimum(x, 0.0) - jnp.log1p(jnp.exp(-jnp.abs(x)))


def _dot(a, b):
    return jnp.dot(a, b, preferred_element_type=F32)


def _dot_nt(a, b):
    return lax.dot_general(a, b, (((1,), (1,)), ((), ())), preferred_element_type=F32)


def _dot_tn(a, b):
    return lax.dot_general(a, b, (((0,), (0,)), ((), ())), preferred_element_type=F32)


def _params(*sem):
    return pltpu.CompilerParams(dimension_semantics=sem, vmem_limit_bytes=VMEM_LIMIT)


def _rows(tm, width):
    return pl.BlockSpec((tm, width), lambda i: (i, 0))


def _whole(shape):
    return pl.BlockSpec(shape, lambda i: (0,) * len(shape))


def _of_layer(l, shape):
    return pl.BlockSpec((None,) + shape, lambda *_: (l,) + (0,) * len(shape))


IN_NAMES = ("aq", "ak", "av", "ao", "ai", "af", "az", "bq", "bk", "bv", "ba", "bz", "cu", "cz", "g")
W_OFF = {name: sum(IN_SIZES[:i]) for i, name in enumerate(IN_NAMES)}
IN_WIDTH = sum(IN_SIZES)
W_MLSTM, W_PLAIN, W_GATE = 4 * M_W, M_W + G_VW + 2 * G_KW, G_VW + S_W + 3 * D_MODEL


def _wprep_kernel(w_ref, wm_ref, wp_ref, wg_ref, ws_ref):
    kb = w_ref.shape[2]

    def put(dst, dst_off, parts, width):
        xs = [w_ref[0, off:off + rows, :] for off, rows in parts]
        used = sum(rows for _, rows in parts)
        if used < width:
            xs.append(jnp.zeros((width - used, kb), F32))
        x = xs[0] if len(xs) == 1 else jnp.concatenate(xs, axis=0)
        dst[0, :, dst_off:dst_off + width] = x.T.astype(BF16)

    def group(dst, names):
        off = 0
        for name in names:
            width = IN_SIZES[IN_NAMES.index(name)]
            for c in range(0, width, LANES):
                put(dst, off + c, [(W_OFF[name] + c, LANES)], LANES)
            off += width

    group(wm_ref, ("aq", "ak", "ao", "az"))
    group(wp_ref, ("av", "bv", "bq", "bk"))
    group(wg_ref, ("bz", "cz", "g"))
    group(ws_ref, ("cu",))
    put(ws_ref, S_W, [(W_OFF["ai"], 2 * M_HEADS), (W_OFF["ba"], G_RANK)], LANES)


def _wprep(w_t, kb):
    depth = w_t.shape[0]
    widths = (W_MLSTM, W_PLAIN, W_GATE, SMALL_W)
    return pl.pallas_call(
        _wprep_kernel, grid=(depth, D_MODEL // kb),
        in_specs=[pl.BlockSpec((1, IN_WIDTH, kb), lambda l, i: (l, 0, i))],
        out_specs=[pl.BlockSpec((1, kb, wd), lambda l, i: (l, i, 0)) for wd in widths],
        out_shape=[jax.ShapeDtypeStruct((depth, D_MODEL, wd), BF16) for wd in widths],
        compiler_params=_params("arbitrary", "arbitrary"), name="wprep",
    )(w_t)


def _scatter_heads(dst, y, heads, d, dp, fill):
    rows = y.shape[0]
    for h in range(heads):
        dst[:, h * dp:h * dp + d] = y[:, h * d:(h + 1) * d].astype(BF16)
        dst[:, h * dp + d:(h + 1) * dp] = jnp.full((rows, dp - d), fill, BF16)


def _proj_mlstm_kernel(hn_ref, w_ref, cq_ref, ck_ref, q_ref, k_ref, oz_ref, halo_x, halo_b, *, tm,
                       tiles_per_seq):
    @pl.when(pl.program_id(0) % tiles_per_seq == 0)
    def _():
        halo_x[...] = jnp.zeros_like(halo_x)
        halo_b[...] = jnp.zeros_like(halo_b)

    hn = hn_ref[...]
    row8 = lax.broadcasted_iota(jnp.int32, (SUBLANES, M_W), 0)

    def shift(x, prev8, d):
        rolled = pltpu.roll(x, d, 0)
        head = jnp.where(row8 < d, pltpu.roll(prev8, d, 0), rolled[0:SUBLANES])
        return jnp.concatenate([head, rolled[SUBLANES:]], axis=0)

    def conv_act(c, x, cw_ref, dst, scale):
        x1 = shift(x, halo_x[c], 1)
        pair = cw_ref[1:2, :] * x + cw_ref[0:1, :] * x1
        y = _silu(cw_ref[3:4, :] * x + cw_ref[2:3, :] * x1 + shift(pair, halo_b[c], 2))
        halo_x[c] = x[tm - SUBLANES:tm]
        halo_b[c] = pair[tm - SUBLANES:tm]
        _scatter_heads(dst, y if scale == 1.0 else y * scale, M_HEADS, M_HD, M_HDP, 0.0)

    xq = _dot(hn, w_ref[:, 0:M_W])
    xk = _dot(hn, w_ref[:, M_W:2 * M_W])
    conv_act(0, xq, cq_ref, q_ref, 1.0)
    o = _dot(hn, w_ref[:, 2 * M_W:3 * M_W])
    conv_act(1, xk, ck_ref, k_ref, M_HD ** -0.5)
    z = _dot(hn, w_ref[:, 3 * M_W:])
    oz_ref[...] = (_sigmoid(o) * _silu(z)).astype(BF16)


def _proj_mlstm(hn, l, w, cq, ck, seq, tm):
    n = hn.shape[0]
    padded = jax.ShapeDtypeStruct((n, M_WP), BF16)
    return pl.pallas_call(
        functools.partial(_proj_mlstm_kernel, tm=tm, tiles_per_seq=seq // tm),
        grid=(n // tm,),
        in_specs=[_rows(tm, D_MODEL), _of_layer(l, (D_MODEL, W_MLSTM)), _of_layer(l, (M_CONV, M_W)),
                  _of_layer(l, (M_CONV, M_W))],
        out_specs=[_rows(tm, M_WP), _rows(tm, M_WP), _rows(tm, M_W)],
        out_shape=[padded, padded, jax.ShapeDtypeStruct((n, M_W), BF16)],
        scratch_shapes=[pltpu.VMEM((2, SUBLANES, M_W), F32)] * 2,
        compiler_params=_params("arbitrary"), name="proj_mlstm",
    )(hn, w, cq, ck)


def _proj_gate_kernel(src_ref, nw_ref, w_ref, gz_ref, cz_ref, g_ref, *hn_out, normalize):
    if normalize:
        x = src_ref[...]
        ms = jnp.mean(x * x, axis=-1, keepdims=True)
        hn = (x * lax.rsqrt(ms + EPS) * nw_ref[...]).astype(BF16)
        hn_out[0][...] = hn
    else:
        hn = src_ref[...]
    gz_ref[...] = _silu(_dot(hn, w_ref[:, 0:G_VW])).astype(BF16)
    cz_ref[...] = _silu(_dot(hn, w_ref[:, G_VW:G_VW + S_W])).astype(BF16)
    off = G_VW + S_W
    for c in range(3 * D_MODEL // CW):
        r = _dot(hn, w_ref[:, off + c * CW:off + (c + 1) * CW])
        g_ref[:, c * CW:(c + 1) * CW] = _sigmoid(r).astype(BF16)


def _proj_gate(src, nw, l, w, tm, normalize):
    n = src.shape[0]
    widths = (G_VW, S_W, 3 * D_MODEL) + ((D_MODEL,) if normalize else ())
    return pl.pallas_call(
        functools.partial(_proj_gate_kernel, normalize=normalize), grid=(n // tm,),
        in_specs=[_rows(tm, D_MODEL), _whole((1, D_MODEL)), _of_layer(l, (D_MODEL, W_GATE))],
        out_specs=[_rows(tm, wd) for wd in widths],
        out_shape=[jax.ShapeDtypeStruct((n, wd), BF16) for wd in widths],
        compiler_params=_params("arbitrary"), name="proj_gate",
    )(src, nw, w)


def _split3(x):
    hi = x.astype(BF16)
    r1 = x - hi.astype(F32)
    mid = r1.astype(BF16)
    return hi, mid, (r1 - mid.astype(F32)).astype(BF16)


def _segment_cumsum(x, seg):
    rows, width = x.shape
    tril = (lax.broadcasted_iota(jnp.int32, (seg, seg), 1)
            <= lax.broadcasted_iota(jnp.int32, (seg, seg), 0)).astype(BF16)
    parts = jnp.concatenate(_split3(x), axis=1)
    out = []
    for s in range(rows // seg):
        acc = _dot(tril, parts[s * seg:(s + 1) * seg])
        out.append(acc[:, :width] + acc[:, width:2 * width] + acc[:, 2 * width:])
    return jnp.concatenate(out, axis=0)


def _proj_va_kernel(hn_ref, wp_ref, ws_ref, gb_ref, wal_ref, bal_ref,
                    mv_ref, gv_ref, gq_ref, gk_ref, cu_ref, gmb_ref, gmt_ref, gcum_ref, *, chunk):
    hn = hn_ref[...]
    r = _dot(hn, ws_ref[...])
    _scatter_heads(mv_ref, _dot(hn, wp_ref[:, 0:M_W]), M_HEADS, M_HD, M_HDP, 0.0)
    for j in range(S_BLOCKS):
        cu_ref[j] = r[:, j * LANES:(j + 1) * LANES]
    raw = r[:, S_W:]
    gates = raw + gb_ref[...]
    lane = lax.broadcasted_iota(jnp.int32, gates.shape, 1)
    fgate = (lane >= M_HEADS) & (lane < 2 * M_HEADS)
    za = _dot(raw.astype(BF16), wal_ref[...]) + bal_ref[...]
    gm = jnp.where(fgate, _segment_cumsum(_log_sigmoid(gates), chunk), gates)
    _scatter_heads(gv_ref, _dot(hn, wp_ref[:, M_W:M_W + G_VW]), G_HEADS, G_DV, G_DVP, 0.0)
    gmt_ref[...] = gm.T
    slots = 2 * M_HEADS * LANES
    spread = (lax.broadcasted_iota(jnp.int32, (3 * LANES, slots), 0) % LANES
              == lax.broadcasted_iota(jnp.int32, (3 * LANES, slots), 1) // LANES).astype(BF16)
    gmb_ref[...] = _dot(jnp.concatenate(_split3(gm), axis=1), spread)
    off = M_W + G_VW
    _scatter_heads(gq_ref, _dot(hn, wp_ref[:, off:off + G_KW]) * (G_DK ** -0.5), G_HEADS, G_DK, G_DKP, 0.0)
    gcum_ref[...] = _segment_cumsum(_log_sigmoid(za) * (1.0 / G_TAU), chunk)
    _scatter_heads(gk_ref, _dot(hn, wp_ref[:, off + G_KW:]), G_HEADS, G_DK, G_DKP, 0.0)


def _proj_va(hn, l, wp, ws, gb, wal, bal, tm, chunk):
    n = hn.shape[0]
    widths = (M_WP, G_VWP, G_KWP, G_KWP)
    return pl.pallas_call(
        functools.partial(_proj_va_kernel, chunk=chunk), grid=(n // tm,),
        in_specs=[_rows(tm, D_MODEL), _of_layer(l, (D_MODEL, W_PLAIN)), _of_layer(l, (D_MODEL, SMALL_W)),
                  _of_layer(l, (1, LANES)), _of_layer(l, (LANES, G_KWP)), _of_layer(l, (1, G_KWP))],
        out_specs=[_rows(tm, wd) for wd in widths] + [
            pl.BlockSpec((S_BLOCKS, tm, LANES), lambda i: (0, i, 0)), _rows(tm, 2 * M_HEADS * LANES),
            pl.BlockSpec((LANES, tm), lambda i: (0, i)), _rows(tm, G_KWP)],
        out_shape=[jax.ShapeDtypeStruct((n, wd), BF16) for wd in widths] + [
            jax.ShapeDtypeStruct((S_BLOCKS, n, LANES), F32),
            jax.ShapeDtypeStruct((n, 2 * M_HEADS * LANES), F32),
            jax.ShapeDtypeStruct((LANES, n), F32),
            jax.ShapeDtypeStruct((n, G_KWP), F32)],
        compiler_params=_params("arbitrary"), name="proj_va",
    )(hn, wp, ws, gb, wal, bal)


def _mlstm_body(q_ref, k_ref, v_ref, oz_ref, gmb_ref, gmt_ref, nw_ref, out_ref, c_scr, m_scr, *, tc):
    row = lax.broadcasted_iota(jnp.int32, (tc, tc), 0)
    col = lax.broadcasted_iota(jnp.int32, (tc, tc), 1)
    causal = col <= row
    real = lax.broadcasted_iota(jnp.int32, (tc, M_HDP), 1) < M_HD
    ones_tile = jnp.ones((tc, LANES), BF16)

    def lanes(x, n):
        return jnp.concatenate([x] * n, axis=1)

    def head(h):
        sl = slice(h * M_HDP, (h + 1) * M_HDP)
        qb = q_ref[:, sl]
        kb = k_ref[:, sl]
        vext = jnp.concatenate([v_ref[:, sl], ones_tile], axis=1)
        i_b = gmb_ref[:, h * LANES:(h + 1) * LANES]
        bcum = gmb_ref[:, (M_HEADS + h) * LANES:(M_HEADS + h + 1) * LANES]
        g_row = gmt_ref[h:h + 1, :] - gmt_ref[M_HEADS + h:M_HEADS + h + 1, :]
        m_prev = m_scr[h, 0:1, :]

        inter = bcum + m_prev
        m_t = jnp.maximum(inter, bcum + jnp.max(jnp.where(causal, g_row, NEG), axis=1, keepdims=True))
        w_intra = jnp.exp(jnp.where(causal, g_row + lanes(bcum - m_t, tc // LANES), NEG))
        w_inter = jnp.exp(inter - m_t)
        s = _dot_nt(qb, kb) * w_intra
        c_mat = c_scr[h]
        num = _dot(s.astype(BF16), vext) + lanes(w_inter, 3) * _dot(qb, c_mat.astype(BF16))
        den = num[:, M_HDP:]
        rden = 1.0 / jnp.maximum(jnp.abs(den), jnp.exp(-m_t))
        num = num[:, :M_HDP]

        b_end = bcum[tc - 1:tc, :]
        to_end = b_end - bcum + i_b
        m_new = jnp.maximum(b_end + m_prev, jnp.max(to_end, axis=0, keepdims=True))
        wk = jnp.exp(to_end - m_new)
        w_prev = jnp.exp(b_end + m_prev - m_new)
        c_scr[h] = lanes(w_prev, 3) * c_mat + _dot_tn((lanes(wk, 2) * kb.astype(F32)).astype(BF16), vext)
        m_scr[h] = jnp.broadcast_to(m_new, (SUBLANES, LANES))

        sc = slice(h * M_HD, (h + 1) * M_HD)
        mean = jnp.sum(num, axis=1, keepdims=True) * (1.0 / M_HD)
        xc = jnp.where(real, num - mean, 0.0)
        var = jnp.sum(xc * xc, axis=1, keepdims=True) * (1.0 / M_HD)
        scale = rden * lax.rsqrt(rden * rden * var + EPS)
        ha = (xc * lanes(scale, 2))[:, :M_HD]
        out_ref[:, sc] = (ha * nw_ref[:, sc] * oz_ref[:, sc].astype(F32)).astype(BF16)

    return [functools.partial(head, h) for h in range(M_HEADS)]


def _gla_body(safe, q_ref, k_ref, v_ref, z_ref, b_ref, nw_ref,
              out_ref, kbuf, bbuf, attn_scr, st_scr, *, t, cs):
    row = lax.broadcasted_iota(jnp.int32, (t, t), 0)
    col = lax.broadcasted_iota(jnp.int32, (t, t), 1)
    causal = col <= row

    def operands(h):
        sl = slice(h * G_DKP, (h + 1) * G_DKP)
        return q_ref[:, sl].astype(F32), k_ref[:, sl].astype(F32), b_ref[:, sl]

    @pl.when(safe)
    def _():
        for h in range(G_HEADS):
            qs, kk, b = operands(h)
            a = _dot_nt((qs * jnp.exp(b)).astype(BF16), (kk * jnp.exp(-b)).astype(BF16))
            attn_scr[h] = jnp.where(causal, a, 0.0)

    @pl.when(jnp.logical_not(safe))
    def _():
        rowv = lax.broadcasted_iota(jnp.int32, (t, G_DKP), 0)
        sub = rowv % cs
        lag = row - col
        kbuf[0:cs, :] = jnp.zeros((cs, G_DKP), F32)
        bbuf[0:cs, :] = jnp.zeros((cs, G_DKP), F32)
        for h in range(G_HEADS):
            qs, kk, b = operands(h)
            qparts, kparts = [], []
            for i in range(1, t // cs):
                r = b[i * cs - 1:i * cs, :]
                inblk = (rowv >= i * cs) & (rowv < (i + 1) * cs)
                qparts.append((qs * jnp.exp(jnp.where(inblk, b - r, NEG))).astype(BF16))
                kparts.append((kk * jnp.exp(jnp.where(rowv < i * cs, r - b, NEG))).astype(BF16))
            attn = _dot_nt(jnp.concatenate(qparts, axis=1), jnp.concatenate(kparts, axis=1))
            kbuf[cs:cs + t, :] = kk
            bbuf[cs:cs + t, :] = b
            for j in range(cs):
                ksh = kbuf[cs - j:cs - j + t, :]
                bsh = bbuf[cs - j:cs - j + t, :]
                e = jnp.exp(jnp.where(sub >= j, b - bsh, NEG))
                dj = jnp.sum(qs * ksh * e, axis=1, keepdims=True)
                attn = attn + jnp.where(lag == j, dj, 0.0)
            attn_scr[h] = attn

    def tail(h):
        slv = slice(h * G_DVP, (h + 1) * G_DVP)
        qs, kk, b = operands(h)
        v = v_ref[:, slv]
        blast = b[t - 1:t, :]
        st = st_scr[h]
        o = (_dot_nt((qs * jnp.exp(b)).astype(BF16), st.astype(BF16))
             + _dot(attn_scr[h].astype(BF16), v))
        kdec = (kk * jnp.exp(blast - b)).astype(BF16)
        st_scr[h] = jnp.exp(blast) * st + _dot_tn(v, kdec)

        sc = slice(h * G_DV, (h + 1) * G_DV)
        var = jnp.sum(o * o, axis=1, keepdims=True) * (1.0 / G_DV)
        hb = (o * lax.rsqrt(var + EPS))[:, :G_DV] * nw_ref[:, sc]
        out_ref[:, sc] = (hb * z_ref[:, sc].astype(F32)).astype(BF16)

    return [functools.partial(tail, h) for h in range(G_HEADS)]


def _mixers_kernel(safe_ref, mq_ref, mk_ref, mv_ref, moz_ref, gmb_ref, mnw_ref,
                   gq_ref, gk_ref, gv_ref, gz_ref, gb_ref, gnw_ref, *rest, t, cs, nb):
    gmt_refs = rest[:nb]
    ya_ref, yb_ref, c_scr, m_scr, kbuf, bbuf, attn_scr, st_scr = rest[nb:]

    @pl.when(pl.program_id(1) == 0)
    def _():
        c_scr[...] = jnp.zeros_like(c_scr)
        m_scr[...] = jnp.zeros_like(m_scr)
        st_scr[...] = jnp.zeros_like(st_scr)

    heads = []
    for i in range(nb):
        safe = safe_ref[(pl.program_id(0) * nb + i) * pl.num_programs(1) + pl.program_id(1)] != 0
        heads += _gla_body(safe, gq_ref.at[i], gk_ref.at[i], gv_ref.at[i], gz_ref.at[i], gb_ref.at[i],
                           gnw_ref, yb_ref.at[i], kbuf, bbuf, attn_scr.at[i], st_scr.at[i], t=t, cs=cs)
    for i in range(nb):
        heads += _mlstm_body(mq_ref.at[i], mk_ref.at[i], mv_ref.at[i], moz_ref.at[i], gmb_ref.at[i],
                             gmt_refs[i], mnw_ref, ya_ref.at[i], c_scr.at[i], m_scr.at[i], tc=t)
    for head in heads:
        head()


def _mixers(mq, mk, mv, moz, gmb, gmt, gq, gk, gv, gz, gcum, l, m_nw, g_nw, batch, seq, t, cs):
    n = batch * seq
    nt = seq // t
    nb = 2 if batch % 2 == 0 else 1

    def blk(width):
        return pl.BlockSpec((nb, t, width), lambda b, c: (b, c, 0))

    def seqs(a):
        return a.reshape(batch, seq, a.shape[-1])

    safe = (jnp.min(gcum.reshape(n // t, t, G_KWP)[:, t - 1, :], axis=-1) >= G_SAFE_LOG_DECAY).astype(jnp.int32)
    ya, yb = pl.pallas_call(
        functools.partial(_mixers_kernel, t=t, cs=cs, nb=nb),
        grid=(batch // nb, nt),
        in_specs=[pl.BlockSpec(memory_space=pltpu.SMEM),
                  blk(M_WP), blk(M_WP), blk(M_WP), blk(M_W), blk(2 * M_HEADS * LANES), _of_layer(l, (1, M_W)),
                  blk(G_KWP), blk(G_KWP), blk(G_VWP), blk(G_VW), blk(G_KWP), _of_layer(l, (1, G_VW))]
                 + [pl.BlockSpec((LANES, t), lambda b, c, i=i: (0, (b * nb + i) * nt + c)) for i in range(nb)],
        out_specs=[blk(M_W), blk(G_VW)],
        out_shape=[jax.ShapeDtypeStruct((batch, seq, M_W), BF16),
                   jax.ShapeDtypeStruct((batch, seq, G_VW), BF16)],
        scratch_shapes=[pltpu.VMEM((nb, M_HEADS, M_HDP, M_HDP + LANES), F32),
                        pltpu.VMEM((nb, M_HEADS, SUBLANES, LANES), F32),
                        pltpu.VMEM((t + cs, G_DKP), F32), pltpu.VMEM((t + cs, G_DKP), F32),
                        pltpu.VMEM((nb, G_HEADS, t, t), F32),
                        pltpu.VMEM((nb, G_HEADS, G_DVP, G_DKP), F32)],
        compiler_params=_params("arbitrary", "arbitrary"),
        name="mixers",
    )(safe, seqs(mq), seqs(mk), seqs(mv), seqs(moz), seqs(gmb), m_nw,
      seqs(gq), seqs(gk), seqs(gv), seqs(gz), seqs(gcum), g_nw, *([gmt] * nb))
    return ya.reshape(n, M_W), yb.reshape(n, G_VW)


def _gelu_tanh(x):
    return 0.5 * x * (1.0 + jnp.tanh(math.sqrt(2.0 / math.pi) * (x + 0.044715 * (x * x * x))))


def _s5_kernel(u_ref, kms_ref, mo_ref, pin_ref, pt_ref, d_ref, y_ref, carry, xs, *, nch):
    @pl.when(pl.program_id(2) == 0)
    def _():
        carry[...] = jnp.zeros_like(carry)

    qw = S_Q * LANES
    ltiles = S_SW // LANES
    parts = S5_PARTS if nch % (S5_PARTS * S_SUB) == 0 else 1
    pch = nch // parts
    ngrp = pch // S_SUB
    ucat = jnp.concatenate([u_ref[0, pl.ds(s, nch, stride=S_Q), :] for s in range(S_Q)],
                           axis=1).astype(BF16)
    x = _dot(ucat, kms_ref[0, :, qw:])
    y_in = _dot(ucat, kms_ref[0, :, :qw])

    def axpy(xr, xi, pr, pi, sr, si):
        return xr + (pr * sr - pi * si), xi + (pr * si + pi * sr)

    def scan(xr, xi, p_ref, pos, length):
        d = 1
        while d < length:
            keep = pos >= d
            sr = jnp.where(keep, pltpu.roll(xr, d, 0), 0.0)
            si = jnp.where(keep, pltpu.roll(xi, d, 0), 0.0)
            xr, xi = axpy(xr, xi, p_ref[0, d - 1:d, :S_SW], p_ref[0, d - 1:d, S_SW:], sr, si)
            d *= 2
        return xr, xi

    def slabs(part, arr):
        return [(part * ltiles + j, arr[:, j * LANES:(j + 1) * LANES]) for j in range(ltiles)]

    rowi = lax.broadcasted_iota(jnp.int32, (pch, S_SW), 0)
    rowg = lax.broadcasted_iota(jnp.int32, (ngrp, S_SW), 0)
    pwr = jnp.concatenate([pin_ref[0, :, :S_SW]] * ngrp, axis=0)
    pwi = jnp.concatenate([pin_ref[0, :, S_SW:]] * ngrp, axis=0)
    cr = carry[0:1, :S_SW]
    ci = carry[0:1, S_SW:]
    for p in range(parts):
        c0 = p * pch
        xr, xi = scan(x[c0:c0 + pch, :S_SW], x[c0:c0 + pch, S_SW:], pin_ref, rowi % S_SUB, S_SUB)
        for part, arr in ((0, xr), (1, xi)):
            for j, piece in slabs(part, arr):
                xs[j, c0:c0 + pch, :] = piece

        def group_ends(part):
            return jnp.concatenate([xs[part * ltiles + j, pl.ds(c0 + S_SUB - 1, ngrp, stride=S_SUB), :]
                                    for j in range(ltiles)], axis=1)

        er, ei = scan(group_ends(0), group_ends(1), pt_ref, rowg, ngrp)
        er, ei = axpy(er, ei, pt_ref[0, 0:ngrp, :S_SW], pt_ref[0, 0:ngrp, S_SW:], cr, ci)
        first = rowg == 0
        gr = jnp.where(first, cr, pltpu.roll(er, 1, 0))
        gi = jnp.where(first, ci, pltpu.roll(ei, 1, 0))
        for part, arr in ((0, gr), (1, gi)):
            for j, piece in slabs(part, arr):
                for k in range(S_SUB):
                    xs[j, pl.ds(c0 + k, ngrp, stride=S_SUB), :] = piece
        sr = jnp.concatenate([xs[j, c0:c0 + pch, :] for j in range(ltiles)], axis=1)
        si = jnp.concatenate([xs[ltiles + j, c0:c0 + pch, :] for j in range(ltiles)], axis=1)
        xr, xi = axpy(xr, xi, pwr, pwi, sr, si)

        firstc = rowi == 0
        xpr = jnp.where(firstc, cr, pltpu.roll(xr, 1, 0))
        xpi = jnp.where(firstc, ci, pltpu.roll(xi, 1, 0))
        y = y_in[c0:c0 + pch] + _dot(jnp.concatenate([xpr, xpi], axis=1).astype(BF16), mo_ref[0])
        r0 = c0 * S_Q
        for t in range(S_Q):
            y_ref[0, pl.ds(r0 + t, pch, stride=S_Q), :] = y[:, t * LANES:(t + 1) * LANES]
        rows = slice(r0, r0 + pch * S_Q)
        y_ref[0, rows, :] = _gelu_tanh(y_ref[0, rows, :] + d_ref[0] * u_ref[0, rows, :])
        cr, ci = er[ngrp - 1:ngrp, :], ei[ngrp - 1:ngrp, :]
    carry[0:1, :S_SW] = cr
    carry[0:1, S_SW:] = ci


def _s5(cu, l, kms, mo, pin, pt, dskip, batch, seq, tm):
    n = batch * seq
    nt = seq // tm
    nch = tm // S_Q
    qw = S_Q * LANES

    def table(rows, cols):
        return pl.BlockSpec((None, 1, rows, cols), lambda j, b, t: (l, j, 0, 0))

    return pl.pallas_call(
        functools.partial(_s5_kernel, nch=nch),
        grid=(S_BLOCKS, batch, nt),
        in_specs=[pl.BlockSpec((1, tm, LANES), lambda j, b, t: (j, b * nt + t, 0)),
                  table(qw, qw + 2 * S_SW), table(2 * S_SW, qw), table(S_SUB, 2 * S_SW),
                  table(nch // S_SUB, 2 * S_SW), table(1, LANES)],
        out_specs=pl.BlockSpec((1, tm, LANES), lambda j, b, t: (j, b * nt + t, 0)),
        out_shape=jax.ShapeDtypeStruct((S_BLOCKS, n, LANES), F32),
        scratch_shapes=[pltpu.VMEM((SUBLANES, 2 * S_SW), F32),
                        pltpu.VMEM((2 * S_SW // LANES, nch, LANES), F32)],
        compiler_params=_params("arbitrary", "arbitrary", "arbitrary"),
        name="s5",
    )(cu, kms, mo, pin, pt, dskip)


def _s5_expand_kernel(msr_ref, msi_ref, mor_ref, moi_ref, kms_ref, mo_ref):
    q = S_Q
    qw = q * LANES

    def embed(m, rows, width):
        shape = (width, S_GPB * width)
        tile = (lax.broadcasted_iota(jnp.int32, shape, 1) % width
                == lax.broadcasted_iota(jnp.int32, shape, 0)).astype(BF16)
        oshape = (S_GPB * rows, S_GPB * width)
        diag = (lax.broadcasted_iota(jnp.int32, oshape, 0) // rows
                == lax.broadcasted_iota(jnp.int32, oshape, 1) // width)
        return jnp.where(diag, _dot(m, tile), 0.0).astype(BF16)

    ms = [(embed(msr_ref[0, 0, s], S_GROUP, S_STATE), embed(msi_ref[0, 0, s], S_GROUP, S_STATE))
          for s in range(q)]
    mo = [(embed(mor_ref[0, 0, tau], S_STATE, S_GROUP), embed(moi_ref[0, 0, tau], S_STATE, S_GROUP))
          for tau in range(q + 1)]
    bbr, bbi = ms[q - 1]
    kd = [(_dot(bbr, mo[tau][0]) + _dot(bbi, mo[tau][1])).astype(BF16) for tau in range(q)]
    zero = jnp.zeros((LANES, LANES), BF16)
    for s in range(q):
        rows = slice(s * LANES, (s + 1) * LANES)
        for t in range(q):
            kms_ref[0, 0, rows, t * LANES:(t + 1) * LANES] = kd[t - s] if t >= s else zero
        kms_ref[0, 0, rows, qw:qw + S_SW] = ms[s][0]
        kms_ref[0, 0, rows, qw + S_SW:] = ms[s][1]
    for t in range(q):
        cols = slice(t * LANES, (t + 1) * LANES)
        mo_ref[0, 0, 0:S_SW, cols] = mo[t + 1][0]
        mo_ref[0, 0, S_SW:, cols] = mo[t + 1][1]


def _s5_expand(msr, msi, mor, moi):
    depth = msr.shape[0]
    qw = S_Q * LANES

    def blk(a):
        return pl.BlockSpec((1, 1) + a.shape[2:], lambda d, j: (d, j, 0, 0, 0))

    return pl.pallas_call(
        _s5_expand_kernel, grid=(depth, S_BLOCKS),
        in_specs=[blk(a) for a in (msr, msi, mor, moi)],
        out_specs=[pl.BlockSpec((1, 1, qw, qw + 2 * S_SW), lambda d, j: (d, j, 0, 0)),
                   pl.BlockSpec((1, 1, 2 * S_SW, qw), lambda d, j: (d, j, 0, 0))],
        out_shape=[jax.ShapeDtypeStruct((depth, S_BLOCKS, qw, qw + 2 * S_SW), BF16),
                   jax.ShapeDtypeStruct((depth, S_BLOCKS, 2 * S_SW, qw), BF16)],
        compiler_params=_params("arbitrary", "arbitrary"), name="s5_expand",
    )(msr, msi, mor, moi)


def _s5_tables(lam_re, lam_im, log_dt, b_re, b_im, c_re, c_im, nch):
    q = S_Q
    dep = lam_re.shape[0]
    lr = jnp.minimum(lam_re.astype(F32), -1e-4)
    li = lam_im.astype(F32)
    dt = jnp.exp(log_dt.astype(F32))[..., None]
    mag = jnp.exp(lr * dt)
    ab_re = mag * jnp.cos(li * dt)
    ab_im = mag * jnp.sin(li * dt)
    nr = ab_re - 1.0
    den = lr * lr + li * li
    coef_re = ((nr * lr + ab_im * li) / den)[:, :, None, :]
    coef_im = ((ab_im * lr - nr * li) / den)[:, :, None, :]
    brt = jnp.swapaxes(b_re.astype(F32), -1, -2)
    bit = jnp.swapaxes(b_im.astype(F32), -1, -2)
    bb_re = coef_re * brt - coef_im * bit
    bb_im = coef_re * bit + coef_im * brt

    def apow(nvals):
        e = jnp.asarray(nvals, F32)[:, None, None, None]
        m = jnp.exp(e * (lr * dt))
        ang = e * (li * dt)
        return m * jnp.cos(ang), m * jnp.sin(ang)

    def per_block(m, rows):
        m = m.reshape(m.shape[0], dep, S_BLOCKS, S_GPB * rows, m.shape[-1])
        return m.transpose(1, 2, 0, 3, 4).astype(BF16)

    ar, ai = apow(list(range(q + 1)))
    crt = jnp.swapaxes(c_re.astype(F32), -1, -2)
    cit = jnp.swapaxes(c_im.astype(F32), -1, -2)
    ca_re = crt[None] * ar[..., None] - cit[None] * ai[..., None]
    ca_im = crt[None] * ai[..., None] + cit[None] * ar[..., None]
    pw_r = ar[:q][::-1][:, :, :, None, :]
    pw_i = ai[:q][::-1][:, :, :, None, :]
    kms, mo = _s5_expand(per_block(pw_r * bb_re - pw_i * bb_im, S_GROUP),
                         per_block(pw_r * bb_im + pw_i * bb_re, S_GROUP),
                         per_block(ca_re, S_STATE), per_block(-ca_im, S_STATE))
    def power_table(nvals):
        pr, pi = apow(nvals)
        pr = pr.reshape(len(nvals), dep, S_BLOCKS, S_SW).transpose(1, 2, 0, 3)
        pi = pi.reshape(len(nvals), dep, S_BLOCKS, S_SW).transpose(1, 2, 0, 3)
        return jnp.concatenate([pr, pi], axis=3)

    pin = power_table([q * (r + 1) for r in range(S_SUB)])
    pt = power_table([q * S_SUB * (k + 1) for k in range(nch // S_SUB)])
    return kms, mo, pin, pt


def _merge_kernel(ya_ref, yb_ref, yc_ref, cz_ref, g_ref, x_ref, wa_ref, wb_ref, wglu_ref, wc_ref,
                  wo_ref, nw_ref, *out_refs, final):
    yc0 = jnp.concatenate([yc_ref[j] for j in range(S_BLOCKS)], axis=1)
    glu = _dot(yc0.astype(BF16), wglu_ref[...])
    yc = yc0 * _sigmoid(glu) * cz_ref[...].astype(F32)
    merged = (g_ref[:, 0:D_MODEL].astype(F32) * _dot(ya_ref[...], wa_ref[...])
              + g_ref[:, D_MODEL:2 * D_MODEL].astype(F32) * _dot(yb_ref[...], wb_ref[...])
              + g_ref[:, 2 * D_MODEL:].astype(F32) * _dot(yc.astype(BF16), wc_ref[...]))
    out = x_ref[...] + _dot(merged.astype(BF16), wo_ref[...])
    ms = jnp.mean(out * out, axis=-1, keepdims=True)
    normed = out * lax.rsqrt(ms + EPS) * nw_ref[...]
    if final:
        out_refs[0][...] = normed
    else:
        out_refs[0][...] = out
        out_refs[1][...] = normed.astype(BF16)


def _merge(ya, yb, yc, cz, g, x2, l, wa, wb, wglu, wc, wo, nw, tm, final):
    n = x2.shape[0]
    x_out = jax.ShapeDtypeStruct((n, D_MODEL), F32)
    hn_out = jax.ShapeDtypeStruct((n, D_MODEL), BF16)
    return pl.pallas_call(
        functools.partial(_merge_kernel, final=final),
        grid=(n // tm,),
        in_specs=[_rows(tm, M_W), _rows(tm, G_VW),
                  pl.BlockSpec((S_BLOCKS, tm, LANES), lambda i: (0, i, 0)),
                  _rows(tm, S_W), _rows(tm, 3 * D_MODEL), _rows(tm, D_MODEL),
                  _of_layer(l, (M_W, D_MODEL)), _of_layer(l, (G_VW, D_MODEL)), _of_layer(l, (S_W, S_W)),
                  _of_layer(l, (S_W, D_MODEL)), _of_layer(l, (D_MODEL, D_MODEL)), _whole((1, D_MODEL))],
        out_specs=[_rows(tm, D_MODEL)] if final else [_rows(tm, D_MODEL), _rows(tm, D_MODEL)],
        out_shape=[x_out] if final else [x_out, hn_out],
        compiler_params=_params("arbitrary"),
        name="merge",
    )(ya, yb, yc, cz, g, x2, wa, wb, wglu, wc, wo, nw)


def _pad_heads(w, heads, d, dp):
    zeros = jnp.zeros(w.shape[:-1] + (dp - d,), w.dtype)
    parts = []
    for h in range(heads):
        parts += [w[..., h * d:(h + 1) * d], zeros]
    return jnp.concatenate(parts, axis=-1)


def _prepare(p, nch):
    depth = p["w_in"].shape[0]
    w_mlstm, w_plain, w_gate, w_small = _wprep(jnp.swapaxes(p["w_in"], 1, 2), LANES)
    gkp = functools.partial(_pad_heads, heads=G_HEADS, d=G_DK, dp=G_DKP)
    kms, mo, pin, pt = _s5_tables(p["s5_lam_re"], p["s5_lam_im"], p["s5_log_dt"], p["s5_B_re"], p["s5_B_im"],
                               p["s5_C_re"], p["s5_C_im"], nch)
    gate_b = p["mlstm_gate_b"]
    return dict(
        w_mlstm=w_mlstm, w_plain=w_plain, w_gate=w_gate, w_small=w_small,
        cq=p["mlstm_conv"][:, :, :M_W], ck=p["mlstm_conv"][:, :, M_W:],
        gb=jnp.concatenate([gate_b[:, 0], gate_b[:, 1],
                            jnp.zeros((depth, LANES - 2 * M_HEADS), F32)], axis=1)[:, None, :],
        m_nw=p["mlstm_norm"][:, None, :],
        wal=jnp.concatenate([jnp.zeros((depth, 2 * M_HEADS, G_KWP), F32), gkp(p["gla_w_alpha"]),
                             jnp.zeros((depth, LANES - 2 * M_HEADS - G_RANK, G_KWP), F32)],
                            axis=1).astype(BF16),
        bal=gkp(p["gla_b_alpha"])[:, None, :],
        g_nw=p["gla_norm"][:, None, :],
        kms=kms, mo=mo, pin=pin, pt=pt,
        dskip=p["s5_D"].reshape(depth, S_BLOCKS, 1, LANES),
        wa=p["w_branch_mlstm"].astype(BF16), wb=p["w_branch_gla"].astype(BF16),
        wglu=p["s5_w_glu"].astype(BF16), wc=p["w_branch_s5"].astype(BF16), wo=p["w_out"].astype(BF16),
    )


def _tiles(batch, seq):
    return dict(proj=min(1024, seq), chunk=min(256, seq), g_sub=16,
                s5=min(4096, seq), merge=min(512, batch * seq))


def _layer(x2, hn, batch, seq, l, w, this_norm, next_norm, final, tl):
    tm = tl["proj"]
    nw = this_norm.reshape(1, D_MODEL)
    if hn is None:
        gz, czs, gs, hn = _proj_gate(x2, nw, l, w["w_gate"], tm, True)
    else:
        gz, czs, gs = _proj_gate(hn, nw, l, w["w_gate"], tm, False)
    q, k, oz = _proj_mlstm(hn, l, w["w_mlstm"], w["cq"], w["ck"], seq, tm)
    mv, gv, gq, gk, cu_s, gm, gmt, gcum = _proj_va(hn, l, w["w_plain"], w["w_small"], w["gb"], w["wal"],
                                                   w["bal"], tm, tl["chunk"])
    ya, yb = _mixers(q, k, mv, oz, gm, gmt, gq, gk, gv, gz, gcum, l, w["m_nw"], w["g_nw"], batch, seq,
                     tl["chunk"], tl["g_sub"])
    yc = _s5(cu_s, l, w["kms"], w["mo"], w["pin"], w["pt"], w["dskip"], batch, seq, tl["s5"])
    return _merge(ya, yb, yc, czs, gs, x2, l, w["wa"], w["wb"], w["wglu"], w["wc"], w["wo"],
                  next_norm.reshape(1, D_MODEL), tl["merge"], final)


def kernel(x, norm_w, w_in, mlstm_conv, mlstm_gate_b, mlstm_norm, gla_w_alpha, gla_b_alpha, gla_norm,
           s5_lam_re, s5_lam_im, s5_log_dt, s5_B_re, s5_B_im, s5_C_re, s5_C_im, s5_D, s5_w_glu,
           w_branch_mlstm, w_branch_gla, w_branch_s5, w_out, final_norm):
    batch, seq, _ = x.shape
    depth = norm_w.shape[0]
    tl = _tiles(batch, seq)
    prep = _prepare(dict(w_in=w_in, mlstm_conv=mlstm_conv, mlstm_gate_b=mlstm_gate_b, mlstm_norm=mlstm_norm,
                         gla_w_alpha=gla_w_alpha, gla_b_alpha=gla_b_alpha, gla_norm=gla_norm,
                         s5_lam_re=s5_lam_re, s5_lam_im=s5_lam_im, s5_log_dt=s5_log_dt, s5_B_re=s5_B_re,
                         s5_B_im=s5_B_im, s5_C_re=s5_C_re, s5_C_im=s5_C_im, s5_D=s5_D, s5_w_glu=s5_w_glu,
                         w_branch_mlstm=w_branch_mlstm, w_branch_gla=w_branch_gla,
                         w_branch_s5=w_branch_s5, w_out=w_out), tl["s5"] // S_Q)
    x2 = x.reshape(batch * seq, D_MODEL)
    hn = None
    for l in range(depth):
        final = l == depth - 1
        res = _layer(x2, hn, batch, seq, l, prep, norm_w[l], final_norm if final else norm_w[l + 1], final, tl)
        if final:
            x2 = res[0]
        else:
            x2, hn = res
    return x2.reshape(batch, seq, D_MODEL)
```

```python
import functools
import math

import jax
import jax.numpy as jnp
from jax import lax
from jax.experimental import pallas as pl
from jax.experimental.pallas import tpu as pltpu

F32 = jnp.float32
BF16 = jnp.bfloat16

EPS = 1e-6
D_MODEL = 1024
LANES = 128
SUBLANES = 8
V7X_VMEM_BYTES = 64 * 1024 * 1024
M_HEADS = 4
M_HD = 192
M_HDP = 256
M_W = M_HEADS * M_HD
M_WP = M_HEADS * M_HDP
M_CONV = 4
G_HEADS = 4
G_DK = 96
G_DKP = 128
G_DV = 192
G_DVP = 256
G_KW = G_HEADS * G_DK
G_KWP = G_HEADS * G_DKP
G_VW = G_HEADS * G_DV
G_VWP = G_HEADS * G_DVP
G_RANK = 16
G_TAU = 16.0
S_W = 512
S_GROUP = 16
S_STATE = 64
S_BLOCKS = S_W // LANES
S_GPB = LANES // S_GROUP
S_SW = S_GPB * S_STATE
S_Q = 8
S_SUB = 4
S5_PARTS = 2

IN_SIZES = (M_W, M_W, M_W, M_W, M_HEADS, M_HEADS, M_W,
            G_KW, G_KW, G_VW, G_RANK, G_VW,
            S_W, S_W, 3 * D_MODEL)
SMALL_W = S_W + LANES
CW = 512

NEG = -1e30
G_SAFE_LOG_DECAY = -60.0
VMEM_LIMIT = V7X_VMEM_BYTES * 7 // 8


def _sigmoid(x):
    return 0.5 * jnp.tanh(0.5 * x) + 0.5


def _silu(x):
    h = 0.5 * x
    return h * jnp.tanh(h) + h


def _log_sigmoid(x):
    return jnp.minimum(x, 0.0) - jnp.log1p(jnp.exp(-jnp.abs(x)))


def _dot(a, b):
    return jnp.dot(a, b, preferred_element_type=F32)


def _dot_nt(a, b):
    return lax.dot_general(a, b, (((1,), (1,)), ((), ())), preferred_element_type=F32)


def _dot_tn(a, b):
    return lax.dot_general(a, b, (((0,), (0,)), ((), ())), preferred_element_type=F32)


def _params(*sem):
    return pltpu.CompilerParams(dimension_semantics=sem, vmem_limit_bytes=VMEM_LIMIT)


def _rows(tm, width):
    return pl.BlockSpec((tm, width), lambda i: (i, 0))


def _whole(shape):
    return pl.BlockSpec(shape, lambda i: (0,) * len(shape))


def _of_layer(l, shape):
    return pl.BlockSpec((None,) + shape, lambda *_: (l,) + (0,) * len(shape))


IN_NAMES = ("aq", "ak", "av", "ao", "ai", "af", "az", "bq", "bk", "bv", "ba", "bz", "cu", "cz", "g")
W_OFF = {name: sum(IN_SIZES[:i]) for i, name in enumerate(IN_NAMES)}
IN_WIDTH = sum(IN_SIZES)
W_MLSTM, W_PLAIN, W_GATE = 4 * M_W, M_W + G_VW + 2 * G_KW, G_VW + S_W + 3 * D_MODEL


def _wprep_kernel(w_ref, wm_ref, wp_ref, wg_ref, ws_ref):
    kb = w_ref.shape[2]

    def put(dst, dst_off, parts, width):
        xs = [w_ref[0, off:off + rows, :] for off, rows in parts]
        used = sum(rows for _, rows in parts)
        if used < width:
            xs.append(jnp.zeros((width - used, kb), F32))
        x = xs[0] if len(xs) == 1 else jnp.concatenate(xs, axis=0)
        dst[0, :, dst_off:dst_off + width] = x.T.astype(BF16)

    def group(dst, names):
        off = 0
        for name in names:
            width = IN_SIZES[IN_NAMES.index(name)]
            for c in range(0, width, LANES):
                put(dst, off + c, [(W_OFF[name] + c, LANES)], LANES)
            off += width

    group(wm_ref, ("aq", "ak", "ao", "az"))
    group(wp_ref, ("av", "bv", "bq", "bk"))
    group(wg_ref, ("bz", "cz", "g"))
    group(ws_ref, ("cu",))
    put(ws_ref, S_W, [(W_OFF["ai"], 2 * M_HEADS), (W_OFF["ba"], G_RANK)], LANES)


def _wprep(w_t, kb):
    depth = w_t.shape[0]
    widths = (W_MLSTM, W_PLAIN, W_GATE, SMALL_W)
    return pl.pallas_call(
        _wprep_kernel, grid=(depth, D_MODEL // kb),
        in_specs=[pl.BlockSpec((1, IN_WIDTH, kb), lambda l, i: (l, 0, i))],
        out_specs=[pl.BlockSpec((1, kb, wd), lambda l, i: (l, i, 0)) for wd in widths],
        out_shape=[jax.ShapeDtypeStruct((depth, D_MODEL, wd), BF16) for wd in widths],
        compiler_params=_params("arbitrary", "arbitrary"), name="wprep",
    )(w_t)


def _scatter_heads(dst, y, heads, d, dp, fill):
    rows = y.shape[0]
    for h in range(heads):
        dst[:, h * dp:h * dp + d] = y[:, h * d:(h + 1) * d].astype(BF16)
        dst[:, h * dp + d:(h + 1) * dp] = jnp.full((rows, dp - d), fill, BF16)


def _proj_mlstm_kernel(hn_ref, w_ref, cq_ref, ck_ref, q_ref, k_ref, oz_ref, halo_x, halo_b, *, tm,
                       tiles_per_seq):
    @pl.when(pl.program_id(0) % tiles_per_seq == 0)
    def _():
        halo_x[...] = jnp.zeros_like(halo_x)
        halo_b[...] = jnp.zeros_like(halo_b)

    hn = hn_ref[...]
    row8 = lax.broadcasted_iota(jnp.int32, (SUBLANES, M_W), 0)

    def shift(x, prev8, d):
        rolled = pltpu.roll(x, d, 0)
        head = jnp.where(row8 < d, pltpu.roll(prev8, d, 0), rolled[0:SUBLANES])
        return jnp.concatenate([head, rolled[SUBLANES:]], axis=0)

    def conv_act(c, x, cw_ref, dst, scale):
        x1 = shift(x, halo_x[c], 1)
        pair = cw_ref[1:2, :] * x + cw_ref[0:1, :] * x1
        y = _silu(cw_ref[3:4, :] * x + cw_ref[2:3, :] * x1 + shift(pair, halo_b[c], 2))
        halo_x[c] = x[tm - SUBLANES:tm]
        halo_b[c] = pair[tm - SUBLANES:tm]
        _scatter_heads(dst, y if scale == 1.0 else y * scale, M_HEADS, M_HD, M_HDP, 0.0)

    xq = _dot(hn, w_ref[:, 0:M_W])
    xk = _dot(hn, w_ref[:, M_W:2 * M_W])
    conv_act(0, xq, cq_ref, q_ref, 1.0)
    o = _dot(hn, w_ref[:, 2 * M_W:3 * M_W])
    conv_act(1, xk, ck_ref, k_ref, M_HD ** -0.5)
    z = _dot(hn, w_ref[:, 3 * M_W:])
    oz_ref[...] = (_sigmoid(o) * _silu(z)).astype(BF16)


def _proj_mlstm(hn, l, w, cq, ck, seq, tm):
    n = hn.shape[0]
    padded = jax.ShapeDtypeStruct((n, M_WP), BF16)
    return pl.pallas_call(
        functools.partial(_proj_mlstm_kernel, tm=tm, tiles_per_seq=seq // tm),
        grid=(n // tm,),
        in_specs=[_rows(tm, D_MODEL), _of_layer(l, (D_MODEL, W_MLSTM)), _of_layer(l, (M_CONV, M_W)),
                  _of_layer(l, (M_CONV, M_W))],
        out_specs=[_rows(tm, M_WP), _rows(tm, M_WP), _rows(tm, M_W)],
        out_shape=[padded, padded, jax.ShapeDtypeStruct((n, M_W), BF16)],
        scratch_shapes=[pltpu.VMEM((2, SUBLANES, M_W), F32)] * 2,
        compiler_params=_params("arbitrary"), name="proj_mlstm",
    )(hn, w, cq, ck)


def _proj_gate_kernel(src_ref, nw_ref, w_ref, gz_ref, cz_ref, g_ref, *hn_out, normalize):
    if normalize:
        x = src_ref[...]
        ms = jnp.mean(x * x, axis=-1, keepdims=True)
        hn = (x * lax.rsqrt(ms + EPS) * nw_ref[...]).astype(BF16)
        hn_out[0][...] = hn
    else:
        hn = src_ref[...]
    gz_ref[...] = _silu(_dot(hn, w_ref[:, 0:G_VW])).astype(BF16)
    cz_ref[...] = _silu(_dot(hn, w_ref[:, G_VW:G_VW + S_W])).astype(BF16)
    off = G_VW + S_W
    for c in range(3 * D_MODEL // CW):
        r = _dot(hn, w_ref[:, off + c * CW:off + (c + 1) * CW])
        g_ref[:, c * CW:(c + 1) * CW] = _sigmoid(r).astype(BF16)


def _proj_gate(src, nw, l, w, tm, normalize):
    n = src.shape[0]
    widths = (G_VW, S_W, 3 * D_MODEL) + ((D_MODEL,) if normalize else ())
    return pl.pallas_call(
        functools.partial(_proj_gate_kernel, normalize=normalize), grid=(n // tm,),
        in_specs=[_rows(tm, D_MODEL), _whole((1, D_MODEL)), _of_layer(l, (D_MODEL, W_GATE))],
        out_specs=[_rows(tm, wd) for wd in widths],
        out_shape=[jax.ShapeDtypeStruct((n, wd), BF16) for wd in widths],
        compiler_params=_params("arbitrary"), name="proj_gate",
    )(src, nw, w)


def _split3(x):
    hi = x.astype(BF16)
    r1 = x - hi.astype(F32)
    mid = r1.astype(BF16)
    return hi, mid, (r1 - mid.astype(F32)).astype(BF16)


def _segment_cumsum(x, seg):
    rows, width = x.shape
    tril = (lax.broadcasted_iota(jnp.int32, (seg, seg), 1)
            <= lax.broadcasted_iota(jnp.int32, (seg, seg), 0)).astype(BF16)
    parts = jnp.concatenate(_split3(x), axis=1)
    out = []
    for s in range(rows // seg):
        acc = _dot(tril, parts[s * seg:(s + 1) * seg])
        out.append(acc[:, :width] + acc[:, width:2 * width] + acc[:, 2 * width:])
    return jnp.concatenate(out, axis=0)


def _proj_va_kernel(hn_ref, wp_ref, ws_ref, gb_ref, wal_ref, bal_ref,
                    mv_ref, gv_ref, gq_ref, gk_ref, cu_ref, gmb_ref, gmt_ref, gcum_ref, *, chunk):
    hn = hn_ref[...]
    r = _dot(hn, ws_ref[...])
    _scatter_heads(mv_ref, _dot(hn, wp_ref[:, 0:M_W]), M_HEADS, M_HD, M_HDP, 0.0)
    for j in range(S_BLOCKS):
        cu_ref[j] = r[:, j * LANES:(j + 1) * LANES]
    raw = r[:, S_W:]
    gates = raw + gb_ref[...]
    lane = lax.broadcasted_iota(jnp.int32, gates.shape, 1)
    fgate = (lane >= M_HEADS) & (lane < 2 * M_HEADS)
    za = _dot(raw.astype(BF16), wal_ref[...]) + bal_ref[...]
    gm = jnp.where(fgate, _segment_cumsum(_log_sigmoid(gates), chunk), gates)
    _scatter_heads(gv_ref, _dot(hn, wp_ref[:, M_W:M_W + G_VW]), G_HEADS, G_DV, G_DVP, 0.0)
    gmt_ref[...] = gm.T
    slots = 2 * M_HEADS * LANES
    spread = (lax.broadcasted_iota(jnp.int32, (3 * LANES, slots), 0) % LANES
              == lax.broadcasted_iota(jnp.int32, (3 * LANES, slots), 1) // LANES).astype(BF16)
    gmb_ref[...] = _dot(jnp.concatenate(_split3(gm), axis=1), spread)
    off = M_W + G_VW
    _scatter_heads(gq_ref, _dot(hn, wp_ref[:, off:off + G_KW]) * (G_DK ** -0.5), G_HEADS, G_DK, G_DKP, 0.0)
    gcum_ref[...] = _segment_cumsum(_log_sigmoid(za) * (1.0 / G_TAU), chunk)
    _scatter_heads(gk_ref, _dot(hn, wp_ref[:, off + G_KW:]), G_HEADS, G_DK, G_DKP, 0.0)


def _proj_va(hn, l, wp, ws, gb, wal, bal, tm, chunk):
    n = hn.shape[0]
    widths = (M_WP, G_VWP, G_KWP, G_KWP)
    return pl.pallas_call(
        functools.partial(_proj_va_kernel, chunk=chunk), grid=(n // tm,),
        in_specs=[_rows(tm, D_MODEL), _of_layer(l, (D_MODEL, W_PLAIN)), _of_layer(l, (D_MODEL, SMALL_W)),
                  _of_layer(l, (1, LANES)), _of_layer(l, (LANES, G_KWP)), _of_layer(l, (1, G_KWP))],
        out_specs=[_rows(tm, wd) for wd in widths] + [
            pl.BlockSpec((S_BLOCKS, tm, LANES), lambda i: (0, i, 0)), _rows(tm, 2 * M_HEADS * LANES),
            pl.BlockSpec((LANES, tm), lambda i: (0, i)), _rows(tm, G_KWP)],
        out_shape=[jax.ShapeDtypeStruct((n, wd), BF16) for wd in widths] + [
            jax.ShapeDtypeStruct((S_BLOCKS, n, LANES), F32),
            jax.ShapeDtypeStruct((n, 2 * M_HEADS * LANES), F32),
            jax.ShapeDtypeStruct((LANES, n), F32),
            jax.ShapeDtypeStruct((n, G_KWP), F32)],
        compiler_params=_params("arbitrary"), name="proj_va",
    )(hn, wp, ws, gb, wal, bal)


def _mlstm_body(q_ref, k_ref, v_ref, oz_ref, gmb_ref, gmt_ref, nw_ref, out_ref, c_scr, m_scr, *, tc):
    row = lax.broadcasted_iota(jnp.int32, (tc, tc), 0)
    col = lax.broadcasted_iota(jnp.int32, (tc, tc), 1)
    causal = col <= row
    real = lax.broadcasted_iota(jnp.int32, (tc, M_HDP), 1) < M_HD
    ones_tile = jnp.ones((tc, LANES), BF16)

    def lanes(x, n):
        return jnp.concatenate([x] * n, axis=1)

    def head(h):
        sl = slice(h * M_HDP, (h + 1) * M_HDP)
        qb = q_ref[:, sl]
        kb = k_ref[:, sl]
        vext = jnp.concatenate([v_ref[:, sl], ones_tile], axis=1)
        i_b = gmb_ref[:, h * LANES:(h + 1) * LANES]
        bcum = gmb_ref[:, (M_HEADS + h) * LANES:(M_HEADS + h + 1) * LANES]
        g_row = gmt_ref[h:h + 1, :] - gmt_ref[M_HEADS + h:M_HEADS + h + 1, :]
        m_prev = m_scr[h, 0:1, :]

        inter = bcum + m_prev
        m_t = jnp.maximum(inter, bcum + jnp.max(jnp.where(causal, g_row, NEG), axis=1, keepdims=True))
        w_intra = jnp.exp(jnp.where(causal, g_row + lanes(bcum - m_t, tc // LANES), NEG))
        w_inter = jnp.exp(inter - m_t)
        s = _dot_nt(qb, kb) * w_intra
        c_mat = c_scr[h]
        num = _dot(s.astype(BF16), vext) + lanes(w_inter, 3) * _dot(qb, c_mat.astype(BF16))
        den = num[:, M_HDP:]
        rden = 1.0 / jnp.maximum(jnp.abs(den), jnp.exp(-m_t))
        num = num[:, :M_HDP]

        b_end = bcum[tc - 1:tc, :]
        to_end = b_end - bcum + i_b
        m_new = jnp.maximum(b_end + m_prev, jnp.max(to_end, axis=0, keepdims=True))
        wk = jnp.exp(to_end - m_new)
        w_prev = jnp.exp(b_end + m_prev - m_new)
        c_scr[h] = lanes(w_prev, 3) * c_mat + _dot_tn((lanes(wk, 2) * kb.astype(F32)).astype(BF16), vext)
        m_scr[h] = jnp.broadcast_to(m_new, (SUBLANES, LANES))

        sc = slice(h * M_HD, (h + 1) * M_HD)
        mean = jnp.sum(num, axis=1, keepdims=True) * (1.0 / M_HD)
        xc = jnp.where(real, num - mean, 0.0)
        var = jnp.sum(xc * xc, axis=1, keepdims=True) * (1.0 / M_HD)
        scale = rden * lax.rsqrt(rden * rden * var + EPS)
        ha = (xc * lanes(scale, 2))[:, :M_HD]
        out_ref[:, sc] = (ha * nw_ref[:, sc] * oz_ref[:, sc].astype(F32)).astype(BF16)

    return [functools.partial(head, h) for h in range(M_HEADS)]


def _gla_body(safe, q_ref, k_ref, v_ref, z_ref, b_ref, nw_ref,
              out_ref, kbuf, bbuf, attn_scr, st_scr, *, t, cs):
    row = lax.broadcasted_iota(jnp.int32, (t, t), 0)
    col = lax.broadcasted_iota(jnp.int32, (t, t), 1)
    causal = col <= row

    def operands(h):
        sl = slice(h * G_DKP, (h + 1) * G_DKP)
        return q_ref[:, sl].astype(F32), k_ref[:, sl].astype(F32), b_ref[:, sl]

    @pl.when(safe)
    def _():
        for h in range(G_HEADS):
            qs, kk, b = operands(h)
            a = _dot_nt((qs * jnp.exp(b)).astype(BF16), (kk * jnp.exp(-b)).astype(BF16))
            attn_scr[h] = jnp.where(causal, a, 0.0)

    @pl.when(jnp.logical_not(safe))
    def _():
        rowv = lax.broadcasted_iota(jnp.int32, (t, G_DKP), 0)
        sub = rowv % cs
        lag = row - col
        kbuf[0:cs, :] = jnp.zeros((cs, G_DKP), F32)
        bbuf[0:cs, :] = jnp.zeros((cs, G_DKP), F32)
        for h in range(G_HEADS):
            qs, kk, b = operands(h)
            qparts, kparts = [], []
            for i in range(1, t // cs):
                r = b[i * cs - 1:i * cs, :]
                inblk = (rowv >= i * cs) & (rowv < (i + 1) * cs)
                qparts.append((qs * jnp.exp(jnp.where(inblk, b - r, NEG))).astype(BF16))
                kparts.append((kk * jnp.exp(jnp.where(rowv < i * cs, r - b, NEG))).astype(BF16))
            attn = _dot_nt(jnp.concatenate(qparts, axis=1), jnp.concatenate(kparts, axis=1))
            kbuf[cs:cs + t, :] = kk
            bbuf[cs:cs + t, :] = b
            for j in range(cs):
                ksh = kbuf[cs - j:cs - j + t, :]
                bsh = bbuf[cs - j:cs - j + t, :]
                e = jnp.exp(jnp.where(sub >= j, b - bsh, NEG))
                dj = jnp.sum(qs * ksh * e, axis=1, keepdims=True)
                attn = attn + jnp.where(lag == j, dj, 0.0)
            attn_scr[h] = attn

    def tail(h):
        slv = slice(h * G_DVP, (h + 1) * G_DVP)
        qs, kk, b = operands(h)
        v = v_ref[:, slv]
        blast = b[t - 1:t, :]
        st = st_scr[h]
        o = (_dot_nt((qs * jnp.exp(b)).astype(BF16), st.astype(BF16))
             + _dot(attn_scr[h].astype(BF16), v))
        kdec = (kk * jnp.exp(blast - b)).astype(BF16)
        st_scr[h] = jnp.exp(blast) * st + _dot_tn(v, kdec)

        sc = slice(h * G_DV, (h + 1) * G_DV)
        var = jnp.sum(o * o, axis=1, keepdims=True) * (1.0 / G_DV)
        hb = (o * lax.rsqrt(var + EPS))[:, :G_DV] * nw_ref[:, sc]
        out_ref[:, sc] = (hb * z_ref[:, sc].astype(F32)).astype(BF16)

    return [functools.partial(tail, h) for h in range(G_HEADS)]


def _mixers_kernel(safe_ref, mq_ref, mk_ref, mv_ref, moz_ref, gmb_ref, gmt_ref, mnw_ref,
                   gq_ref, gk_ref, gv_ref, gz_ref, gb_ref, gnw_ref, ya_ref, yb_ref,
                   c_scr, m_scr, kbuf, bbuf, attn_scr, st_scr, *, t, cs):
    @pl.when(pl.program_id(1) == 0)
    def _():
        c_scr[...] = jnp.zeros_like(c_scr)
        m_scr[...] = jnp.zeros_like(m_scr)
        st_scr[...] = jnp.zeros_like(st_scr)

    safe = safe_ref[pl.program_id(0) * pl.num_programs(1) + pl.program_id(1)] != 0
    gla_heads = _gla_body(safe, gq_ref, gk_ref, gv_ref, gz_ref, gb_ref, gnw_ref, yb_ref, kbuf, bbuf,
                          attn_scr, st_scr, t=t, cs=cs)
    mlstm_heads = _mlstm_body(mq_ref, mk_ref, mv_ref, moz_ref, gmb_ref, gmt_ref, mnw_ref, ya_ref,
                              c_scr, m_scr, tc=t)
    for head in gla_heads + mlstm_heads:
        head()


def _mixers(mq, mk, mv, moz, gmb, gmt, gq, gk, gv, gz, gcum, l, m_nw, g_nw, batch, seq, t, cs):
    n = batch * seq
    nt = seq // t

    def blk(width):
        return pl.BlockSpec((t, width), lambda b, c: (b * nt + c, 0))

    safe = (jnp.min(gcum.reshape(n // t, t, G_KWP)[:, t - 1, :], axis=-1) >= G_SAFE_LOG_DECAY).astype(jnp.int32)
    return pl.pallas_call(
        functools.partial(_mixers_kernel, t=t, cs=cs),
        grid=(batch, nt),
        in_specs=[pl.BlockSpec(memory_space=pltpu.SMEM),
                  blk(M_WP), blk(M_WP), blk(M_WP), blk(M_W), blk(2 * M_HEADS * LANES),
                  pl.BlockSpec((LANES, t), lambda b, c: (0, b * nt + c)), _of_layer(l, (1, M_W)),
                  blk(G_KWP), blk(G_KWP), blk(G_VWP), blk(G_VW), blk(G_KWP), _of_layer(l, (1, G_VW))],
        out_specs=[blk(M_W), blk(G_VW)],
        out_shape=[jax.ShapeDtypeStruct((n, M_W), BF16), jax.ShapeDtypeStruct((n, G_VW), BF16)],
        scratch_shapes=[pltpu.VMEM((M_HEADS, M_HDP, M_HDP + LANES), F32),
                        pltpu.VMEM((M_HEADS, SUBLANES, LANES), F32),
                        pltpu.VMEM((t + cs, G_DKP), F32), pltpu.VMEM((t + cs, G_DKP), F32),
                        pltpu.VMEM((G_HEADS, t, t), F32),
                        pltpu.VMEM((G_HEADS, G_DVP, G_DKP), F32)],
        compiler_params=_params("arbitrary", "arbitrary"),
        name="mixers",
    )(safe, mq, mk, mv, moz, gmb, gmt, m_nw, gq, gk, gv, gz, gcum, g_nw)


def _gelu_tanh(x):
    return 0.5 * x * (1.0 + jnp.tanh(math.sqrt(2.0 / math.pi) * (x + 0.044715 * (x * x * x))))


def _s5_kernel(u_ref, kms_ref, mo_ref, pin_ref, pt_ref, d_ref, y_ref, carry, xs, *, nch):
    @pl.when(pl.program_id(2) == 0)
    def _():
        carry[...] = jnp.zeros_like(carry)

    qw = S_Q * LANES
    ltiles = S_SW // LANES
    parts = S5_PARTS if nch % (S5_PARTS * S_SUB) == 0 else 1
    pch = nch // parts
    ngrp = pch // S_SUB
    ucat = jnp.concatenate([u_ref[0, pl.ds(s, nch, stride=S_Q), :] for s in range(S_Q)],
                           axis=1).astype(BF16)
    x = _dot(ucat, kms_ref[0, :, qw:])
    y_in = _dot(ucat, kms_ref[0, :, :qw])

    def axpy(xr, xi, pr, pi, sr, si):
        return xr + (pr * sr - pi * si), xi + (pr * si + pi * sr)

    def scan(xr, xi, p_ref, pos, length):
        d = 1
        while d < length:
            keep = pos >= d
            sr = jnp.where(keep, pltpu.roll(xr, d, 0), 0.0)
            si = jnp.where(keep, pltpu.roll(xi, d, 0), 0.0)
            xr, xi = axpy(xr, xi, p_ref[0, d - 1:d, :S_SW], p_ref[0, d - 1:d, S_SW:], sr, si)
            d *= 2
        return xr, xi

    def slabs(part, arr):
        return [(part * ltiles + j, arr[:, j * LANES:(j + 1) * LANES]) for j in range(ltiles)]

    rowi = lax.broadcasted_iota(jnp.int32, (pch, S_SW), 0)
    rowg = lax.broadcasted_iota(jnp.int32, (ngrp, S_SW), 0)
    pwr = jnp.concatenate([pin_ref[0, :, :S_SW]] * ngrp, axis=0)
    pwi = jnp.concatenate([pin_ref[0, :, S_SW:]] * ngrp, axis=0)
    cr = carry[0:1, :S_SW]
    ci = carry[0:1, S_SW:]
    for p in range(parts):
        c0 = p * pch
        xr, xi = scan(x[c0:c0 + pch, :S_SW], x[c0:c0 + pch, S_SW:], pin_ref, rowi % S_SUB, S_SUB)
        for part, arr in ((0, xr), (1, xi)):
            for j, piece in slabs(part, arr):
                xs[j, c0:c0 + pch, :] = piece

        def group_ends(part):
            return jnp.concatenate([xs[part * ltiles + j, pl.ds(c0 + S_SUB - 1, ngrp, stride=S_SUB), :]
                                    for j in range(ltiles)], axis=1)

        er, ei = scan(group_ends(0), group_ends(1), pt_ref, rowg, ngrp)
        er, ei = axpy(er, ei, pt_ref[0, 0:ngrp, :S_SW], pt_ref[0, 0:ngrp, S_SW:], cr, ci)
        first = rowg == 0
        gr = jnp.where(first, cr, pltpu.roll(er, 1, 0))
        gi = jnp.where(first, ci, pltpu.roll(ei, 1, 0))
        for part, arr in ((0, gr), (1, gi)):
            for j, piece in slabs(part, arr):
                for k in range(S_SUB):
                    xs[j, pl.ds(c0 + k, ngrp, stride=S_SUB), :] = piece
        sr = jnp.concatenate([xs[j, c0:c0 + pch, :] for j in range(ltiles)], axis=1)
        si = jnp.concatenate([xs[ltiles + j, c0:c0 + pch, :] for j in range(ltiles)], axis=1)
        xr, xi = axpy(xr, xi, pwr, pwi, sr, si)

        firstc = rowi == 0
        xpr = jnp.where(firstc, cr, pltpu.roll(xr, 1, 0))
        xpi = jnp.where(firstc, ci, pltpu.roll(xi, 1, 0))
        y = y_in[c0:c0 + pch] + _dot(jnp.concatenate([xpr, xpi], axis=1).astype(BF16), mo_ref[0])
        r0 = c0 * S_Q
        for t in range(S_Q):
            y_ref[0, pl.ds(r0 + t, pch, stride=S_Q), :] = y[:, t * LANES:(t + 1) * LANES]
        rows = slice(r0, r0 + pch * S_Q)
        y_ref[0, rows, :] = _gelu_tanh(y_ref[0, rows, :] + d_ref[0] * u_ref[0, rows, :])
        cr, ci = er[ngrp - 1:ngrp, :], ei[ngrp - 1:ngrp, :]
    carry[0:1, :S_SW] = cr
    carry[0:1, S_SW:] = ci


def _s5(cu, l, kms, mo, pin, pt, dskip, batch, seq, tm):
    n = batch * seq
    nt = seq // tm
    nch = tm // S_Q
    qw = S_Q * LANES

    def table(rows, cols):
        return pl.BlockSpec((None, 1, rows, cols), lambda j, b, t: (l, j, 0, 0))

    return pl.pallas_call(
        functools.partial(_s5_kernel, nch=nch),
        grid=(S_BLOCKS, batch, nt),
        in_specs=[pl.BlockSpec((1, tm, LANES), lambda j, b, t: (j, b * nt + t, 0)),
                  table(qw, qw + 2 * S_SW), table(2 * S_SW, qw), table(S_SUB, 2 * S_SW),
                  table(nch // S_SUB, 2 * S_SW), table(1, LANES)],
        out_specs=pl.BlockSpec((1, tm, LANES), lambda j, b, t: (j, b * nt + t, 0)),
        out_shape=jax.ShapeDtypeStruct((S_BLOCKS, n, LANES), F32),
        scratch_shapes=[pltpu.VMEM((SUBLANES, 2 * S_SW), F32),
                        pltpu.VMEM((2 * S_SW // LANES, nch, LANES), F32)],
        compiler_params=_params("arbitrary", "arbitrary", "arbitrary"),
        name="s5",
    )(cu, kms, mo, pin, pt, dskip)


def _s5_expand_kernel(msr_ref, msi_ref, mor_ref, moi_ref, kms_ref, mo_ref):
    q = S_Q
    qw = q * LANES

    def embed(m, rows, width):
        shape = (width, S_GPB * width)
        tile = (lax.broadcasted_iota(jnp.int32, shape, 1) % width
                == lax.broadcasted_iota(jnp.int32, shape, 0)).astype(BF16)
        oshape = (S_GPB * rows, S_GPB * width)
        diag = (lax.broadcasted_iota(jnp.int32, oshape, 0) // rows
                == lax.broadcasted_iota(jnp.int32, oshape, 1) // width)
        return jnp.where(diag, _dot(m, tile), 0.0).astype(BF16)

    ms = [(embed(msr_ref[0, 0, s], S_GROUP, S_STATE), embed(msi_ref[0, 0, s], S_GROUP, S_STATE))
          for s in range(q)]
    mo = [(embed(mor_ref[0, 0, tau], S_STATE, S_GROUP), embed(moi_ref[0, 0, tau], S_STATE, S_GROUP))
          for tau in range(q + 1)]
    bbr, bbi = ms[q - 1]
    kd = [(_dot(bbr, mo[tau][0]) + _dot(bbi, mo[tau][1])).astype(BF16) for tau in range(q)]
    zero = jnp.zeros((LANES, LANES), BF16)
    for s in range(q):
        rows = slice(s * LANES, (s + 1) * LANES)
        for t in range(q):
            kms_ref[0, 0, rows, t * LANES:(t + 1) * LANES] = kd[t - s] if t >= s else zero
        kms_ref[0, 0, rows, qw:qw + S_SW] = ms[s][0]
        kms_ref[0, 0, rows, qw + S_SW:] = ms[s][1]
    for t in range(q):
        cols = slice(t * LANES, (t + 1) * LANES)
        mo_ref[0, 0, 0:S_SW, cols] = mo[t + 1][0]
        mo_ref[0, 0, S_SW:, cols] = mo[t + 1][1]


def _s5_expand(msr, msi, mor, moi):
    depth = msr.shape[0]
    qw = S_Q * LANES

    def blk(a):
        return pl.BlockSpec((1, 1) + a.shape[2:], lambda d, j: (d, j, 0, 0, 0))

    return pl.pallas_call(
        _s5_expand_kernel, grid=(depth, S_BLOCKS),
        in_specs=[blk(a) for a in (msr, msi, mor, moi)],
        out_specs=[pl.BlockSpec((1, 1, qw, qw + 2 * S_SW), lambda d, j: (d, j, 0, 0)),
                   pl.BlockSpec((1, 1, 2 * S_SW, qw), lambda d, j: (d, j, 0, 0))],
        out_shape=[jax.ShapeDtypeStruct((depth, S_BLOCKS, qw, qw + 2 * S_SW), BF16),
                   jax.ShapeDtypeStruct((depth, S_BLOCKS, 2 * S_SW, qw), BF16)],
        compiler_params=_params("arbitrary", "arbitrary"), name="s5_expand",
    )(msr, msi, mor, moi)


def _s5_tables(lam_re, lam_im, log_dt, b_re, b_im, c_re, c_im, nch):
    q = S_Q
    dep = lam_re.shape[0]
    lr = jnp.minimum(lam_re.astype(F32), -1e-4)
    li = lam_im.astype(F32)
    dt = jnp.exp(log_dt.astype(F32))[..., None]
    mag = jnp.exp(lr * dt)
    ab_re = mag * jnp.cos(li * dt)
    ab_im = mag * jnp.sin(li * dt)
    nr = ab_re - 1.0
    den = lr * lr + li * li
    coef_re = ((nr * lr + ab_im * li) / den)[:, :, None, :]
    coef_im = ((ab_im * lr - nr * li) / den)[:, :, None, :]
    brt = jnp.swapaxes(b_re.astype(F32), -1, -2)
    bit = jnp.swapaxes(b_im.astype(F32), -1, -2)
    bb_re = coef_re * brt - coef_im * bit
    bb_im = coef_re * bit + coef_im * brt

    def apow(nvals):
        e = jnp.asarray(nvals, F32)[:, None, None, None]
        m = jnp.exp(e * (lr * dt))
        ang = e * (li * dt)
        return m * jnp.cos(ang), m * jnp.sin(ang)

    def per_block(m, rows):
        m = m.reshape(m.shape[0], dep, S_BLOCKS, S_GPB * rows, m.shape[-1])
        return m.transpose(1, 2, 0, 3, 4).astype(BF16)

    ar, ai = apow(list(range(q + 1)))
    crt = jnp.swapaxes(c_re.astype(F32), -1, -2)
    cit = jnp.swapaxes(c_im.astype(F32), -1, -2)
    ca_re = crt[None] * ar[..., None] - cit[None] * ai[..., None]
    ca_im = crt[None] * ai[..., None] + cit[None] * ar[..., None]
    pw_r = ar[:q][::-1][:, :, :, None, :]
    pw_i = ai[:q][::-1][:, :, :, None, :]
    kms, mo = _s5_expand(per_block(pw_r * bb_re - pw_i * bb_im, S_GROUP),
                         per_block(pw_r * bb_im + pw_i * bb_re, S_GROUP),
                         per_block(ca_re, S_STATE), per_block(-ca_im, S_STATE))
    def power_table(nvals):
        pr, pi = apow(nvals)
        pr = pr.reshape(len(nvals), dep, S_BLOCKS, S_SW).transpose(1, 2, 0, 3)
        pi = pi.reshape(len(nvals), dep, S_BLOCKS, S_SW).transpose(1, 2, 0, 3)
        return jnp.concatenate([pr, pi], axis=3)

    pin = power_table([q * (r + 1) for r in range(S_SUB)])
    pt = power_table([q * S_SUB * (k + 1) for k in range(nch // S_SUB)])
    return kms, mo, pin, pt


def _merge_kernel(ya_ref, yb_ref, yc_ref, cz_ref, g_ref, x_ref, wa_ref, wb_ref, wglu_ref, wc_ref,
                  wo_ref, nw_ref, *out_refs, final):
    yc0 = jnp.concatenate([yc_ref[j] for j in range(S_BLOCKS)], axis=1)
    glu = _dot(yc0.astype(BF16), wglu_ref[...])
    yc = yc0 * _sigmoid(glu) * cz_ref[...].astype(F32)
    merged = (g_ref[:, 0:D_MODEL].astype(F32) * _dot(ya_ref[...], wa_ref[...])
              + g_ref[:, D_MODEL:2 * D_MODEL].astype(F32) * _dot(yb_ref[...], wb_ref[...])
              + g_ref[:, 2 * D_MODEL:].astype(F32) * _dot(yc.astype(BF16), wc_ref[...]))
    out = x_ref[...] + _dot(merged.astype(BF16), wo_ref[...])
    ms = jnp.mean(out * out, axis=-1, keepdims=True)
    normed = out * lax.rsqrt(ms + EPS) * nw_ref[...]
    if final:
        out_refs[0][...] = normed
    else:
        out_refs[0][...] = out
        out_refs[1][...] = normed.astype(BF16)


def _merge(ya, yb, yc, cz, g, x2, l, wa, wb, wglu, wc, wo, nw, tm, final):
    n = x2.shape[0]
    x_out = jax.ShapeDtypeStruct((n, D_MODEL), F32)
    hn_out = jax.ShapeDtypeStruct((n, D_MODEL), BF16)
    return pl.pallas_call(
        functools.partial(_merge_kernel, final=final),
        grid=(n // tm,),
        in_specs=[_rows(tm, M_W), _rows(tm, G_VW),
                  pl.BlockSpec((S_BLOCKS, tm, LANES), lambda i: (0, i, 0)),
                  _rows(tm, S_W), _rows(tm, 3 * D_MODEL), _rows(tm, D_MODEL),
                  _of_layer(l, (M_W, D_MODEL)), _of_layer(l, (G_VW, D_MODEL)), _of_layer(l, (S_W, S_W)),
                  _of_layer(l, (S_W, D_MODEL)), _of_layer(l, (D_MODEL, D_MODEL)), _whole((1, D_MODEL))],
        out_specs=[_rows(tm, D_MODEL)] if final else [_rows(tm, D_MODEL), _rows(tm, D_MODEL)],
        out_shape=[x_out] if final else [x_out, hn_out],
        compiler_params=_params("arbitrary"),
        name="merge",
    )(ya, yb, yc, cz, g, x2, wa, wb, wglu, wc, wo, nw)


def _pad_heads(w, heads, d, dp):
    zeros = jnp.zeros(w.shape[:-1] + (dp - d,), w.dtype)
    parts = []
    for h in range(heads):
        parts += [w[..., h * d:(h + 1) * d], zeros]
    return jnp.concatenate(parts, axis=-1)


def _prepare(p, nch):
    depth = p["w_in"].shape[0]
    w_mlstm, w_plain, w_gate, w_small = _wprep(jnp.swapaxes(p["w_in"], 1, 2), LANES)
    gkp = functools.partial(_pad_heads, heads=G_HEADS, d=G_DK, dp=G_DKP)
    kms, mo, pin, pt = _s5_tables(p["s5_lam_re"], p["s5_lam_im"], p["s5_log_dt"], p["s5_B_re"], p["s5_B_im"],
                               p["s5_C_re"], p["s5_C_im"], nch)
    gate_b = p["mlstm_gate_b"]
    return dict(
        w_mlstm=w_mlstm, w_plain=w_plain, w_gate=w_gate, w_small=w_small,
        cq=p["mlstm_conv"][:, :, :M_W], ck=p["mlstm_conv"][:, :, M_W:],
        gb=jnp.concatenate([gate_b[:, 0], gate_b[:, 1],
                            jnp.zeros((depth, LANES - 2 * M_HEADS), F32)], axis=1)[:, None, :],
        m_nw=p["mlstm_norm"][:, None, :],
        wal=jnp.concatenate([jnp.zeros((depth, 2 * M_HEADS, G_KWP), F32), gkp(p["gla_w_alpha"]),
                             jnp.zeros((depth, LANES - 2 * M_HEADS - G_RANK, G_KWP), F32)],
                            axis=1).astype(BF16),
        bal=gkp(p["gla_b_alpha"])[:, None, :],
        g_nw=p["gla_norm"][:, None, :],
        kms=kms, mo=mo, pin=pin, pt=pt,
        dskip=p["s5_D"].reshape(depth, S_BLOCKS, 1, LANES),
        wa=p["w_branch_mlstm"].astype(BF16), wb=p["w_branch_gla"].astype(BF16),
        wglu=p["s5_w_glu"].astype(BF16), wc=p["w_branch_s5"].astype(BF16), wo=p["w_out"].astype(BF16),
    )


def _tiles(batch, seq):
    return dict(proj=min(1024, seq), chunk=min(256, seq), g_sub=16,
                s5=min(4096, seq), merge=min(512, batch * seq))


def _layer(x2, hn, batch, seq, l, w, this_norm, next_norm, final, tl):
    tm = tl["proj"]
    nw = this_norm.reshape(1, D_MODEL)
    if hn is None:
        gz, czs, gs, hn = _proj_gate(x2, nw, l, w["w_gate"], tm, True)
    else:
        gz, czs, gs = _proj_gate(hn, nw, l, w["w_gate"], tm, False)
    q, k, oz = _proj_mlstm(hn, l, w["w_mlstm"], w["cq"], w["ck"], seq, tm)
    mv, gv, gq, gk, cu_s, gm, gmt, gcum = _proj_va(hn, l, w["w_plain"], w["w_small"], w["gb"], w["wal"],
                                                   w["bal"], tm, tl["chunk"])
    ya, yb = _mixers(q, k, mv, oz, gm, gmt, gq, gk, gv, gz, gcum, l, w["m_nw"], w["g_nw"], batch, seq,
                     tl["chunk"], tl["g_sub"])
    yc = _s5(cu_s, l, w["kms"], w["mo"], w["pin"], w["pt"], w["dskip"], batch, seq, tl["s5"])
    return _merge(ya, yb, yc, czs, gs, x2, l, w["wa"], w["wb"], w["wglu"], w["wc"], w["wo"],
                  next_norm.reshape(1, D_MODEL), tl["merge"], final)


def kernel(x, norm_w, w_in, mlstm_conv, mlstm_gate_b, mlstm_norm, gla_w_alpha, gla_b_alpha, gla_norm,
           s5_lam_re, s5_lam_im, s5_log_dt, s5_B_re, s5_B_im, s5_C_re, s5_C_im, s5_D, s5_w_glu,
           w_branch_mlstm, w_branch_gla, w_branch_s5, w_out, final_norm):
    batch, seq, _ = x.shape
    depth = norm_w.shape[0]
    tl = _tiles(batch, seq)
    prep = _prepare(dict(w_in=w_in, mlstm_conv=mlstm_conv, mlstm_gate_b=mlstm_gate_b, mlstm_norm=mlstm_norm,
                         gla_w_alpha=gla_w_alpha, gla_b_alpha=gla_b_alpha, gla_norm=gla_norm,
                         s5_lam_re=s5_lam_re, s5_lam_im=s5_lam_im, s5_log_dt=s5_log_dt, s5_B_re=s5_B_re,
                         s5_B_im=s5_B_im, s5_C_re=s5_C_re, s5_C_im=s5_C_im, s5_D=s5_D, s5_w_glu=s5_w_glu,
                         w_branch_mlstm=w_branch_mlstm, w_branch_gla=w_branch_gla,
                         w_branch_s5=w_branch_s5, w_out=w_out), tl["s5"] // S_Q)
    x2 = x.reshape(batch * seq, D_MODEL)
    hn = None
    for l in range(depth):
        final = l == depth - 1
        res = _layer(x2, hn, batch, seq, l, prep, norm_w[l], final_norm if final else norm_w[l + 1], final, tl)
        if final:
            x2 = res[0]
        else:
            x2, hn = res
    return x2.reshape(batch, seq, D_MODEL)
```

```python
import functools
import math

import jax
import jax.numpy as jnp
from jax import lax
from jax.experimental import pallas as pl
from jax.experimental.pallas import tpu as pltpu

F32 = jnp.float32
BF16 = jnp.bfloat16

EPS = 1e-6
D_MODEL = 1024
LANES = 128
SUBLANES = 8
V7X_VMEM_BYTES = 64 * 1024 * 1024
M_HEADS = 4
M_HD = 192
M_HDP = 256
M_W = M_HEADS * M_HD
M_WP = M_HEADS * M_HDP
M_CONV = 4
G_HEADS = 4
G_DK = 96
G_DKP = 128
G_DV = 192
G_DVP = 256
G_KW = G_HEADS * G_DK
G_KWP = G_HEADS * G_DKP
G_VW = G_HEADS * G_DV
G_VWP = G_HEADS * G_DVP
G_RANK = 16
G_TAU = 16.0
S_W = 512
S_GROUP = 16
S_STATE = 64
S_BLOCKS = S_W // LANES
S_GPB = LANES // S_GROUP
S_SW = S_GPB * S_STATE
S_Q = 8
S_SUB = 4
S5_PARTS = 2

IN_SIZES = (M_W, M_W, M_W, M_W, M_HEADS, M_HEADS, M_W,
            G_KW, G_KW, G_VW, G_RANK, G_VW,
            S_W, S_W, 3 * D_MODEL)
SMALL_W = S_W + LANES
CW = 512

NEG = -1e30
G_SAFE_LOG_DECAY = -60.0
VMEM_LIMIT = V7X_VMEM_BYTES * 7 // 8


def _sigmoid(x):
    return 0.5 * jnp.tanh(0.5 * x) + 0.5


def _silu(x):
    h = 0.5 * x
    return h * jnp.tanh(h) + h


def _log_sigmoid(x):
    return jnp.minimum(x, 0.0) - jnp.log1p(jnp.exp(-jnp.abs(x)))


def _dot(a, b):
    return jnp.dot(a, b, preferred_element_type=F32)


def _dot_nt(a, b):
    return lax.dot_general(a, b, (((1,), (1,)), ((), ())), preferred_element_type=F32)


def _dot_tn(a, b):
    return lax.dot_general(a, b, (((0,), (0,)), ((), ())), preferred_element_type=F32)


def _params(*sem):
    return pltpu.CompilerParams(dimension_semantics=sem, vmem_limit_bytes=VMEM_LIMIT)


def _rows(tm, width):
    return pl.BlockSpec((tm, width), lambda i: (i, 0))


def _whole(shape):
    return pl.BlockSpec(shape, lambda i: (0,) * len(shape))


def _of_layer(l, shape):
    return pl.BlockSpec((None,) + shape, lambda *_: (l,) + (0,) * len(shape))


IN_NAMES = ("aq", "ak", "av", "ao", "ai", "af", "az", "bq", "bk", "bv", "ba", "bz", "cu", "cz", "g")
W_OFF = {name: sum(IN_SIZES[:i]) for i, name in enumerate(IN_NAMES)}
IN_WIDTH = sum(IN_SIZES)
W_MLSTM, W_PLAIN, W_GATE = 4 * M_W, M_W + G_VW + 2 * G_KW, G_VW + S_W + 3 * D_MODEL


def _wprep_kernel(w_ref, wm_ref, wp_ref, wg_ref, ws_ref):
    kb = w_ref.shape[2]

    def put(dst, dst_off, parts, width):
        xs = [w_ref[0, off:off + rows, :] for off, rows in parts]
        used = sum(rows for _, rows in parts)
        if used < width:
            xs.append(jnp.zeros((width - used, kb), F32))
        x = xs[0] if len(xs) == 1 else jnp.concatenate(xs, axis=0)
        dst[0, :, dst_off:dst_off + width] = x.T.astype(BF16)

    def group(dst, names):
        off = 0
        for name in names:
            width = IN_SIZES[IN_NAMES.index(name)]
            for c in range(0, width, LANES):
                put(dst, off + c, [(W_OFF[name] + c, LANES)], LANES)
            off += width

    group(wm_ref, ("aq", "ak", "ao", "az"))
    group(wp_ref, ("av", "bv", "bq", "bk"))
    group(wg_ref, ("bz", "cz", "g"))
    group(ws_ref, ("cu",))
    put(ws_ref, S_W, [(W_OFF["ai"], 2 * M_HEADS), (W_OFF["ba"], G_RANK)], LANES)


def _wprep(w_t, kb):
    depth = w_t.shape[0]
    widths = (W_MLSTM, W_PLAIN, W_GATE, SMALL_W)
    return pl.pallas_call(
        _wprep_kernel, grid=(depth, D_MODEL // kb),
        in_specs=[pl.BlockSpec((1, IN_WIDTH, kb), lambda l, i: (l, 0, i))],
        out_specs=[pl.BlockSpec((1, kb, wd), lambda l, i: (l, i, 0)) for wd in widths],
        out_shape=[jax.ShapeDtypeStruct((depth, D_MODEL, wd), BF16) for wd in widths],
        compiler_params=_params("arbitrary", "arbitrary"), name="wprep",
    )(w_t)


def _scatter_heads(dst, y, heads, d, dp, fill):
    rows = y.shape[0]
    for h in range(heads):
        dst[:, h * dp:h * dp + d] = y[:, h * d:(h + 1) * d].astype(BF16)
        dst[:, h * dp + d:(h + 1) * dp] = jnp.full((rows, dp - d), fill, BF16)


def _proj_mlstm_kernel(hn_ref, w_ref, cq_ref, ck_ref, q_ref, k_ref, oz_ref, halo_x, halo_b, *, tm,
                       tiles_per_seq):
    @pl.when(pl.program_id(0) % tiles_per_seq == 0)
    def _():
        halo_x[...] = jnp.zeros_like(halo_x)
        halo_b[...] = jnp.zeros_like(halo_b)

    hn = hn_ref[...]
    row8 = lax.broadcasted_iota(jnp.int32, (SUBLANES, M_W), 0)

    def shift(x, prev8, d):
        rolled = pltpu.roll(x, d, 0)
        head = jnp.where(row8 < d, pltpu.roll(prev8, d, 0), rolled[0:SUBLANES])
        return jnp.concatenate([head, rolled[SUBLANES:]], axis=0)

    def conv_act(c, x, cw_ref, dst, scale):
        x1 = shift(x, halo_x[c], 1)
        pair = cw_ref[1:2, :] * x + cw_ref[0:1, :] * x1
        y = _silu(cw_ref[3:4, :] * x + cw_ref[2:3, :] * x1 + shift(pair, halo_b[c], 2))
        halo_x[c] = x[tm - SUBLANES:tm]
        halo_b[c] = pair[tm - SUBLANES:tm]
        _scatter_heads(dst, y if scale == 1.0 else y * scale, M_HEADS, M_HD, M_HDP, 0.0)

    xq = _dot(hn, w_ref[:, 0:M_W])
    xk = _dot(hn, w_ref[:, M_W:2 * M_W])
    conv_act(0, xq, cq_ref, q_ref, 1.0)
    o = _dot(hn, w_ref[:, 2 * M_W:3 * M_W])
    conv_act(1, xk, ck_ref, k_ref, M_HD ** -0.5)
    z = _dot(hn, w_ref[:, 3 * M_W:])
    oz_ref[...] = (_sigmoid(o) * _silu(z)).astype(BF16)


def _proj_mlstm(hn, l, w, cq, ck, seq, tm):
    n = hn.shape[0]
    padded = jax.ShapeDtypeStruct((n, M_WP), BF16)
    return pl.pallas_call(
        functools.partial(_proj_mlstm_kernel, tm=tm, tiles_per_seq=seq // tm),
        grid=(n // tm,),
        in_specs=[_rows(tm, D_MODEL), _of_layer(l, (D_MODEL, W_MLSTM)), _of_layer(l, (M_CONV, M_W)),
                  _of_layer(l, (M_CONV, M_W))],
        out_specs=[_rows(tm, M_WP), _rows(tm, M_WP), _rows(tm, M_W)],
        out_shape=[padded, padded, jax.ShapeDtypeStruct((n, M_W), BF16)],
        scratch_shapes=[pltpu.VMEM((2, SUBLANES, M_W), F32)] * 2,
        compiler_params=_params("arbitrary"), name="proj_mlstm",
    )(hn, w, cq, ck)


def _proj_gate_kernel(src_ref, nw_ref, w_ref, gz_ref, cz_ref, g_ref, *hn_out, normalize):
    if normalize:
        x = src_ref[...]
        ms = jnp.mean(x * x, axis=-1, keepdims=True)
        hn = (x * lax.rsqrt(ms + EPS) * nw_ref[...]).astype(BF16)
        hn_out[0][...] = hn
    else:
        hn = src_ref[...]
    gz_ref[...] = _silu(_dot(hn, w_ref[:, 0:G_VW])).astype(BF16)
    cz_ref[...] = _silu(_dot(hn, w_ref[:, G_VW:G_VW + S_W])).astype(BF16)
    off = G_VW + S_W
    for c in range(3 * D_MODEL // CW):
        r = _dot(hn, w_ref[:, off + c * CW:off + (c + 1) * CW])
        g_ref[:, c * CW:(c + 1) * CW] = _sigmoid(r).astype(BF16)


def _proj_gate(src, nw, l, w, tm, normalize):
    n = src.shape[0]
    widths = (G_VW, S_W, 3 * D_MODEL) + ((D_MODEL,) if normalize else ())
    return pl.pallas_call(
        functools.partial(_proj_gate_kernel, normalize=normalize), grid=(n // tm,),
        in_specs=[_rows(tm, D_MODEL), _whole((1, D_MODEL)), _of_layer(l, (D_MODEL, W_GATE))],
        out_specs=[_rows(tm, wd) for wd in widths],
        out_shape=[jax.ShapeDtypeStruct((n, wd), BF16) for wd in widths],
        compiler_params=_params("arbitrary"), name="proj_gate",
    )(src, nw, w)


def _split3(x):
    hi = x.astype(BF16)
    r1 = x - hi.astype(F32)
    mid = r1.astype(BF16)
    return hi, mid, (r1 - mid.astype(F32)).astype(BF16)


def _segment_cumsum(x, seg):
    rows, width = x.shape
    tril = (lax.broadcasted_iota(jnp.int32, (seg, seg), 1)
            <= lax.broadcasted_iota(jnp.int32, (seg, seg), 0)).astype(BF16)
    parts = jnp.concatenate(_split3(x), axis=1)
    out = []
    for s in range(rows // seg):
        acc = _dot(tril, parts[s * seg:(s + 1) * seg])
        out.append(acc[:, :width] + acc[:, width:2 * width] + acc[:, 2 * width:])
    return jnp.concatenate(out, axis=0)


def _proj_va_kernel(hn_ref, wp_ref, ws_ref, gb_ref, wal_ref, bal_ref,
                    mv_ref, gv_ref, gq_ref, gk_ref, cu_ref, gmb_ref, gmt_ref, gcum_ref, *, chunk):
    hn = hn_ref[...]
    r = _dot(hn, ws_ref[...])
    _scatter_heads(mv_ref, _dot(hn, wp_ref[:, 0:M_W]), M_HEADS, M_HD, M_HDP, 0.0)
    for j in range(S_BLOCKS):
        cu_ref[j] = r[:, j * LANES:(j + 1) * LANES]
    raw = r[:, S_W:]
    gates = raw + gb_ref[...]
    lane = lax.broadcasted_iota(jnp.int32, gates.shape, 1)
    fgate = (lane >= M_HEADS) & (lane < 2 * M_HEADS)
    za = _dot(raw.astype(BF16), wal_ref[...]) + bal_ref[...]
    gm = jnp.where(fgate, _segment_cumsum(_log_sigmoid(gates), chunk), gates)
    _scatter_heads(gv_ref, _dot(hn, wp_ref[:, M_W:M_W + G_VW]), G_HEADS, G_DV, G_DVP, 0.0)
    gmt_ref[...] = gm.T
    slots = 2 * M_HEADS * LANES
    spread = (lax.broadcasted_iota(jnp.int32, (3 * LANES, slots), 0) % LANES
              == lax.broadcasted_iota(jnp.int32, (3 * LANES, slots), 1) // LANES).astype(BF16)
    gmb_ref[...] = _dot(jnp.concatenate(_split3(gm), axis=1), spread)
    off = M_W + G_VW
    _scatter_heads(gq_ref, _dot(hn, wp_ref[:, off:off + G_KW]) * (G_DK ** -0.5), G_HEADS, G_DK, G_DKP, 0.0)
    gcum_ref[...] = _segment_cumsum(_log_sigmoid(za) * (1.0 / G_TAU), chunk)
    _scatter_heads(gk_ref, _dot(hn, wp_ref[:, off + G_KW:]), G_HEADS, G_DK, G_DKP, 0.0)


def _proj_va(hn, l, wp, ws, gb, wal, bal, tm, chunk):
    n = hn.shape[0]
    widths = (M_WP, G_VWP, G_KWP, G_KWP)
    return pl.pallas_call(
        functools.partial(_proj_va_kernel, chunk=chunk), grid=(n // tm,),
        in_specs=[_rows(tm, D_MODEL), _of_layer(l, (D_MODEL, W_PLAIN)), _of_layer(l, (D_MODEL, SMALL_W)),
                  _of_layer(l, (1, LANES)), _of_layer(l, (LANES, G_KWP)), _of_layer(l, (1, G_KWP))],
        out_specs=[_rows(tm, wd) for wd in widths] + [
            pl.BlockSpec((S_BLOCKS, tm, LANES), lambda i: (0, i, 0)), _rows(tm, 2 * M_HEADS * LANES),
            pl.BlockSpec((LANES, tm), lambda i: (0, i)), _rows(tm, G_KWP)],
        out_shape=[jax.ShapeDtypeStruct((n, wd), BF16) for wd in widths] + [
            jax.ShapeDtypeStruct((S_BLOCKS, n, LANES), F32),
            jax.ShapeDtypeStruct((n, 2 * M_HEADS * LANES), F32),
            jax.ShapeDtypeStruct((LANES, n), F32),
            jax.ShapeDtypeStruct((n, G_KWP), F32)],
        compiler_params=_params("arbitrary"), name="proj_va",
    )(hn, wp, ws, gb, wal, bal)


def _mlstm_body(q_ref, k_ref, v_ref, oz_ref, gmb_ref, gmt_ref, nw_ref, out_ref, c_scr, m_scr, *, tc):
    row = lax.broadcasted_iota(jnp.int32, (tc, tc), 0)
    col = lax.broadcasted_iota(jnp.int32, (tc, tc), 1)
    causal = col <= row
    real = lax.broadcasted_iota(jnp.int32, (tc, M_HDP), 1) < M_HD
    ones_tile = jnp.ones((tc, LANES), BF16)

    def lanes(x, n):
        return jnp.concatenate([x] * n, axis=1)

    def head(h):
        sl = slice(h * M_HDP, (h + 1) * M_HDP)
        qb = q_ref[:, sl]
        kb = k_ref[:, sl]
        vext = jnp.concatenate([v_ref[:, sl], ones_tile], axis=1)
        i_b = gmb_ref[:, h * LANES:(h + 1) * LANES]
        bcum = gmb_ref[:, (M_HEADS + h) * LANES:(M_HEADS + h + 1) * LANES]
        g_row = gmt_ref[h:h + 1, :] - gmt_ref[M_HEADS + h:M_HEADS + h + 1, :]
        m_prev = m_scr[h, 0:1, :]

        inter = bcum + m_prev
        m_t = jnp.maximum(inter, bcum + jnp.max(jnp.where(causal, g_row, NEG), axis=1, keepdims=True))
        w_intra = jnp.exp(jnp.where(causal, g_row + lanes(bcum - m_t, tc // LANES), NEG))
        w_inter = jnp.exp(inter - m_t)
        s = _dot_nt(qb, kb) * w_intra
        c_mat = c_scr[h]
        num = _dot(s.astype(BF16), vext) + lanes(w_inter, 3) * _dot(qb, c_mat.astype(BF16))
        den = num[:, M_HDP:]
        rden = 1.0 / jnp.maximum(jnp.abs(den), jnp.exp(-m_t))
        num = num[:, :M_HDP]

        b_end = bcum[tc - 1:tc, :]
        to_end = b_end - bcum + i_b
        m_new = jnp.maximum(b_end + m_prev, jnp.max(to_end, axis=0, keepdims=True))
        wk = jnp.exp(to_end - m_new)
        w_prev = jnp.exp(b_end + m_prev - m_new)
        c_scr[h] = lanes(w_prev, 3) * c_mat + _dot_tn((lanes(wk, 2) * kb.astype(F32)).astype(BF16), vext)
        m_scr[h] = jnp.broadcast_to(m_new, (SUBLANES, LANES))

        sc = slice(h * M_HD, (h + 1) * M_HD)
        mean = jnp.sum(num, axis=1, keepdims=True) * (1.0 / M_HD)
        xc = jnp.where(real, num - mean, 0.0)
        var = jnp.sum(xc * xc, axis=1, keepdims=True) * (1.0 / M_HD)
        scale = rden * lax.rsqrt(rden * rden * var + EPS)
        ha = (xc * lanes(scale, 2))[:, :M_HD]
        out_ref[:, sc] = (ha * nw_ref[:, sc] * oz_ref[:, sc].astype(F32)).astype(BF16)

    return [functools.partial(head, h) for h in range(M_HEADS)]


def _gla_body(safe, q_ref, k_ref, v_ref, z_ref, b_ref, nw_ref,
              out_ref, kbuf, bbuf, attn_scr, st_scr, *, t, cs):
    row = lax.broadcasted_iota(jnp.int32, (t, t), 0)
    col = lax.broadcasted_iota(jnp.int32, (t, t), 1)
    causal = col <= row

    def operands(h):
        sl = slice(h * G_DKP, (h + 1) * G_DKP)
        return q_ref[:, sl].astype(F32), k_ref[:, sl].astype(F32), b_ref[:, sl]

    @pl.when(safe)
    def _():
        for h in range(G_HEADS):
            qs, kk, b = operands(h)
            a = _dot_nt((qs * jnp.exp(b)).astype(BF16), (kk * jnp.exp(-b)).astype(BF16))
            attn_scr[h] = jnp.where(causal, a, 0.0)

    @pl.when(jnp.logical_not(safe))
    def _():
        rowv = lax.broadcasted_iota(jnp.int32, (t, G_DKP), 0)
        sub = rowv % cs
        lag = row - col
        kbuf[0:cs, :] = jnp.zeros((cs, G_DKP), F32)
        bbuf[0:cs, :] = jnp.zeros((cs, G_DKP), F32)
        for h in range(G_HEADS):
            qs, kk, b = operands(h)
            qparts, kparts = [], []
            for i in range(1, t // cs):
                r = b[i * cs - 1:i * cs, :]
                inblk = (rowv >= i * cs) & (rowv < (i + 1) * cs)
                qparts.append((qs * jnp.exp(jnp.where(inblk, b - r, NEG))).astype(BF16))
                kparts.append((kk * jnp.exp(jnp.where(rowv < i * cs, r - b, NEG))).astype(BF16))
            attn = _dot_nt(jnp.concatenate(qparts, axis=1), jnp.concatenate(kparts, axis=1))
            kbuf[cs:cs + t, :] = kk
            bbuf[cs:cs + t, :] = b
            for j in range(cs):
                ksh = kbuf[cs - j:cs - j + t, :]
                bsh = bbuf[cs - j:cs - j + t, :]
                e = jnp.exp(jnp.where(sub >= j, b - bsh, NEG))
                dj = jnp.sum(qs * ksh * e, axis=1, keepdims=True)
                attn = attn + jnp.where(lag == j, dj, 0.0)
            attn_scr[h] = attn

    def tail(h):
        slv = slice(h * G_DVP, (h + 1) * G_DVP)
        qs, kk, b = operands(h)
        v = v_ref[:, slv]
        blast = b[t - 1:t, :]
        st = st_scr[h]
        o = (_dot_nt((qs * jnp.exp(b)).astype(BF16), st.astype(BF16))
             + _dot(attn_scr[h].astype(BF16), v))
        kdec = (kk * jnp.exp(blast - b)).astype(BF16)
        st_scr[h] = jnp.exp(blast) * st + _dot_tn(v, kdec)

        sc = slice(h * G_DV, (h + 1) * G_DV)
        var = jnp.sum(o * o, axis=1, keepdims=True) * (1.0 / G_DV)
        hb = (o * lax.rsqrt(var + EPS))[:, :G_DV] * nw_ref[:, sc]
        out_ref[:, sc] = (hb * z_ref[:, sc].astype(F32)).astype(BF16)

    return [functools.partial(tail, h) for h in range(G_HEADS)]


def _mixers_kernel(safe_ref, mq_ref, mk_ref, mv_ref, moz_ref, gmb_ref, gmt_ref, mnw_ref,
                   gq_ref, gk_ref, gv_ref, gz_ref, gb_ref, gnw_ref, ya_ref, yb_ref,
                   c_scr, m_scr, kbuf, bbuf, attn_scr, st_scr, *, t, cs):
    @pl.when(pl.program_id(1) == 0)
    def _():
        c_scr[...] = jnp.zeros_like(c_scr)
        m_scr[...] = jnp.zeros_like(m_scr)
        st_scr[...] = jnp.zeros_like(st_scr)

    safe = safe_ref[pl.program_id(0) * pl.num_programs(1) + pl.program_id(1)] != 0
    gla_heads = _gla_body(safe, gq_ref, gk_ref, gv_ref, gz_ref, gb_ref, gnw_ref, yb_ref, kbuf, bbuf,
                          attn_scr, st_scr, t=t, cs=cs)
    mlstm_heads = _mlstm_body(mq_ref, mk_ref, mv_ref, moz_ref, gmb_ref, gmt_ref, mnw_ref, ya_ref,
                              c_scr, m_scr, tc=t)
    for head in gla_heads + mlstm_heads:
        head()


def _mixers(mq, mk, mv, moz, gmb, gmt, gq, gk, gv, gz, gcum, l, m_nw, g_nw, batch, seq, t, cs):
    n = batch * seq
    nt = seq // t

    def blk(width):
        return pl.BlockSpec((t, width), lambda b, c: (b * nt + c, 0))

    safe = (jnp.min(gcum.reshape(n // t, t, G_KWP)[:, t - 1, :], axis=-1) >= G_SAFE_LOG_DECAY).astype(jnp.int32)
    return pl.pallas_call(
        functools.partial(_mixers_kernel, t=t, cs=cs),
        grid=(batch, nt),
        in_specs=[pl.BlockSpec(memory_space=pltpu.SMEM),
                  blk(M_WP), blk(M_WP), blk(M_WP), blk(M_W), blk(2 * M_HEADS * LANES),
                  pl.BlockSpec((LANES, t), lambda b, c: (0, b * nt + c)), _of_layer(l, (1, M_W)),
                  blk(G_KWP), blk(G_KWP), blk(G_VWP), blk(G_VW), blk(G_KWP), _of_layer(l, (1, G_VW))],
        out_specs=[blk(M_W), blk(G_VW)],
        out_shape=[jax.ShapeDtypeStruct((n, M_W), BF16), jax.ShapeDtypeStruct((n, G_VW), BF16)],
        scratch_shapes=[pltpu.VMEM((M_HEADS, M_HDP, M_HDP + LANES), F32),
                        pltpu.VMEM((M_HEADS, SUBLANES, LANES), F32),
                        pltpu.VMEM((t + cs, G_DKP), F32), pltpu.VMEM((t + cs, G_DKP), F32),
                        pltpu.VMEM((G_HEADS, t, t), F32),
                        pltpu.VMEM((G_HEADS, G_DVP, G_DKP), F32)],
        compiler_params=_params("arbitrary", "arbitrary"),
        name="mixers",
    )(safe, mq, mk, mv, moz, gmb, gmt, m_nw, gq, gk, gv, gz, gcum, g_nw)


def _gelu_tanh(x):
    return 0.5 * x * (1.0 + jnp.tanh(math.sqrt(2.0 / math.pi) * (x + 0.044715 * (x * x * x))))


def _s5_kernel(u_ref, kms_ref, mo_ref, pin_ref, pt_ref, d_ref, y_ref, carry, xs, *, nch):
    @pl.when(pl.program_id(2) == 0)
    def _():
        carry[...] = jnp.zeros_like(carry)

    qw = S_Q * LANES
    ltiles = S_SW // LANES
    parts = S5_PARTS if nch % (S5_PARTS * S_SUB) == 0 else 1
    pch = nch // parts
    ngrp = pch // S_SUB
    ucat = jnp.concatenate([u_ref[0, pl.ds(s, nch, stride=S_Q), :] for s in range(S_Q)],
                           axis=1).astype(BF16)
    x = _dot(ucat, kms_ref[0, :, qw:])
    y_in = _dot(ucat, kms_ref[0, :, :qw])

    def axpy(xr, xi, pr, pi, sr, si):
        return xr + (pr * sr - pi * si), xi + (pr * si + pi * sr)

    def scan(xr, xi, p_ref, pos, length):
        d = 1
        while d < length:
            keep = pos >= d
            sr = jnp.where(keep, pltpu.roll(xr, d, 0), 0.0)
            si = jnp.where(keep, pltpu.roll(xi, d, 0), 0.0)
            xr, xi = axpy(xr, xi, p_ref[0, d - 1:d, :S_SW], p_ref[0, d - 1:d, S_SW:], sr, si)
            d *= 2
        return xr, xi

    def slabs(part, arr):
        return [(part * ltiles + j, arr[:, j * LANES:(j + 1) * LANES]) for j in range(ltiles)]

    rowi = lax.broadcasted_iota(jnp.int32, (pch, S_SW), 0)
    rowg = lax.broadcasted_iota(jnp.int32, (ngrp, S_SW), 0)
    pwr = jnp.concatenate([pin_ref[0, :, :S_SW]] * ngrp, axis=0)
    pwi = jnp.concatenate([pin_ref[0, :, S_SW:]] * ngrp, axis=0)
    cr = carry[0:1, :S_SW]
    ci = carry[0:1, S_SW:]
    for p in range(parts):
        c0 = p * pch
        xr, xi = scan(x[c0:c0 + pch, :S_SW], x[c0:c0 + pch, S_SW:], pin_ref, rowi % S_SUB, S_SUB)
        for part, arr in ((0, xr), (1, xi)):
            for j, piece in slabs(part, arr):
                xs[j, c0:c0 + pch, :] = piece

        def group_ends(part):
            return jnp.concatenate([xs[part * ltiles + j, pl.ds(c0 + S_SUB - 1, ngrp, stride=S_SUB), :]
                                    for j in range(ltiles)], axis=1)

        er, ei = scan(group_ends(0), group_ends(1), pt_ref, rowg, ngrp)
        er, ei = axpy(er, ei, pt_ref[0, 0:ngrp, :S_SW], pt_ref[0, 0:ngrp, S_SW:], cr, ci)
        first = rowg == 0
        gr = jnp.where(first, cr, pltpu.roll(er, 1, 0))
        gi = jnp.where(first, ci, pltpu.roll(ei, 1, 0))
        for part, arr in ((0, gr), (1, gi)):
            for j, piece in slabs(part, arr):
                for k in range(S_SUB):
                    xs[j, pl.ds(c0 + k, ngrp, stride=S_SUB), :] = piece
        sr = jnp.concatenate([xs[j, c0:c0 + pch, :] for j in range(ltiles)], axis=1)
        si = jnp.concatenate([xs[ltiles + j, c0:c0 + pch, :] for j in range(ltiles)], axis=1)
        xr, xi = axpy(xr, xi, pwr, pwi, sr, si)

        firstc = rowi == 0
        xpr = jnp.where(firstc, cr, pltpu.roll(xr, 1, 0))
        xpi = jnp.where(firstc, ci, pltpu.roll(xi, 1, 0))
        y = y_in[c0:c0 + pch] + _dot(jnp.concatenate([xpr, xpi], axis=1).astype(BF16), mo_ref[0])
        r0 = c0 * S_Q
        for t in range(S_Q):
            y_ref[0, pl.ds(r0 + t, pch, stride=S_Q), :] = y[:, t * LANES:(t + 1) * LANES]
        rows = slice(r0, r0 + pch * S_Q)
        y_ref[0, rows, :] = y_ref[0, rows, :] + d_ref[0] * u_ref[0, rows, :]
        cr, ci = er[ngrp - 1:ngrp, :], ei[ngrp - 1:ngrp, :]
    carry[0:1, :S_SW] = cr
    carry[0:1, S_SW:] = ci


def _s5(cu, l, kms, mo, pin, pt, dskip, batch, seq, tm):
    n = batch * seq
    nt = seq // tm
    nch = tm // S_Q
    qw = S_Q * LANES

    def table(rows, cols):
        return pl.BlockSpec((None, 1, rows, cols), lambda j, b, t: (l, j, 0, 0))

    return pl.pallas_call(
        functools.partial(_s5_kernel, nch=nch),
        grid=(S_BLOCKS, batch, nt),
        in_specs=[pl.BlockSpec((1, tm, LANES), lambda j, b, t: (j, b * nt + t, 0)),
                  table(qw, qw + 2 * S_SW), table(2 * S_SW, qw), table(S_SUB, 2 * S_SW),
                  table(nch // S_SUB, 2 * S_SW), table(1, LANES)],
        out_specs=pl.BlockSpec((1, tm, LANES), lambda j, b, t: (j, b * nt + t, 0)),
        out_shape=jax.ShapeDtypeStruct((S_BLOCKS, n, LANES), F32),
        scratch_shapes=[pltpu.VMEM((SUBLANES, 2 * S_SW), F32),
                        pltpu.VMEM((2 * S_SW // LANES, nch, LANES), F32)],
        compiler_params=_params("arbitrary", "arbitrary", "arbitrary"),
        name="s5",
    )(cu, kms, mo, pin, pt, dskip)


def _s5_expand_kernel(msr_ref, msi_ref, mor_ref, moi_ref, kms_ref, mo_ref):
    q = S_Q
    qw = q * LANES

    def embed(m, rows, width):
        shape = (width, S_GPB * width)
        tile = (lax.broadcasted_iota(jnp.int32, shape, 1) % width
                == lax.broadcasted_iota(jnp.int32, shape, 0)).astype(BF16)
        oshape = (S_GPB * rows, S_GPB * width)
        diag = (lax.broadcasted_iota(jnp.int32, oshape, 0) // rows
                == lax.broadcasted_iota(jnp.int32, oshape, 1) // width)
        return jnp.where(diag, _dot(m, tile), 0.0).astype(BF16)

    ms = [(embed(msr_ref[0, 0, s], S_GROUP, S_STATE), embed(msi_ref[0, 0, s], S_GROUP, S_STATE))
          for s in range(q)]
    mo = [(embed(mor_ref[0, 0, tau], S_STATE, S_GROUP), embed(moi_ref[0, 0, tau], S_STATE, S_GROUP))
          for tau in range(q + 1)]
    bbr, bbi = ms[q - 1]
    kd = [(_dot(bbr, mo[tau][0]) + _dot(bbi, mo[tau][1])).astype(BF16) for tau in range(q)]
    zero = jnp.zeros((LANES, LANES), BF16)
    for s in range(q):
        rows = slice(s * LANES, (s + 1) * LANES)
        for t in range(q):
            kms_ref[0, 0, rows, t * LANES:(t + 1) * LANES] = kd[t - s] if t >= s else zero
        kms_ref[0, 0, rows, qw:qw + S_SW] = ms[s][0]
        kms_ref[0, 0, rows, qw + S_SW:] = ms[s][1]
    for t in range(q):
        cols = slice(t * LANES, (t + 1) * LANES)
        mo_ref[0, 0, 0:S_SW, cols] = mo[t + 1][0]
        mo_ref[0, 0, S_SW:, cols] = mo[t + 1][1]


def _s5_expand(msr, msi, mor, moi):
    depth = msr.shape[0]
    qw = S_Q * LANES

    def blk(a):
        return pl.BlockSpec((1, 1) + a.shape[2:], lambda d, j: (d, j, 0, 0, 0))

    return pl.pallas_call(
        _s5_expand_kernel, grid=(depth, S_BLOCKS),
        in_specs=[blk(a) for a in (msr, msi, mor, moi)],
        out_specs=[pl.BlockSpec((1, 1, qw, qw + 2 * S_SW), lambda d, j: (d, j, 0, 0)),
                   pl.BlockSpec((1, 1, 2 * S_SW, qw), lambda d, j: (d, j, 0, 0))],
        out_shape=[jax.ShapeDtypeStruct((depth, S_BLOCKS, qw, qw + 2 * S_SW), BF16),
                   jax.ShapeDtypeStruct((depth, S_BLOCKS, 2 * S_SW, qw), BF16)],
        compiler_params=_params("arbitrary", "arbitrary"), name="s5_expand",
    )(msr, msi, mor, moi)


def _s5_tables(lam_re, lam_im, log_dt, b_re, b_im, c_re, c_im, nch):
    q = S_Q
    dep = lam_re.shape[0]
    lr = jnp.minimum(lam_re.astype(F32), -1e-4)
    li = lam_im.astype(F32)
    dt = jnp.exp(log_dt.astype(F32))[..., None]
    mag = jnp.exp(lr * dt)
    ab_re = mag * jnp.cos(li * dt)
    ab_im = mag * jnp.sin(li * dt)
    nr = ab_re - 1.0
    den = lr * lr + li * li
    coef_re = ((nr * lr + ab_im * li) / den)[:, :, None, :]
    coef_im = ((ab_im * lr - nr * li) / den)[:, :, None, :]
    brt = jnp.swapaxes(b_re.astype(F32), -1, -2)
    bit = jnp.swapaxes(b_im.astype(F32), -1, -2)
    bb_re = coef_re * brt - coef_im * bit
    bb_im = coef_re * bit + coef_im * brt

    def apow(nvals):
        e = jnp.asarray(nvals, F32)[:, None, None, None]
        m = jnp.exp(e * (lr * dt))
        ang = e * (li * dt)
        return m * jnp.cos(ang), m * jnp.sin(ang)

    def per_block(m, rows):
        m = m.reshape(m.shape[0], dep, S_BLOCKS, S_GPB * rows, m.shape[-1])
        return m.transpose(1, 2, 0, 3, 4).astype(BF16)

    ar, ai = apow(list(range(q + 1)))
    crt = jnp.swapaxes(c_re.astype(F32), -1, -2)
    cit = jnp.swapaxes(c_im.astype(F32), -1, -2)
    ca_re = crt[None] * ar[..., None] - cit[None] * ai[..., None]
    ca_im = crt[None] * ai[..., None] + cit[None] * ar[..., None]
    pw_r = ar[:q][::-1][:, :, :, None, :]
    pw_i = ai[:q][::-1][:, :, :, None, :]
    kms, mo = _s5_expand(per_block(pw_r * bb_re - pw_i * bb_im, S_GROUP),
                         per_block(pw_r * bb_im + pw_i * bb_re, S_GROUP),
                         per_block(ca_re, S_STATE), per_block(-ca_im, S_STATE))
    def power_table(nvals):
        pr, pi = apow(nvals)
        pr = pr.reshape(len(nvals), dep, S_BLOCKS, S_SW).transpose(1, 2, 0, 3)
        pi = pi.reshape(len(nvals), dep, S_BLOCKS, S_SW).transpose(1, 2, 0, 3)
        return jnp.concatenate([pr, pi], axis=3)

    pin = power_table([q * (r + 1) for r in range(S_SUB)])
    pt = power_table([q * S_SUB * (k + 1) for k in range(nch // S_SUB)])
    return kms, mo, pin, pt


def _merge_kernel(ya_ref, yb_ref, yc_ref, cz_ref, g_ref, x_ref, wa_ref, wb_ref, wglu_ref, wc_ref,
                  wo_ref, nw_ref, *out_refs, final):
    yc0 = _gelu_tanh(jnp.concatenate([yc_ref[j] for j in range(S_BLOCKS)], axis=1))
    glu = _dot(yc0.astype(BF16), wglu_ref[...])
    yc = yc0 * _sigmoid(glu) * cz_ref[...].astype(F32)
    merged = (g_ref[:, 0:D_MODEL].astype(F32) * _dot(ya_ref[...], wa_ref[...])
              + g_ref[:, D_MODEL:2 * D_MODEL].astype(F32) * _dot(yb_ref[...], wb_ref[...])
              + g_ref[:, 2 * D_MODEL:].astype(F32) * _dot(yc.astype(BF16), wc_ref[...]))
    out = x_ref[...] + _dot(merged.astype(BF16), wo_ref[...])
    ms = jnp.mean(out * out, axis=-1, keepdims=True)
    normed = out * lax.rsqrt(ms + EPS) * nw_ref[...]
    if final:
        out_refs[0][...] = normed
    else:
        out_refs[0][...] = out
        out_refs[1][...] = normed.astype(BF16)


def _merge(ya, yb, yc, cz, g, x2, l, wa, wb, wglu, wc, wo, nw, tm, final):
    n = x2.shape[0]
    x_out = jax.ShapeDtypeStruct((n, D_MODEL), F32)
    hn_out = jax.ShapeDtypeStruct((n, D_MODEL), BF16)
    return pl.pallas_call(
        functools.partial(_merge_kernel, final=final),
        grid=(n // tm,),
        in_specs=[_rows(tm, M_W), _rows(tm, G_VW),
                  pl.BlockSpec((S_BLOCKS, tm, LANES), lambda i: (0, i, 0)),
                  _rows(tm, S_W), _rows(tm, 3 * D_MODEL), _rows(tm, D_MODEL),
                  _of_layer(l, (M_W, D_MODEL)), _of_layer(l, (G_VW, D_MODEL)), _of_layer(l, (S_W, S_W)),
                  _of_layer(l, (S_W, D_MODEL)), _of_layer(l, (D_MODEL, D_MODEL)), _whole((1, D_MODEL))],
        out_specs=[_rows(tm, D_MODEL)] if final else [_rows(tm, D_MODEL), _rows(tm, D_MODEL)],
        out_shape=[x_out] if final else [x_out, hn_out],
        compiler_params=_params("arbitrary"),
        name="merge",
    )(ya, yb, yc, cz, g, x2, wa, wb, wglu, wc, wo, nw)


def _pad_heads(w, heads, d, dp):
    zeros = jnp.zeros(w.shape[:-1] + (dp - d,), w.dtype)
    parts = []
    for h in range(heads):
        parts += [w[..., h * d:(h + 1) * d], zeros]
    return jnp.concatenate(parts, axis=-1)


def _prepare(p, nch):
    depth = p["w_in"].shape[0]
    w_mlstm, w_plain, w_gate, w_small = _wprep(jnp.swapaxes(p["w_in"], 1, 2), LANES)
    gkp = functools.partial(_pad_heads, heads=G_HEADS, d=G_DK, dp=G_DKP)
    kms, mo, pin, pt = _s5_tables(p["s5_lam_re"], p["s5_lam_im"], p["s5_log_dt"], p["s5_B_re"], p["s5_B_im"],
                               p["s5_C_re"], p["s5_C_im"], nch)
    gate_b = p["mlstm_gate_b"]
    return dict(
        w_mlstm=w_mlstm, w_plain=w_plain, w_gate=w_gate, w_small=w_small,
        cq=p["mlstm_conv"][:, :, :M_W], ck=p["mlstm_conv"][:, :, M_W:],
        gb=jnp.concatenate([gate_b[:, 0], gate_b[:, 1],
                            jnp.zeros((depth, LANES - 2 * M_HEADS), F32)], axis=1)[:, None, :],
        m_nw=p["mlstm_norm"][:, None, :],
        wal=jnp.concatenate([jnp.zeros((depth, 2 * M_HEADS, G_KWP), F32), gkp(p["gla_w_alpha"]),
                             jnp.zeros((depth, LANES - 2 * M_HEADS - G_RANK, G_KWP), F32)],
                            axis=1).astype(BF16),
        bal=gkp(p["gla_b_alpha"])[:, None, :],
        g_nw=p["gla_norm"][:, None, :],
        kms=kms, mo=mo, pin=pin, pt=pt,
        dskip=p["s5_D"].reshape(depth, S_BLOCKS, 1, LANES),
        wa=p["w_branch_mlstm"].astype(BF16), wb=p["w_branch_gla"].astype(BF16),
        wglu=p["s5_w_glu"].astype(BF16), wc=p["w_branch_s5"].astype(BF16), wo=p["w_out"].astype(BF16),
    )


def _tiles(batch, seq):
    return dict(proj=min(1024, seq), chunk=min(256, seq), g_sub=16,
                s5=min(4096, seq), merge=min(512, batch * seq))


def _layer(x2, hn, batch, seq, l, w, this_norm, next_norm, final, tl):
    tm = tl["proj"]
    nw = this_norm.reshape(1, D_MODEL)
    if hn is None:
        gz, czs, gs, hn = _proj_gate(x2, nw, l, w["w_gate"], tm, True)
    else:
        gz, czs, gs = _proj_gate(hn, nw, l, w["w_gate"], tm, False)
    q, k, oz = _proj_mlstm(hn, l, w["w_mlstm"], w["cq"], w["ck"], seq, tm)
    mv, gv, gq, gk, cu_s, gm, gmt, gcum = _proj_va(hn, l, w["w_plain"], w["w_small"], w["gb"], w["wal"],
                                                   w["bal"], tm, tl["chunk"])
    ya, yb = _mixers(q, k, mv, oz, gm, gmt, gq, gk, gv, gz, gcum, l, w["m_nw"], w["g_nw"], batch, seq,
                     tl["chunk"], tl["g_sub"])
    yc = _s5(cu_s, l, w["kms"], w["mo"], w["pin"], w["pt"], w["dskip"], batch, seq, tl["s5"])
    return _merge(ya, yb, yc, czs, gs, x2, l, w["wa"], w["wb"], w["wglu"], w["wc"], w["wo"],
                  next_norm.reshape(1, D_MODEL), tl["merge"], final)


def kernel(x, norm_w, w_in, mlstm_conv, mlstm_gate_b, mlstm_norm, gla_w_alpha, gla_b_alpha, gla_norm,
           s5_lam_re, s5_lam_im, s5_log_dt, s5_B_re, s5_B_im, s5_C_re, s5_C_im, s5_D, s5_w_glu,
           w_branch_mlstm, w_branch_gla, w_branch_s5, w_out, final_norm):
    batch, seq, _ = x.shape
    depth = norm_w.shape[0]
    tl = _tiles(batch, seq)
    prep = _prepare(dict(w_in=w_in, mlstm_conv=mlstm_conv, mlstm_gate_b=mlstm_gate_b, mlstm_norm=mlstm_norm,
                         gla_w_alpha=gla_w_alpha, gla_b_alpha=gla_b_alpha, gla_norm=gla_norm,
                         s5_lam_re=s5_lam_re, s5_lam_im=s5_lam_im, s5_log_dt=s5_log_dt, s5_B_re=s5_B_re,
                         s5_B_im=s5_B_im, s5_C_re=s5_C_re, s5_C_im=s5_C_im, s5_D=s5_D, s5_w_glu=s5_w_glu,
                         w_branch_mlstm=w_branch_mlstm, w_branch_gla=w_branch_gla,
                         w_branch_s5=w_branch_s5, w_out=w_out), tl["s5"] // S_Q)
    x2 = x.reshape(batch * seq, D_MODEL)
    hn = None
    for l in range(depth):
        final = l == depth - 1
        res = _layer(x2, hn, batch, seq, l, prep, norm_w[l], final_norm if final else norm_w[l + 1], final, tl)
        if final:
            x2 = res[0]
        else:
            x2, hn = res
    return x2.reshape(batch, seq, D_MODEL)
```
